```python
import jax, jax.numpy as jnp
from jax import lax
import numpy as np

D_MODEL = 1024
BATCH = 32
SEQ = 256
DEPTH = 1
DEC_BATCH = 4
DEC_SEQ = 2048
PAST_LEN = 256

GRID_W = 64
N_HEADS = 8
HEAD_DIM = 64
D_ATTN = N_HEADS * HEAD_DIM
N_FGROUPS = 4
FGROUP_DIM = 128
D_FOUR = N_FGROUPS * FGROUP_DIM
D_IN = D_FOUR + 3 * D_ATTN
WIN_H_MAX = 8
WIN_W = 16
Q_BLOCK_W = 16
SLAB_W = 32
RPB_H = 2 * WIN_H_MAX - 1
RPB_W = 2 * WIN_W - 1
N_EXPERTS = 32
TOP_K = 4
D_FF = 1024
SWIGLU_LIMIT = 7.0
SWIGLU_ALPHA = 1.702
EPS = 1e-6
Q_BLOCK = 128

kernel_name = "hybrid_fnet_natten_moe_diffusion_step"


def rmsnorm(x, g):
    xf = x.astype(jnp.float32)
    y = xf * lax.rsqrt(jnp.mean(xf * xf, axis=-1, keepdims=True) + EPS)
    return (y * g.astype(jnp.float32)).astype(x.dtype)


def adaln(cvec, w_ada, b_ada):
    mod = jax.nn.silu(cvec) @ w_ada + b_ada
    return jnp.split(mod, 6, axis=-1)


def project(h, w_in, g_q, g_k):
    B, T, _ = h.shape
    proj = h @ w_in
    u, q, k, v = jnp.split(proj, [D_FOUR, D_FOUR + D_ATTN, D_FOUR + 2 * D_ATTN], axis=-1)
    q = rmsnorm(q.reshape(B, T, N_HEADS, HEAD_DIM), g_q).transpose(0, 2, 1, 3)
    k = rmsnorm(k.reshape(B, T, N_HEADS, HEAD_DIM), g_k).transpose(0, 2, 1, 3)
    v = v.reshape(B, T, N_HEADS, HEAD_DIM).transpose(0, 2, 1, 3)
    return u, q, k, v


def fourier_mix(u):
    B, T, _ = u.shape
    uf = u.astype(jnp.float32).reshape(B, T, N_FGROUPS, FGROUP_DIM)
    y = jnp.fft.fft2(uf, axes=(1, 3), norm="ortho").real
    return y.reshape(B, T, D_FOUR).astype(u.dtype)


def context_attention(q, k, v):
    B, H, S, dh = q.shape
    scale = dh ** -0.5
    nblk = S // Q_BLOCK
    qb = q.reshape(B, H, nblk, Q_BLOCK, dh).transpose(2, 0, 1, 3, 4)

    def block(qi):
        s = jnp.einsum('bhqd,bhkd->bhqk', qi, k).astype(jnp.float32) * scale
        p = jax.nn.softmax(s, axis=-1)
        return jnp.einsum('bhqk,bhkd->bhqd', p.astype(v.dtype), v)

    o = lax.map(block, qb)
    return o.transpose(1, 2, 0, 3, 4).reshape(B, H, S, dh)


def neighbourhood_attention(q, k, v, k_ctx, v_ctx, rpb):
    B, H, T, dh = q.shape
    scale = dh ** -0.5
    rows = T // GRID_W
    kh = min(WIN_H_MAX, rows)
    nb = GRID_W // Q_BLOCK_W
    r = jnp.arange(rows)
    rs = jnp.clip(r - kh // 2, 0, rows - kh)
    key_rows = rs[:, None] + jnp.arange(kh)
    qcol = jnp.arange(GRID_W).reshape(nb, Q_BLOCK_W)
    sb = jnp.clip(jnp.arange(nb) * Q_BLOCK_W - WIN_W // 2, 0, GRID_W - SLAB_W)
    key_cols = sb[:, None] + jnp.arange(SLAB_W)
    tok_idx = (key_rows[:, None, :, None] * GRID_W + key_cols[None, :, None, :]).reshape(rows, nb, kh * SLAB_W)
    kg = jnp.take(k, tok_idx, axis=2)
    vg = jnp.take(v, tok_idx, axis=2)
    qb = q.reshape(B, H, rows, nb, Q_BLOCK_W, dh)
    s_win = jnp.einsum('bhrnqd,bhrnkd->bhrnqk', qb, kg).astype(jnp.float32) * scale
    cs = jnp.clip(qcol - WIN_W // 2, 0, GRID_W - WIN_W)
    kc = key_cols[:, None, :]
    in_win = (kc >= cs[..., None]) & (kc < cs[..., None] + WIN_W)
    mask = jnp.broadcast_to(in_win[:, :, None, :], (nb, Q_BLOCK_W, kh, SLAB_W)).reshape(nb, Q_BLOCK_W, kh * SLAB_W)
    dr = key_rows - r[:, None] + (WIN_H_MAX - 1)
    dc = jnp.clip(kc - qcol[..., None], -(WIN_W - 1), WIN_W - 1) + (WIN_W - 1)
    bias = rpb[:, dr[:, None, None, :, None], dc[None, :, :, None, :]]
    bias = bias.reshape(H, rows, nb, Q_BLOCK_W, kh * SLAB_W).astype(jnp.float32)
    s_win = jnp.where(mask[None, None, None], s_win + bias[None], -jnp.inf)
    s_ctx = jnp.einsum('bhrnqd,bhpd->bhrnqp', qb, k_ctx).astype(jnp.float32) * scale
    p = jax.nn.softmax(jnp.concatenate([s_win, s_ctx], axis=-1), axis=-1)
    nw = kh * SLAB_W
    o = (jnp.einsum('bhrnqk,bhrnkd->bhrnqd', p[..., :nw].astype(v.dtype), vg)
         + jnp.einsum('bhrnqp,bhpd->bhrnqd', p[..., nw:].astype(v.dtype), v_ctx))
    return o.reshape(B, H, T, dh)


def heads_to_tokens(o):
    B, H, T, dh = o.shape
    return o.transpose(0, 2, 1, 3).reshape(B, T, H * dh)


def merge_branches(h, a_four, a_attn, w_fmap, w_amap, w_gate, b_gate, w_out):
    fa = a_four @ w_fmap
    fb = a_attn @ w_amap
    ga, gb = jnp.split(jax.nn.sigmoid(h @ w_gate + b_gate), 2, axis=-1)
    return (ga * fa + gb * fb) @ w_out


def moe(h, w_router, b_router, w1, b1, w2, b2):
    B, T, D = h.shape
    hf = h.reshape(B * T, D)
    logits = (hf @ w_router + b_router).astype(jnp.float32)
    vals, idx = lax.top_k(logits, TOP_K)
    wts = jax.nn.softmax(vals, axis=-1)
    combine = jnp.einsum('nk,nke->ne', wts, jax.nn.one_hot(idx, N_EXPERTS, dtype=jnp.float32))
    out = jnp.zeros((B * T, D), jnp.float32)
    for e in range(N_EXPERTS):
        gu = hf @ w1[e] + b1[e]
        gate = jnp.minimum(gu[:, :D_FF], SWIGLU_LIMIT)
        up = jnp.clip(gu[:, D_FF:], -SWIGLU_LIMIT, SWIGLU_LIMIT)
        glu = gate * jax.nn.sigmoid(SWIGLU_ALPHA * gate)
        y = ((up + 1) * glu) @ w2[e] + b2[e]
        out = out + combine[:, e:e + 1] * y.astype(jnp.float32)
    return out.astype(h.dtype).reshape(B, T, D)


def setup_inputs(seed: int = 0) -> dict:
    key = jax.random.key(seed)
    ks = jax.random.split(key, 24)
    n = jax.random.normal
    D = D_MODEL
    return {
        "x_prompt": n(ks[0], (BATCH, SEQ, D), jnp.float32),
        "x_sample": n(ks[1], (DEC_BATCH, DEC_SEQ, D), jnp.float32),
        "cache_k": n(ks[2], (DEC_BATCH, DEPTH, N_HEADS, PAST_LEN, HEAD_DIM), jnp.float32),
        "cache_v": n(ks[3], (DEC_BATCH, DEPTH, N_HEADS, PAST_LEN, HEAD_DIM), jnp.float32),
        "c": n(ks[4], (DEC_BATCH, D), jnp.float32),
        "c_ctx": n(ks[5], (D,), jnp.float32),
        "g_norm1": 1.0 + 0.01 * n(ks[6], (DEPTH, D), jnp.float32),
        "w_ada": 0.5 * D ** -0.5 * n(ks[7], (DEPTH, D, 6 * D), jnp.float32),
        "b_ada": 0.01 * n(ks[8], (DEPTH, 6 * D), jnp.float32),
        "w_in": D ** -0.5 * n(ks[9], (DEPTH, D, D_IN), jnp.float32),
        "g_q": 1.0 + 0.01 * n(ks[10], (DEPTH, HEAD_DIM), jnp.float32),
        "g_k": 1.0 + 0.01 * n(ks[11], (DEPTH, HEAD_DIM), jnp.float32),
        "rpb": 0.1 * n(ks[12], (DEPTH, N_HEADS, RPB_H, RPB_W), jnp.float32),
        "w_fmap": D_FOUR ** -0.5 * n(ks[13], (DEPTH, D_FOUR, D), jnp.float32),
        "w_amap": D_ATTN ** -0.5 * n(ks[14], (DEPTH, D_ATTN, D), jnp.float32),
        "w_gate": D ** -0.5 * n(ks[15], (DEPTH, D, 2 * D), jnp.float32),
        "b_gate": 0.01 * n(ks[16], (DEPTH, 2 * D), jnp.float32),
        "w_out": D ** -0.5 * n(ks[17], (DEPTH, D, D), jnp.float32),
        "g_norm2": 1.0 + 0.01 * n(ks[18], (DEPTH, D), jnp.float32),
        "w_router": D ** -0.5 * n(ks[19], (DEPTH, D, N_EXPERTS), jnp.float32),
        "b_router": 0.01 * n(ks[20], (DEPTH, N_EXPERTS), jnp.float32),
        "w1": D ** -0.5 * n(ks[21], (DEPTH, N_EXPERTS, D, 2 * D_FF), jnp.float32),
        "b1": 0.01 * n(ks[22], (DEPTH, N_EXPERTS, 2 * D_FF), jnp.float32),
        "w2": D_FF ** -0.5 * n(ks[23], (DEPTH, N_EXPERTS, D_FF, D), jnp.float32),
        "b2": 0.01 * n(jax.random.fold_in(key, 99), (DEPTH, N_EXPERTS, D), jnp.float32),
    }


def reference(x_prompt, x_sample, cache_k, cache_v, c, c_ctx, g_norm1, w_ada, b_ada, w_in, g_q, g_k, rpb,
              w_fmap, w_amap, w_gate, b_gate, w_out, g_norm2, w_router, b_router, w1, b1, w2, b2):
    xp = x_prompt
    new_ks, new_vs = [], []
    for l in range(DEPTH):
        sh1, sc1, gt1, sh2, sc2, gt2 = adaln(c_ctx, w_ada[l], b_ada[l])
        h = rmsnorm(xp, g_norm1[l]) * (1 + sc1) + sh1
        u, q, k, v = project(h, w_in[l], g_q[l], g_k[l])
        new_ks.append(k)
        new_vs.append(v)
        a_attn = heads_to_tokens(context_attention(q, k, v))
        a_four = fourier_mix(u)
        xp = xp + gt1 * merge_branches(h, a_four, a_attn, w_fmap[l], w_amap[l], w_gate[l], b_gate[l], w_out[l])
        h = rmsnorm(xp, g_norm2[l]) * (1 + sc2) + sh2
        xp = xp + gt2 * moe(h, w_router[l], b_router[l], w1[l], b1[l], w2[l], b2[l])
    new_k = jnp.stack(new_ks, axis=1)
    new_v = jnp.stack(new_vs, axis=1)

    xs = x_sample
    for l in range(DEPTH):
        sh1, sc1, gt1, sh2, sc2, gt2 = [m[:, None, :] for m in adaln(c, w_ada[l], b_ada[l])]
        h = rmsnorm(xs, g_norm1[l]) * (1 + sc1) + sh1
        u, q, k, v = project(h, w_in[l], g_q[l], g_k[l])
        a_attn = heads_to_tokens(neighbourhood_attention(q, k, v, cache_k[:, l], cache_v[:, l], rpb[l]))
        a_four = fourier_mix(u)
        xs = xs + gt1 * merge_branches(h, a_four, a_attn, w_fmap[l], w_amap[l], w_gate[l], b_gate[l], w_out[l])
        h = rmsnorm(xs, g_norm2[l]) * (1 + sc2) + sh2
        xs = xs + gt2 * moe(h, w_router[l], b_router[l], w1[l], b1[l], w2[l], b2[l])

    return (xp, xs, new_k, new_v)
```

```python
import functools

import numpy as np
import jax
import jax.numpy as jnp
from jax import lax
from jax.experimental import pallas as pl
from jax.experimental.pallas import tpu as pltpu

F32 = jnp.float32
BF16 = jnp.bfloat16

GRID_W = 64
N_HEADS = 8
HEAD_DIM = 64
N_FGROUPS = 4
FGROUP_DIM = 128
WIN_H_MAX = 8
WIN_W = 16
TOP_K = 4
SWIGLU_LIMIT = 7.0
SWIGLU_ALPHA = 1.702
EPS = 1e-6

LANES = 128
TOK_TILE = 512
ATT_TILE = 256
MOE_TILE = 512
VMEM_LIMIT = 56 * 1024 * 1024
NEG_BIG = -1e30


def _cparams(n_axes, vmem=VMEM_LIMIT):
    return pltpu.CompilerParams(dimension_semantics=("arbitrary",) * n_axes, vmem_limit_bytes=vmem)


def _rms(x):
    return x * lax.rsqrt(jnp.mean(x * x, axis=-1, keepdims=True) + EPS)


def _ada_kernel(c_ref, w_ref, b_ref, o_ref):
    cv = c_ref[...]
    s = cv * jax.nn.sigmoid(cv)
    o_ref[...] = jnp.dot(s, w_ref[...], precision=lax.Precision.HIGHEST,
                         preferred_element_type=F32) + b_ref[...]


def _ada(cvecs, w_ada, b_ada):
    rows, d = cvecs.shape
    n = w_ada.shape[1]
    blk = 1024
    return pl.pallas_call(
        _ada_kernel,
        out_shape=jax.ShapeDtypeStruct((rows, n), F32),
        grid=(n // blk,),
        in_specs=[pl.BlockSpec((rows, d), lambda j: (0, 0)),
                  pl.BlockSpec((d, blk), lambda j: (0, j)),
                  pl.BlockSpec((1, blk), lambda j: (0, j))],
        out_specs=pl.BlockSpec((rows, blk), lambda j: (0, j)),
        compiler_params=_cparams(1),
        name="ada",
    )(cvecs, w_ada, b_ada.reshape(1, n))


def _proj_kernel(n_ctx_tiles, xp_ref, xs_ref, mod_ref, g1_ref, win_ref, hsum_ref, gq_ref, gk_ref,
                 u_ref, q_ref, k_ref, v_ref, nk_ref, nv_ref):
    t = pl.program_id(0)
    is_ctx = t < n_ctx_tiles
    x = jnp.where(is_ctx, xp_ref[...], xs_ref[...])
    sh1 = mod_ref[0, 0:1, :]
    sc1 = mod_ref[0, 1:2, :]
    h = _rms(x) * g1_ref[...] * (1.0 + sc1) + sh1
    proj = jnp.dot(h.astype(BF16), win_ref[...], preferred_element_type=F32)
    da = q_ref.shape[1]
    df = u_ref.shape[1]
    u_ref[...] = proj[:, :df].astype(BF16)
    q = proj[:, df:df + da]
    k = proj[:, df + da:df + 2 * da]
    v = proj[:, df + 2 * da:]
    msq = jnp.dot((q * q).astype(BF16), hsum_ref[...], preferred_element_type=F32)
    msk = jnp.dot((k * k).astype(BF16), hsum_ref[...], preferred_element_type=F32)
    qn = q * lax.rsqrt(msq + EPS) * gq_ref[...]
    kn = k * lax.rsqrt(msk + EPS) * gk_ref[...]
    q_ref[...] = qn.astype(BF16)
    k_ref[...] = kn.astype(BF16)
    v_ref[...] = v.astype(BF16)

    @pl.when(is_ctx)
    def _():
        nb, _, nh, s, dh = nk_ref.shape
        for b in range(nb):
            for hd in range(nh):
                nk_ref[b, 0, hd] = kn[b * s:(b + 1) * s, hd * dh:(hd + 1) * dh]
                nv_ref[b, 0, hd] = v[b * s:(b + 1) * s, hd * dh:(hd + 1) * dh]


def _proj(xp, xs, mod, g1, w_in_b, hsum, gq_t, gk_t, batch, seq, tiles_per_lat_batch):
    n_ctx, d = xp.shape
    n_lat = xs.shape[0]
    tm = TOK_TILE
    n_ctx_tiles = n_ctx // tm
    n_tiles = (n_ctx + n_lat) // tm
    n = n_ctx + n_lat
    d_in = w_in_b.shape[1]
    da = N_HEADS * HEAD_DIM
    df = d_in - 3 * da
    bpt = tm // seq
    last = n_ctx_tiles - 1

    def mod_row(t):
        return jnp.where(t < n_ctx_tiles, 0, 1 + (t - n_ctx_tiles) // tiles_per_lat_batch)

    tok = lambda w: pl.BlockSpec((tm, w), lambda t: (t, 0))
    full = lambda a: pl.BlockSpec(a.shape, lambda t: (0,) * a.ndim)
    kv_spec = pl.BlockSpec((bpt, 1, N_HEADS, seq, HEAD_DIM), lambda t: (jnp.minimum(t, last), 0, 0, 0, 0))
    kv_shape = jax.ShapeDtypeStruct((batch, 1, N_HEADS, seq, HEAD_DIM), F32)
    return pl.pallas_call(
        functools.partial(_proj_kernel, n_ctx_tiles),
        out_shape=(jax.ShapeDtypeStruct((n, df), BF16),) + (jax.ShapeDtypeStruct((n, da), BF16),) * 3
        + (kv_shape, kv_shape),
        grid=(n_tiles,),
        in_specs=[pl.BlockSpec((tm, d), lambda t: (jnp.minimum(t, last), 0)),
                  pl.BlockSpec((tm, d), lambda t: (jnp.maximum(t - n_ctx_tiles, 0), 0)),
                  pl.BlockSpec((1,) + mod.shape[1:], lambda t: (mod_row(t), 0, 0)),
                  full(g1), full(w_in_b), full(hsum), full(gq_t), full(gk_t)],
        out_specs=(tok(df), tok(da), tok(da), tok(da), kv_spec, kv_spec),
        compiler_params=_cparams(1),
        name="proj",
    )(xp, xs, mod, g1, w_in_b, hsum, gq_t, gk_t)


def _softmax_pv(q2, key_blocks, val_blocks, bias_blocks, scale):
    lane = lax.broadcasted_iota(jnp.int32, (1, LANES), 1)
    out = None
    for half in range(2):
        hmask = (lane < HEAD_DIM) if half == 0 else (lane >= HEAD_DIM)
        qh = jnp.where(hmask, q2, jnp.zeros_like(q2))
        scores = []
        for kb, bb in zip(key_blocks, bias_blocks):
            s = lax.dot_general(qh, kb, (((1,), (1,)), ((), ())), preferred_element_type=F32) * scale
            if bb is not None:
                s = s + bb[half]
            scores.append(s)
        m = functools.reduce(jnp.maximum, [jnp.max(s, axis=-1, keepdims=True) for s in scores])
        ps = [jnp.exp(s - m) for s in scores]
        denom = functools.reduce(lambda a, b: a + b, [jnp.sum(p, axis=-1, keepdims=True) for p in ps])
        o = functools.reduce(lambda a, b: a + b,
                             [jnp.dot(p.astype(BF16), vb, preferred_element_type=F32)
                              for p, vb in zip(ps, val_blocks)])
        o = o / denom
        out = o if out is None else jnp.where(hmask, o, out)
    return out


def _attn_ctx_kernel(q_ref, k_ref, v_ref, o_ref):
    scale = HEAD_DIM ** -0.5
    for p in range(q_ref.shape[1] // LANES):
        sl = slice(p * LANES, (p + 1) * LANES)
        o = _softmax_pv(q_ref[:, sl], [k_ref[:, sl]], [v_ref[:, sl]], [None], scale)
        o_ref[:, sl] = o.astype(BF16)


def _attn_ctx(q, k, v, batch, seq):
    da = q.shape[1]
    spec = pl.BlockSpec((seq, da), lambda b: (b, 0))
    return pl.pallas_call(
        _attn_ctx_kernel,
        out_shape=jax.ShapeDtypeStruct((batch * seq, da), BF16),
        grid=(batch,),
        in_specs=[spec, spec, spec],
        out_specs=spec,
        compiler_params=_cparams(1),
        name="attn_ctx",
    )(q, k, v)


def _attn_lat_kernel(q_ref, k0_ref, k1_ref, k2_ref, v0_ref, v1_ref, v2_ref, kc_ref, vc_ref, bias_ref, o_ref):
    scale = HEAD_DIM ** -0.5
    tk = k0_ref.shape[0]
    for p in range(q_ref.shape[1] // LANES):
        sl = slice(p * LANES, (p + 1) * LANES)
        keys = [k0_ref[:, sl], k1_ref[:, sl], k2_ref[:, sl], kc_ref[0, :, sl]]
        vals = [v0_ref[:, sl], v1_ref[:, sl], v2_ref[:, sl], vc_ref[0, :, sl]]
        biases = [[bias_ref[0, 2 * p + half, :, d * tk:(d + 1) * tk] for half in range(2)] for d in range(3)]
        o = _softmax_pv(q_ref[:, sl], keys, vals, biases + [None], scale)
        o_ref[:, sl] = o.astype(BF16)


def _attn_lat(q, k, v, kc, vc, bias, n_ctx, dec_batch, rows):
    da = q.shape[1]
    tq = ATT_TILE
    rows_per_tile = tq // GRID_W
    tiles = rows // rows_per_tile
    base = n_ctx // tq
    n_lat = dec_batch * rows * GRID_W
    max_start = tiles - 3

    def qmap(j, b):
        return (base + b * tiles + j, 0)

    def kmap(d):
        return lambda j, b: (base + b * tiles + jnp.clip(j - 1, 0, max_start) + d, 0)

    def bmap(j, b):
        return (jnp.where(j == 0, 0, jnp.where(j == tiles - 1, 2, 1)), 0, 0, 0)

    blk = lambda m: pl.BlockSpec((tq, da), m)
    cspec = pl.BlockSpec((1,) + kc.shape[1:], lambda j, b: (b, 0, 0))
    return pl.pallas_call(
        _attn_lat_kernel,
        out_shape=jax.ShapeDtypeStruct((n_lat, da), BF16),
        grid=(tiles, dec_batch),
        in_specs=[blk(qmap)] + [blk(kmap(d)) for d in range(3)] + [blk(kmap(d)) for d in range(3)]
        + [cspec, cspec, pl.BlockSpec((1,) + bias.shape[1:], bmap)],
        out_specs=pl.BlockSpec((tq, da), lambda j, b: (b * tiles + j, 0)),
        compiler_params=_cparams(2),
        name="attn_lat",
    )(q, k, k, k, v, v, v, kc, vc, bias)


def _window_tables(rows):
    rpt = ATT_TILE // GRID_W
    krows = 3 * rpt
    tiles = rows // rpt
    kh = min(WIN_H_MAX, rows)
    rpb_h = 2 * WIN_H_MAX - 1
    rpb_w = 2 * WIN_W - 1
    er = np.zeros((3, rpt, krows, rpb_h), np.float32)
    for var, j in enumerate((0, 1, tiles - 1)):
        r0 = j * rpt
        k0 = int(np.clip(j - 1, 0, tiles - 3)) * rpt
        for qi in range(rpt):
            qrow = r0 + qi
            rs = int(np.clip(qrow - kh // 2, 0, rows - kh))
            for ki in range(krows):
                krow = k0 + ki
                if rs <= krow < rs + kh:
                    er[var, qi, ki, krow - qrow + WIN_H_MAX - 1] = 1.0
    ec = np.zeros((GRID_W, GRID_W, rpb_w), np.float32)
    for qc in range(GRID_W):
        cs = int(np.clip(qc - WIN_W // 2, 0, GRID_W - WIN_W))
        for kc in range(cs, cs + WIN_W):
            ec[qc, kc, int(np.clip(kc - qc, -(WIN_W - 1), WIN_W - 1)) + WIN_W - 1] = 1.0
    valid = np.einsum("vqka,xyb->vqxky", er, ec) > 0.5
    return er, ec, valid


def _fourier_kernel(u_ref, w1_ref, ct_ref, st_ref, o_ref, p_scr, q_scr):
    @pl.when(pl.program_id(1) == 0)
    def _():
        pq = jnp.dot(u_ref[...], w1_ref[...], preferred_element_type=F32)
        df = p_scr.shape[1]
        p_scr[...] = pq[:, :df].astype(BF16)
        q_scr[...] = pq[:, df:].astype(BF16)

    o = (jnp.dot(ct_ref[...], p_scr[...], preferred_element_type=F32)
         - jnp.dot(st_ref[...], q_scr[...], preferred_element_type=F32))
    o_ref[...] = o.astype(BF16)


def _fourier(u, w1, ct, st, batch, t_len, first_block, row_tile):
    df = u.shape[1]
    steps = t_len // row_tile
    return pl.pallas_call(
        _fourier_kernel,
        out_shape=jax.ShapeDtypeStruct((batch * t_len, df), BF16),
        grid=(batch, steps),
        in_specs=[pl.BlockSpec((t_len, df), lambda b, i: (first_block + b, 0)),
                  pl.BlockSpec(w1.shape, lambda b, i: (0, 0)),
                  pl.BlockSpec((row_tile, t_len), lambda b, i: (i, 0)),
                  pl.BlockSpec((row_tile, t_len), lambda b, i: (i, 0))],
        out_specs=pl.BlockSpec((row_tile, df), lambda b, i: (b * steps + i, 0)),
        scratch_shapes=[pltpu.VMEM((t_len, df), BF16), pltpu.VMEM((t_len, df), BF16)],
        compiler_params=_cparams(2),
        name=f"fourier_{t_len}",
    )(u, w1, ct, st)


def _dft_tables(t_len):
    c = FGROUP_DIM
    jk = np.outer(np.arange(c), np.arange(c)) % c
    ang = 2.0 * np.pi * jk / c
    eye = np.eye(N_FGROUPS)
    w1 = np.concatenate([np.kron(eye, np.cos(ang)), np.kron(eye, np.sin(ang))], axis=1) / np.sqrt(c)
    tt = np.outer(np.arange(t_len), np.arange(t_len)) % t_len
    angt = 2.0 * np.pi * tt / t_len
    return (w1.astype(np.float32), (np.cos(angt) / np.sqrt(t_len)).astype(np.float32),
            (np.sin(angt) / np.sqrt(t_len)).astype(np.float32))


def _merge_kernel(n_ctx_tiles, xp_ref, xs_ref, mod_ref, g1_ref, g2_ref, afc_ref, afl_ref, aac_ref, aal_ref,
                  wf_ref, wa_ref, wg_ref, bg_ref, wo_ref, wr_ref, br_ref,
                  x1_ref, h2_ref, idx_ref, wts_ref, rank_ref, cnt_ref, carry):
    t = pl.program_id(0)
    is_ctx = t < n_ctx_tiles

    @pl.when(t == 0)
    def _():
        carry[...] = jnp.zeros_like(carry)

    x = jnp.where(is_ctx, xp_ref[...], xs_ref[...])
    a_four = jnp.where(is_ctx, afc_ref[...], afl_ref[...])
    a_attn = jnp.where(is_ctx, aac_ref[...], aal_ref[...])
    sh1 = mod_ref[0, 0:1, :]
    sc1 = mod_ref[0, 1:2, :]
    gt1 = mod_ref[0, 2:3, :]
    sh2 = mod_ref[0, 3:4, :]
    sc2 = mod_ref[0, 4:5, :]
    d = x.shape[1]
    hb = (_rms(x) * g1_ref[...] * (1.0 + sc1) + sh1).astype(BF16)
    gates = jax.nn.sigmoid(jnp.dot(hb, wg_ref[...], preferred_element_type=F32) + bg_ref[...])
    fa = jnp.dot(a_four, wf_ref[...], preferred_element_type=F32)
    fb = jnp.dot(a_attn, wa_ref[...], preferred_element_type=F32)
    mix = gates[:, :d] * fa + gates[:, d:] * fb
    x1 = x + gt1 * jnp.dot(mix.astype(BF16), wo_ref[...], preferred_element_type=F32)
    x1_ref[...] = x1
    h2 = _rms(x1) * g2_ref[...] * (1.0 + sc2) + sh2
    h2_ref[...] = h2.astype(BF16)

    logits = jnp.dot(h2, wr_ref[...], precision=lax.Precision.HIGHEST, preferred_element_type=F32) + br_ref[...]
    tm, ne = logits.shape
    col = lax.broadcasted_iota(jnp.int32, (tm, ne), 1)
    colk = lax.broadcasted_iota(jnp.int32, (tm, TOP_K), 1)
    lg = logits
    vals, idxs = [], []
    for _ in range(TOP_K):
        m = jnp.max(lg, axis=-1, keepdims=True)
        am = jnp.min(jnp.where(lg == m, col, ne), axis=-1, keepdims=True)
        vals.append(m)
        idxs.append(am)
        lg = jnp.where(col == am, -jnp.inf, lg)
    es = [jnp.exp(v - vals[0]) for v in vals]
    den = functools.reduce(lambda a, b: a + b, es)
    onehot = functools.reduce(lambda a, b: a + b, [(col == am).astype(F32) for am in idxs])
    r_i = lax.broadcasted_iota(jnp.int32, (tm, tm), 0)
    c_i = lax.broadcasted_iota(jnp.int32, (tm, tm), 1)
    lower = (r_i > c_i).astype(BF16)
    before = jnp.dot(lower, onehot.astype(BF16), preferred_element_type=F32) + carry[...]
    idx_o = jnp.zeros((tm, TOP_K), jnp.int32)
    wts_o = jnp.zeros((tm, TOP_K), F32)
    rank_o = jnp.zeros((tm, TOP_K), jnp.int32)
    for kk in range(TOP_K):
        rk = jnp.sum(jnp.where(col == idxs[kk], before, 0.0), axis=-1, keepdims=True).astype(jnp.int32)
        idx_o = jnp.where(colk == kk, idxs[kk], idx_o)
        wts_o = jnp.where(colk == kk, es[kk] / den, wts_o)
        rank_o = jnp.where(colk == kk, rk, rank_o)
    idx_ref[...] = idx_o
    wts_ref[...] = wts_o
    rank_ref[...] = rank_o
    carry[...] = carry[...] + jnp.sum(onehot, axis=0, keepdims=True)
    cnt_ref[...] = carry[...]


def _merge(xp, xs, mod, g1, g2, af_c, af_l, aa_c, aa_l, wf, wa, wg, bg, wo, wr, br, tiles_per_lat_batch):
    n_ctx, d = xp.shape
    n = n_ctx + xs.shape[0]
    tm = TOK_TILE
    n_ctx_tiles = n_ctx // tm
    last = n_ctx_tiles - 1
    ne = wr.shape[1]
    da = aa_c.shape[1]
    df = af_c.shape[1]

    def mod_row(t):
        return jnp.where(t < n_ctx_tiles, 0, 1 + (t - n_ctx_tiles) // tiles_per_lat_batch)

    cmap = lambda t: (jnp.minimum(t, last), 0)
    lmap = lambda t: (jnp.maximum(t - n_ctx_tiles, 0), 0)
    full = lambda a: pl.BlockSpec(a.shape, lambda t: (0,) * a.ndim)
    tok = lambda w: pl.BlockSpec((tm, w), lambda t: (t, 0))
    return pl.pallas_call(
        functools.partial(_merge_kernel, n_ctx_tiles),
        out_shape=(jax.ShapeDtypeStruct((n, d), F32), jax.ShapeDtypeStruct((n, d), BF16),
                   jax.ShapeDtypeStruct((n, TOP_K), jnp.int32), jax.ShapeDtypeStruct((n, TOP_K), F32),
                   jax.ShapeDtypeStruct((n, TOP_K), jnp.int32), jax.ShapeDtypeStruct((1, ne), F32)),
        grid=(n // tm,),
        in_specs=[pl.BlockSpec((tm, d), cmap), pl.BlockSpec((tm, d), lmap),
                  pl.BlockSpec((1,) + mod.shape[1:], lambda t: (mod_row(t), 0, 0)),
                  full(g1), full(g2),
                  pl.BlockSpec((tm, df), cmap), pl.BlockSpec((tm, df), lmap),
                  pl.BlockSpec((tm, da), cmap), pl.BlockSpec((tm, da), lmap),
                  full(wf), full(wa), full(wg), full(bg), full(wo), full(wr), full(br)],
        out_specs=(tok(d), tok(d), tok(TOP_K), tok(TOP_K), tok(TOP_K),
                   pl.BlockSpec((1, ne), lambda t: (0, 0))),
        scratch_shapes=[pltpu.VMEM((1, ne), F32)],
        compiler_params=_cparams(1),
        name="merge",
    )(xp, xs, mod, g1, g2, af_c, af_l, aa_c, aa_l, wf, wa, wg, bg, wo, wr, br)


def _moe_kernel(te_ref, nt_ref, x_ref, w1_ref, b1_ref, w2_ref, b2_ref, y_ref, w1b, w2b):
    i = pl.program_id(0)
    e = te_ref[i]
    prev = te_ref[jnp.maximum(i - 1, 0)]

    @pl.when((i == 0) | (e != prev))
    def _():
        w1b[...] = w1_ref[0].astype(BF16)
        w2b[...] = w2_ref[0].astype(BF16)

    @pl.when(i < nt_ref[0])
    def _():
        dff = w2b.shape[0]
        gu = jnp.dot(x_ref[...], w1b[...], preferred_element_type=F32) + b1_ref[0]
        gate = jnp.minimum(gu[:, :dff], SWIGLU_LIMIT)
        up = jnp.clip(gu[:, dff:], -SWIGLU_LIMIT, SWIGLU_LIMIT)
        glu = gate * jax.nn.sigmoid(SWIGLU_ALPHA * gate)
        act = ((up + 1.0) * glu).astype(BF16)
        y = jnp.dot(act, w2b[...], preferred_element_type=F32) + b2_ref[0]
        y_ref[...] = y.astype(BF16)


def _moe(tile_expert, n_tiles, xs_sorted, w1, b1, w2, b2):
    p, d = xs_sorted.shape
    ne, _, dff2 = w1.shape
    dff = w2.shape[1]
    tm = MOE_TILE
    max_tiles = p // tm

    def row_map(i, te, nt):
        return (jnp.minimum(i, nt[0] - 1), 0)

    def w_map(i, te, nt):
        return (te[i], 0, 0)

    grid_spec = pltpu.PrefetchScalarGridSpec(
        num_scalar_prefetch=2,
        grid=(max_tiles,),
        in_specs=[pl.BlockSpec((tm, d), row_map),
                  pl.BlockSpec((1, d, dff2), w_map),
                  pl.BlockSpec((1, 1, dff2), w_map),
                  pl.BlockSpec((1, dff, d), w_map),
                  pl.BlockSpec((1, 1, d), w_map)],
        out_specs=pl.BlockSpec((tm, d), row_map),
        scratch_shapes=[pltpu.VMEM((d, dff2), BF16), pltpu.VMEM((dff, d), BF16)],
    )
    return pl.pallas_call(
        _moe_kernel,
        out_shape=jax.ShapeDtypeStruct((p, d), BF16),
        grid_spec=grid_spec,
        compiler_params=_cparams(1),
        name="moe",
    )(tile_expert, n_tiles, xs_sorted, w1, b1.reshape(ne, 1, dff2), w2, b2.reshape(ne, 1, d))


def _combine_kernel(n_ctx_tiles, x1_ref, yg_ref, wts_ref, mod_ref, op_ref, os_ref):
    t = pl.program_id(0)
    gt2 = mod_ref[0, 5:6, :]
    w = wts_ref[...]
    acc = w[:, 0:1] * yg_ref[0].astype(F32)
    for kk in range(1, TOP_K):
        acc = acc + w[:, kk:kk + 1] * yg_ref[kk].astype(F32)
    out = x1_ref[...] + gt2 * acc

    @pl.when(t < n_ctx_tiles)
    def _():
        op_ref[...] = out

    @pl.when(t >= n_ctx_tiles)
    def _():
        os_ref[...] = out


def _combine(x1, yg, wts, mod, n_ctx, tiles_per_lat_batch):
    n, d = x1.shape
    tm = TOK_TILE
    n_ctx_tiles = n_ctx // tm
    last = n_ctx_tiles - 1

    def mod_row(t):
        return jnp.where(t < n_ctx_tiles, 0, 1 + (t - n_ctx_tiles) // tiles_per_lat_batch)

    return pl.pallas_call(
        functools.partial(_combine_kernel, n_ctx_tiles),
        out_shape=(jax.ShapeDtypeStruct((n_ctx, d), F32), jax.ShapeDtypeStruct((n - n_ctx, d), F32)),
        grid=(n // tm,),
        in_specs=[pl.BlockSpec((tm, d), lambda t: (t, 0)),
                  pl.BlockSpec((TOP_K, tm, d), lambda t: (0, t, 0)),
                  pl.BlockSpec((tm, TOP_K), lambda t: (t, 0)),
                  pl.BlockSpec((1,) + mod.shape[1:], lambda t: (mod_row(t), 0, 0))],
        out_specs=(pl.BlockSpec((tm, d), lambda t: (jnp.minimum(t, last), 0)),
                   pl.BlockSpec((tm, d), lambda t: (jnp.maximum(t - n_ctx_tiles, 0), 0))),
        compiler_params=_cparams(1),
        name="combine",
    )(x1, yg, wts, mod)


def kernel(x_prompt, x_sample, cache_k, cache_v, c, c_ctx, g_norm1, w_ada, b_ada, w_in, g_q, g_k, rpb,
           w_fmap, w_amap, w_gate, b_gate, w_out, g_norm2, w_router, b_router, w1, b1, w2, b2):
    batch, seq, d = x_prompt.shape
    dec_batch, dec_seq, _ = x_sample.shape
    assert w_ada.shape[0] == 1, "single-layer trunk"
    rows = dec_seq // GRID_W
    da = N_HEADS * HEAD_DIM
    n_ctx = batch * seq
    n_lat = dec_batch * dec_seq
    n = n_ctx + n_lat
    ne = w_router.shape[2]
    assert TOK_TILE % seq == 0 and n_ctx % TOK_TILE == 0 and dec_seq % TOK_TILE == 0
    assert rows % (ATT_TILE // GRID_W) == 0 and rows >= 3 * (ATT_TILE // GRID_W) and rows >= WIN_H_MAX
    assert seq == ATT_TILE and cache_k.shape[3] == ATT_TILE
    tiles_per_lat_batch = dec_seq // TOK_TILE

    n_mod_rows = 8
    cvecs = jnp.zeros((n_mod_rows, d), F32).at[0].set(c_ctx).at[1:1 + dec_batch].set(c)
    mod = _ada(cvecs, w_ada[0], b_ada[0]).reshape(n_mod_rows, 6, d)

    xp = x_prompt.reshape(n_ctx, d)
    xs = x_sample.reshape(n_lat, d)
    g1 = g_norm1[0].reshape(1, d)
    g2 = g_norm2[0].reshape(1, d)
    hsum = jnp.asarray(np.kron(np.eye(N_HEADS), np.full((HEAD_DIM, HEAD_DIM), 1.0 / HEAD_DIM)), BF16)
    gq_t = jnp.tile(g_q[0], N_HEADS).reshape(1, da)
    gk_t = jnp.tile(g_k[0], N_HEADS).reshape(1, da)

    u, q, k, v, new_k, new_v = _proj(xp, xs, mod, g1, w_in[0].astype(BF16), hsum, gq_t, gk_t,
                                      batch, seq, tiles_per_lat_batch)

    aa_c = _attn_ctx(q, k, v, batch, seq)
    er, ec, valid = _window_tables(rows)
    bias = jnp.einsum("vqka,hab,xyb->vhqxky", er, rpb[0], ec, precision=lax.Precision.HIGHEST)
    bias = jnp.where(valid[:, None], bias, NEG_BIG).reshape(3, N_HEADS, ATT_TILE, 3 * ATT_TILE)
    past = cache_k.shape[3]
    kc = cache_k[:, 0].transpose(0, 2, 1, 3).reshape(dec_batch, past, da).astype(BF16)
    vc = cache_v[:, 0].transpose(0, 2, 1, 3).reshape(dec_batch, past, da).astype(BF16)
    aa_l = _attn_lat(q, k, v, kc, vc, bias, n_ctx, dec_batch, rows)

    w1c, ct_c, st_c = _dft_tables(seq)
    _, ct_l, st_l = _dft_tables(dec_seq)
    w1c = jnp.asarray(w1c).astype(BF16)
    af_c = _fourier(u, w1c, jnp.asarray(ct_c).astype(BF16), jnp.asarray(st_c).astype(BF16),
                    batch, seq, 0, seq)
    af_l = _fourier(u, w1c, jnp.asarray(ct_l).astype(BF16), jnp.asarray(st_l).astype(BF16),
                    dec_batch, dec_seq, n_ctx // dec_seq, TOK_TILE)

    x1, h2, idx, wts, rank, counts = _merge(
        xp, xs, mod, g1, g2, af_c, af_l, aa_c, aa_l,
        w_fmap[0].astype(BF16), w_amap[0].astype(BF16), w_gate[0].astype(BF16), b_gate[0].reshape(1, -1),
        w_out[0].astype(BF16), w_router[0], b_router[0].reshape(1, ne), tiles_per_lat_batch)

    tm = MOE_TILE
    max_tiles = (n * TOP_K) // tm + ne
    cnt = counts[0].astype(jnp.int32)
    tiles_e = (cnt + tm - 1) // tm
    tile_end = jnp.cumsum(tiles_e)
    pad_off = (tile_end - tiles_e) * tm
    n_tiles = tile_end[-1:]
    tile_ids = jnp.arange(max_tiles, dtype=jnp.int32)
    tile_expert = jnp.sum((tile_ids[:, None] >= tile_end[None, :]).astype(jnp.int32), axis=1)
    last_e = jnp.sum((n_tiles - 1 >= tile_end).astype(jnp.int32))
    tile_expert = jnp.minimum(tile_expert, last_e).astype(jnp.int32)
    pos = jnp.sum(jnp.where(idx[:, :, None] == jnp.arange(ne)[None, None, :], pad_off[None, None, :], 0),
                  axis=-1) + rank

    p_rows = max_tiles * tm
    tok_of_slot = jnp.zeros((p_rows,), jnp.int32).at[pos.reshape(-1)].set(
        jnp.arange(n * TOP_K, dtype=jnp.int32) // TOP_K)
    xs_sorted = jnp.take(h2, tok_of_slot, axis=0)
    y_sorted = _moe(tile_expert, n_tiles.astype(jnp.int32), xs_sorted, w1[0], b1[0], w2[0], b2[0])
    yg = jnp.take(y_sorted, pos.T, axis=0)

    y_p, y_s = _combine(x1, yg, wts, mod, n_ctx, tiles_per_lat_batch)
    return (y_p.reshape(batch, seq, d), y_s.reshape(dec_batch, dec_seq, d), new_k, new_v)
```

```python
import functools

import numpy as np
import jax
import jax.numpy as jnp
from jax import lax
from jax.experimental import pallas as pl
from jax.experimental.pallas import tpu as pltpu
from jax.experimental.pallas import tpu_sc as plsc

F32 = jnp.float32
BF16 = jnp.bfloat16

GRID_W = 64
N_HEADS = 8
HEAD_DIM = 64
N_FGROUPS = 4
FGROUP_DIM = 128
WIN_H_MAX = 8
WIN_W = 16
TOP_K = 4
SWIGLU_LIMIT = 7.0
SWIGLU_ALPHA = 1.702
EPS = 1e-6

LANES = 128
TOK_TILE = 512
ATT_TILE = 256
MOE_TILE = 512
VMEM_LIMIT = 56 * 1024 * 1024
NEG_BIG = -1e30

SC_CORES = 2
SC_SUBCORES = 16
SC_CHUNK = 32


def _cparams(n_axes, vmem=VMEM_LIMIT):
    return pltpu.CompilerParams(dimension_semantics=("arbitrary",) * n_axes, vmem_limit_bytes=vmem)


def _rms(x):
    return x * lax.rsqrt(jnp.mean(x * x, axis=-1, keepdims=True) + EPS)


def _ada_kernel(c_ref, w_ref, b_ref, o_ref):
    cv = c_ref[...]
    s = cv * jax.nn.sigmoid(cv)
    o_ref[...] = jnp.dot(s, w_ref[...], precision=lax.Precision.HIGHEST,
                         preferred_element_type=F32) + b_ref[...]


def _ada(cvecs, w_ada, b_ada):
    rows, d = cvecs.shape
    n = w_ada.shape[1]
    blk = 1024
    return pl.pallas_call(
        _ada_kernel,
        out_shape=jax.ShapeDtypeStruct((rows, n), F32),
        grid=(n // blk,),
        in_specs=[pl.BlockSpec((rows, d), lambda j: (0, 0)),
                  pl.BlockSpec((d, blk), lambda j: (0, j)),
                  pl.BlockSpec((1, blk), lambda j: (0, j))],
        out_specs=pl.BlockSpec((rows, blk), lambda j: (0, j)),
        compiler_params=_cparams(1),
        name="ada",
    )(cvecs, w_ada, b_ada.reshape(1, n))


def _proj_kernel(n_ctx_tiles, xp_ref, xs_ref, mod_ref, g1_ref, win_ref, hsum_ref, gq_ref, gk_ref,
                 u_ref, q_ref, k_ref, v_ref, nk_ref, nv_ref):
    t = pl.program_id(0)
    is_ctx = t < n_ctx_tiles
    x = jnp.where(is_ctx, xp_ref[...], xs_ref[...])
    sh1 = mod_ref[0, 0:1, :]
    sc1 = mod_ref[0, 1:2, :]
    h = _rms(x) * g1_ref[...] * (1.0 + sc1) + sh1
    proj = jnp.dot(h.astype(BF16), win_ref[...], preferred_element_type=F32)
    da = q_ref.shape[1]
    df = u_ref.shape[1]
    u_ref[...] = proj[:, :df].astype(BF16)
    q = proj[:, df:df + da]
    k = proj[:, df + da:df + 2 * da]
    v = proj[:, df + 2 * da:]
    msq = jnp.dot((q * q).astype(BF16), hsum_ref[...], preferred_element_type=F32)
    msk = jnp.dot((k * k).astype(BF16), hsum_ref[...], preferred_element_type=F32)
    qn = q * lax.rsqrt(msq + EPS) * gq_ref[...]
    kn = k * lax.rsqrt(msk + EPS) * gk_ref[...]
    q_ref[...] = qn.astype(BF16)
    k_ref[...] = kn.astype(BF16)
    v_ref[...] = v.astype(BF16)

    @pl.when(is_ctx)
    def _():
        nb, _, nh, s, dh = nk_ref.shape
        for b in range(nb):
            for hd in range(nh):
                nk_ref[b, 0, hd] = kn[b * s:(b + 1) * s, hd * dh:(hd + 1) * dh]
                nv_ref[b, 0, hd] = v[b * s:(b + 1) * s, hd * dh:(hd + 1) * dh]


def _proj(xp, xs, mod, g1, w_in_b, hsum, gq_t, gk_t, batch, seq, tiles_per_lat_batch):
    n_ctx, d = xp.shape
    n_lat = xs.shape[0]
    tm = TOK_TILE
    n_ctx_tiles = n_ctx // tm
    n_tiles = (n_ctx + n_lat) // tm
    n = n_ctx + n_lat
    d_in = w_in_b.shape[1]
    da = N_HEADS * HEAD_DIM
    df = d_in - 3 * da
    bpt = tm // seq
    last = n_ctx_tiles - 1

    def mod_row(t):
        return jnp.where(t < n_ctx_tiles, 0, 1 + (t - n_ctx_tiles) // tiles_per_lat_batch)

    tok = lambda w: pl.BlockSpec((tm, w), lambda t: (t, 0))
    full = lambda a: pl.BlockSpec(a.shape, lambda t: (0,) * a.ndim)
    kv_spec = pl.BlockSpec((bpt, 1, N_HEADS, seq, HEAD_DIM), lambda t: (jnp.minimum(t, last), 0, 0, 0, 0))
    kv_shape = jax.ShapeDtypeStruct((batch, 1, N_HEADS, seq, HEAD_DIM), F32)
    return pl.pallas_call(
        functools.partial(_proj_kernel, n_ctx_tiles),
        out_shape=(jax.ShapeDtypeStruct((n, df), BF16),) + (jax.ShapeDtypeStruct((n, da), BF16),) * 3
        + (kv_shape, kv_shape),
        grid=(n_tiles,),
        in_specs=[pl.BlockSpec((tm, d), lambda t: (jnp.minimum(t, last), 0)),
                  pl.BlockSpec((tm, d), lambda t: (jnp.maximum(t - n_ctx_tiles, 0), 0)),
                  pl.BlockSpec((1,) + mod.shape[1:], lambda t: (mod_row(t), 0, 0)),
                  full(g1), full(w_in_b), full(hsum), full(gq_t), full(gk_t)],
        out_specs=(tok(df), tok(da), tok(da), tok(da), kv_spec, kv_spec),
        compiler_params=_cparams(1),
        name="proj",
    )(xp, xs, mod, g1, w_in_b, hsum, gq_t, gk_t)


def _softmax_pv(q2, key_blocks, val_blocks, bias_blocks, scale):
    lane = lax.broadcasted_iota(jnp.int32, (1, LANES), 1)
    out = None
    for half in range(2):
        hmask = (lane < HEAD_DIM) if half == 0 else (lane >= HEAD_DIM)
        qh = jnp.where(hmask, q2, jnp.zeros_like(q2))
        scores = []
        for kb, bb in zip(key_blocks, bias_blocks):
            s = lax.dot_general(qh, kb, (((1,), (1,)), ((), ())), preferred_element_type=F32) * scale
            if bb is not None:
                s = s + bb[half]
            scores.append(s)
        m = functools.reduce(jnp.maximum, [jnp.max(s, axis=-1, keepdims=True) for s in scores])
        ps = [jnp.exp(s - m) for s in scores]
        denom = functools.reduce(lambda a, b: a + b, [jnp.sum(p, axis=-1, keepdims=True) for p in ps])
        o = functools.reduce(lambda a, b: a + b,
                             [jnp.dot(p.astype(BF16), vb, preferred_element_type=F32)
                              for p, vb in zip(ps, val_blocks)])
        o = o / denom
        out = o if out is None else jnp.where(hmask, o, out)
    return out


def _attn_ctx_kernel(q_ref, k_ref, v_ref, o_ref):
    scale = HEAD_DIM ** -0.5
    for p in range(q_ref.shape[1] // LANES):
        sl = slice(p * LANES, (p + 1) * LANES)
        o = _softmax_pv(q_ref[:, sl], [k_ref[:, sl]], [v_ref[:, sl]], [None], scale)
        o_ref[:, sl] = o.astype(BF16)


def _attn_ctx(q, k, v, batch, seq):
    da = q.shape[1]
    spec = pl.BlockSpec((seq, da), lambda b: (b, 0))
    return pl.pallas_call(
        _attn_ctx_kernel,
        out_shape=jax.ShapeDtypeStruct((batch * seq, da), BF16),
        grid=(batch,),
        in_specs=[spec, spec, spec],
        out_specs=spec,
        compiler_params=_cparams(1),
        name="attn_ctx",
    )(q, k, v)


def _attn_lat_kernel(q_ref, k0_ref, k1_ref, k2_ref, v0_ref, v1_ref, v2_ref, kc_ref, vc_ref, bias_ref, o_ref):
    scale = HEAD_DIM ** -0.5
    tk = k0_ref.shape[0]
    for p in range(q_ref.shape[1] // LANES):
        sl = slice(p * LANES, (p + 1) * LANES)
        keys = [k0_ref[:, sl], k1_ref[:, sl], k2_ref[:, sl], kc_ref[0, :, sl]]
        vals = [v0_ref[:, sl], v1_ref[:, sl], v2_ref[:, sl], vc_ref[0, :, sl]]
        biases = [[bias_ref[0, 2 * p + half, :, d * tk:(d + 1) * tk] for half in range(2)] for d in range(3)]
        o = _softmax_pv(q_ref[:, sl], keys, vals, biases + [None], scale)
        o_ref[:, sl] = o.astype(BF16)


def _attn_lat(q, k, v, kc, vc, bias, n_ctx, dec_batch, rows):
    da = q.shape[1]
    tq = ATT_TILE
    rows_per_tile = tq // GRID_W
    tiles = rows // rows_per_tile
    base = n_ctx // tq
    n_lat = dec_batch * rows * GRID_W
    max_start = tiles - 3

    def qmap(j, b):
        return (base + b * tiles + j, 0)

    def kmap(d):
        return lambda j, b: (base + b * tiles + jnp.clip(j - 1, 0, max_start) + d, 0)

    def bmap(j, b):
        return (jnp.where(j == 0, 0, jnp.where(j == tiles - 1, 2, 1)), 0, 0, 0)

    blk = lambda m: pl.BlockSpec((tq, da), m)
    cspec = pl.BlockSpec((1,) + kc.shape[1:], lambda j, b: (b, 0, 0))
    return pl.pallas_call(
        _attn_lat_kernel,
        out_shape=jax.ShapeDtypeStruct((n_lat, da), BF16),
        grid=(tiles, dec_batch),
        in_specs=[blk(qmap)] + [blk(kmap(d)) for d in range(3)] + [blk(kmap(d)) for d in range(3)]
        + [cspec, cspec, pl.BlockSpec((1,) + bias.shape[1:], bmap)],
        out_specs=pl.BlockSpec((tq, da), lambda j, b: (b * tiles + j, 0)),
        compiler_params=_cparams(2),
        name="attn_lat",
    )(q, k, k, k, v, v, v, kc, vc, bias)


def _window_tables(rows):
    rpt = ATT_TILE // GRID_W
    krows = 3 * rpt
    tiles = rows // rpt
    kh = min(WIN_H_MAX, rows)
    rpb_h = 2 * WIN_H_MAX - 1
    rpb_w = 2 * WIN_W - 1
    er = np.zeros((3, rpt, krows, rpb_h), np.float32)
    for var, j in enumerate((0, 1, tiles - 1)):
        r0 = j * rpt
        k0 = int(np.clip(j - 1, 0, tiles - 3)) * rpt
        for qi in range(rpt):
            qrow = r0 + qi
            rs = int(np.clip(qrow - kh // 2, 0, rows - kh))
            for ki in range(krows):
                krow = k0 + ki
                if rs <= krow < rs + kh:
                    er[var, qi, ki, krow - qrow + WIN_H_MAX - 1] = 1.0
    ec = np.zeros((GRID_W, GRID_W, rpb_w), np.float32)
    for qc in range(GRID_W):
        cs = int(np.clip(qc - WIN_W // 2, 0, GRID_W - WIN_W))
        for kc in range(cs, cs + WIN_W):
            ec[qc, kc, int(np.clip(kc - qc, -(WIN_W - 1), WIN_W - 1)) + WIN_W - 1] = 1.0
    valid = np.einsum("vqka,xyb->vqxky", er, ec) > 0.5
    return er, ec, valid


def _fourier_kernel(u_ref, w1_ref, ct_ref, st_ref, o_ref, p_scr, q_scr):
    @pl.when(pl.program_id(1) == 0)
    def _():
        pq = jnp.dot(u_ref[...], w1_ref[...], preferred_element_type=F32)
        df = p_scr.shape[1]
        p_scr[...] = pq[:, :df].astype(BF16)
        q_scr[...] = pq[:, df:].astype(BF16)

    o = (jnp.dot(ct_ref[...], p_scr[...], preferred_element_type=F32)
         - jnp.dot(st_ref[...], q_scr[...], preferred_element_type=F32))
    o_ref[...] = o.astype(BF16)


def _fourier(u, w1, ct, st, batch, t_len, first_block, row_tile):
    df = u.shape[1]
    steps = t_len // row_tile
    return pl.pallas_call(
        _fourier_kernel,
        out_shape=jax.ShapeDtypeStruct((batch * t_len, df), BF16),
        grid=(batch, steps),
        in_specs=[pl.BlockSpec((t_len, df), lambda b, i: (first_block + b, 0)),
                  pl.BlockSpec(w1.shape, lambda b, i: (0, 0)),
                  pl.BlockSpec((row_tile, t_len), lambda b, i: (i, 0)),
                  pl.BlockSpec((row_tile, t_len), lambda b, i: (i, 0))],
        out_specs=pl.BlockSpec((row_tile, df), lambda b, i: (b * steps + i, 0)),
        scratch_shapes=[pltpu.VMEM((t_len, df), BF16), pltpu.VMEM((t_len, df), BF16)],
        compiler_params=_cparams(2),
        name=f"fourier_{t_len}",
    )(u, w1, ct, st)


def _dft_tables(t_len):
    c = FGROUP_DIM
    jk = np.outer(np.arange(c), np.arange(c)) % c
    ang = 2.0 * np.pi * jk / c
    eye = np.eye(N_FGROUPS)
    w1 = np.concatenate([np.kron(eye, np.cos(ang)), np.kron(eye, np.sin(ang))], axis=1) / np.sqrt(c)
    tt = np.outer(np.arange(t_len), np.arange(t_len)) % t_len
    angt = 2.0 * np.pi * tt / t_len
    return (w1.astype(np.float32), (np.cos(angt) / np.sqrt(t_len)).astype(np.float32),
            (np.sin(angt) / np.sqrt(t_len)).astype(np.float32))


def _merge_kernel(n_ctx_tiles, xp_ref, xs_ref, mod_ref, g1_ref, g2_ref, afc_ref, afl_ref, aac_ref, aal_ref,
                  wf_ref, wa_ref, wg_ref, bg_ref, wo_ref, wr_ref, br_ref,
                  x1_ref, h2_ref, idx_ref, wts_ref, rank_ref, cnt_ref, carry):
    t = pl.program_id(0)
    is_ctx = t < n_ctx_tiles

    @pl.when(t == 0)
    def _():
        carry[...] = jnp.zeros_like(carry)

    x = jnp.where(is_ctx, xp_ref[...], xs_ref[...])
    a_four = jnp.where(is_ctx, afc_ref[...], afl_ref[...])
    a_attn = jnp.where(is_ctx, aac_ref[...], aal_ref[...])
    sh1 = mod_ref[0, 0:1, :]
    sc1 = mod_ref[0, 1:2, :]
    gt1 = mod_ref[0, 2:3, :]
    sh2 = mod_ref[0, 3:4, :]
    sc2 = mod_ref[0, 4:5, :]
    d = x.shape[1]
    hb = (_rms(x) * g1_ref[...] * (1.0 + sc1) + sh1).astype(BF16)
    gates = jax.nn.sigmoid(jnp.dot(hb, wg_ref[...], preferred_element_type=F32) + bg_ref[...])
    fa = jnp.dot(a_four, wf_ref[...], preferred_element_type=F32)
    fb = jnp.dot(a_attn, wa_ref[...], preferred_element_type=F32)
    mix = gates[:, :d] * fa + gates[:, d:] * fb
    x1 = x + gt1 * jnp.dot(mix.astype(BF16), wo_ref[...], preferred_element_type=F32)
    x1_ref[...] = x1
    h2 = _rms(x1) * g2_ref[...] * (1.0 + sc2) + sh2
    h2_ref[...] = h2

    logits = jnp.dot(h2, wr_ref[...], precision=lax.Precision.HIGHEST, preferred_element_type=F32) + br_ref[...]
    tm, ne = logits.shape
    col = lax.broadcasted_iota(jnp.int32, (tm, ne), 1)
    colk = lax.broadcasted_iota(jnp.int32, (tm, TOP_K), 1)
    lg = logits
    vals, idxs = [], []
    for _ in range(TOP_K):
        m = jnp.max(lg, axis=-1, keepdims=True)
        am = jnp.min(jnp.where(lg == m, col, ne), axis=-1, keepdims=True)
        vals.append(m)
        idxs.append(am)
        lg = jnp.where(col == am, -jnp.inf, lg)
    es = [jnp.exp(v - vals[0]) for v in vals]
    den = functools.reduce(lambda a, b: a + b, es)
    onehot = functools.reduce(lambda a, b: a + b, [(col == am).astype(F32) for am in idxs])
    r_i = lax.broadcasted_iota(jnp.int32, (tm, tm), 0)
    c_i = lax.broadcasted_iota(jnp.int32, (tm, tm), 1)
    lower = (r_i > c_i).astype(BF16)
    before = jnp.dot(lower, onehot.astype(BF16), preferred_element_type=F32) + carry[...]
    idx_o = jnp.zeros((tm, TOP_K), jnp.int32)
    wts_o = jnp.zeros((tm, TOP_K), F32)
    rank_o = jnp.zeros((tm, TOP_K), jnp.int32)
    for kk in range(TOP_K):
        rk = jnp.sum(jnp.where(col == idxs[kk], before, 0.0), axis=-1, keepdims=True).astype(jnp.int32)
        idx_o = jnp.where(colk == kk, idxs[kk], idx_o)
        wts_o = jnp.where(colk == kk, es[kk] / den, wts_o)
        rank_o = jnp.where(colk == kk, rk, rank_o)
    idx_ref[...] = idx_o
    wts_ref[...] = wts_o
    rank_ref[...] = rank_o
    carry[...] = carry[...] + jnp.sum(onehot, axis=0, keepdims=True)
    cnt_ref[...] = carry[...]


def _merge(xp, xs, mod, g1, g2, af_c, af_l, aa_c, aa_l, wf, wa, wg, bg, wo, wr, br, tiles_per_lat_batch):
    n_ctx, d = xp.shape
    n = n_ctx + xs.shape[0]
    tm = TOK_TILE
    n_ctx_tiles = n_ctx // tm
    last = n_ctx_tiles - 1
    ne = wr.shape[1]
    da = aa_c.shape[1]
    df = af_c.shape[1]

    def mod_row(t):
        return jnp.where(t < n_ctx_tiles, 0, 1 + (t - n_ctx_tiles) // tiles_per_lat_batch)

    cmap = lambda t: (jnp.minimum(t, last), 0)
    lmap = lambda t: (jnp.maximum(t - n_ctx_tiles, 0), 0)
    full = lambda a: pl.BlockSpec(a.shape, lambda t: (0,) * a.ndim)
    tok = lambda w: pl.BlockSpec((tm, w), lambda t: (t, 0))
    return pl.pallas_call(
        functools.partial(_merge_kernel, n_ctx_tiles),
        out_shape=(jax.ShapeDtypeStruct((n, d), F32), jax.ShapeDtypeStruct((n, d), F32),
                   jax.ShapeDtypeStruct((n, TOP_K), jnp.int32), jax.ShapeDtypeStruct((n, TOP_K), F32),
                   jax.ShapeDtypeStruct((n, TOP_K), jnp.int32), jax.ShapeDtypeStruct((1, ne), F32)),
        grid=(n // tm,),
        in_specs=[pl.BlockSpec((tm, d), cmap), pl.BlockSpec((tm, d), lmap),
                  pl.BlockSpec((1,) + mod.shape[1:], lambda t: (mod_row(t), 0, 0)),
                  full(g1), full(g2),
                  pl.BlockSpec((tm, df), cmap), pl.BlockSpec((tm, df), lmap),
                  pl.BlockSpec((tm, da), cmap), pl.BlockSpec((tm, da), lmap),
                  full(wf), full(wa), full(wg), full(bg), full(wo), full(wr), full(br)],
        out_specs=(tok(d), tok(d), tok(TOP_K), tok(TOP_K), tok(TOP_K),
                   pl.BlockSpec((1, ne), lambda t: (0, 0))),
        scratch_shapes=[pltpu.VMEM((1, ne), F32)],
        compiler_params=_cparams(1),
        name="merge",
    )(xp, xs, mod, g1, g2, af_c, af_l, aa_c, aa_l, wf, wa, wg, bg, wo, wr, br)


def _moe_kernel(te_ref, nt_ref, x_ref, w1_ref, b1_ref, w2_ref, b2_ref, y_ref, w1b, w2b):
    i = pl.program_id(0)
    e = te_ref[i]
    prev = te_ref[jnp.maximum(i - 1, 0)]

    @pl.when((i == 0) | (e != prev))
    def _():
        w1b[...] = w1_ref[0].astype(BF16)
        w2b[...] = w2_ref[0].astype(BF16)

    @pl.when(i < nt_ref[0])
    def _():
        dff = w2b.shape[0]
        gu = jnp.dot(x_ref[...].astype(BF16), w1b[...], preferred_element_type=F32) + b1_ref[0]
        gate = jnp.minimum(gu[:, :dff], SWIGLU_LIMIT)
        up = jnp.clip(gu[:, dff:], -SWIGLU_LIMIT, SWIGLU_LIMIT)
        glu = gate * jax.nn.sigmoid(SWIGLU_ALPHA * gate)
        act = ((up + 1.0) * glu).astype(BF16)
        y = jnp.dot(act, w2b[...], preferred_element_type=F32) + b2_ref[0]
        y_ref[...] = y


def _moe(tile_expert, n_tiles, xs_sorted, w1, b1, w2, b2):
    p, d = xs_sorted.shape
    ne, _, dff2 = w1.shape
    dff = w2.shape[1]
    tm = MOE_TILE
    max_tiles = p // tm

    def row_map(i, te, nt):
        return (jnp.minimum(i, nt[0] - 1), 0)

    def w_map(i, te, nt):
        return (te[i], 0, 0)

    grid_spec = pltpu.PrefetchScalarGridSpec(
        num_scalar_prefetch=2,
        grid=(max_tiles,),
        in_specs=[pl.BlockSpec((tm, d), row_map),
                  pl.BlockSpec((1, d, dff2), w_map),
                  pl.BlockSpec((1, 1, dff2), w_map),
                  pl.BlockSpec((1, dff, d), w_map),
                  pl.BlockSpec((1, 1, d), w_map)],
        out_specs=pl.BlockSpec((tm, d), row_map),
        scratch_shapes=[pltpu.VMEM((d, dff2), BF16), pltpu.VMEM((dff, d), BF16)],
    )
    return pl.pallas_call(
        _moe_kernel,
        out_shape=jax.ShapeDtypeStruct((p, d), F32),
        grid_spec=grid_spec,
        compiler_params=_cparams(1),
        name="moe",
    )(tile_expert, n_tiles, xs_sorted, w1, b1.reshape(ne, 1, dff2), w2, b2.reshape(ne, 1, d))


def _sc_mesh():
    return plsc.VectorSubcoreMesh(core_axis_name="c", subcore_axis_name="s",
                                  num_cores=SC_CORES, num_subcores=SC_SUBCORES)


def _sc_dispatch(h, idx, p_rows):
    n, d = h.shape
    nw, items, chunk = idx.shape
    tok_w = n // nw
    n_chunks = items // TOP_K

    @functools.partial(
        pl.kernel, mesh=_sc_mesh(),
        out_type=jax.ShapeDtypeStruct((p_rows, d), h.dtype),
        scratch_types=[pltpu.VMEM((items, chunk), jnp.int32),
                       pltpu.VMEM((2, chunk, d), h.dtype),
                       pltpu.SemaphoreType.DMA((2,)), pltpu.SemaphoreType.DMA((2,))],
        name="sc_dispatch",
    )
    def run(h_hbm, idx_hbm, out_hbm, idx_v, rows_v, lsem, ssem):
        wid = lax.axis_index("s") * SC_CORES + lax.axis_index("c")
        base = wid * tok_w
        pltpu.sync_copy(idx_hbm.at[wid], idx_v)

        def load(j, slot):
            return pltpu.make_async_copy(h_hbm.at[pl.ds(base + j * chunk, chunk)], rows_v.at[slot], lsem.at[slot])

        def scat(j, kk, slot):
            return pltpu.make_async_copy(rows_v.at[slot], out_hbm.at[idx_v.at[j * TOP_K + kk]], ssem.at[slot])

        load(0, 0).start()

        @pl.loop(0, n_chunks, step=2)
        def _(j0):
            for b in range(2):
                j = j0 + b

                @pl.when(j >= 1)
                def _():
                    for kk in range(TOP_K):
                        scat(j - 1, kk, 1 - b).wait()

                @pl.when(j + 1 < n_chunks)
                def _():
                    load(j + 1, 1 - b).start()

                load(j, b).wait()
                for kk in range(TOP_K):
                    scat(j, kk, b).start()

        for kk in range(TOP_K):
            scat(n_chunks - 1, kk, (n_chunks - 1) % 2).wait()

    return run(h, idx)


def _sc_gather(y, idx, n):
    d = y.shape[1]
    nw, items, chunk = idx.shape
    tok_w = n // nw

    @functools.partial(
        pl.kernel, mesh=_sc_mesh(),
        out_type=jax.ShapeDtypeStruct((TOP_K, n, d), y.dtype),
        scratch_types=[pltpu.VMEM((items, chunk), jnp.int32),
                       pltpu.VMEM((2, chunk, d), y.dtype),
                       pltpu.SemaphoreType.DMA((2,)), pltpu.SemaphoreType.DMA((2,))],
        name="sc_gather",
    )
    def run(y_hbm, idx_hbm, out_hbm, idx_v, rows_v, gsem, wsem):
        wid = lax.axis_index("s") * SC_CORES + lax.axis_index("c")
        base = wid * tok_w
        pltpu.sync_copy(idx_hbm.at[wid], idx_v)

        def gather(i, slot):
            return pltpu.make_async_copy(y_hbm.at[idx_v.at[i]], rows_v.at[slot], gsem.at[slot])

        def write(i, slot):
            dst = out_hbm.at[i % TOP_K, pl.ds(base + (i // TOP_K) * chunk, chunk)]
            return pltpu.make_async_copy(rows_v.at[slot], dst, wsem.at[slot])

        gather(0, 0).start()

        @pl.loop(0, items, step=2)
        def _(i0):
            for b in range(2):
                i = i0 + b

                @pl.when(i >= 1)
                def _():
                    write(i - 1, 1 - b).wait()

                @pl.when(i + 1 < items)
                def _():
                    gather(i + 1, 1 - b).start()

                gather(i, b).wait()
                write(i, b).start()

        write(items - 1, (items - 1) % 2).wait()

    return run(y, idx)


def _combine_kernel(n_ctx_tiles, x1_ref, yg_ref, wts_ref, mod_ref, op_ref, os_ref):
    t = pl.program_id(0)
    gt2 = mod_ref[0, 5:6, :]
    w = wts_ref[...]
    acc = w[:, 0:1] * yg_ref[0].astype(F32)
    for kk in range(1, TOP_K):
        acc = acc + w[:, kk:kk + 1] * yg_ref[kk].astype(F32)
    out = x1_ref[...] + gt2 * acc

    @pl.when(t < n_ctx_tiles)
    def _():
        op_ref[...] = out

    @pl.when(t >= n_ctx_tiles)
    def _():
        os_ref[...] = out


def _combine(x1, yg, wts, mod, n_ctx, tiles_per_lat_batch):
    n, d = x1.shape
    tm = TOK_TILE
    n_ctx_tiles = n_ctx // tm
    last = n_ctx_tiles - 1

    def mod_row(t):
        return jnp.where(t < n_ctx_tiles, 0, 1 + (t - n_ctx_tiles) // tiles_per_lat_batch)

    return pl.pallas_call(
        functools.partial(_combine_kernel, n_ctx_tiles),
        out_shape=(jax.ShapeDtypeStruct((n_ctx, d), F32), jax.ShapeDtypeStruct((n - n_ctx, d), F32)),
        grid=(n // tm,),
        in_specs=[pl.BlockSpec((tm, d), lambda t: (t, 0)),
                  pl.BlockSpec((TOP_K, tm, d), lambda t: (0, t, 0)),
                  pl.BlockSpec((tm, TOP_K), lambda t: (t, 0)),
                  pl.BlockSpec((1,) + mod.shape[1:], lambda t: (mod_row(t), 0, 0))],
        out_specs=(pl.BlockSpec((tm, d), lambda t: (jnp.minimum(t, last), 0)),
                   pl.BlockSpec((tm, d), lambda t: (jnp.maximum(t - n_ctx_tiles, 0), 0))),
        compiler_params=_cparams(1),
        name="combine",
    )(x1, yg, wts, mod)


def kernel(x_prompt, x_sample, cache_k, cache_v, c, c_ctx, g_norm1, w_ada, b_ada, w_in, g_q, g_k, rpb,
           w_fmap, w_amap, w_gate, b_gate, w_out, g_norm2, w_router, b_router, w1, b1, w2, b2):
    batch, seq, d = x_prompt.shape
    dec_batch, dec_seq, _ = x_sample.shape
    assert w_ada.shape[0] == 1, "single-layer trunk"
    rows = dec_seq // GRID_W
    da = N_HEADS * HEAD_DIM
    n_ctx = batch * seq
    n_lat = dec_batch * dec_seq
    n = n_ctx + n_lat
    ne = w_router.shape[2]
    assert TOK_TILE % seq == 0 and n_ctx % TOK_TILE == 0 and dec_seq % TOK_TILE == 0
    assert rows % (ATT_TILE // GRID_W) == 0 and rows >= 3 * (ATT_TILE // GRID_W) and rows >= WIN_H_MAX
    assert seq == ATT_TILE and cache_k.shape[3] == ATT_TILE
    tiles_per_lat_batch = dec_seq // TOK_TILE

    n_mod_rows = 8
    cvecs = jnp.zeros((n_mod_rows, d), F32).at[0].set(c_ctx).at[1:1 + dec_batch].set(c)
    mod = _ada(cvecs, w_ada[0], b_ada[0]).reshape(n_mod_rows, 6, d)

    xp = x_prompt.reshape(n_ctx, d)
    xs = x_sample.reshape(n_lat, d)
    g1 = g_norm1[0].reshape(1, d)
    g2 = g_norm2[0].reshape(1, d)
    hsum = jnp.asarray(np.kron(np.eye(N_HEADS), np.full((HEAD_DIM, HEAD_DIM), 1.0 / HEAD_DIM)), BF16)
    gq_t = jnp.tile(g_q[0], N_HEADS).reshape(1, da)
    gk_t = jnp.tile(g_k[0], N_HEADS).reshape(1, da)

    u, q, k, v, new_k, new_v = _proj(xp, xs, mod, g1, w_in[0].astype(BF16), hsum, gq_t, gk_t,
                                      batch, seq, tiles_per_lat_batch)

    aa_c = _attn_ctx(q, k, v, batch, seq)
    er, ec, valid = _window_tables(rows)
    bias = jnp.einsum("vqka,hab,xyb->vhqxky", er, rpb[0], ec, precision=lax.Precision.HIGHEST)
    bias = jnp.where(valid[:, None], bias, NEG_BIG).reshape(3, N_HEADS, ATT_TILE, 3 * ATT_TILE)
    past = cache_k.shape[3]
    kc = cache_k[:, 0].transpose(0, 2, 1, 3).reshape(dec_batch, past, da).astype(BF16)
    vc = cache_v[:, 0].transpose(0, 2, 1, 3).reshape(dec_batch, past, da).astype(BF16)
    aa_l = _attn_lat(q, k, v, kc, vc, bias, n_ctx, dec_batch, rows)

    w1c, ct_c, st_c = _dft_tables(seq)
    _, ct_l, st_l = _dft_tables(dec_seq)
    w1c = jnp.asarray(w1c).astype(BF16)
    af_c = _fourier(u, w1c, jnp.asarray(ct_c).astype(BF16), jnp.asarray(st_c).astype(BF16),
                    batch, seq, 0, seq)
    af_l = _fourier(u, w1c, jnp.asarray(ct_l).astype(BF16), jnp.asarray(st_l).astype(BF16),
                    dec_batch, dec_seq, n_ctx // dec_seq, TOK_TILE)

    x1, h2, idx, wts, rank, counts = _merge(
        xp, xs, mod, g1, g2, af_c, af_l, aa_c, aa_l,
        w_fmap[0].astype(BF16), w_amap[0].astype(BF16), w_gate[0].astype(BF16), b_gate[0].reshape(1, -1),
        w_out[0].astype(BF16), w_router[0], b_router[0].reshape(1, ne), tiles_per_lat_batch)

    tm = MOE_TILE
    max_tiles = (n * TOP_K) // tm + ne
    cnt = counts[0].astype(jnp.int32)
    tiles_e = (cnt + tm - 1) // tm
    tile_end = jnp.cumsum(tiles_e)
    pad_off = (tile_end - tiles_e) * tm
    n_tiles = tile_end[-1:]
    tile_ids = jnp.arange(max_tiles, dtype=jnp.int32)
    tile_expert = jnp.sum((tile_ids[:, None] >= tile_end[None, :]).astype(jnp.int32), axis=1)
    last_e = jnp.sum((n_tiles - 1 >= tile_end).astype(jnp.int32))
    tile_expert = jnp.minimum(tile_expert, last_e).astype(jnp.int32)
    pos = jnp.sum(jnp.where(idx[:, :, None] == jnp.arange(ne)[None, None, :], pad_off[None, None, :], 0),
                  axis=-1) + rank

    p_rows = max_tiles * tm
    n_workers = SC_CORES * SC_SUBCORES
    assert n % (n_workers * SC_CHUNK * 2) == 0
    chunks = n // (n_workers * SC_CHUNK)
    pos_w = pos.astype(jnp.int32).reshape(n_workers, chunks, SC_CHUNK, TOP_K).transpose(0, 1, 3, 2)
    pos_w = pos_w.reshape(n_workers, chunks * TOP_K, SC_CHUNK)
    xs_sorted = _sc_dispatch(h2, pos_w, p_rows)
    y_sorted = _moe(tile_expert, n_tiles.astype(jnp.int32), xs_sorted, w1[0], b1[0], w2[0], b2[0])
    yg = _sc_gather(y_sorted, pos_w, n)

    y_p, y_s = _combine(x1, yg, wts, mod, n_ctx, tiles_per_lat_batch)
    return (y_p.reshape(batch, seq, d), y_s.reshape(dec_batch, dec_seq, d), new_k, new_v)
```

```python
import functools

import numpy as np
import jax
import jax.numpy as jnp
from jax import lax
from jax.experimental import pallas as pl
from jax.experimental.pallas import tpu as pltpu
from jax.experimental.pallas import tpu_sc as plsc

F32 = jnp.float32
BF16 = jnp.bfloat16

GRID_W = 64
N_HEADS = 8
HEAD_DIM = 64
N_FGROUPS = 4
FGROUP_DIM = 128
WIN_H_MAX = 8
WIN_W = 16
TOP_K = 4
SWIGLU_LIMIT = 7.0
SWIGLU_ALPHA = 1.702
EPS = 1e-6

LANES = 128
TOK_TILE = 512
ATT_TILE = 256
MOE_TILE = 512
VMEM_LIMIT = 56 * 1024 * 1024
NEG_BIG = -1e30

SC_CORES = 2
SC_SUBCORES = 16
SC_CHUNK = 64


def _cparams(n_axes, vmem=VMEM_LIMIT):
    return pltpu.CompilerParams(dimension_semantics=("arbitrary",) * n_axes, vmem_limit_bytes=vmem)


def _rms(x):
    return x * lax.rsqrt(jnp.mean(x * x, axis=-1, keepdims=True) + EPS)


def _pack_halves(x):
    c = x.shape[1] // 2
    lo = lax.bitcast_convert_type(x[:, :c].astype(BF16).astype(F32), jnp.uint32)
    hi = lax.bitcast_convert_type(x[:, c:].astype(BF16).astype(F32), jnp.uint32)
    return lax.bitcast_convert_type(hi | (lo >> 16), jnp.int32)


def _unpack_halves(w):
    u = lax.bitcast_convert_type(w, jnp.uint32)
    lo = lax.bitcast_convert_type(u << 16, F32)
    hi = lax.bitcast_convert_type(u & jnp.uint32(0xFFFF0000), F32)
    return lo, hi


def _ada_kernel(c_ref, w_ref, b_ref, o_ref):
    cv = c_ref[...]
    s = cv * jax.nn.sigmoid(cv)
    o_ref[...] = jnp.dot(s, w_ref[...], precision=lax.Precision.HIGHEST,
                         preferred_element_type=F32) + b_ref[...]


def _ada(cvecs, w_ada, b_ada):
    rows, d = cvecs.shape
    n = w_ada.shape[1]
    blk = 1024
    return pl.pallas_call(
        _ada_kernel,
        out_shape=jax.ShapeDtypeStruct((rows, n), F32),
        grid=(n // blk,),
        in_specs=[pl.BlockSpec((rows, d), lambda j: (0, 0)),
                  pl.BlockSpec((d, blk), lambda j: (0, j)),
                  pl.BlockSpec((1, blk), lambda j: (0, j))],
        out_specs=pl.BlockSpec((rows, blk), lambda j: (0, j)),
        compiler_params=_cparams(1),
        name="ada",
    )(cvecs, w_ada, b_ada.reshape(1, n))


def _proj_kernel(n_ctx_tiles, xp_ref, xs_ref, mod_ref, g1_ref, win_ref, hsum_ref, gq_ref, gk_ref,
                 u_ref, q_ref, k_ref, v_ref, nk_ref, nv_ref):
    t = pl.program_id(0)
    is_ctx = t < n_ctx_tiles
    x = jnp.where(is_ctx, xp_ref[...], xs_ref[...])
    sh1 = mod_ref[0, 0:1, :]
    sc1 = mod_ref[0, 1:2, :]
    h = _rms(x) * g1_ref[...] * (1.0 + sc1) + sh1
    proj = jnp.dot(h.astype(BF16), win_ref[...], preferred_element_type=F32)
    da = q_ref.shape[1]
    df = u_ref.shape[1]
    u_ref[...] = proj[:, :df].astype(BF16)
    q = proj[:, df:df + da]
    k = proj[:, df + da:df + 2 * da]
    v = proj[:, df + 2 * da:]
    msq = jnp.dot((q * q).astype(BF16), hsum_ref[...], preferred_element_type=F32)
    msk = jnp.dot((k * k).astype(BF16), hsum_ref[...], preferred_element_type=F32)
    qn = q * lax.rsqrt(msq + EPS) * gq_ref[...]
    kn = k * lax.rsqrt(msk + EPS) * gk_ref[...]
    q_ref[...] = qn.astype(BF16)
    k_ref[...] = kn.astype(BF16)
    v_ref[...] = v.astype(BF16)

    @pl.when(is_ctx)
    def _():
        nb, _, nh, s, dh = nk_ref.shape
        for b in range(nb):
            for hd in range(nh):
                nk_ref[b, 0, hd] = kn[b * s:(b + 1) * s, hd * dh:(hd + 1) * dh]
                nv_ref[b, 0, hd] = v[b * s:(b + 1) * s, hd * dh:(hd + 1) * dh]


def _proj(xp, xs, mod, g1, w_in_b, hsum, gq_t, gk_t, batch, seq, tiles_per_lat_batch):
    n_ctx, d = xp.shape
    n_lat = xs.shape[0]
    tm = TOK_TILE
    n_ctx_tiles = n_ctx // tm
    n_tiles = (n_ctx + n_lat) // tm
    n = n_ctx + n_lat
    d_in = w_in_b.shape[1]
    da = N_HEADS * HEAD_DIM
    df = d_in - 3 * da
    bpt = tm // seq
    last = n_ctx_tiles - 1

    def mod_row(t):
        return jnp.where(t < n_ctx_tiles, 0, 1 + (t - n_ctx_tiles) // tiles_per_lat_batch)

    tok = lambda w: pl.BlockSpec((tm, w), lambda t: (t, 0))
    full = lambda a: pl.BlockSpec(a.shape, lambda t: (0,) * a.ndim)
    kv_spec = pl.BlockSpec((bpt, 1, N_HEADS, seq, HEAD_DIM), lambda t: (jnp.minimum(t, last), 0, 0, 0, 0))
    kv_shape = jax.ShapeDtypeStruct((batch, 1, N_HEADS, seq, HEAD_DIM), F32)
    return pl.pallas_call(
        functools.partial(_proj_kernel, n_ctx_tiles),
        out_shape=(jax.ShapeDtypeStruct((n, df), BF16),) + (jax.ShapeDtypeStruct((n, da), BF16),) * 3
        + (kv_shape, kv_shape),
        grid=(n_tiles,),
        in_specs=[pl.BlockSpec((tm, d), lambda t: (jnp.minimum(t, last), 0)),
                  pl.BlockSpec((tm, d), lambda t: (jnp.maximum(t - n_ctx_tiles, 0), 0)),
                  pl.BlockSpec((1,) + mod.shape[1:], lambda t: (mod_row(t), 0, 0)),
                  full(g1), full(w_in_b), full(hsum), full(gq_t), full(gk_t)],
        out_specs=(tok(df), tok(da), tok(da), tok(da), kv_spec, kv_spec),
        compiler_params=_cparams(1),
        name="proj",
    )(xp, xs, mod, g1, w_in_b, hsum, gq_t, gk_t)


def _softmax_pv(q2, key_blocks, val_blocks, bias_blocks, scale):
    lane = lax.broadcasted_iota(jnp.int32, (1, LANES), 1)
    out = None
    for half in range(2):
        hmask = (lane < HEAD_DIM) if half == 0 else (lane >= HEAD_DIM)
        qh = jnp.where(hmask, q2, jnp.zeros_like(q2))
        scores = []
        for kb, bb in zip(key_blocks, bias_blocks):
            s = lax.dot_general(qh, kb, (((1,), (1,)), ((), ())), preferred_element_type=F32) * scale
            if bb is not None:
                s = s + bb[half]
            scores.append(s)
        m = functools.reduce(jnp.maximum, [jnp.max(s, axis=-1, keepdims=True) for s in scores])
        ps = [jnp.exp(s - m) for s in scores]
        denom = functools.reduce(lambda a, b: a + b, [jnp.sum(p, axis=-1, keepdims=True) for p in ps])
        o = functools.reduce(lambda a, b: a + b,
                             [jnp.dot(p.astype(BF16), vb, preferred_element_type=F32)
                              for p, vb in zip(ps, val_blocks)])
        o = o / denom
        out = o if out is None else jnp.where(hmask, o, out)
    return out


def _attn_ctx_kernel(q_ref, k_ref, v_ref, o_ref):
    scale = HEAD_DIM ** -0.5
    for p in range(q_ref.shape[1] // LANES):
        sl = slice(p * LANES, (p + 1) * LANES)
        o = _softmax_pv(q_ref[:, sl], [k_ref[:, sl]], [v_ref[:, sl]], [None], scale)
        o_ref[:, sl] = o.astype(BF16)


def _attn_ctx(q, k, v, batch, seq):
    da = q.shape[1]
    spec = pl.BlockSpec((seq, da), lambda b: (b, 0))
    return pl.pallas_call(
        _attn_ctx_kernel,
        out_shape=jax.ShapeDtypeStruct((batch * seq, da), BF16),
        grid=(batch,),
        in_specs=[spec, spec, spec],
        out_specs=spec,
        compiler_params=_cparams(1),
        name="attn_ctx",
    )(q, k, v)


def _attn_lat_kernel(q_ref, k0_ref, k1_ref, k2_ref, v0_ref, v1_ref, v2_ref, kc_ref, vc_ref, bias_ref, o_ref):
    scale = HEAD_DIM ** -0.5
    tk = k0_ref.shape[0]
    for p in range(q_ref.shape[1] // LANES):
        sl = slice(p * LANES, (p + 1) * LANES)
        keys = [k0_ref[:, sl], k1_ref[:, sl], k2_ref[:, sl], kc_ref[0, :, sl]]
        vals = [v0_ref[:, sl], v1_ref[:, sl], v2_ref[:, sl], vc_ref[0, :, sl]]
        biases = [[bias_ref[0, 2 * p + half, :, d * tk:(d + 1) * tk] for half in range(2)] for d in range(3)]
        o = _softmax_pv(q_ref[:, sl], keys, vals, biases + [None], scale)
        o_ref[:, sl] = o.astype(BF16)


def _attn_lat(q, k, v, kc, vc, bias, n_ctx, dec_batch, rows):
    da = q.shape[1]
    tq = ATT_TILE
    rows_per_tile = tq // GRID_W
    tiles = rows // rows_per_tile
    base = n_ctx // tq
    n_lat = dec_batch * rows * GRID_W
    max_start = tiles - 3

    def qmap(j, b):
        return (base + b * tiles + j, 0)

    def kmap(d):
        return lambda j, b: (base + b * tiles + jnp.clip(j - 1, 0, max_start) + d, 0)

    def bmap(j, b):
        return (jnp.where(j == 0, 0, jnp.where(j == tiles - 1, 2, 1)), 0, 0, 0)

    blk = lambda m: pl.BlockSpec((tq, da), m)
    cspec = pl.BlockSpec((1,) + kc.shape[1:], lambda j, b: (b, 0, 0))
    return pl.pallas_call(
        _attn_lat_kernel,
        out_shape=jax.ShapeDtypeStruct((n_lat, da), BF16),
        grid=(tiles, dec_batch),
        in_specs=[blk(qmap)] + [blk(kmap(d)) for d in range(3)] + [blk(kmap(d)) for d in range(3)]
        + [cspec, cspec, pl.BlockSpec((1,) + bias.shape[1:], bmap)],
        out_specs=pl.BlockSpec((tq, da), lambda j, b: (b * tiles + j, 0)),
        compiler_params=_cparams(2),
        name="attn_lat",
    )(q, k, k, k, v, v, v, kc, vc, bias)


def _window_tables(rows):
    rpt = ATT_TILE // GRID_W
    krows = 3 * rpt
    tiles = rows // rpt
    kh = min(WIN_H_MAX, rows)
    rpb_h = 2 * WIN_H_MAX - 1
    rpb_w = 2 * WIN_W - 1
    er = np.zeros((3, rpt, krows, rpb_h), np.float32)
    for var, j in enumerate((0, 1, tiles - 1)):
        r0 = j * rpt
        k0 = int(np.clip(j - 1, 0, tiles - 3)) * rpt
        for qi in range(rpt):
            qrow = r0 + qi
            rs = int(np.clip(qrow - kh // 2, 0, rows - kh))
            for ki in range(krows):
                krow = k0 + ki
                if rs <= krow < rs + kh:
                    er[var, qi, ki, krow - qrow + WIN_H_MAX - 1] = 1.0
    ec = np.zeros((GRID_W, GRID_W, rpb_w), np.float32)
    for qc in range(GRID_W):
        cs = int(np.clip(qc - WIN_W // 2, 0, GRID_W - WIN_W))
        for kc in range(cs, cs + WIN_W):
            ec[qc, kc, int(np.clip(kc - qc, -(WIN_W - 1), WIN_W - 1)) + WIN_W - 1] = 1.0
    valid = np.einsum("vqka,xyb->vqxky", er, ec) > 0.5
    return er, ec, valid


def _fourier_kernel(u_ref, w1_ref, ct_ref, st_ref, o_ref, p_scr, q_scr):
    @pl.when(pl.program_id(1) == 0)
    def _():
        pq = jnp.dot(u_ref[...], w1_ref[...], preferred_element_type=F32)
        df = p_scr.shape[1]
        p_scr[...] = pq[:, :df].astype(BF16)
        q_scr[...] = pq[:, df:].astype(BF16)

    o = (jnp.dot(ct_ref[...], p_scr[...], preferred_element_type=F32)
         - jnp.dot(st_ref[...], q_scr[...], preferred_element_type=F32))
    o_ref[...] = o.astype(BF16)


def _fourier(u, w1, ct, st, batch, t_len, first_block, row_tile):
    df = u.shape[1]
    steps = t_len // row_tile
    return pl.pallas_call(
        _fourier_kernel,
        out_shape=jax.ShapeDtypeStruct((batch * t_len, df), BF16),
        grid=(batch, steps),
        in_specs=[pl.BlockSpec((t_len, df), lambda b, i: (first_block + b, 0)),
                  pl.BlockSpec(w1.shape, lambda b, i: (0, 0)),
                  pl.BlockSpec((row_tile, t_len), lambda b, i: (i, 0)),
                  pl.BlockSpec((row_tile, t_len), lambda b, i: (i, 0))],
        out_specs=pl.BlockSpec((row_tile, df), lambda b, i: (b * steps + i, 0)),
        scratch_shapes=[pltpu.VMEM((t_len, df), BF16), pltpu.VMEM((t_len, df), BF16)],
        compiler_params=_cparams(2),
        name=f"fourier_{t_len}",
    )(u, w1, ct, st)


def _dft_tables(t_len):
    c = FGROUP_DIM
    jk = np.outer(np.arange(c), np.arange(c)) % c
    ang = 2.0 * np.pi * jk / c
    eye = np.eye(N_FGROUPS)
    w1 = np.concatenate([np.kron(eye, np.cos(ang)), np.kron(eye, np.sin(ang))], axis=1) / np.sqrt(c)
    tt = np.outer(np.arange(t_len), np.arange(t_len)) % t_len
    angt = 2.0 * np.pi * tt / t_len
    return (w1.astype(np.float32), (np.cos(angt) / np.sqrt(t_len)).astype(np.float32),
            (np.sin(angt) / np.sqrt(t_len)).astype(np.float32))


def _merge_kernel(n_ctx_tiles, xp_ref, xs_ref, mod_ref, g1_ref, g2_ref, afc_ref, afl_ref, aac_ref, aal_ref,
                  wf_ref, wa_ref, wg_ref, bg_ref, wo_ref, wr_ref, br_ref,
                  x1_ref, h2_ref, idx_ref, wts_ref, rank_ref, cnt_ref, carry):
    t = pl.program_id(0)
    is_ctx = t < n_ctx_tiles

    @pl.when(t == 0)
    def _():
        carry[...] = jnp.zeros_like(carry)

    x = jnp.where(is_ctx, xp_ref[...], xs_ref[...])
    a_four = jnp.where(is_ctx, afc_ref[...], afl_ref[...])
    a_attn = jnp.where(is_ctx, aac_ref[...], aal_ref[...])
    sh1 = mod_ref[0, 0:1, :]
    sc1 = mod_ref[0, 1:2, :]
    gt1 = mod_ref[0, 2:3, :]
    sh2 = mod_ref[0, 3:4, :]
    sc2 = mod_ref[0, 4:5, :]
    d = x.shape[1]
    hb = (_rms(x) * g1_ref[...] * (1.0 + sc1) + sh1).astype(BF16)
    gates = jax.nn.sigmoid(jnp.dot(hb, wg_ref[...], preferred_element_type=F32) + bg_ref[...])
    fa = jnp.dot(a_four, wf_ref[...], preferred_element_type=F32)
    fb = jnp.dot(a_attn, wa_ref[...], preferred_element_type=F32)
    mix = gates[:, :d] * fa + gates[:, d:] * fb
    x1 = x + gt1 * jnp.dot(mix.astype(BF16), wo_ref[...], preferred_element_type=F32)
    x1_ref[...] = x1
    h2 = _rms(x1) * g2_ref[...] * (1.0 + sc2) + sh2
    h2_ref[...] = _pack_halves(h2)

    logits = jnp.dot(h2, wr_ref[...], precision=lax.Precision.HIGHEST, preferred_element_type=F32) + br_ref[...]
    tm, ne = logits.shape
    col = lax.broadcasted_iota(jnp.int32, (tm, ne), 1)
    colk = lax.broadcasted_iota(jnp.int32, (tm, TOP_K), 1)
    lg = logits
    vals, idxs = [], []
    for _ in range(TOP_K):
        m = jnp.max(lg, axis=-1, keepdims=True)
        am = jnp.min(jnp.where(lg == m, col, ne), axis=-1, keepdims=True)
        vals.append(m)
        idxs.append(am)
        lg = jnp.where(col == am, -jnp.inf, lg)
    es = [jnp.exp(v - vals[0]) for v in vals]
    den = functools.reduce(lambda a, b: a + b, es)
    onehot = functools.reduce(lambda a, b: a + b, [(col == am).astype(F32) for am in idxs])
    r_i = lax.broadcasted_iota(jnp.int32, (tm, tm), 0)
    c_i = lax.broadcasted_iota(jnp.int32, (tm, tm), 1)
    lower = (r_i > c_i).astype(BF16)
    before = jnp.dot(lower, onehot.astype(BF16), preferred_element_type=F32) + carry[...]
    idx_o = jnp.zeros((tm, TOP_K), jnp.int32)
    wts_o = jnp.zeros((tm, TOP_K), F32)
    rank_o = jnp.zeros((tm, TOP_K), jnp.int32)
    for kk in range(TOP_K):
        rk = jnp.sum(jnp.where(col == idxs[kk], before, 0.0), axis=-1, keepdims=True).astype(jnp.int32)
        idx_o = jnp.where(colk == kk, idxs[kk], idx_o)
        wts_o = jnp.where(colk == kk, es[kk] / den, wts_o)
        rank_o = jnp.where(colk == kk, rk, rank_o)
    idx_ref[...] = idx_o
    wts_ref[...] = wts_o
    rank_ref[...] = rank_o
    carry[...] = carry[...] + jnp.sum(onehot, axis=0, keepdims=True)
    cnt_ref[...] = carry[...]


def _merge(xp, xs, mod, g1, g2, af_c, af_l, aa_c, aa_l, wf, wa, wg, bg, wo, wr, br, tiles_per_lat_batch):
    n_ctx, d = xp.shape
    n = n_ctx + xs.shape[0]
    tm = TOK_TILE
    n_ctx_tiles = n_ctx // tm
    last = n_ctx_tiles - 1
    ne = wr.shape[1]
    da = aa_c.shape[1]
    df = af_c.shape[1]

    def mod_row(t):
        return jnp.where(t < n_ctx_tiles, 0, 1 + (t - n_ctx_tiles) // tiles_per_lat_batch)

    cmap = lambda t: (jnp.minimum(t, last), 0)
    lmap = lambda t: (jnp.maximum(t - n_ctx_tiles, 0), 0)
    full = lambda a: pl.BlockSpec(a.shape, lambda t: (0,) * a.ndim)
    tok = lambda w: pl.BlockSpec((tm, w), lambda t: (t, 0))
    return pl.pallas_call(
        functools.partial(_merge_kernel, n_ctx_tiles),
        out_shape=(jax.ShapeDtypeStruct((n, d), F32), jax.ShapeDtypeStruct((n, d // 2), jnp.int32),
                   jax.ShapeDtypeStruct((n, TOP_K), jnp.int32), jax.ShapeDtypeStruct((n, TOP_K), F32),
                   jax.ShapeDtypeStruct((n, TOP_K), jnp.int32), jax.ShapeDtypeStruct((1, ne), F32)),
        grid=(n // tm,),
        in_specs=[pl.BlockSpec((tm, d), cmap), pl.BlockSpec((tm, d), lmap),
                  pl.BlockSpec((1,) + mod.shape[1:], lambda t: (mod_row(t), 0, 0)),
                  full(g1), full(g2),
                  pl.BlockSpec((tm, df), cmap), pl.BlockSpec((tm, df), lmap),
                  pl.BlockSpec((tm, da), cmap), pl.BlockSpec((tm, da), lmap),
                  full(wf), full(wa), full(wg), full(bg), full(wo), full(wr), full(br)],
        out_specs=(tok(d), tok(d // 2), tok(TOP_K), tok(TOP_K), tok(TOP_K),
                   pl.BlockSpec((1, ne), lambda t: (0, 0))),
        scratch_shapes=[pltpu.VMEM((1, ne), F32)],
        compiler_params=_cparams(1),
        name="merge",
    )(xp, xs, mod, g1, g2, af_c, af_l, aa_c, aa_l, wf, wa, wg, bg, wo, wr, br)


def _moe_kernel(te_ref, nt_ref, nx_ref, x_ref, w1_hbm, b1_ref, w2_hbm, b2_ref, y_ref, w1s, w2s, w1b, w2b, sem):
    i = pl.program_id(0)
    e = te_ref[i]
    prev = te_ref[jnp.maximum(i - 1, 0)]
    first_of_run = (i == 0) | (e != prev)

    def stage(expert):
        return (pltpu.make_async_copy(w1_hbm.at[expert], w1s, sem.at[0]),
                pltpu.make_async_copy(w2_hbm.at[expert], w2s, sem.at[1]))

    @pl.when(i == 0)
    def _():
        for cp in stage(e):
            cp.start()

    @pl.when(first_of_run)
    def _():
        for cp in stage(e):
            cp.wait()
        w1b[...] = w1s[...].astype(BF16)
        w2b[...] = w2s[...].astype(BF16)

        @pl.when(nx_ref[i] >= 0)
        def _():
            for cp in stage(nx_ref[i]):
                cp.start()

    @pl.when(i < nt_ref[0])
    def _():
        dff = w2b.shape[0]
        x_lo, x_hi = _unpack_halves(x_ref[...])
        half = x_ref.shape[1]
        gu = (jnp.dot(x_lo.astype(BF16), w1b[:half, :], preferred_element_type=F32)
              + jnp.dot(x_hi.astype(BF16), w1b[half:, :], preferred_element_type=F32) + b1_ref[0])
        gate = jnp.minimum(gu[:, :dff], SWIGLU_LIMIT)
        up = jnp.clip(gu[:, dff:], -SWIGLU_LIMIT, SWIGLU_LIMIT)
        glu = gate * jax.nn.sigmoid(SWIGLU_ALPHA * gate)
        act = ((up + 1.0) * glu).astype(BF16)
        y = jnp.dot(act, w2b[...], preferred_element_type=F32) + b2_ref[0]
        y_ref[...] = _pack_halves(y)


def _moe(tile_expert, n_tiles, next_expert, xs_sorted, w1, b1, w2, b2):
    p, dh = xs_sorted.shape
    ne, d, dff2 = w1.shape
    dff = w2.shape[1]
    tm = MOE_TILE
    max_tiles = p // tm

    def row_map(i, te, nt, nx):
        return (jnp.minimum(i, nt[0] - 1), 0)

    def b_map(i, te, nt, nx):
        return (te[i], 0, 0)

    grid_spec = pltpu.PrefetchScalarGridSpec(
        num_scalar_prefetch=3,
        grid=(max_tiles,),
        in_specs=[pl.BlockSpec((tm, dh), row_map),
                  pl.BlockSpec(memory_space=pl.ANY),
                  pl.BlockSpec((1, 1, dff2), b_map),
                  pl.BlockSpec(memory_space=pl.ANY),
                  pl.BlockSpec((1, 1, d), b_map)],
        out_specs=pl.BlockSpec((tm, dh), row_map),
        scratch_shapes=[pltpu.VMEM((d, dff2), F32), pltpu.VMEM((dff, d), F32),
                        pltpu.VMEM((d, dff2), BF16), pltpu.VMEM((dff, d), BF16),
                        pltpu.SemaphoreType.DMA((2,))],
    )
    return pl.pallas_call(
        _moe_kernel,
        out_shape=jax.ShapeDtypeStruct((p, dh), jnp.int32),
        grid_spec=grid_spec,
        compiler_params=_cparams(1),
        name="moe",
    )(tile_expert, n_tiles, next_expert, xs_sorted, w1, b1.reshape(ne, 1, dff2), w2, b2.reshape(ne, 1, d))


def _sc_mesh():
    return plsc.VectorSubcoreMesh(core_axis_name="c", subcore_axis_name="s",
                                  num_cores=SC_CORES, num_subcores=SC_SUBCORES)


def _sc_dispatch(h, idx, p_rows):
    n, d = h.shape
    nw, items, chunk = idx.shape
    tok_w = n // nw
    n_chunks = items // TOP_K

    @functools.partial(
        pl.kernel, mesh=_sc_mesh(),
        out_type=jax.ShapeDtypeStruct((p_rows, d), h.dtype),
        scratch_types=[pltpu.VMEM((items, chunk), jnp.int32),
                       pltpu.VMEM((2, chunk, d), h.dtype),
                       pltpu.SemaphoreType.DMA((2,)), pltpu.SemaphoreType.DMA((2,))],
        name="sc_dispatch",
    )
    def run(h_hbm, idx_hbm, out_hbm, idx_v, rows_v, lsem, ssem):
        wid = lax.axis_index("s") * SC_CORES + lax.axis_index("c")
        base = wid * tok_w
        pltpu.sync_copy(idx_hbm.at[wid], idx_v)

        def load(j, slot):
            return pltpu.make_async_copy(h_hbm.at[pl.ds(base + j * chunk, chunk)], rows_v.at[slot], lsem.at[slot])

        def scat(j, kk, slot):
            return pltpu.make_async_copy(rows_v.at[slot], out_hbm.at[idx_v.at[j * TOP_K + kk]], ssem.at[slot])

        load(0, 0).start()

        @pl.loop(0, n_chunks, step=2)
        def _(j0):
            for b in range(2):
                j = j0 + b

                @pl.when(j >= 1)
                def _():
                    for kk in range(TOP_K):
                        scat(j - 1, kk, 1 - b).wait()

                @pl.when(j + 1 < n_chunks)
                def _():
                    load(j + 1, 1 - b).start()

                load(j, b).wait()
                for kk in range(TOP_K):
                    scat(j, kk, b).start()

        for kk in range(TOP_K):
            scat(n_chunks - 1, kk, (n_chunks - 1) % 2).wait()

    return run(h, idx)


def _sc_gather(y, idx, n):
    d = y.shape[1]
    nw, items, chunk = idx.shape
    tok_w = n // nw

    @functools.partial(
        pl.kernel, mesh=_sc_mesh(),
        out_type=jax.ShapeDtypeStruct((TOP_K, n, d), y.dtype),
        scratch_types=[pltpu.VMEM((items, chunk), jnp.int32),
                       pltpu.VMEM((2, chunk, d), y.dtype),
                       pltpu.SemaphoreType.DMA((2,)), pltpu.SemaphoreType.DMA((2,))],
        name="sc_gather",
    )
    def run(y_hbm, idx_hbm, out_hbm, idx_v, rows_v, gsem, wsem):
        wid = lax.axis_index("s") * SC_CORES + lax.axis_index("c")
        base = wid * tok_w
        pltpu.sync_copy(idx_hbm.at[wid], idx_v)

        def gather(i, slot):
            return pltpu.make_async_copy(y_hbm.at[idx_v.at[i]], rows_v.at[slot], gsem.at[slot])

        def write(i, slot):
            dst = out_hbm.at[i % TOP_K, pl.ds(base + (i // TOP_K) * chunk, chunk)]
            return pltpu.make_async_copy(rows_v.at[slot], dst, wsem.at[slot])

        gather(0, 0).start()

        @pl.loop(0, items, step=2)
        def _(i0):
            for b in range(2):
                i = i0 + b

                @pl.when(i >= 1)
                def _():
                    write(i - 1, 1 - b).wait()

                @pl.when(i + 1 < items)
                def _():
                    gather(i + 1, 1 - b).start()

                gather(i, b).wait()
                write(i, b).start()

        write(items - 1, (items - 1) % 2).wait()

    return run(y, idx)


def _combine_kernel(n_ctx_tiles, x1_ref, yg_ref, wts_ref, mod_ref, op_ref, os_ref):
    t = pl.program_id(0)
    gt2 = mod_ref[0, 5:6, :]
    w = wts_ref[...]
    acc_lo, acc_hi = None, None
    for kk in range(TOP_K):
        y_lo, y_hi = _unpack_halves(yg_ref[kk])
        wk = w[:, kk:kk + 1]
        acc_lo = wk * y_lo if acc_lo is None else acc_lo + wk * y_lo
        acc_hi = wk * y_hi if acc_hi is None else acc_hi + wk * y_hi
    out = x1_ref[...] + gt2 * jnp.concatenate([acc_lo, acc_hi], axis=1)

    @pl.when(t < n_ctx_tiles)
    def _():
        op_ref[...] = out

    @pl.when(t >= n_ctx_tiles)
    def _():
        os_ref[...] = out


def _combine(x1, yg, wts, mod, n_ctx, tiles_per_lat_batch):
    n, d = x1.shape
    tm = TOK_TILE
    n_ctx_tiles = n_ctx // tm
    last = n_ctx_tiles - 1

    def mod_row(t):
        return jnp.where(t < n_ctx_tiles, 0, 1 + (t - n_ctx_tiles) // tiles_per_lat_batch)

    return pl.pallas_call(
        functools.partial(_combine_kernel, n_ctx_tiles),
        out_shape=(jax.ShapeDtypeStruct((n_ctx, d), F32), jax.ShapeDtypeStruct((n - n_ctx, d), F32)),
        grid=(n // tm,),
        in_specs=[pl.BlockSpec((tm, d), lambda t: (t, 0)),
                  pl.BlockSpec((TOP_K, tm, d // 2), lambda t: (0, t, 0)),
                  pl.BlockSpec((tm, TOP_K), lambda t: (t, 0)),
                  pl.BlockSpec((1,) + mod.shape[1:], lambda t: (mod_row(t), 0, 0))],
        out_specs=(pl.BlockSpec((tm, d), lambda t: (jnp.minimum(t, last), 0)),
                   pl.BlockSpec((tm, d), lambda t: (jnp.maximum(t - n_ctx_tiles, 0), 0))),
        compiler_params=_cparams(1),
        name="combine",
    )(x1, yg, wts, mod)


def kernel(x_prompt, x_sample, cache_k, cache_v, c, c_ctx, g_norm1, w_ada, b_ada, w_in, g_q, g_k, rpb,
           w_fmap, w_amap, w_gate, b_gate, w_out, g_norm2, w_router, b_router, w1, b1, w2, b2):
    batch, seq, d = x_prompt.shape
    dec_batch, dec_seq, _ = x_sample.shape
    assert w_ada.shape[0] == 1, "single-layer trunk"
    rows = dec_seq // GRID_W
    da = N_HEADS * HEAD_DIM
    n_ctx = batch * seq
    n_lat = dec_batch * dec_seq
    n = n_ctx + n_lat
    ne = w_router.shape[2]
    assert TOK_TILE % seq == 0 and n_ctx % TOK_TILE == 0 and dec_seq % TOK_TILE == 0
    assert rows % (ATT_TILE // GRID_W) == 0 and rows >= 3 * (ATT_TILE // GRID_W) and rows >= WIN_H_MAX
    assert seq == ATT_TILE and cache_k.shape[3] == ATT_TILE
    tiles_per_lat_batch = dec_seq // TOK_TILE

    n_mod_rows = 8
    cvecs = jnp.zeros((n_mod_rows, d), F32).at[0].set(c_ctx).at[1:1 + dec_batch].set(c)
    mod = _ada(cvecs, w_ada[0], b_ada[0]).reshape(n_mod_rows, 6, d)

    xp = x_prompt.reshape(n_ctx, d)
    xs = x_sample.reshape(n_lat, d)
    g1 = g_norm1[0].reshape(1, d)
    g2 = g_norm2[0].reshape(1, d)
    hsum = jnp.asarray(np.kron(np.eye(N_HEADS), np.full((HEAD_DIM, HEAD_DIM), 1.0 / HEAD_DIM)), BF16)
    gq_t = jnp.tile(g_q[0], N_HEADS).reshape(1, da)
    gk_t = jnp.tile(g_k[0], N_HEADS).reshape(1, da)

    u, q, k, v, new_k, new_v = _proj(xp, xs, mod, g1, w_in[0].astype(BF16), hsum, gq_t, gk_t,
                                      batch, seq, tiles_per_lat_batch)

    aa_c = _attn_ctx(q, k, v, batch, seq)
    er, ec, valid = _window_tables(rows)
    bias = jnp.einsum("vqka,hab,xyb->vhqxky", er, rpb[0], ec, precision=lax.Precision.HIGHEST)
    bias = jnp.where(valid[:, None], bias, NEG_BIG).reshape(3, N_HEADS, ATT_TILE, 3 * ATT_TILE)
    past = cache_k.shape[3]
    kc = cache_k[:, 0].transpose(0, 2, 1, 3).reshape(dec_batch, past, da).astype(BF16)
    vc = cache_v[:, 0].transpose(0, 2, 1, 3).reshape(dec_batch, past, da).astype(BF16)
    aa_l = _attn_lat(q, k, v, kc, vc, bias, n_ctx, dec_batch, rows)

    w1c, ct_c, st_c = _dft_tables(seq)
    _, ct_l, st_l = _dft_tables(dec_seq)
    w1c = jnp.asarray(w1c).astype(BF16)
    af_c = _fourier(u, w1c, jnp.asarray(ct_c).astype(BF16), jnp.asarray(st_c).astype(BF16),
                    batch, seq, 0, seq)
    af_l = _fourier(u, w1c, jnp.asarray(ct_l).astype(BF16), jnp.asarray(st_l).astype(BF16),
                    dec_batch, dec_seq, n_ctx // dec_seq, TOK_TILE)

    x1, h2, idx, wts, rank, counts = _merge(
        xp, xs, mod, g1, g2, af_c, af_l, aa_c, aa_l,
        w_fmap[0].astype(BF16), w_amap[0].astype(BF16), w_gate[0].astype(BF16), b_gate[0].reshape(1, -1),
        w_out[0].astype(BF16), w_router[0], b_router[0].reshape(1, ne), tiles_per_lat_batch)

    tm = MOE_TILE
    max_tiles = (n * TOP_K) // tm + ne
    cnt = counts[0].astype(jnp.int32)
    tiles_e = (cnt + tm - 1) // tm
    tile_end = jnp.cumsum(tiles_e)
    pad_off = (tile_end - tiles_e) * tm
    n_tiles = tile_end[-1:]
    tile_ids = jnp.arange(max_tiles, dtype=jnp.int32)
    tile_expert = jnp.sum((tile_ids[:, None] >= tile_end[None, :]).astype(jnp.int32), axis=1)
    last_e = jnp.sum((n_tiles - 1 >= tile_end).astype(jnp.int32))
    tile_expert = jnp.minimum(tile_expert, last_e).astype(jnp.int32)
    pos = jnp.sum(jnp.where(idx[:, :, None] == jnp.arange(ne)[None, None, :], pad_off[None, None, :], 0),
                  axis=-1) + rank

    p_rows = max_tiles * tm
    n_workers = SC_CORES * SC_SUBCORES
    assert n % (n_workers * SC_CHUNK * 2) == 0
    chunks = n // (n_workers * SC_CHUNK)
    pos_w = pos.astype(jnp.int32).reshape(n_workers, chunks, SC_CHUNK, TOP_K).transpose(0, 1, 3, 2)
    pos_w = pos_w.reshape(n_workers, chunks * TOP_K, SC_CHUNK)
    xs_sorted = _sc_dispatch(h2, pos_w, p_rows)
    e_ids = jnp.arange(ne, dtype=jnp.int32)
    later_used = (e_ids[None, :] > e_ids[:, None]) & (tiles_e[None, :] > 0)
    next_of_e = jnp.min(jnp.where(later_used, e_ids[None, :], ne), axis=1)
    next_of_e = jnp.where(next_of_e == ne, -1, next_of_e)
    next_expert = jnp.sum(jnp.where(tile_expert[:, None] == e_ids[None, :], next_of_e[None, :], 0), axis=1)
    y_sorted = _moe(tile_expert, n_tiles.astype(jnp.int32), next_expert.astype(jnp.int32), xs_sorted,
                    w1[0], b1[0], w2[0], b2[0])
    yg = _sc_gather(y_sorted, pos_w, n)

    y_p, y_s = _combine(x1, yg, wts, mod, n_ctx, tiles_per_lat_batch)
    return (y_p.reshape(batch, seq, d), y_s.reshape(dec_batch, dec_seq, d), new_k, new_v)
```

```python
import functools
import math

import numpy as np
import jax
import jax.numpy as jnp
from jax import lax
from jax.experimental import pallas as pl
from jax.experimental.pallas import tpu as pltpu
from jax.experimental.pallas import tpu_sc as plsc

F32 = jnp.float32
BF16 = jnp.bfloat16

GRID_W = 64
N_HEADS = 8
HEAD_DIM = 64
N_FGROUPS = 4
FGROUP_DIM = 128
WIN_H_MAX = 8
WIN_W = 16
TOP_K = 4
SWIGLU_LIMIT = 7.0
SWIGLU_ALPHA = 1.702
EPS = 1e-6

LANES = 128
TOK_TILE = 512
ATT_TILE = 256
MOE_TILE = 1024
MOE_SUB = 256
MERGE_PARTS = 1
VMEM_LIMIT = 56 * 1024 * 1024
NEG_BIG = -1e30

SC_CORES = 2
SC_SUBCORES = 16
SC_CHUNK = 64


def _cparams(n_axes, vmem=VMEM_LIMIT):
    return pltpu.CompilerParams(dimension_semantics=("arbitrary",) * n_axes, vmem_limit_bytes=vmem)


def _rms(x):
    return x * lax.rsqrt(jnp.mean(x * x, axis=-1, keepdims=True) + EPS)


def _pack_halves(x):
    c = x.shape[1] // 2
    lo = lax.bitcast_convert_type(x[:, :c].astype(BF16).astype(F32), jnp.uint32)
    hi = lax.bitcast_convert_type(x[:, c:].astype(BF16).astype(F32), jnp.uint32)
    return lax.bitcast_convert_type(hi | (lo >> 16), jnp.int32)


def _unpack_halves(w):
    u = lax.bitcast_convert_type(w, jnp.uint32)
    lo = lax.bitcast_convert_type(u << 16, F32)
    hi = lax.bitcast_convert_type(u & jnp.uint32(0xFFFF0000), F32)
    return lo, hi


def _ada_kernel(c_ref, w_ref, b_ref, o_ref):
    cv = c_ref[...]
    s = cv * jax.nn.sigmoid(cv)
    w = w_ref[...]
    s_hi = s.astype(BF16)
    s_lo = (s - s_hi.astype(F32)).astype(BF16)
    w_hi = w.astype(BF16)
    w_lo = (w - w_hi.astype(F32)).astype(BF16)
    o_ref[...] = (jnp.dot(s_hi, w_hi, preferred_element_type=F32)
                  + jnp.dot(s_hi, w_lo, preferred_element_type=F32)
                  + jnp.dot(s_lo, w_hi, preferred_element_type=F32) + b_ref[...])


def _ada(cvecs, w_ada, b_ada):
    rows, d = cvecs.shape
    n = w_ada.shape[1]
    blk = 1024
    return pl.pallas_call(
        _ada_kernel,
        out_shape=jax.ShapeDtypeStruct((rows, n), F32),
        grid=(n // blk,),
        in_specs=[pl.BlockSpec((rows, d), lambda j: (0, 0)),
                  pl.BlockSpec((d, blk), lambda j: (0, j)),
                  pl.BlockSpec((1, blk), lambda j: (0, j))],
        out_specs=pl.BlockSpec((rows, blk), lambda j: (0, j)),
        compiler_params=_cparams(1),
        name="ada",
    )(cvecs, w_ada, b_ada.reshape(1, n))


def _proj_kernel(n_ctx_tiles, xp_ref, xs_ref, mod_ref, g1_ref, win_ref, hsum_ref, gq_ref, gk_ref,
                 u_ref, q_ref, k_ref, v_ref, nk_ref, nv_ref):
    t = pl.program_id(0)
    is_ctx = t < n_ctx_tiles
    x = jnp.where(is_ctx, xp_ref[...], xs_ref[...])
    sh1 = mod_ref[0, 0:1, :]
    sc1 = mod_ref[0, 1:2, :]
    h = _rms(x) * g1_ref[...] * (1.0 + sc1) + sh1
    proj = jnp.dot(h.astype(BF16), win_ref[...], preferred_element_type=F32)
    da = q_ref.shape[1]
    df = u_ref.shape[1]
    u_ref[...] = proj[:, :df].astype(BF16)
    q = proj[:, df:df + da]
    k = proj[:, df + da:df + 2 * da]
    v = proj[:, df + 2 * da:]
    msq = jnp.dot((q * q).astype(BF16), hsum_ref[...], preferred_element_type=F32)
    msk = jnp.dot((k * k).astype(BF16), hsum_ref[...], preferred_element_type=F32)
    qn = q * lax.rsqrt(msq + EPS) * gq_ref[...]
    kn = k * lax.rsqrt(msk + EPS) * gk_ref[...]
    q_ref[...] = qn.astype(BF16)
    k_ref[...] = kn.astype(BF16)
    v_ref[...] = v.astype(BF16)

    @pl.when(is_ctx)
    def _():
        nb, _, nh, s, dh = nk_ref.shape
        for b in range(nb):
            for hd in range(nh):
                nk_ref[b, 0, hd] = kn[b * s:(b + 1) * s, hd * dh:(hd + 1) * dh]
                nv_ref[b, 0, hd] = v[b * s:(b + 1) * s, hd * dh:(hd + 1) * dh]


def _proj(xp, xs, mod, g1, w_in_b, hsum, gq_t, gk_t, batch, seq, tiles_per_lat_batch):
    n_ctx, d = xp.shape
    n_lat = xs.shape[0]
    tm = TOK_TILE
    n_ctx_tiles = n_ctx // tm
    n_tiles = (n_ctx + n_lat) // tm
    n = n_ctx + n_lat
    d_in = w_in_b.shape[1]
    da = N_HEADS * HEAD_DIM
    df = d_in - 3 * da
    bpt = tm // seq
    last = n_ctx_tiles - 1

    def mod_row(t):
        return jnp.where(t < n_ctx_tiles, 0, 1 + (t - n_ctx_tiles) // tiles_per_lat_batch)

    tok = lambda w: pl.BlockSpec((tm, w), lambda t: (t, 0))
    full = lambda a: pl.BlockSpec(a.shape, lambda t: (0,) * a.ndim)
    kv_spec = pl.BlockSpec((bpt, 1, N_HEADS, seq, HEAD_DIM), lambda t: (jnp.minimum(t, last), 0, 0, 0, 0))
    kv_shape = jax.ShapeDtypeStruct((batch, 1, N_HEADS, seq, HEAD_DIM), F32)
    return pl.pallas_call(
        functools.partial(_proj_kernel, n_ctx_tiles),
        out_shape=(jax.ShapeDtypeStruct((n, df), BF16),) + (jax.ShapeDtypeStruct((n, da), BF16),) * 3
        + (kv_shape, kv_shape),
        grid=(n_tiles,),
        in_specs=[pl.BlockSpec((tm, d), lambda t: (jnp.minimum(t, last), 0)),
                  pl.BlockSpec((tm, d), lambda t: (jnp.maximum(t - n_ctx_tiles, 0), 0)),
                  pl.BlockSpec((1,) + mod.shape[1:], lambda t: (mod_row(t), 0, 0)),
                  full(g1), full(w_in_b), full(hsum), full(gq_t), full(gk_t)],
        out_specs=(tok(df), tok(da), tok(da), tok(da), kv_spec, kv_spec),
        compiler_params=_cparams(1),
        name="proj",
    )(xp, xs, mod, g1, w_in_b, hsum, gq_t, gk_t)


def _softmax_pv(q2, key_blocks, val_blocks, bias_blocks, scale):
    lane = lax.broadcasted_iota(jnp.int32, (1, LANES), 1)
    assert math.frexp(scale)[0] == 0.5, "a power-of-two scale is exact in bf16 and commutes with the dot"
    q2 = q2 * jnp.asarray(scale, q2.dtype)
    out = None
    for half in range(2):
        hmask = (lane < HEAD_DIM) if half == 0 else (lane >= HEAD_DIM)
        qh = jnp.where(hmask, q2, jnp.zeros_like(q2))
        scores = []
        for kb, bb in zip(key_blocks, bias_blocks):
            s = lax.dot_general(qh, kb, (((1,), (1,)), ((), ())), preferred_element_type=F32)
            if bb is not None:
                s = s + bb[half]
            scores.append(s)
        m = jnp.max(functools.reduce(jnp.maximum, scores), axis=-1, keepdims=True)
        ps = [jnp.exp(s - m) for s in scores]
        denom = jnp.sum(functools.reduce(lambda a, b: a + b, ps), axis=-1, keepdims=True)
        o = functools.reduce(lambda a, b: a + b,
                             [jnp.dot(p.astype(BF16), vb, preferred_element_type=F32)
                              for p, vb in zip(ps, val_blocks)])
        o = o / denom
        out = o if out is None else jnp.where(hmask, o, out)
    return out


def _attn_ctx_kernel(q_ref, k_ref, v_ref, o_ref):
    scale = HEAD_DIM ** -0.5
    for p in range(q_ref.shape[1] // LANES):
        sl = slice(p * LANES, (p + 1) * LANES)
        o = _softmax_pv(q_ref[:, sl], [k_ref[:, sl]], [v_ref[:, sl]], [None], scale)
        o_ref[:, sl] = o.astype(BF16)


def _attn_ctx(q, k, v, batch, seq):
    da = q.shape[1]
    spec = pl.BlockSpec((seq, da), lambda b: (b, 0))
    return pl.pallas_call(
        _attn_ctx_kernel,
        out_shape=jax.ShapeDtypeStruct((batch * seq, da), BF16),
        grid=(batch,),
        in_specs=[spec, spec, spec],
        out_specs=spec,
        compiler_params=_cparams(1),
        name="attn_ctx",
    )(q, k, v)


def _attn_lat_kernel(q_ref, k0_ref, k1_ref, k2_ref, v0_ref, v1_ref, v2_ref, kc_ref, vc_ref, bias_ref, o_ref):
    scale = HEAD_DIM ** -0.5
    tk = k0_ref.shape[0]
    for p in range(q_ref.shape[1] // LANES):
        sl = slice(p * LANES, (p + 1) * LANES)
        keys = [k0_ref[:, sl], k1_ref[:, sl], k2_ref[:, sl], kc_ref[0, :, sl]]
        vals = [v0_ref[:, sl], v1_ref[:, sl], v2_ref[:, sl], vc_ref[0, :, sl]]
        biases = [[bias_ref[0, 2 * p + half, :, d * tk:(d + 1) * tk] for half in range(2)] for d in range(3)]
        o = _softmax_pv(q_ref[:, sl], keys, vals, biases + [None], scale)
        o_ref[:, sl] = o.astype(BF16)


def _attn_lat(q, k, v, kc, vc, bias, n_ctx, dec_batch, rows):
    da = q.shape[1]
    tq = ATT_TILE
    rows_per_tile = tq // GRID_W
    tiles = rows // rows_per_tile
    base = n_ctx // tq
    n_lat = dec_batch * rows * GRID_W
    max_start = tiles - 3

    def qmap(j, b):
        return (base + b * tiles + j, 0)

    def kmap(d):
        return lambda j, b: (base + b * tiles + jnp.clip(j - 1, 0, max_start) + d, 0)

    def bmap(j, b):
        return (jnp.where(j == 0, 0, jnp.where(j == tiles - 1, 2, 1)), 0, 0, 0)

    blk = lambda m: pl.BlockSpec((tq, da), m)
    cspec = pl.BlockSpec((1,) + kc.shape[1:], lambda j, b: (b, 0, 0))
    return pl.pallas_call(
        _attn_lat_kernel,
        out_shape=jax.ShapeDtypeStruct((n_lat, da), BF16),
        grid=(tiles, dec_batch),
        in_specs=[blk(qmap)] + [blk(kmap(d)) for d in range(3)] + [blk(kmap(d)) for d in range(3)]
        + [cspec, cspec, pl.BlockSpec((1,) + bias.shape[1:], bmap)],
        out_specs=pl.BlockSpec((tq, da), lambda j, b: (b * tiles + j, 0)),
        compiler_params=_cparams(2),
        name="attn_lat",
    )(q, k, k, k, v, v, v, kc, vc, bias)


def _window_tables(rows):
    rpt = ATT_TILE // GRID_W
    krows = 3 * rpt
    tiles = rows // rpt
    kh = min(WIN_H_MAX, rows)
    rpb_w = 2 * WIN_W - 1
    arow = -np.ones((3, rpt, krows), np.int64)
    for var, j in enumerate((0, 1, tiles - 1)):
        r0 = j * rpt
        k0 = int(np.clip(j - 1, 0, tiles - 3)) * rpt
        for qi in range(rpt):
            qrow = r0 + qi
            rs = int(np.clip(qrow - kh // 2, 0, rows - kh))
            for ki in range(krows):
                krow = k0 + ki
                if rs <= krow < rs + kh:
                    arow[var, qi, ki] = krow - qrow + WIN_H_MAX - 1
    ec = np.zeros((GRID_W, GRID_W, rpb_w), np.float32)
    for qc in range(GRID_W):
        cs = int(np.clip(qc - WIN_W // 2, 0, GRID_W - WIN_W))
        for kc in range(cs, cs + WIN_W):
            ec[qc, kc, int(np.clip(kc - qc, -(WIN_W - 1), WIN_W - 1)) + WIN_W - 1] = 1.0
    return arow.tolist(), ec


def _bias_kernel(arow, u_ref, o_ref):
    var_id = pl.program_id(0)
    for var, table in enumerate(arow):
        @pl.when(var_id == var)
        def _(table=table):
            for qi, row in enumerate(table):
                for ki, a in enumerate(row):
                    blk = u_ref[0, a] if a >= 0 else jnp.full((GRID_W, GRID_W), NEG_BIG, F32)
                    o_ref[0, 0, qi * GRID_W:(qi + 1) * GRID_W, ki * GRID_W:(ki + 1) * GRID_W] = blk


def _window_bias(rpb, rows):
    arow, ec = _window_tables(rows)
    nh = rpb.shape[0]
    col_blocks = jnp.einsum("hab,xyb->haxy", rpb, ec, precision=lax.Precision.HIGHEST)
    col_blocks = jnp.where(ec.sum(-1) > 0.5, col_blocks, NEG_BIG)
    return pl.pallas_call(
        functools.partial(_bias_kernel, arow),
        out_shape=jax.ShapeDtypeStruct((len(arow), nh, ATT_TILE, 3 * ATT_TILE), F32),
        grid=(len(arow), nh),
        in_specs=[pl.BlockSpec((1,) + col_blocks.shape[1:], lambda v, h: (h, 0, 0, 0))],
        out_specs=pl.BlockSpec((1, 1, ATT_TILE, 3 * ATT_TILE), lambda v, h: (v, h, 0, 0)),
        compiler_params=_cparams(2),
        name="window_bias",
    )(col_blocks)


def _fourier_kernel(u_ref, w1_ref, ct_ref, st_ref, o_ref, p_scr, q_scr):
    @pl.when(pl.program_id(1) == 0)
    def _():
        pq = jnp.dot(u_ref[...], w1_ref[...], preferred_element_type=F32)
        df = p_scr.shape[1]
        p_scr[...] = pq[:, :df].astype(BF16)
        q_scr[...] = pq[:, df:].astype(BF16)

    o = (jnp.dot(ct_ref[...], p_scr[...], preferred_element_type=F32)
         - jnp.dot(st_ref[...], q_scr[...], preferred_element_type=F32))
    o_ref[...] = o.astype(BF16)


def _fourier(u, w1, ct, st, batch, t_len, first_block, row_tile):
    df = u.shape[1]
    steps = t_len // row_tile
    return pl.pallas_call(
        _fourier_kernel,
        out_shape=jax.ShapeDtypeStruct((batch * t_len, df), BF16),
        grid=(batch, steps),
        in_specs=[pl.BlockSpec((t_len, df), lambda b, i: (first_block + b, 0)),
                  pl.BlockSpec(w1.shape, lambda b, i: (0, 0)),
                  pl.BlockSpec((row_tile, t_len), lambda b, i: (i, 0)),
                  pl.BlockSpec((row_tile, t_len), lambda b, i: (i, 0))],
        out_specs=pl.BlockSpec((row_tile, df), lambda b, i: (b * steps + i, 0)),
        scratch_shapes=[pltpu.VMEM((t_len, df), BF16), pltpu.VMEM((t_len, df), BF16)],
        compiler_params=_cparams(2),
        name=f"fourier_{t_len}",
    )(u, w1, ct, st)


def _dft_tables(t_len):
    c = FGROUP_DIM
    jk = np.outer(np.arange(c), np.arange(c)) % c
    ang = 2.0 * np.pi * jk / c
    eye = np.eye(N_FGROUPS)
    w1 = np.concatenate([np.kron(eye, np.cos(ang)), np.kron(eye, np.sin(ang))], axis=1) / np.sqrt(c)
    tt = np.outer(np.arange(t_len), np.arange(t_len)) % t_len
    angt = 2.0 * np.pi * tt / t_len
    return (w1.astype(np.float32), (np.cos(angt) / np.sqrt(t_len)).astype(np.float32),
            (np.sin(angt) / np.sqrt(t_len)).astype(np.float32))


def _merge_kernel(n_ctx_tiles, xp_ref, xs_ref, mod_ref, g1_ref, g2_ref, afc_ref, afl_ref, aac_ref, aal_ref,
                  wf_ref, wa_ref, wg_ref, bg_ref, wo_ref, wrh_ref, wrl_ref, br_ref,
                  x1_ref, h2_ref, idx_ref, wts_ref, rank_ref, cnt_ref, carry):
    t = pl.program_id(0)
    is_ctx = t < n_ctx_tiles

    @pl.when(t == 0)
    def _():
        carry[...] = jnp.zeros_like(carry)

    sh1 = mod_ref[0, 0:1, :]
    sc1 = mod_ref[0, 1:2, :]
    gt1 = mod_ref[0, 2:3, :]
    sh2 = mod_ref[0, 3:4, :]
    sc2 = mod_ref[0, 4:5, :]
    d = xp_ref.shape[1]
    part_rows = xp_ref.shape[0] // MERGE_PARTS
    logit_parts = []
    for part in range(MERGE_PARTS):
        rs = slice(part * part_rows, (part + 1) * part_rows)
        x = jnp.where(is_ctx, xp_ref[rs, :], xs_ref[rs, :])
        a_four = jnp.where(is_ctx, afc_ref[rs, :], afl_ref[rs, :])
        a_attn = jnp.where(is_ctx, aac_ref[rs, :], aal_ref[rs, :])
        hb = (_rms(x) * g1_ref[...] * (1.0 + sc1) + sh1).astype(BF16)
        gates = jax.nn.sigmoid(jnp.dot(hb, wg_ref[...], preferred_element_type=F32) + bg_ref[...])
        fa = jnp.dot(a_four, wf_ref[...], preferred_element_type=F32)
        fb = jnp.dot(a_attn, wa_ref[...], preferred_element_type=F32)
        mix = gates[:, :d] * fa + gates[:, d:] * fb
        x1 = x + gt1 * jnp.dot(mix.astype(BF16), wo_ref[...], preferred_element_type=F32)
        x1_ref[rs, :] = x1
        h2 = _rms(x1) * g2_ref[...] * (1.0 + sc2) + sh2
        h2_ref[rs, :] = _pack_halves(h2)
        h2_hi = h2.astype(BF16)
        h2_lo = (h2 - h2_hi.astype(F32)).astype(BF16)
        logit_parts.append(jnp.dot(h2_hi, wrh_ref[...], preferred_element_type=F32)
                           + jnp.dot(h2_hi, wrl_ref[...], preferred_element_type=F32)
                           + jnp.dot(h2_lo, wrh_ref[...], preferred_element_type=F32))

    logits = jnp.concatenate(logit_parts, axis=0) + br_ref[...]
    tm, ne = logits.shape
    col = lax.broadcasted_iota(jnp.int32, (tm, ne), 1)
    colk = lax.broadcasted_iota(jnp.int32, (tm, TOP_K), 1)
    lg = logits
    vals, idxs = [], []
    for _ in range(TOP_K):
        m = jnp.max(lg, axis=-1, keepdims=True)
        am = jnp.min(jnp.where(lg == m, col, ne), axis=-1, keepdims=True)
        vals.append(m)
        idxs.append(am)
        lg = jnp.where(col == am, -jnp.inf, lg)
    es = [jnp.exp(v - vals[0]) for v in vals]
    den = functools.reduce(lambda a, b: a + b, es)
    onehot = functools.reduce(lambda a, b: a + b, [(col == am).astype(F32) for am in idxs])
    r_i = lax.broadcasted_iota(jnp.int32, (tm, tm), 0)
    c_i = lax.broadcasted_iota(jnp.int32, (tm, tm), 1)
    lower = (r_i > c_i).astype(BF16)
    before = jnp.dot(lower, onehot.astype(BF16), preferred_element_type=F32) + carry[...]
    idx_o = jnp.zeros((tm, TOP_K), jnp.int32)
    wts_o = jnp.zeros((tm, TOP_K), F32)
    rank_o = jnp.zeros((tm, TOP_K), jnp.int32)
    for kk in range(TOP_K):
        rk = jnp.sum(jnp.where(col == idxs[kk], before, 0.0), axis=-1, keepdims=True).astype(jnp.int32)
        idx_o = jnp.where(colk == kk, idxs[kk], idx_o)
        wts_o = jnp.where(colk == kk, es[kk] / den, wts_o)
        rank_o = jnp.where(colk == kk, rk, rank_o)
    idx_ref[...] = idx_o
    wts_ref[...] = wts_o
    rank_ref[...] = rank_o
    carry[...] = carry[...] + jnp.sum(onehot, axis=0, keepdims=True)
    cnt_ref[...] = carry[...]


def _merge(xp, xs, mod, g1, g2, af_c, af_l, aa_c, aa_l, wf, wa, wg, bg, wo, wr, br, tiles_per_lat_batch):
    n_ctx, d = xp.shape
    n = n_ctx + xs.shape[0]
    tm = TOK_TILE
    n_ctx_tiles = n_ctx // tm
    last = n_ctx_tiles - 1
    ne = wr.shape[1]
    da = aa_c.shape[1]
    df = af_c.shape[1]

    def mod_row(t):
        return jnp.where(t < n_ctx_tiles, 0, 1 + (t - n_ctx_tiles) // tiles_per_lat_batch)

    cmap = lambda t: (jnp.minimum(t, last), 0)
    lmap = lambda t: (jnp.maximum(t - n_ctx_tiles, 0), 0)
    full = lambda a: pl.BlockSpec(a.shape, lambda t: (0,) * a.ndim)
    tok = lambda w: pl.BlockSpec((tm, w), lambda t: (t, 0))
    wr_hi = wr.astype(BF16)
    wr_lo = (wr - wr_hi.astype(F32)).astype(BF16)
    return pl.pallas_call(
        functools.partial(_merge_kernel, n_ctx_tiles),
        out_shape=(jax.ShapeDtypeStruct((n, d), F32), jax.ShapeDtypeStruct((n, d // 2), jnp.int32),
                   jax.ShapeDtypeStruct((n, TOP_K), jnp.int32), jax.ShapeDtypeStruct((n, TOP_K), F32),
                   jax.ShapeDtypeStruct((n, TOP_K), jnp.int32), jax.ShapeDtypeStruct((1, ne), F32)),
        grid=(n // tm,),
        in_specs=[pl.BlockSpec((tm, d), cmap), pl.BlockSpec((tm, d), lmap),
                  pl.BlockSpec((1,) + mod.shape[1:], lambda t: (mod_row(t), 0, 0)),
                  full(g1), full(g2),
                  pl.BlockSpec((tm, df), cmap), pl.BlockSpec((tm, df), lmap),
                  pl.BlockSpec((tm, da), cmap), pl.BlockSpec((tm, da), lmap),
                  full(wf), full(wa), full(wg), full(bg), full(wo), full(wr_hi), full(wr_lo), full(br)],
        out_specs=(tok(d), tok(d // 2), tok(TOP_K), tok(TOP_K), tok(TOP_K),
                   pl.BlockSpec((1, ne), lambda t: (0, 0))),
        scratch_shapes=[pltpu.VMEM((1, ne), F32)],
        compiler_params=_cparams(1),
        name="merge",
    )(xp, xs, mod, g1, g2, af_c, af_l, aa_c, aa_l, wf, wa, wg, bg, wo, wr_hi, wr_lo, br)


def _moe_kernel(te_ref, nt_ref, nx_ref, rv_ref, x_ref, w1_hbm, b1_ref, w2_hbm, b2_ref, y_ref,
                w1s, w2s, w1b, w2b, sem):
    i = pl.program_id(0)
    e = te_ref[i]
    prev = te_ref[jnp.maximum(i - 1, 0)]
    first_of_run = (i == 0) | (e != prev)

    def stage(expert):
        return (pltpu.make_async_copy(w1_hbm.at[expert], w1s, sem.at[0]),
                pltpu.make_async_copy(w2_hbm.at[expert], w2s, sem.at[1]))

    @pl.when(i == 0)
    def _():
        for cp in stage(e):
            cp.start()

    @pl.when(first_of_run)
    def _():
        for cp in stage(e):
            cp.wait()
        w1b[...] = w1s[...].astype(BF16)
        w2b[...] = w2s[...].astype(BF16)

        @pl.when(nx_ref[i] >= 0)
        def _():
            for cp in stage(nx_ref[i]):
                cp.start()

    for sub in range(x_ref.shape[0] // MOE_SUB):
        @pl.when(rv_ref[i] > sub * MOE_SUB)
        def _(sub=sub):
            rs = slice(sub * MOE_SUB, (sub + 1) * MOE_SUB)
            dff = w2b.shape[0]
            x_lo, x_hi = _unpack_halves(x_ref[rs, :])
            half = x_ref.shape[1]
            gu = (jnp.dot(x_lo.astype(BF16), w1b[:half, :], preferred_element_type=F32)
                  + jnp.dot(x_hi.astype(BF16), w1b[half:, :], preferred_element_type=F32) + b1_ref[0])
            gate = jnp.minimum(gu[:, :dff], SWIGLU_LIMIT)
            up = jnp.clip(gu[:, dff:], -SWIGLU_LIMIT, SWIGLU_LIMIT)
            glu = gate * jax.nn.sigmoid(SWIGLU_ALPHA * gate)
            act = ((up + 1.0) * glu).astype(BF16)
            y = jnp.dot(act, w2b[...], preferred_element_type=F32) + b2_ref[0]
            y_ref[rs, :] = _pack_halves(y)


def _moe(tile_expert, n_tiles, next_expert, rows_valid, xs_sorted, w1, b1, w2, b2):
    p, dh = xs_sorted.shape
    ne, d, dff2 = w1.shape
    dff = w2.shape[1]
    tm = MOE_TILE
    max_tiles = p // tm

    def row_map(i, te, nt, nx, rv):
        return (jnp.minimum(i, nt[0] - 1), 0)

    def b_map(i, te, nt, nx, rv):
        return (te[i], 0, 0)

    grid_spec = pltpu.PrefetchScalarGridSpec(
        num_scalar_prefetch=4,
        grid=(max_tiles,),
        in_specs=[pl.BlockSpec((tm, dh), row_map),
                  pl.BlockSpec(memory_space=pl.ANY),
                  pl.BlockSpec((1, 1, dff2), b_map),
                  pl.BlockSpec(memory_space=pl.ANY),
                  pl.BlockSpec((1, 1, d), b_map)],
        out_specs=pl.BlockSpec((tm, dh), row_map),
        scratch_shapes=[pltpu.VMEM((d, dff2), F32), pltpu.VMEM((dff, d), F32),
                        pltpu.VMEM((d, dff2), BF16), pltpu.VMEM((dff, d), BF16),
                        pltpu.SemaphoreType.DMA((2,))],
    )
    return pl.pallas_call(
        _moe_kernel,
        out_shape=jax.ShapeDtypeStruct((p, dh), jnp.int32),
        grid_spec=grid_spec,
        compiler_params=_cparams(1),
        name="moe",
    )(tile_expert, n_tiles, next_expert, rows_valid, xs_sorted,
      w1, b1.reshape(ne, 1, dff2), w2, b2.reshape(ne, 1, d))


def _sc_mesh():
    return plsc.VectorSubcoreMesh(core_axis_name="c", subcore_axis_name="s",
                                  num_cores=SC_CORES, num_subcores=SC_SUBCORES)


def _sc_dispatch(h, idx, p_rows):
    n, d = h.shape
    nw, items, chunk = idx.shape
    tok_w = n // nw
    n_chunks = items // TOP_K

    @functools.partial(
        pl.kernel, mesh=_sc_mesh(),
        out_type=jax.ShapeDtypeStruct((p_rows, d), h.dtype),
        scratch_types=[pltpu.VMEM((items, chunk), jnp.int32),
                       pltpu.VMEM((2, chunk, d), h.dtype),
                       pltpu.SemaphoreType.DMA((2,)), pltpu.SemaphoreType.DMA((2,))],
        name="sc_dispatch",
    )
    def run(h_hbm, idx_hbm, out_hbm, idx_v, rows_v, lsem, ssem):
        wid = lax.axis_index("s") * SC_CORES + lax.axis_index("c")
        base = wid * tok_w
        pltpu.sync_copy(idx_hbm.at[wid], idx_v)

        def load(j, slot):
            return pltpu.make_async_copy(h_hbm.at[pl.ds(base + j * chunk, chunk)], rows_v.at[slot], lsem.at[slot])

        def scat(j, kk, slot):
            return pltpu.make_async_copy(rows_v.at[slot], out_hbm.at[idx_v.at[j * TOP_K + kk]], ssem.at[slot])

        load(0, 0).start()

        @pl.loop(0, n_chunks, step=2)
        def _(j0):
            for b in range(2):
                j = j0 + b

                @pl.when(j >= 1)
                def _():
                    for kk in range(TOP_K):
                        scat(j - 1, kk, 1 - b).wait()

                @pl.when(j + 1 < n_chunks)
                def _():
                    load(j + 1, 1 - b).start()

                load(j, b).wait()
                for kk in range(TOP_K):
                    scat(j, kk, b).start()

        for kk in range(TOP_K):
            scat(n_chunks - 1, kk, (n_chunks - 1) % 2).wait()

    return run(h, idx)


def _sc_gather(y, idx, n):
    d = y.shape[1]
    nw, items, chunk = idx.shape
    tok_w = n // nw

    @functools.partial(
        pl.kernel, mesh=_sc_mesh(),
        out_type=jax.ShapeDtypeStruct((TOP_K, n, d), y.dtype),
        scratch_types=[pltpu.VMEM((items, chunk), jnp.int32),
                       pltpu.VMEM((2, chunk, d), y.dtype),
                       pltpu.SemaphoreType.DMA((2,)), pltpu.SemaphoreType.DMA((2,))],
        name="sc_gather",
    )
    def run(y_hbm, idx_hbm, out_hbm, idx_v, rows_v, gsem, wsem):
        wid = lax.axis_index("s") * SC_CORES + lax.axis_index("c")
        base = wid * tok_w
        pltpu.sync_copy(idx_hbm.at[wid], idx_v)

        def gather(i, slot):
            return pltpu.make_async_copy(y_hbm.at[idx_v.at[i]], rows_v.at[slot], gsem.at[slot])

        def write(i, slot):
            dst = out_hbm.at[i % TOP_K, pl.ds(base + (i // TOP_K) * chunk, chunk)]
            return pltpu.make_async_copy(rows_v.at[slot], dst, wsem.at[slot])

        gather(0, 0).start()

        @pl.loop(0, items, step=2)
        def _(i0):
            for b in range(2):
                i = i0 + b

                @pl.when(i >= 1)
                def _():
                    write(i - 1, 1 - b).wait()

                @pl.when(i + 1 < items)
                def _():
                    gather(i + 1, 1 - b).start()

                gather(i, b).wait()
                write(i, b).start()

        write(items - 1, (items - 1) % 2).wait()

    return run(y, idx)


def _combine_kernel(n_ctx_tiles, x1_ref, yg_ref, wts_ref, mod_ref, op_ref, os_ref):
    t = pl.program_id(0)
    gt2 = mod_ref[0, 5:6, :]
    w = wts_ref[...]
    acc_lo, acc_hi = None, None
    for kk in range(TOP_K):
        y_lo, y_hi = _unpack_halves(yg_ref[kk])
        wk = w[:, kk:kk + 1]
        acc_lo = wk * y_lo if acc_lo is None else acc_lo + wk * y_lo
        acc_hi = wk * y_hi if acc_hi is None else acc_hi + wk * y_hi
    out = x1_ref[...] + gt2 * jnp.concatenate([acc_lo, acc_hi], axis=1)

    @pl.when(t < n_ctx_tiles)
    def _():
        op_ref[...] = out

    @pl.when(t >= n_ctx_tiles)
    def _():
        os_ref[...] = out


def _combine(x1, yg, wts, mod, n_ctx, tiles_per_lat_batch):
    n, d = x1.shape
    tm = TOK_TILE
    n_ctx_tiles = n_ctx // tm
    last = n_ctx_tiles - 1

    def mod_row(t):
        return jnp.where(t < n_ctx_tiles, 0, 1 + (t - n_ctx_tiles) // tiles_per_lat_batch)

    return pl.pallas_call(
        functools.partial(_combine_kernel, n_ctx_tiles),
        out_shape=(jax.ShapeDtypeStruct((n_ctx, d), F32), jax.ShapeDtypeStruct((n - n_ctx, d), F32)),
        grid=(n // tm,),
        in_specs=[pl.BlockSpec((tm, d), lambda t: (t, 0)),
                  pl.BlockSpec((TOP_K, tm, d // 2), lambda t: (0, t, 0)),
                  pl.BlockSpec((tm, TOP_K), lambda t: (t, 0)),
                  pl.BlockSpec((1,) + mod.shape[1:], lambda t: (mod_row(t), 0, 0))],
        out_specs=(pl.BlockSpec((tm, d), lambda t: (jnp.minimum(t, last), 0)),
                   pl.BlockSpec((tm, d), lambda t: (jnp.maximum(t - n_ctx_tiles, 0), 0))),
        compiler_params=_cparams(1),
        name="combine",
    )(x1, yg, wts, mod)


def kernel(x_prompt, x_sample, cache_k, cache_v, c, c_ctx, g_norm1, w_ada, b_ada, w_in, g_q, g_k, rpb,
           w_fmap, w_amap, w_gate, b_gate, w_out, g_norm2, w_router, b_router, w1, b1, w2, b2):
    batch, seq, d = x_prompt.shape
    dec_batch, dec_seq, _ = x_sample.shape
    assert w_ada.shape[0] == 1, "single-layer trunk"
    rows = dec_seq // GRID_W
    da = N_HEADS * HEAD_DIM
    n_ctx = batch * seq
    n_lat = dec_batch * dec_seq
    n = n_ctx + n_lat
    ne = w_router.shape[2]
    assert TOK_TILE % seq == 0 and n_ctx % TOK_TILE == 0 and dec_seq % TOK_TILE == 0
    assert rows % (ATT_TILE // GRID_W) == 0 and rows >= 3 * (ATT_TILE // GRID_W) and rows >= WIN_H_MAX
    assert seq == ATT_TILE and cache_k.shape[3] == ATT_TILE
    tiles_per_lat_batch = dec_seq // TOK_TILE

    n_mod_rows = 8
    cvecs = jnp.zeros((n_mod_rows, d), F32).at[0].set(c_ctx).at[1:1 + dec_batch].set(c)
    mod = _ada(cvecs, w_ada[0], b_ada[0]).reshape(n_mod_rows, 6, d)

    xp = x_prompt.reshape(n_ctx, d)
    xs = x_sample.reshape(n_lat, d)
    g1 = g_norm1[0].reshape(1, d)
    g2 = g_norm2[0].reshape(1, d)
    hsum = jnp.asarray(np.kron(np.eye(N_HEADS), np.full((HEAD_DIM, HEAD_DIM), 1.0 / HEAD_DIM)), BF16)
    gq_t = jnp.tile(g_q[0], N_HEADS).reshape(1, da)
    gk_t = jnp.tile(g_k[0], N_HEADS).reshape(1, da)

    u, q, k, v, new_k, new_v = _proj(xp, xs, mod, g1, w_in[0].astype(BF16), hsum, gq_t, gk_t,
                                      batch, seq, tiles_per_lat_batch)

    aa_c = _attn_ctx(q, k, v, batch, seq)
    bias = _window_bias(rpb[0], rows)
    past = cache_k.shape[3]
    kc = cache_k[:, 0].transpose(0, 2, 1, 3).reshape(dec_batch, past, da).astype(BF16)
    vc = cache_v[:, 0].transpose(0, 2, 1, 3).reshape(dec_batch, past, da).astype(BF16)
    aa_l = _attn_lat(q, k, v, kc, vc, bias, n_ctx, dec_batch, rows)

    w1c, ct_c, st_c = _dft_tables(seq)
    _, ct_l, st_l = _dft_tables(dec_seq)
    w1c = jnp.asarray(w1c).astype(BF16)
    af_c = _fourier(u, w1c, jnp.asarray(ct_c).astype(BF16), jnp.asarray(st_c).astype(BF16),
                    batch, seq, 0, seq)
    af_l = _fourier(u, w1c, jnp.asarray(ct_l).astype(BF16), jnp.asarray(st_l).astype(BF16),
                    dec_batch, dec_seq, n_ctx // dec_seq, TOK_TILE)

    x1, h2, idx, wts, rank, counts = _merge(
        xp, xs, mod, g1, g2, af_c, af_l, aa_c, aa_l,
        w_fmap[0].astype(BF16), w_amap[0].astype(BF16), w_gate[0].astype(BF16), b_gate[0].reshape(1, -1),
        w_out[0].astype(BF16), w_router[0], b_router[0].reshape(1, ne), tiles_per_lat_batch)

    tm = MOE_TILE
    max_tiles = (n * TOP_K) // tm + ne
    cnt = counts[0].astype(jnp.int32)
    tiles_e = (cnt + tm - 1) // tm
    tile_end = jnp.cumsum(tiles_e)
    pad_off = (tile_end - tiles_e) * tm
    n_tiles = tile_end[-1:]
    tile_ids = jnp.arange(max_tiles, dtype=jnp.int32)
    tile_expert = jnp.sum((tile_ids[:, None] >= tile_end[None, :]).astype(jnp.int32), axis=1)
    last_e = jnp.sum((n_tiles - 1 >= tile_end).astype(jnp.int32))
    tile_expert = jnp.minimum(tile_expert, last_e).astype(jnp.int32)
    pos = jnp.sum(jnp.where(idx[:, :, None] == jnp.arange(ne)[None, None, :], pad_off[None, None, :], 0),
                  axis=-1) + rank

    p_rows = max_tiles * tm
    n_workers = SC_CORES * SC_SUBCORES
    assert n % (n_workers * SC_CHUNK * 2) == 0
    chunks = n // (n_workers * SC_CHUNK)
    pos_w = pos.astype(jnp.int32).reshape(n_workers, chunks, SC_CHUNK, TOP_K).transpose(0, 1, 3, 2)
    pos_w = pos_w.reshape(n_workers, chunks * TOP_K, SC_CHUNK)
    xs_sorted = _sc_dispatch(h2, pos_w, p_rows)
    e_ids = jnp.arange(ne, dtype=jnp.int32)
    later_used = (e_ids[None, :] > e_ids[:, None]) & (tiles_e[None, :] > 0)
    next_of_e = jnp.min(jnp.where(later_used, e_ids[None, :], ne), axis=1)
    next_of_e = jnp.where(next_of_e == ne, -1, next_of_e)
    is_e = tile_expert[:, None] == e_ids[None, :]
    next_expert = jnp.sum(jnp.where(is_e, next_of_e[None, :], 0), axis=1)
    rows_left = jnp.sum(jnp.where(is_e, (cnt + pad_off)[None, :], 0), axis=1) - tile_ids * tm
    rows_valid = jnp.where(tile_ids < n_tiles[0], jnp.clip(rows_left, 0, tm), 0)
    y_sorted = _moe(tile_expert, n_tiles.astype(jnp.int32), next_expert.astype(jnp.int32),
                    rows_valid.astype(jnp.int32), xs_sorted, w1[0], b1[0], w2[0], b2[0])
    yg = _sc_gather(y_sorted, pos_w, n)

    y_p, y_s = _combine(x1, yg, wts, mod, n_ctx, tiles_per_lat_batch)
    return (y_p.reshape(batch, seq, d), y_s.reshape(dec_batch, dec_seq, d), new_k, new_v)
```

```python
import functools
import math

import numpy as np
import jax
import jax.numpy as jnp
from jax import lax
from jax.experimental import pallas as pl
from jax.experimental.pallas import tpu as pltpu
from jax.experimental.pallas import tpu_sc as plsc

F32 = jnp.float32
BF16 = jnp.bfloat16

GRID_W = 64
N_HEADS = 8
HEAD_DIM = 64
N_FGROUPS = 4
FGROUP_DIM = 128
WIN_H_MAX = 8
WIN_W = 16
TOP_K = 4
SWIGLU_LIMIT = 7.0
SWIGLU_ALPHA = 1.702
EPS = 1e-6

LANES = 128
TOK_TILE = 512
ATT_TILE = 256
MOE_TILE = 1024
MOE_SUB = 256
MERGE_PARTS = 1
VMEM_LIMIT = 56 * 1024 * 1024
NEG_BIG = -1e30

SC_CORES = 2
SC_SUBCORES = 16
SC_CHUNK = 64


def _cparams(n_axes, vmem=VMEM_LIMIT):
    return pltpu.CompilerParams(dimension_semantics=("arbitrary",) * n_axes, vmem_limit_bytes=vmem)


def _rms(x):
    return x * lax.rsqrt(jnp.mean(x * x, axis=-1, keepdims=True) + EPS)


def _pack_halves(x):
    c = x.shape[1] // 2
    lo = lax.bitcast_convert_type(x[:, :c].astype(BF16).astype(F32), jnp.uint32)
    hi = lax.bitcast_convert_type(x[:, c:].astype(BF16).astype(F32), jnp.uint32)
    return lax.bitcast_convert_type(hi | (lo >> 16), jnp.int32)


def _unpack_halves(w):
    u = lax.bitcast_convert_type(w, jnp.uint32)
    lo = lax.bitcast_convert_type(u << 16, F32)
    hi = lax.bitcast_convert_type(u & jnp.uint32(0xFFFF0000), F32)
    return lo, hi


def _ada_kernel(c_ref, w_ref, b_ref, o_ref):
    cv = c_ref[...]
    s = cv * jax.nn.sigmoid(cv)
    w = w_ref[...]
    s_hi = s.astype(BF16)
    s_lo = (s - s_hi.astype(F32)).astype(BF16)
    w_hi = w.astype(BF16)
    w_lo = (w - w_hi.astype(F32)).astype(BF16)
    o_ref[...] = (jnp.dot(s_hi, w_hi, preferred_element_type=F32)
                  + jnp.dot(s_hi, w_lo, preferred_element_type=F32)
                  + jnp.dot(s_lo, w_hi, preferred_element_type=F32) + b_ref[...])


def _ada(cvecs, w_ada, b_ada):
    rows, d = cvecs.shape
    n = w_ada.shape[1]
    blk = 1024
    return pl.pallas_call(
        _ada_kernel,
        out_shape=jax.ShapeDtypeStruct((rows, n), F32),
        grid=(n // blk,),
        in_specs=[pl.BlockSpec((rows, d), lambda j: (0, 0)),
                  pl.BlockSpec((d, blk), lambda j: (0, j)),
                  pl.BlockSpec((1, blk), lambda j: (0, j))],
        out_specs=pl.BlockSpec((rows, blk), lambda j: (0, j)),
        compiler_params=_cparams(1),
        name="ada",
    )(cvecs, w_ada, b_ada.reshape(1, n))


def _proj_kernel(n_ctx_tiles, xp_ref, xs_ref, mod_ref, g1_ref, win_ref, hsum_ref, gq_ref, gk_ref,
                 u_ref, q_ref, k_ref, v_ref, nk_ref, nv_ref):
    t = pl.program_id(0)
    is_ctx = t < n_ctx_tiles
    x = jnp.where(is_ctx, xp_ref[...], xs_ref[...])
    sh1 = mod_ref[0, 0:1, :]
    sc1 = mod_ref[0, 1:2, :]
    h = _rms(x) * g1_ref[...] * (1.0 + sc1) + sh1
    proj = jnp.dot(h.astype(BF16), win_ref[...], preferred_element_type=F32)
    da = q_ref.shape[1]
    df = u_ref.shape[1]
    u_ref[...] = proj[:, :df].astype(BF16)
    q = proj[:, df:df + da]
    k = proj[:, df + da:df + 2 * da]
    v = proj[:, df + 2 * da:]
    msq = jnp.dot((q * q).astype(BF16), hsum_ref[...], preferred_element_type=F32)
    msk = jnp.dot((k * k).astype(BF16), hsum_ref[...], preferred_element_type=F32)
    qn = q * lax.rsqrt(msq + EPS) * gq_ref[...]
    kn = k * lax.rsqrt(msk + EPS) * gk_ref[...]
    q_ref[...] = qn.astype(BF16)
    k_ref[...] = kn.astype(BF16)
    v_ref[...] = v.astype(BF16)

    @pl.when(is_ctx)
    def _():
        nb, _, nh, s, dh = nk_ref.shape
        for b in range(nb):
            for hd in range(nh):
                nk_ref[b, 0, hd] = kn[b * s:(b + 1) * s, hd * dh:(hd + 1) * dh]
                nv_ref[b, 0, hd] = v[b * s:(b + 1) * s, hd * dh:(hd + 1) * dh]


def _proj(xp, xs, mod, g1, w_in_b, hsum, gq_t, gk_t, batch, seq, tiles_per_lat_batch):
    n_ctx, d = xp.shape
    n_lat = xs.shape[0]
    tm = TOK_TILE
    n_ctx_tiles = n_ctx // tm
    n_tiles = (n_ctx + n_lat) // tm
    n = n_ctx + n_lat
    d_in = w_in_b.shape[1]
    da = N_HEADS * HEAD_DIM
    df = d_in - 3 * da
    bpt = tm // seq
    last = n_ctx_tiles - 1

    def mod_row(t):
        return jnp.where(t < n_ctx_tiles, 0, 1 + (t - n_ctx_tiles) // tiles_per_lat_batch)

    tok = lambda w: pl.BlockSpec((tm, w), lambda t: (t, 0))
    full = lambda a: pl.BlockSpec(a.shape, lambda t: (0,) * a.ndim)
    kv_spec = pl.BlockSpec((bpt, 1, N_HEADS, seq, HEAD_DIM), lambda t: (jnp.minimum(t, last), 0, 0, 0, 0))
    kv_shape = jax.ShapeDtypeStruct((batch, 1, N_HEADS, seq, HEAD_DIM), F32)
    return pl.pallas_call(
        functools.partial(_proj_kernel, n_ctx_tiles),
        out_shape=(jax.ShapeDtypeStruct((n, df), BF16),) + (jax.ShapeDtypeStruct((n, da), BF16),) * 3
        + (kv_shape, kv_shape),
        grid=(n_tiles,),
        in_specs=[pl.BlockSpec((tm, d), lambda t: (jnp.minimum(t, last), 0)),
                  pl.BlockSpec((tm, d), lambda t: (jnp.maximum(t - n_ctx_tiles, 0), 0)),
                  pl.BlockSpec((1,) + mod.shape[1:], lambda t: (mod_row(t), 0, 0)),
                  full(g1), full(w_in_b), full(hsum), full(gq_t), full(gk_t)],
        out_specs=(tok(df), tok(da), tok(da), tok(da), kv_spec, kv_spec),
        compiler_params=_cparams(1),
        name="proj",
    )(xp, xs, mod, g1, w_in_b, hsum, gq_t, gk_t)


def _softmax_pv(q2, key_blocks, val_blocks, bias_blocks, scale):
    lane = lax.broadcasted_iota(jnp.int32, (1, LANES), 1)
    assert math.frexp(scale)[0] == 0.5, "a power-of-two scale is exact in bf16 and commutes with the dot"
    q2 = q2 * jnp.asarray(scale, q2.dtype)
    out = None
    for half in range(2):
        hmask = (lane < HEAD_DIM) if half == 0 else (lane >= HEAD_DIM)
        qh = jnp.where(hmask, q2, jnp.zeros_like(q2))
        scores = []
        for kb, bb in zip(key_blocks, bias_blocks):
            s = lax.dot_general(qh, kb, (((1,), (1,)), ((), ())), preferred_element_type=F32)
            if bb is not None:
                s = s + bb[half]
            scores.append(s)
        m = jnp.max(functools.reduce(jnp.maximum, scores), axis=-1, keepdims=True)
        ps = [jnp.exp(s - m) for s in scores]
        denom = jnp.sum(functools.reduce(lambda a, b: a + b, ps), axis=-1, keepdims=True)
        o = functools.reduce(lambda a, b: a + b,
                             [jnp.dot(p.astype(BF16), vb, preferred_element_type=F32)
                              for p, vb in zip(ps, val_blocks)])
        o = o / denom
        out = o if out is None else jnp.where(hmask, o, out)
    return out


def _attn_ctx_kernel(q_ref, k_ref, v_ref, o_ref):
    scale = HEAD_DIM ** -0.5
    for p in range(q_ref.shape[1] // LANES):
        sl = slice(p * LANES, (p + 1) * LANES)
        o = _softmax_pv(q_ref[:, sl], [k_ref[:, sl]], [v_ref[:, sl]], [None], scale)
        o_ref[:, sl] = o.astype(BF16)


def _attn_ctx(q, k, v, batch, seq):
    da = q.shape[1]
    spec = pl.BlockSpec((seq, da), lambda b: (b, 0))
    return pl.pallas_call(
        _attn_ctx_kernel,
        out_shape=jax.ShapeDtypeStruct((batch * seq, da), BF16),
        grid=(batch,),
        in_specs=[spec, spec, spec],
        out_specs=spec,
        compiler_params=_cparams(1),
        name="attn_ctx",
    )(q, k, v)


def _attn_lat_kernel(q_ref, k0_ref, k1_ref, k2_ref, v0_ref, v1_ref, v2_ref, kc_ref, vc_ref, bias_ref, o_ref):
    scale = HEAD_DIM ** -0.5
    tk = k0_ref.shape[0]
    for p in range(q_ref.shape[1] // LANES):
        sl = slice(p * LANES, (p + 1) * LANES)
        keys = [k0_ref[:, sl], k1_ref[:, sl], k2_ref[:, sl], kc_ref[0, :, sl]]
        vals = [v0_ref[:, sl], v1_ref[:, sl], v2_ref[:, sl], vc_ref[0, :, sl]]
        biases = [[bias_ref[0, 2 * p + half, :, d * tk:(d + 1) * tk] for half in range(2)] for d in range(3)]
        o = _softmax_pv(q_ref[:, sl], keys, vals, biases + [None], scale)
        o_ref[:, sl] = o.astype(BF16)


def _attn_lat(q, k, v, kc, vc, bias, n_ctx, dec_batch, rows):
    da = q.shape[1]
    tq = ATT_TILE
    rows_per_tile = tq // GRID_W
    tiles = rows // rows_per_tile
    base = n_ctx // tq
    n_lat = dec_batch * rows * GRID_W
    max_start = tiles - 3

    def qmap(j, b):
        return (base + b * tiles + j, 0)

    def kmap(d):
        return lambda j, b: (base + b * tiles + jnp.clip(j - 1, 0, max_start) + d, 0)

    def bmap(j, b):
        return (jnp.where(j == 0, 0, jnp.where(j == tiles - 1, 2, 1)), 0, 0, 0)

    blk = lambda m: pl.BlockSpec((tq, da), m)
    cspec = pl.BlockSpec((1,) + kc.shape[1:], lambda j, b: (b, 0, 0))
    return pl.pallas_call(
        _attn_lat_kernel,
        out_shape=jax.ShapeDtypeStruct((n_lat, da), BF16),
        grid=(tiles, dec_batch),
        in_specs=[blk(qmap)] + [blk(kmap(d)) for d in range(3)] + [blk(kmap(d)) for d in range(3)]
        + [cspec, cspec, pl.BlockSpec((1,) + bias.shape[1:], bmap)],
        out_specs=pl.BlockSpec((tq, da), lambda j, b: (b * tiles + j, 0)),
        compiler_params=_cparams(2),
        name="attn_lat",
    )(q, k, k, k, v, v, v, kc, vc, bias)


def _window_tables(rows):
    rpt = ATT_TILE // GRID_W
    krows = 3 * rpt
    tiles = rows // rpt
    kh = min(WIN_H_MAX, rows)
    rpb_w = 2 * WIN_W - 1
    arow = -np.ones((3, rpt, krows), np.int64)
    for var, j in enumerate((0, 1, tiles - 1)):
        r0 = j * rpt
        k0 = int(np.clip(j - 1, 0, tiles - 3)) * rpt
        for qi in range(rpt):
            qrow = r0 + qi
            rs = int(np.clip(qrow - kh // 2, 0, rows - kh))
            for ki in range(krows):
                krow = k0 + ki
                if rs <= krow < rs + kh:
                    arow[var, qi, ki] = krow - qrow + WIN_H_MAX - 1
    ec = np.zeros((GRID_W, GRID_W, rpb_w), np.float32)
    for qc in range(GRID_W):
        cs = int(np.clip(qc - WIN_W // 2, 0, GRID_W - WIN_W))
        for kc in range(cs, cs + WIN_W):
            ec[qc, kc, int(np.clip(kc - qc, -(WIN_W - 1), WIN_W - 1)) + WIN_W - 1] = 1.0
    return arow.tolist(), ec


def _bias_kernel(arow, u_ref, o_ref):
    var_id = pl.program_id(0)
    for var, table in enumerate(arow):
        @pl.when(var_id == var)
        def _(table=table):
            for qi, row in enumerate(table):
                for ki, a in enumerate(row):
                    blk = u_ref[0, a] if a >= 0 else jnp.full((GRID_W, GRID_W), NEG_BIG, F32)
                    o_ref[0, 0, qi * GRID_W:(qi + 1) * GRID_W, ki * GRID_W:(ki + 1) * GRID_W] = blk


def _window_bias(rpb, rows):
    arow, ec = _window_tables(rows)
    nh = rpb.shape[0]
    col_blocks = jnp.einsum("hab,xyb->haxy", rpb, ec, precision=lax.Precision.HIGHEST)
    col_blocks = jnp.where(ec.sum(-1) > 0.5, col_blocks, NEG_BIG)
    return pl.pallas_call(
        functools.partial(_bias_kernel, arow),
        out_shape=jax.ShapeDtypeStruct((len(arow), nh, ATT_TILE, 3 * ATT_TILE), F32),
        grid=(len(arow), nh),
        in_specs=[pl.BlockSpec((1,) + col_blocks.shape[1:], lambda v, h: (h, 0, 0, 0))],
        out_specs=pl.BlockSpec((1, 1, ATT_TILE, 3 * ATT_TILE), lambda v, h: (v, h, 0, 0)),
        compiler_params=_cparams(2),
        name="window_bias",
    )(col_blocks)


def _fourier_kernel(u_ref, w1_ref, ct_ref, st_ref, o_ref, p_scr, q_scr):
    @pl.when(pl.program_id(1) == 0)
    def _():
        pq = jnp.dot(u_ref[...], w1_ref[...], preferred_element_type=F32)
        df = p_scr.shape[1]
        p_scr[...] = pq[:, :df].astype(BF16)
        q_scr[...] = pq[:, df:].astype(BF16)

    o = (jnp.dot(ct_ref[...], p_scr[...], preferred_element_type=F32)
         - jnp.dot(st_ref[...], q_scr[...], preferred_element_type=F32))
    o_ref[...] = o.astype(BF16)


def _fourier(u, w1, ct, st, batch, t_len, first_block, row_tile):
    df = u.shape[1]
    steps = t_len // row_tile
    return pl.pallas_call(
        _fourier_kernel,
        out_shape=jax.ShapeDtypeStruct((batch * t_len, df), BF16),
        grid=(batch, steps),
        in_specs=[pl.BlockSpec((t_len, df), lambda b, i: (first_block + b, 0)),
                  pl.BlockSpec(w1.shape, lambda b, i: (0, 0)),
                  pl.BlockSpec((row_tile, t_len), lambda b, i: (i, 0)),
                  pl.BlockSpec((row_tile, t_len), lambda b, i: (i, 0))],
        out_specs=pl.BlockSpec((row_tile, df), lambda b, i: (b * steps + i, 0)),
        scratch_shapes=[pltpu.VMEM((t_len, df), BF16), pltpu.VMEM((t_len, df), BF16)],
        compiler_params=_cparams(2),
        name=f"fourier_{t_len}",
    )(u, w1, ct, st)


def _dft_tables(t_len):
    c = FGROUP_DIM
    jk = np.outer(np.arange(c), np.arange(c)) % c
    ang = 2.0 * np.pi * jk / c
    eye = np.eye(N_FGROUPS)
    w1 = np.concatenate([np.kron(eye, np.cos(ang)), np.kron(eye, np.sin(ang))], axis=1) / np.sqrt(c)
    tt = np.outer(np.arange(t_len), np.arange(t_len)) % t_len
    angt = 2.0 * np.pi * tt / t_len
    return (w1.astype(np.float32), (np.cos(angt) / np.sqrt(t_len)).astype(np.float32),
            (np.sin(angt) / np.sqrt(t_len)).astype(np.float32))


def _merge_kernel(n_ctx_tiles, xp_ref, xs_ref, mod_ref, g1_ref, g2_ref, afc_ref, afl_ref, aac_ref, aal_ref,
                  wf_ref, wa_ref, wg_ref, bg_ref, wo_ref, wrh_ref, wrl_ref, br_ref,
                  x1_ref, h2_ref, idx_ref, wts_ref, rank_ref, cnt_ref, carry):
    t = pl.program_id(0)
    is_ctx = t < n_ctx_tiles

    @pl.when(t == 0)
    def _():
        carry[...] = jnp.zeros_like(carry)

    sh1 = mod_ref[0, 0:1, :]
    sc1 = mod_ref[0, 1:2, :]
    gt1 = mod_ref[0, 2:3, :]
    sh2 = mod_ref[0, 3:4, :]
    sc2 = mod_ref[0, 4:5, :]
    d = xp_ref.shape[1]
    part_rows = xp_ref.shape[0] // MERGE_PARTS
    logit_parts = []
    for part in range(MERGE_PARTS):
        rs = slice(part * part_rows, (part + 1) * part_rows)
        x = jnp.where(is_ctx, xp_ref[rs, :], xs_ref[rs, :])
        a_four = jnp.where(is_ctx, afc_ref[rs, :], afl_ref[rs, :])
        a_attn = jnp.where(is_ctx, aac_ref[rs, :], aal_ref[rs, :])
        hb = (_rms(x) * g1_ref[...] * (1.0 + sc1) + sh1).astype(BF16)
        gates = jax.nn.sigmoid(jnp.dot(hb, wg_ref[...], preferred_element_type=F32) + bg_ref[...])
        fa = jnp.dot(a_four, wf_ref[...], preferred_element_type=F32)
        fb = jnp.dot(a_attn, wa_ref[...], preferred_element_type=F32)
        mix = gates[:, :d] * fa + gates[:, d:] * fb
        x1 = x + gt1 * jnp.dot(mix.astype(BF16), wo_ref[...], preferred_element_type=F32)
        x1_ref[rs, :] = x1
        h2 = _rms(x1) * g2_ref[...] * (1.0 + sc2) + sh2
        h2_ref[rs, :] = _pack_halves(h2)
        h2_hi = h2.astype(BF16)
        h2_lo = (h2 - h2_hi.astype(F32)).astype(BF16)
        logit_parts.append(jnp.dot(h2_hi, wrh_ref[...], preferred_element_type=F32)
                           + jnp.dot(h2_hi, wrl_ref[...], preferred_element_type=F32)
                           + jnp.dot(h2_lo, wrh_ref[...], preferred_element_type=F32))

    logits = jnp.concatenate(logit_parts, axis=0) + br_ref[...]
    tm, ne = logits.shape
    col = lax.broadcasted_iota(jnp.int32, (tm, ne), 1)
    colk = lax.broadcasted_iota(jnp.int32, (tm, TOP_K), 1)
    lg = logits
    vals, idxs = [], []
    for _ in range(TOP_K):
        m = jnp.max(lg, axis=-1, keepdims=True)
        am = jnp.min(jnp.where(lg == m, col, ne), axis=-1, keepdims=True)
        vals.append(m)
        idxs.append(am)
        lg = jnp.where(col == am, -jnp.inf, lg)
    es = [jnp.exp(v - vals[0]) for v in vals]
    den = functools.reduce(lambda a, b: a + b, es)
    onehot = functools.reduce(lambda a, b: a + b, [(col == am).astype(F32) for am in idxs])
    r_i = lax.broadcasted_iota(jnp.int32, (tm, tm), 0)
    c_i = lax.broadcasted_iota(jnp.int32, (tm, tm), 1)
    lower = (r_i > c_i).astype(BF16)
    before = jnp.dot(lower, onehot.astype(BF16), preferred_element_type=F32) + carry[...]
    idx_o = jnp.zeros((tm, TOP_K), jnp.int32)
    wts_o = jnp.zeros((tm, TOP_K), F32)
    rank_o = jnp.zeros((tm, TOP_K), jnp.int32)
    for kk in range(TOP_K):
        rk = jnp.sum(jnp.where(col == idxs[kk], before, 0.0), axis=-1, keepdims=True).astype(jnp.int32)
        idx_o = jnp.where(colk == kk, idxs[kk], idx_o)
        wts_o = jnp.where(colk == kk, es[kk] / den, wts_o)
        rank_o = jnp.where(colk == kk, rk, rank_o)
    idx_ref[...] = idx_o
    wts_ref[...] = wts_o
    rank_ref[...] = rank_o
    carry[...] = carry[...] + jnp.sum(onehot, axis=0, keepdims=True)
    cnt_ref[...] = carry[...]


def _merge(xp, xs, mod, g1, g2, af_c, af_l, aa_c, aa_l, wf, wa, wg, bg, wo, wr, br, tiles_per_lat_batch):
    n_ctx, d = xp.shape
    n = n_ctx + xs.shape[0]
    tm = TOK_TILE
    n_ctx_tiles = n_ctx // tm
    last = n_ctx_tiles - 1
    ne = wr.shape[1]
    da = aa_c.shape[1]
    df = af_c.shape[1]

    def mod_row(t):
        return jnp.where(t < n_ctx_tiles, 0, 1 + (t - n_ctx_tiles) // tiles_per_lat_batch)

    cmap = lambda t: (jnp.minimum(t, last), 0)
    lmap = lambda t: (jnp.maximum(t - n_ctx_tiles, 0), 0)
    full = lambda a: pl.BlockSpec(a.shape, lambda t: (0,) * a.ndim)
    tok = lambda w: pl.BlockSpec((tm, w), lambda t: (t, 0))
    wr_hi = wr.astype(BF16)
    wr_lo = (wr - wr_hi.astype(F32)).astype(BF16)
    return pl.pallas_call(
        functools.partial(_merge_kernel, n_ctx_tiles),
        out_shape=(jax.ShapeDtypeStruct((n, d), F32), jax.ShapeDtypeStruct((n, d // 2), jnp.int32),
                   jax.ShapeDtypeStruct((n, TOP_K), jnp.int32), jax.ShapeDtypeStruct((n, TOP_K), F32),
                   jax.ShapeDtypeStruct((n, TOP_K), jnp.int32), jax.ShapeDtypeStruct((1, ne), F32)),
        grid=(n // tm,),
        in_specs=[pl.BlockSpec((tm, d), cmap), pl.BlockSpec((tm, d), lmap),
                  pl.BlockSpec((1,) + mod.shape[1:], lambda t: (mod_row(t), 0, 0)),
                  full(g1), full(g2),
                  pl.BlockSpec((tm, df), cmap), pl.BlockSpec((tm, df), lmap),
                  pl.BlockSpec((tm, da), cmap), pl.BlockSpec((tm, da), lmap),
                  full(wf), full(wa), full(wg), full(bg), full(wo), full(wr_hi), full(wr_lo), full(br)],
        out_specs=(tok(d), tok(d // 2), tok(TOP_K), tok(TOP_K), tok(TOP_K),
                   pl.BlockSpec((1, ne), lambda t: (0, 0))),
        scratch_shapes=[pltpu.VMEM((1, ne), F32)],
        compiler_params=_cparams(1),
        name="merge",
    )(xp, xs, mod, g1, g2, af_c, af_l, aa_c, aa_l, wf, wa, wg, bg, wo, wr_hi, wr_lo, br)


def _moe_kernel(te_ref, nt_ref, nx_ref, rv_ref, x_ref, w1_hbm, b1_ref, w2_hbm, b2_ref, y_ref,
                w1s, w2s, w1b, w2b, sem):
    i = pl.program_id(0)
    e = te_ref[i]
    prev = te_ref[jnp.maximum(i - 1, 0)]
    first_of_run = (i == 0) | (e != prev)

    def stage(expert):
        return (pltpu.make_async_copy(w1_hbm.at[expert], w1s, sem.at[0]),
                pltpu.make_async_copy(w2_hbm.at[expert], w2s, sem.at[1]))

    @pl.when(i == 0)
    def _():
        for cp in stage(e):
            cp.start()

    @pl.when(first_of_run)
    def _():
        for cp in stage(e):
            cp.wait()
        w1b[...] = w1s[...].astype(BF16)
        w2b[...] = w2s[...].astype(BF16)

        @pl.when(nx_ref[i] >= 0)
        def _():
            for cp in stage(nx_ref[i]):
                cp.start()

    def expert_rows(rs):
        dff = w2b.shape[0]
        x_lo, x_hi = _unpack_halves(x_ref[rs, :])
        x = jnp.concatenate([x_lo.astype(BF16), x_hi.astype(BF16)], axis=1)
        gu = jnp.dot(x, w1b[...], preferred_element_type=F32) + b1_ref[0]
        gate = jnp.minimum(gu[:, :dff], SWIGLU_LIMIT)
        up = jnp.clip(gu[:, dff:], -SWIGLU_LIMIT, SWIGLU_LIMIT)
        glu = gate * jax.nn.sigmoid(SWIGLU_ALPHA * gate)
        act = ((up + 1.0) * glu).astype(BF16)
        y = jnp.dot(act, w2b[...], preferred_element_type=F32) + b2_ref[0]
        y_ref[rs, :] = _pack_halves(y)

    tm = x_ref.shape[0]
    rv = rv_ref[i]

    @pl.when(rv == tm)
    def _():
        expert_rows(slice(0, tm))

    for sub in range(tm // MOE_SUB):
        @pl.when((rv < tm) & (rv > sub * MOE_SUB))
        def _(sub=sub):
            expert_rows(slice(sub * MOE_SUB, (sub + 1) * MOE_SUB))


def _moe(tile_expert, n_tiles, next_expert, rows_valid, xs_sorted, w1, b1, w2, b2):
    p, dh = xs_sorted.shape
    ne, d, dff2 = w1.shape
    dff = w2.shape[1]
    tm = MOE_TILE
    max_tiles = p // tm

    def row_map(i, te, nt, nx, rv):
        return (jnp.minimum(i, nt[0] - 1), 0)

    def b_map(i, te, nt, nx, rv):
        return (te[i], 0, 0)

    grid_spec = pltpu.PrefetchScalarGridSpec(
        num_scalar_prefetch=4,
        grid=(max_tiles,),
        in_specs=[pl.BlockSpec((tm, dh), row_map),
                  pl.BlockSpec(memory_space=pl.ANY),
                  pl.BlockSpec((1, 1, dff2), b_map),
                  pl.BlockSpec(memory_space=pl.ANY),
                  pl.BlockSpec((1, 1, d), b_map)],
        out_specs=pl.BlockSpec((tm, dh), row_map),
        scratch_shapes=[pltpu.VMEM((d, dff2), F32), pltpu.VMEM((dff, d), F32),
                        pltpu.VMEM((d, dff2), BF16), pltpu.VMEM((dff, d), BF16),
                        pltpu.SemaphoreType.DMA((2,))],
    )
    return pl.pallas_call(
        _moe_kernel,
        out_shape=jax.ShapeDtypeStruct((p, dh), jnp.int32),
        grid_spec=grid_spec,
        compiler_params=_cparams(1),
        name="moe",
    )(tile_expert, n_tiles, next_expert, rows_valid, xs_sorted,
      w1, b1.reshape(ne, 1, dff2), w2, b2.reshape(ne, 1, d))


def _sc_mesh():
    return plsc.VectorSubcoreMesh(core_axis_name="c", subcore_axis_name="s",
                                  num_cores=SC_CORES, num_subcores=SC_SUBCORES)


def _sc_dispatch(h, idx, p_rows):
    n, d = h.shape
    nw, items, chunk = idx.shape
    tok_w = n // nw
    n_chunks = items // TOP_K

    @functools.partial(
        pl.kernel, mesh=_sc_mesh(),
        out_type=jax.ShapeDtypeStruct((p_rows, d), h.dtype),
        scratch_types=[pltpu.VMEM((items, chunk), jnp.int32),
                       pltpu.VMEM((2, chunk, d), h.dtype),
                       pltpu.SemaphoreType.DMA((2,)), pltpu.SemaphoreType.DMA((2,))],
        name="sc_dispatch",
    )
    def run(h_hbm, idx_hbm, out_hbm, idx_v, rows_v, lsem, ssem):
        wid = lax.axis_index("s") * SC_CORES + lax.axis_index("c")
        base = wid * tok_w
        pltpu.sync_copy(idx_hbm.at[wid], idx_v)

        def load(j, slot):
            return pltpu.make_async_copy(h_hbm.at[pl.ds(base + j * chunk, chunk)], rows_v.at[slot], lsem.at[slot])

        def scat(j, kk, slot):
            return pltpu.make_async_copy(rows_v.at[slot], out_hbm.at[idx_v.at[j * TOP_K + kk]], ssem.at[slot])

        load(0, 0).start()

        @pl.loop(0, n_chunks, step=2)
        def _(j0):
            for b in range(2):
                j = j0 + b

                @pl.when(j >= 1)
                def _():
                    for kk in range(TOP_K):
                        scat(j - 1, kk, 1 - b).wait()

                @pl.when(j + 1 < n_chunks)
                def _():
                    load(j + 1, 1 - b).start()

                load(j, b).wait()
                for kk in range(TOP_K):
                    scat(j, kk, b).start()

        for kk in range(TOP_K):
            scat(n_chunks - 1, kk, (n_chunks - 1) % 2).wait()

    return run(h, idx)


def _sc_gather(y, idx, n):
    d = y.shape[1]
    nw, items, chunk = idx.shape
    tok_w = n // nw

    @functools.partial(
        pl.kernel, mesh=_sc_mesh(),
        out_type=jax.ShapeDtypeStruct((TOP_K, n, d), y.dtype),
        scratch_types=[pltpu.VMEM((items, chunk), jnp.int32),
                       pltpu.VMEM((2, chunk, d), y.dtype),
                       pltpu.SemaphoreType.DMA((2,)), pltpu.SemaphoreType.DMA((2,))],
        name="sc_gather",
    )
    def run(y_hbm, idx_hbm, out_hbm, idx_v, rows_v, gsem, wsem):
        wid = lax.axis_index("s") * SC_CORES + lax.axis_index("c")
        base = wid * tok_w
        pltpu.sync_copy(idx_hbm.at[wid], idx_v)

        def gather(i, slot):
            return pltpu.make_async_copy(y_hbm.at[idx_v.at[i]], rows_v.at[slot], gsem.at[slot])

        def write(i, slot):
            dst = out_hbm.at[i % TOP_K, pl.ds(base + (i // TOP_K) * chunk, chunk)]
            return pltpu.make_async_copy(rows_v.at[slot], dst, wsem.at[slot])

        gather(0, 0).start()

        @pl.loop(0, items, step=2)
        def _(i0):
            for b in range(2):
                i = i0 + b

                @pl.when(i >= 1)
                def _():
                    write(i - 1, 1 - b).wait()

                @pl.when(i + 1 < items)
                def _():
                    gather(i + 1, 1 - b).start()

                gather(i, b).wait()
                write(i, b).start()

        write(items - 1, (items - 1) % 2).wait()

    return run(y, idx)


def _combine_kernel(n_ctx_tiles, x1_ref, yg_ref, wts_ref, mod_ref, op_ref, os_ref):
    t = pl.program_id(0)
    gt2 = mod_ref[0, 5:6, :]
    w = wts_ref[...]
    acc_lo, acc_hi = None, None
    for kk in range(TOP_K):
        y_lo, y_hi = _unpack_halves(yg_ref[kk])
        wk = w[:, kk:kk + 1]
        acc_lo = wk * y_lo if acc_lo is None else acc_lo + wk * y_lo
        acc_hi = wk * y_hi if acc_hi is None else acc_hi + wk * y_hi
    out = x1_ref[...] + gt2 * jnp.concatenate([acc_lo, acc_hi], axis=1)

    @pl.when(t < n_ctx_tiles)
    def _():
        op_ref[...] = out

    @pl.when(t >= n_ctx_tiles)
    def _():
        os_ref[...] = out


def _combine(x1, yg, wts, mod, n_ctx, tiles_per_lat_batch):
    n, d = x1.shape
    tm = TOK_TILE
    n_ctx_tiles = n_ctx // tm
    last = n_ctx_tiles - 1

    def mod_row(t):
        return jnp.where(t < n_ctx_tiles, 0, 1 + (t - n_ctx_tiles) // tiles_per_lat_batch)

    return pl.pallas_call(
        functools.partial(_combine_kernel, n_ctx_tiles),
        out_shape=(jax.ShapeDtypeStruct((n_ctx, d), F32), jax.ShapeDtypeStruct((n - n_ctx, d), F32)),
        grid=(n // tm,),
        in_specs=[pl.BlockSpec((tm, d), lambda t: (t, 0)),
                  pl.BlockSpec((TOP_K, tm, d // 2), lambda t: (0, t, 0)),
                  pl.BlockSpec((tm, TOP_K), lambda t: (t, 0)),
                  pl.BlockSpec((1,) + mod.shape[1:], lambda t: (mod_row(t), 0, 0))],
        out_specs=(pl.BlockSpec((tm, d), lambda t: (jnp.minimum(t, last), 0)),
                   pl.BlockSpec((tm, d), lambda t: (jnp.maximum(t - n_ctx_tiles, 0), 0))),
        compiler_params=_cparams(1),
        name="combine",
    )(x1, yg, wts, mod)


def kernel(x_prompt, x_sample, cache_k, cache_v, c, c_ctx, g_norm1, w_ada, b_ada, w_in, g_q, g_k, rpb,
           w_fmap, w_amap, w_gate, b_gate, w_out, g_norm2, w_router, b_router, w1, b1, w2, b2):
    batch, seq, d = x_prompt.shape
    dec_batch, dec_seq, _ = x_sample.shape
    assert w_ada.shape[0] == 1, "single-layer trunk"
    rows = dec_seq // GRID_W
    da = N_HEADS * HEAD_DIM
    n_ctx = batch * seq
    n_lat = dec_batch * dec_seq
    n = n_ctx + n_lat
    ne = w_router.shape[2]
    assert TOK_TILE % seq == 0 and n_ctx % TOK_TILE == 0 and dec_seq % TOK_TILE == 0
    assert rows % (ATT_TILE // GRID_W) == 0 and rows >= 3 * (ATT_TILE // GRID_W) and rows >= WIN_H_MAX
    assert seq == ATT_TILE and cache_k.shape[3] == ATT_TILE
    tiles_per_lat_batch = dec_seq // TOK_TILE

    n_mod_rows = 8
    cvecs = jnp.zeros((n_mod_rows, d), F32).at[0].set(c_ctx).at[1:1 + dec_batch].set(c)
    mod = _ada(cvecs, w_ada[0], b_ada[0]).reshape(n_mod_rows, 6, d)

    xp = x_prompt.reshape(n_ctx, d)
    xs = x_sample.reshape(n_lat, d)
    g1 = g_norm1[0].reshape(1, d)
    g2 = g_norm2[0].reshape(1, d)
    hsum = jnp.asarray(np.kron(np.eye(N_HEADS), np.full((HEAD_DIM, HEAD_DIM), 1.0 / HEAD_DIM)), BF16)
    gq_t = jnp.tile(g_q[0], N_HEADS).reshape(1, da)
    gk_t = jnp.tile(g_k[0], N_HEADS).reshape(1, da)

    u, q, k, v, new_k, new_v = _proj(xp, xs, mod, g1, w_in[0].astype(BF16), hsum, gq_t, gk_t,
                                      batch, seq, tiles_per_lat_batch)

    aa_c = _attn_ctx(q, k, v, batch, seq)
    bias = _window_bias(rpb[0], rows)
    past = cache_k.shape[3]
    kc = cache_k[:, 0].transpose(0, 2, 1, 3).reshape(dec_batch, past, da).astype(BF16)
    vc = cache_v[:, 0].transpose(0, 2, 1, 3).reshape(dec_batch, past, da).astype(BF16)
    aa_l = _attn_lat(q, k, v, kc, vc, bias, n_ctx, dec_batch, rows)

    w1c, ct_c, st_c = _dft_tables(seq)
    _, ct_l, st_l = _dft_tables(dec_seq)
    w1c = jnp.asarray(w1c).astype(BF16)
    af_c = _fourier(u, w1c, jnp.asarray(ct_c).astype(BF16), jnp.asarray(st_c).astype(BF16),
                    batch, seq, 0, seq)
    af_l = _fourier(u, w1c, jnp.asarray(ct_l).astype(BF16), jnp.asarray(st_l).astype(BF16),
                    dec_batch, dec_seq, n_ctx // dec_seq, TOK_TILE)

    x1, h2, idx, wts, rank, counts = _merge(
        xp, xs, mod, g1, g2, af_c, af_l, aa_c, aa_l,
        w_fmap[0].astype(BF16), w_amap[0].astype(BF16), w_gate[0].astype(BF16), b_gate[0].reshape(1, -1),
        w_out[0].astype(BF16), w_router[0], b_router[0].reshape(1, ne), tiles_per_lat_batch)

    tm = MOE_TILE
    max_tiles = (n * TOP_K) // tm + ne
    cnt = counts[0].astype(jnp.int32)
    tiles_e = (cnt + tm - 1) // tm
    tile_end = jnp.cumsum(tiles_e)
    pad_off = (tile_end - tiles_e) * tm
    n_tiles = tile_end[-1:]
    tile_ids = jnp.arange(max_tiles, dtype=jnp.int32)
    tile_expert = jnp.sum((tile_ids[:, None] >= tile_end[None, :]).astype(jnp.int32), axis=1)
    last_e = jnp.sum((n_tiles - 1 >= tile_end).astype(jnp.int32))
    tile_expert = jnp.minimum(tile_expert, last_e).astype(jnp.int32)
    pos = jnp.sum(jnp.where(idx[:, :, None] == jnp.arange(ne)[None, None, :], pad_off[None, None, :], 0),
                  axis=-1) + rank

    p_rows = max_tiles * tm
    n_workers = SC_CORES * SC_SUBCORES

    def index_lists(pos_rows):
        rows = pos_rows.shape[0]
        assert rows % (n_workers * SC_CHUNK * 2) == 0
        chunks = rows // (n_workers * SC_CHUNK)
        lists = pos_rows.astype(jnp.int32).reshape(n_workers, chunks, SC_CHUNK, TOP_K).transpose(0, 1, 3, 2)
        return lists.reshape(n_workers, chunks * TOP_K, SC_CHUNK)

    pos_lists = index_lists(pos)
    xs_sorted = _sc_dispatch(h2, pos_lists, p_rows)
    e_ids = jnp.arange(ne, dtype=jnp.int32)
    later_used = (e_ids[None, :] > e_ids[:, None]) & (tiles_e[None, :] > 0)
    next_of_e = jnp.min(jnp.where(later_used, e_ids[None, :], ne), axis=1)
    next_of_e = jnp.where(next_of_e == ne, -1, next_of_e)
    is_e = tile_expert[:, None] == e_ids[None, :]
    next_expert = jnp.sum(jnp.where(is_e, next_of_e[None, :], 0), axis=1)
    rows_left = jnp.sum(jnp.where(is_e, (cnt + pad_off)[None, :], 0), axis=1) - tile_ids * tm
    rows_valid = jnp.where(tile_ids < n_tiles[0], jnp.clip(rows_left, 0, tm), 0)
    y_sorted = _moe(tile_expert, n_tiles.astype(jnp.int32), next_expert.astype(jnp.int32),
                    rows_valid.astype(jnp.int32), xs_sorted, w1[0], b1[0], w2[0], b2[0])
    yg = _sc_gather(y_sorted, pos_lists, n)
    y_p, y_s = _combine(x1, yg, wts, mod, n_ctx, tiles_per_lat_batch)
    return (y_p.reshape(batch, seq, d), y_s.reshape(dec_batch, dec_seq, d), new_k, new_v)
```

```python
import functools
import math

import numpy as np
import jax
import jax.numpy as jnp
from jax import lax
from jax.experimental import pallas as pl
from jax.experimental.pallas import tpu as pltpu
from jax.experimental.pallas import tpu_sc as plsc

F32 = jnp.float32
BF16 = jnp.bfloat16

GRID_W = 64
N_HEADS = 8
HEAD_DIM = 64
N_FGROUPS = 4
FGROUP_DIM = 128
WIN_H_MAX = 8
WIN_W = 16
TOP_K = 4
SWIGLU_LIMIT = 7.0
SWIGLU_ALPHA = 1.702
EPS = 1e-6
LOG2E = math.log2(math.e)
QK_PRESCALE = HEAD_DIM ** -0.5 * LOG2E

LANES = 128
TOK_TILE = 512
ATT_TILE = 256
MOE_TILE = 1024
MOE_SUB = 256
MERGE_PARTS = 1
VMEM_LIMIT = 56 * 1024 * 1024
NEG_BIG = -1e30

SC_CORES = 2
SC_SUBCORES = 16
SC_CHUNK = 64


def _cparams(n_axes, vmem=VMEM_LIMIT):
    return pltpu.CompilerParams(dimension_semantics=("arbitrary",) * n_axes, vmem_limit_bytes=vmem)


def _rms(x):
    return x * lax.rsqrt(jnp.mean(x * x, axis=-1, keepdims=True) + EPS)


def _pack_halves(x):
    c = x.shape[1] // 2
    lo = lax.bitcast_convert_type(x[:, :c].astype(BF16).astype(F32), jnp.uint32)
    hi = lax.bitcast_convert_type(x[:, c:].astype(BF16).astype(F32), jnp.uint32)
    return lax.bitcast_convert_type(hi | (lo >> 16), jnp.int32)


def _unpack_halves(w):
    u = lax.bitcast_convert_type(w, jnp.uint32)
    lo = lax.bitcast_convert_type(u << 16, F32)
    hi = lax.bitcast_convert_type(u & jnp.uint32(0xFFFF0000), F32)
    return lo, hi


def _ada_kernel(c_ref, w_ref, b_ref, o_ref):
    cv = c_ref[...]
    s = cv * jax.nn.sigmoid(cv)
    w = w_ref[...]
    s_hi = s.astype(BF16)
    s_lo = (s - s_hi.astype(F32)).astype(BF16)
    w_hi = w.astype(BF16)
    w_lo = (w - w_hi.astype(F32)).astype(BF16)
    o_ref[...] = (jnp.dot(s_hi, w_hi, preferred_element_type=F32)
                  + jnp.dot(s_hi, w_lo, preferred_element_type=F32)
                  + jnp.dot(s_lo, w_hi, preferred_element_type=F32) + b_ref[...])


def _ada(cvecs, w_ada, b_ada):
    rows, d = cvecs.shape
    n = w_ada.shape[1]
    blk = 1024
    return pl.pallas_call(
        _ada_kernel,
        out_shape=jax.ShapeDtypeStruct((rows, n), F32),
        grid=(n // blk,),
        in_specs=[pl.BlockSpec((rows, d), lambda j: (0, 0)),
                  pl.BlockSpec((d, blk), lambda j: (0, j)),
                  pl.BlockSpec((1, blk), lambda j: (0, j))],
        out_specs=pl.BlockSpec((rows, blk), lambda j: (0, j)),
        compiler_params=_cparams(1),
        name="ada",
    )(cvecs, w_ada, b_ada.reshape(1, n))


def _proj_kernel(n_ctx_tiles, xp_ref, xs_ref, mod_ref, g1_ref, win_ref, hsum_ref, gq_ref, gk_ref,
                 u_ref, q_ref, k_ref, v_ref, nk_ref, nv_ref):
    t = pl.program_id(0)
    is_ctx = t < n_ctx_tiles
    x = jnp.where(is_ctx, xp_ref[...], xs_ref[...])
    sh1 = mod_ref[0, 0:1, :]
    sc1 = mod_ref[0, 1:2, :]
    h = _rms(x) * g1_ref[...] * (1.0 + sc1) + sh1
    proj = jnp.dot(h.astype(BF16), win_ref[...], preferred_element_type=F32)
    da = q_ref.shape[1]
    df = u_ref.shape[1]
    u_ref[...] = proj[:, :df].astype(BF16)
    q = proj[:, df:df + da]
    k = proj[:, df + da:df + 2 * da]
    v = proj[:, df + 2 * da:]
    msq = jnp.dot((q * q).astype(BF16), hsum_ref[...], preferred_element_type=F32)
    msk = jnp.dot((k * k).astype(BF16), hsum_ref[...], preferred_element_type=F32)
    qn = q * lax.rsqrt(msq + EPS) * gq_ref[...]
    kn = k * lax.rsqrt(msk + EPS) * gk_ref[...]
    q_ref[...] = (qn * QK_PRESCALE).astype(BF16)
    k_ref[...] = kn.astype(BF16)
    v_ref[...] = v.astype(BF16)

    @pl.when(is_ctx)
    def _():
        nb, _, nh, s, dh = nk_ref.shape
        for b in range(nb):
            for hd in range(nh):
                nk_ref[b, 0, hd] = kn[b * s:(b + 1) * s, hd * dh:(hd + 1) * dh]
                nv_ref[b, 0, hd] = v[b * s:(b + 1) * s, hd * dh:(hd + 1) * dh]


def _proj(xp, xs, mod, g1, w_in_b, hsum, gq_t, gk_t, batch, seq, tiles_per_lat_batch):
    n_ctx, d = xp.shape
    n_lat = xs.shape[0]
    tm = TOK_TILE
    n_ctx_tiles = n_ctx // tm
    n_tiles = (n_ctx + n_lat) // tm
    n = n_ctx + n_lat
    d_in = w_in_b.shape[1]
    da = N_HEADS * HEAD_DIM
    df = d_in - 3 * da
    bpt = tm // seq
    last = n_ctx_tiles - 1

    def mod_row(t):
        return jnp.where(t < n_ctx_tiles, 0, 1 + (t - n_ctx_tiles) // tiles_per_lat_batch)

    tok = lambda w: pl.BlockSpec((tm, w), lambda t: (t, 0))
    full = lambda a: pl.BlockSpec(a.shape, lambda t: (0,) * a.ndim)
    kv_spec = pl.BlockSpec((bpt, 1, N_HEADS, seq, HEAD_DIM), lambda t: (jnp.minimum(t, last), 0, 0, 0, 0))
    kv_shape = jax.ShapeDtypeStruct((batch, 1, N_HEADS, seq, HEAD_DIM), F32)
    return pl.pallas_call(
        functools.partial(_proj_kernel, n_ctx_tiles),
        out_shape=(jax.ShapeDtypeStruct((n, df), BF16),) + (jax.ShapeDtypeStruct((n, da), BF16),) * 3
        + (kv_shape, kv_shape),
        grid=(n_tiles,),
        in_specs=[pl.BlockSpec((tm, d), lambda t: (jnp.minimum(t, last), 0)),
                  pl.BlockSpec((tm, d), lambda t: (jnp.maximum(t - n_ctx_tiles, 0), 0)),
                  pl.BlockSpec((1,) + mod.shape[1:], lambda t: (mod_row(t), 0, 0)),
                  full(g1), full(w_in_b), full(hsum), full(gq_t), full(gk_t)],
        out_specs=(tok(df), tok(da), tok(da), tok(da), kv_spec, kv_spec),
        compiler_params=_cparams(1),
        name="proj",
    )(xp, xs, mod, g1, w_in_b, hsum, gq_t, gk_t)


def _softmax_pv(q2, key_blocks, val_blocks, bias_blocks):
    lane = lax.broadcasted_iota(jnp.int32, (1, LANES), 1)
    masks = [lane < HEAD_DIM, lane >= HEAD_DIM]
    all_scores = []
    for half in range(2):
        qh = jnp.where(masks[half], q2, jnp.zeros_like(q2))
        scores = []
        for kb, bb in zip(key_blocks, bias_blocks):
            s = lax.dot_general(qh, kb, (((1,), (1,)), ((), ())), preferred_element_type=F32)
            if bb is not None:
                s = s + bb[half]
            scores.append(s)
        all_scores.append(scores)
    outs = []
    for scores in all_scores:
        m = jnp.max(functools.reduce(jnp.maximum, scores), axis=-1, keepdims=True)
        ps = [jnp.exp2(s - m) for s in scores]
        denom = jnp.sum(functools.reduce(lambda a, b: a + b, ps), axis=-1, keepdims=True)
        o = functools.reduce(lambda a, b: a + b,
                             [jnp.dot(p.astype(BF16), vb, preferred_element_type=F32)
                              for p, vb in zip(ps, val_blocks)])
        outs.append(o / denom)
    return jnp.where(masks[0], outs[0], outs[1])


def _attn_ctx_kernel(q_ref, k_ref, v_ref, o_ref):
    for p in range(q_ref.shape[1] // LANES):
        sl = slice(p * LANES, (p + 1) * LANES)
        o = _softmax_pv(q_ref[:, sl], [k_ref[:, sl]], [v_ref[:, sl]], [None])
        o_ref[:, sl] = o.astype(BF16)


def _attn_ctx(q, k, v, batch, seq):
    da = q.shape[1]
    spec = pl.BlockSpec((seq, da), lambda b: (b, 0))
    return pl.pallas_call(
        _attn_ctx_kernel,
        out_shape=jax.ShapeDtypeStruct((batch * seq, da), BF16),
        grid=(batch,),
        in_specs=[spec, spec, spec],
        out_specs=spec,
        compiler_params=_cparams(1),
        name="attn_ctx",
    )(q, k, v)


def _attn_lat_kernel(q_ref, k0_ref, k1_ref, k2_ref, v0_ref, v1_ref, v2_ref, kc_ref, vc_ref, bias_ref, o_ref):
    tk = k0_ref.shape[0]
    for p in range(q_ref.shape[1] // LANES):
        sl = slice(p * LANES, (p + 1) * LANES)
        keys = [k0_ref[:, sl], k1_ref[:, sl], k2_ref[:, sl], kc_ref[0, :, sl]]
        vals = [v0_ref[:, sl], v1_ref[:, sl], v2_ref[:, sl], vc_ref[0, :, sl]]
        biases = [[bias_ref[0, 2 * p + half, :, d * tk:(d + 1) * tk] for half in range(2)] for d in range(3)]
        o = _softmax_pv(q_ref[:, sl], keys, vals, biases + [None])
        o_ref[:, sl] = o.astype(BF16)


def _attn_lat(q, k, v, kc, vc, bias, n_ctx, dec_batch, rows):
    da = q.shape[1]
    tq = ATT_TILE
    rows_per_tile = tq // GRID_W
    tiles = rows // rows_per_tile
    base = n_ctx // tq
    n_lat = dec_batch * rows * GRID_W
    max_start = tiles - 3

    def qmap(j, b):
        return (base + b * tiles + j, 0)

    def kmap(d):
        return lambda j, b: (base + b * tiles + jnp.clip(j - 1, 0, max_start) + d, 0)

    def bmap(j, b):
        return (jnp.where(j == 0, 0, jnp.where(j == tiles - 1, 2, 1)), 0, 0, 0)

    blk = lambda m: pl.BlockSpec((tq, da), m)
    cspec = pl.BlockSpec((1,) + kc.shape[1:], lambda j, b: (b, 0, 0))
    return pl.pallas_call(
        _attn_lat_kernel,
        out_shape=jax.ShapeDtypeStruct((n_lat, da), BF16),
        grid=(tiles, dec_batch),
        in_specs=[blk(qmap)] + [blk(kmap(d)) for d in range(3)] + [blk(kmap(d)) for d in range(3)]
        + [cspec, cspec, pl.BlockSpec((1,) + bias.shape[1:], bmap)],
        out_specs=pl.BlockSpec((tq, da), lambda j, b: (b * tiles + j, 0)),
        compiler_params=_cparams(2),
        name="attn_lat",
    )(q, k, k, k, v, v, v, kc, vc, bias)


def _window_tables(rows):
    rpt = ATT_TILE // GRID_W
    krows = 3 * rpt
    tiles = rows // rpt
    kh = min(WIN_H_MAX, rows)
    rpb_w = 2 * WIN_W - 1
    arow = -np.ones((3, rpt, krows), np.int64)
    for var, j in enumerate((0, 1, tiles - 1)):
        r0 = j * rpt
        k0 = int(np.clip(j - 1, 0, tiles - 3)) * rpt
        for qi in range(rpt):
            qrow = r0 + qi
            rs = int(np.clip(qrow - kh // 2, 0, rows - kh))
            for ki in range(krows):
                krow = k0 + ki
                if rs <= krow < rs + kh:
                    arow[var, qi, ki] = krow - qrow + WIN_H_MAX - 1
    ec = np.zeros((GRID_W, GRID_W, rpb_w), np.float32)
    for qc in range(GRID_W):
        cs = int(np.clip(qc - WIN_W // 2, 0, GRID_W - WIN_W))
        for kc in range(cs, cs + WIN_W):
            ec[qc, kc, int(np.clip(kc - qc, -(WIN_W - 1), WIN_W - 1)) + WIN_W - 1] = 1.0
    return arow.tolist(), ec


def _bias_kernel(arow, u_ref, o_ref):
    var_id = pl.program_id(0)
    for var, table in enumerate(arow):
        @pl.when(var_id == var)
        def _(table=table):
            for qi, row in enumerate(table):
                for ki, a in enumerate(row):
                    blk = u_ref[0, a] if a >= 0 else jnp.full((GRID_W, GRID_W), NEG_BIG, F32)
                    o_ref[0, 0, qi * GRID_W:(qi + 1) * GRID_W, ki * GRID_W:(ki + 1) * GRID_W] = blk


def _window_bias(rpb, rows):
    arow, ec = _window_tables(rows)
    nh = rpb.shape[0]
    col_blocks = jnp.einsum("hab,xyb->haxy", rpb, ec, precision=lax.Precision.HIGHEST)
    col_blocks = jnp.where(ec.sum(-1) > 0.5, col_blocks * LOG2E, NEG_BIG)
    return pl.pallas_call(
        functools.partial(_bias_kernel, arow),
        out_shape=jax.ShapeDtypeStruct((len(arow), nh, ATT_TILE, 3 * ATT_TILE), F32),
        grid=(len(arow), nh),
        in_specs=[pl.BlockSpec((1,) + col_blocks.shape[1:], lambda v, h: (h, 0, 0, 0))],
        out_specs=pl.BlockSpec((1, 1, ATT_TILE, 3 * ATT_TILE), lambda v, h: (v, h, 0, 0)),
        compiler_params=_cparams(2),
        name="window_bias",
    )(col_blocks)


def _fourier_kernel(u_ref, w1_ref, ct_ref, st_ref, o_ref, p_scr, q_scr):
    @pl.when(pl.program_id(1) == 0)
    def _():
        pq = jnp.dot(u_ref[...], w1_ref[...], preferred_element_type=F32)
        df = p_scr.shape[1]
        p_scr[...] = pq[:, :df].astype(BF16)
        q_scr[...] = pq[:, df:].astype(BF16)

    o = (jnp.dot(ct_ref[...], p_scr[...], preferred_element_type=F32)
         - jnp.dot(st_ref[...], q_scr[...], preferred_element_type=F32))
    o_ref[...] = o.astype(BF16)


def _fourier(u, w1, ct, st, batch, t_len, first_block, row_tile):
    df = u.shape[1]
    steps = t_len // row_tile
    return pl.pallas_call(
        _fourier_kernel,
        out_shape=jax.ShapeDtypeStruct((batch * t_len, df), BF16),
        grid=(batch, steps),
        in_specs=[pl.BlockSpec((t_len, df), lambda b, i: (first_block + b, 0)),
                  pl.BlockSpec(w1.shape, lambda b, i: (0, 0)),
                  pl.BlockSpec((row_tile, t_len), lambda b, i: (i, 0)),
                  pl.BlockSpec((row_tile, t_len), lambda b, i: (i, 0))],
        out_specs=pl.BlockSpec((row_tile, df), lambda b, i: (b * steps + i, 0)),
        scratch_shapes=[pltpu.VMEM((t_len, df), BF16), pltpu.VMEM((t_len, df), BF16)],
        compiler_params=_cparams(2),
        name=f"fourier_{t_len}",
    )(u, w1, ct, st)


def _dft_tables(t_len):
    c = FGROUP_DIM
    jk = np.outer(np.arange(c), np.arange(c)) % c
    ang = 2.0 * np.pi * jk / c
    eye = np.eye(N_FGROUPS)
    w1 = np.concatenate([np.kron(eye, np.cos(ang)), np.kron(eye, np.sin(ang))], axis=1) / np.sqrt(c)
    tt = np.outer(np.arange(t_len), np.arange(t_len)) % t_len
    angt = 2.0 * np.pi * tt / t_len
    return (w1.astype(np.float32), (np.cos(angt) / np.sqrt(t_len)).astype(np.float32),
            (np.sin(angt) / np.sqrt(t_len)).astype(np.float32))


def _merge_kernel(n_ctx_tiles, xp_ref, xs_ref, mod_ref, g1_ref, g2_ref, afc_ref, afl_ref, aac_ref, aal_ref,
                  wf_ref, wa_ref, wg_ref, bg_ref, wo_ref, wrh_ref, wrl_ref, br_ref,
                  x1_ref, h2_ref, idx_ref, wts_ref, rank_ref, cnt_ref, carry):
    t = pl.program_id(0)
    is_ctx = t < n_ctx_tiles

    @pl.when(t == 0)
    def _():
        carry[...] = jnp.zeros_like(carry)

    sh1 = mod_ref[0, 0:1, :]
    sc1 = mod_ref[0, 1:2, :]
    gt1 = mod_ref[0, 2:3, :]
    sh2 = mod_ref[0, 3:4, :]
    sc2 = mod_ref[0, 4:5, :]
    d = xp_ref.shape[1]
    part_rows = xp_ref.shape[0] // MERGE_PARTS
    logit_parts = []
    for part in range(MERGE_PARTS):
        rs = slice(part * part_rows, (part + 1) * part_rows)
        x = jnp.where(is_ctx, xp_ref[rs, :], xs_ref[rs, :])
        a_four = jnp.where(is_ctx, afc_ref[rs, :], afl_ref[rs, :])
        a_attn = jnp.where(is_ctx, aac_ref[rs, :], aal_ref[rs, :])
        hb = (_rms(x) * g1_ref[...] * (1.0 + sc1) + sh1).astype(BF16)
        gates = jax.nn.sigmoid(jnp.dot(hb, wg_ref[...], preferred_element_type=F32) + bg_ref[...])
        fa = jnp.dot(a_four, wf_ref[...], preferred_element_type=F32)
        fb = jnp.dot(a_attn, wa_ref[...], preferred_element_type=F32)
        mix = gates[:, :d] * fa + gates[:, d:] * fb
        x1 = x + gt1 * jnp.dot(mix.astype(BF16), wo_ref[...], preferred_element_type=F32)
        x1_ref[rs, :] = x1
        h2 = _rms(x1) * g2_ref[...] * (1.0 + sc2) + sh2
        h2_ref[rs, :] = _pack_halves(h2)
        h2_hi = h2.astype(BF16)
        h2_lo = (h2 - h2_hi.astype(F32)).astype(BF16)
        logit_parts.append(jnp.dot(h2_hi, wrh_ref[...], preferred_element_type=F32)
                           + jnp.dot(h2_hi, wrl_ref[...], preferred_element_type=F32)
                           + jnp.dot(h2_lo, wrh_ref[...], preferred_element_type=F32))

    logits = jnp.concatenate(logit_parts, axis=0) + br_ref[...]
    tm, ne = logits.shape
    col = lax.broadcasted_iota(jnp.int32, (tm, ne), 1)
    colk = lax.broadcasted_iota(jnp.int32, (tm, TOP_K), 1)
    lg = logits
    vals, idxs = [], []
    for _ in range(TOP_K):
        m = jnp.max(lg, axis=-1, keepdims=True)
        am = jnp.min(jnp.where(lg == m, col, ne), axis=-1, keepdims=True)
        vals.append(m)
        idxs.append(am)
        lg = jnp.where(col == am, -jnp.inf, lg)
    es = [jnp.exp(v - vals[0]) for v in vals]
    den = functools.reduce(lambda a, b: a + b, es)
    onehot = functools.reduce(lambda a, b: a + b, [(col == am).astype(F32) for am in idxs])
    r_i = lax.broadcasted_iota(jnp.int32, (tm, tm), 0)
    c_i = lax.broadcasted_iota(jnp.int32, (tm, tm), 1)
    lower = (r_i > c_i).astype(BF16)
    before = jnp.dot(lower, onehot.astype(BF16), preferred_element_type=F32) + carry[...]
    idx_o = jnp.zeros((tm, TOP_K), jnp.int32)
    wts_o = jnp.zeros((tm, TOP_K), F32)
    rank_o = jnp.zeros((tm, TOP_K), jnp.int32)
    for kk in range(TOP_K):
        rk = jnp.sum(jnp.where(col == idxs[kk], before, 0.0), axis=-1, keepdims=True).astype(jnp.int32)
        idx_o = jnp.where(colk == kk, idxs[kk], idx_o)
        wts_o = jnp.where(colk == kk, es[kk] / den, wts_o)
        rank_o = jnp.where(colk == kk, rk, rank_o)
    idx_ref[...] = idx_o
    wts_ref[...] = wts_o
    rank_ref[...] = rank_o
    carry[...] = carry[...] + jnp.sum(onehot, axis=0, keepdims=True)
    cnt_ref[...] = carry[...]


def _merge(xp, xs, mod, g1, g2, af_c, af_l, aa_c, aa_l, wf, wa, wg, bg, wo, wr, br, tiles_per_lat_batch):
    n_ctx, d = xp.shape
    n = n_ctx + xs.shape[0]
    tm = TOK_TILE
    n_ctx_tiles = n_ctx // tm
    last = n_ctx_tiles - 1
    ne = wr.shape[1]
    da = aa_c.shape[1]
    df = af_c.shape[1]

    def mod_row(t):
        return jnp.where(t < n_ctx_tiles, 0, 1 + (t - n_ctx_tiles) // tiles_per_lat_batch)

    cmap = lambda t: (jnp.minimum(t, last), 0)
    lmap = lambda t: (jnp.maximum(t - n_ctx_tiles, 0), 0)
    full = lambda a: pl.BlockSpec(a.shape, lambda t: (0,) * a.ndim)
    tok = lambda w: pl.BlockSpec((tm, w), lambda t: (t, 0))
    wr_hi = wr.astype(BF16)
    wr_lo = (wr - wr_hi.astype(F32)).astype(BF16)
    return pl.pallas_call(
        functools.partial(_merge_kernel, n_ctx_tiles),
        out_shape=(jax.ShapeDtypeStruct((n, d), F32), jax.ShapeDtypeStruct((n, d // 2), jnp.int32),
                   jax.ShapeDtypeStruct((n, TOP_K), jnp.int32), jax.ShapeDtypeStruct((n, TOP_K), F32),
                   jax.ShapeDtypeStruct((n, TOP_K), jnp.int32), jax.ShapeDtypeStruct((1, ne), F32)),
        grid=(n // tm,),
        in_specs=[pl.BlockSpec((tm, d), cmap), pl.BlockSpec((tm, d), lmap),
                  pl.BlockSpec((1,) + mod.shape[1:], lambda t: (mod_row(t), 0, 0)),
                  full(g1), full(g2),
                  pl.BlockSpec((tm, df), cmap), pl.BlockSpec((tm, df), lmap),
                  pl.BlockSpec((tm, da), cmap), pl.BlockSpec((tm, da), lmap),
                  full(wf), full(wa), full(wg), full(bg), full(wo), full(wr_hi), full(wr_lo), full(br)],
        out_specs=(tok(d), tok(d // 2), tok(TOP_K), tok(TOP_K), tok(TOP_K),
                   pl.BlockSpec((1, ne), lambda t: (0, 0))),
        scratch_shapes=[pltpu.VMEM((1, ne), F32)],
        compiler_params=_cparams(1),
        name="merge",
    )(xp, xs, mod, g1, g2, af_c, af_l, aa_c, aa_l, wf, wa, wg, bg, wo, wr_hi, wr_lo, br)


def _moe_kernel(te_ref, nt_ref, nx_ref, rv_ref, sl_ref, x_ref, w1_hbm, b1_ref, w2_hbm, b2_ref, y_ref,
                w1s, w2s, sem):
    i = pl.program_id(0)
    e = te_ref[i]
    prev = te_ref[jnp.maximum(i - 1, 0)]
    first_of_run = (i == 0) | (e != prev)
    slot = sl_ref[i]

    def stage(expert, s):
        return (pltpu.make_async_copy(w1_hbm.at[expert], w1s.at[s], sem.at[0, s]),
                pltpu.make_async_copy(w2_hbm.at[expert], w2s.at[s], sem.at[1, s]))

    @pl.when(i == 0)
    def _():
        for cp in stage(e, slot):
            cp.start()

    @pl.when(first_of_run)
    def _():
        for cp in stage(e, slot):
            cp.wait()

        @pl.when(nx_ref[i] >= 0)
        def _():
            for cp in stage(nx_ref[i], 1 - slot):
                cp.start()

    def expert_rows(rs):
        dff = w2s.shape[1]
        x_lo, x_hi = _unpack_halves(x_ref[rs, :])
        x = jnp.concatenate([x_lo.astype(BF16), x_hi.astype(BF16)], axis=1)
        gu = jnp.dot(x, w1s[slot].astype(BF16), preferred_element_type=F32) + b1_ref[0]
        gate = jnp.minimum(gu[:, :dff], SWIGLU_LIMIT)
        up = jnp.clip(gu[:, dff:], -SWIGLU_LIMIT, SWIGLU_LIMIT)
        glu = gate * jax.nn.sigmoid(SWIGLU_ALPHA * gate)
        act = ((up + 1.0) * glu).astype(BF16)
        y = jnp.dot(act, w2s[slot].astype(BF16), preferred_element_type=F32) + b2_ref[0]
        y_ref[rs, :] = _pack_halves(y)

    tm = x_ref.shape[0]
    rv = rv_ref[i]

    @pl.when(rv == tm)
    def _():
        expert_rows(slice(0, tm))

    for sub in range(tm // MOE_SUB):
        @pl.when((rv < tm) & (rv > sub * MOE_SUB))
        def _(sub=sub):
            expert_rows(slice(sub * MOE_SUB, (sub + 1) * MOE_SUB))


def _moe(tile_expert, n_tiles, next_expert, rows_valid, slots, xs_sorted, w1, b1, w2, b2):
    p, dh = xs_sorted.shape
    ne, d, dff2 = w1.shape
    dff = w2.shape[1]
    tm = MOE_TILE
    max_tiles = p // tm

    def row_map(i, te, nt, nx, rv, sl):
        return (jnp.minimum(i, nt[0] - 1), 0)

    def b_map(i, te, nt, nx, rv, sl):
        return (te[i], 0, 0)

    grid_spec = pltpu.PrefetchScalarGridSpec(
        num_scalar_prefetch=5,
        grid=(max_tiles,),
        in_specs=[pl.BlockSpec((tm, dh), row_map),
                  pl.BlockSpec(memory_space=pl.ANY),
                  pl.BlockSpec((1, 1, dff2), b_map),
                  pl.BlockSpec(memory_space=pl.ANY),
                  pl.BlockSpec((1, 1, d), b_map)],
        out_specs=pl.BlockSpec((tm, dh), row_map),
        scratch_shapes=[pltpu.VMEM((2, d, dff2), F32), pltpu.VMEM((2, dff, d), F32),
                        pltpu.SemaphoreType.DMA((2, 2))],
    )
    return pl.pallas_call(
        _moe_kernel,
        out_shape=jax.ShapeDtypeStruct((p, dh), jnp.int32),
        grid_spec=grid_spec,
        compiler_params=_cparams(1),
        name="moe",
    )(tile_expert, n_tiles, next_expert, rows_valid, slots, xs_sorted,
      w1, b1.reshape(ne, 1, dff2), w2, b2.reshape(ne, 1, d))


def _sc_mesh():
    return plsc.VectorSubcoreMesh(core_axis_name="c", subcore_axis_name="s",
                                  num_cores=SC_CORES, num_subcores=SC_SUBCORES)


def _sc_dispatch(h, idx, p_rows):
    n, d = h.shape
    nw, items, chunk = idx.shape
    tok_w = n // nw
    n_chunks = items // TOP_K

    @functools.partial(
        pl.kernel, mesh=_sc_mesh(),
        out_type=jax.ShapeDtypeStruct((p_rows, d), h.dtype),
        scratch_types=[pltpu.VMEM((items, chunk), jnp.int32),
                       pltpu.VMEM((2, chunk, d), h.dtype),
                       pltpu.SemaphoreType.DMA((2,)), pltpu.SemaphoreType.DMA((2,))],
        name="sc_dispatch",
    )
    def run(h_hbm, idx_hbm, out_hbm, idx_v, rows_v, lsem, ssem):
        wid = lax.axis_index("s") * SC_CORES + lax.axis_index("c")
        base = wid * tok_w
        pltpu.sync_copy(idx_hbm.at[wid], idx_v)

        def load(j, slot):
            return pltpu.make_async_copy(h_hbm.at[pl.ds(base + j * chunk, chunk)], rows_v.at[slot], lsem.at[slot])

        def scat(j, kk, slot):
            return pltpu.make_async_copy(rows_v.at[slot], out_hbm.at[idx_v.at[j * TOP_K + kk]], ssem.at[slot])

        load(0, 0).start()

        @pl.loop(0, n_chunks, step=2)
        def _(j0):
            for b in range(2):
                j = j0 + b

                @pl.when(j >= 1)
                def _():
                    for kk in range(TOP_K):
                        scat(j - 1, kk, 1 - b).wait()

                @pl.when(j + 1 < n_chunks)
                def _():
                    load(j + 1, 1 - b).start()

                load(j, b).wait()
                for kk in range(TOP_K):
                    scat(j, kk, b).start()

        for kk in range(TOP_K):
            scat(n_chunks - 1, kk, (n_chunks - 1) % 2).wait()

    return run(h, idx)


def _sc_gather(y, idx, n):
    d = y.shape[1]
    nw, items, chunk = idx.shape
    tok_w = n // nw

    @functools.partial(
        pl.kernel, mesh=_sc_mesh(),
        out_type=jax.ShapeDtypeStruct((TOP_K, n, d), y.dtype),
        scratch_types=[pltpu.VMEM((items, chunk), jnp.int32),
                       pltpu.VMEM((2, chunk, d), y.dtype),
                       pltpu.SemaphoreType.DMA((2,)), pltpu.SemaphoreType.DMA((2,))],
        name="sc_gather",
    )
    def run(y_hbm, idx_hbm, out_hbm, idx_v, rows_v, gsem, wsem):
        wid = lax.axis_index("s") * SC_CORES + lax.axis_index("c")
        base = wid * tok_w
        pltpu.sync_copy(idx_hbm.at[wid], idx_v)

        def gather(i, slot):
            return pltpu.make_async_copy(y_hbm.at[idx_v.at[i]], rows_v.at[slot], gsem.at[slot])

        def write(i, slot):
            dst = out_hbm.at[i % TOP_K, pl.ds(base + (i // TOP_K) * chunk, chunk)]
            return pltpu.make_async_copy(rows_v.at[slot], dst, wsem.at[slot])

        gather(0, 0).start()

        @pl.loop(0, items, step=2)
        def _(i0):
            for b in range(2):
                i = i0 + b

                @pl.when(i >= 1)
                def _():
                    write(i - 1, 1 - b).wait()

                @pl.when(i + 1 < items)
                def _():
                    gather(i + 1, 1 - b).start()

                gather(i, b).wait()
                write(i, b).start()

        write(items - 1, (items - 1) % 2).wait()

    return run(y, idx)


def _combine_kernel(n_ctx_tiles, x1_ref, yg_ref, wts_ref, mod_ref, op_ref, os_ref):
    t = pl.program_id(0)
    gt2 = mod_ref[0, 5:6, :]
    w = wts_ref[...]
    acc_lo, acc_hi = None, None
    for kk in range(TOP_K):
        y_lo, y_hi = _unpack_halves(yg_ref[kk])
        wk = w[:, kk:kk + 1]
        acc_lo = wk * y_lo if acc_lo is None else acc_lo + wk * y_lo
        acc_hi = wk * y_hi if acc_hi is None else acc_hi + wk * y_hi
    out = x1_ref[...] + gt2 * jnp.concatenate([acc_lo, acc_hi], axis=1)

    @pl.when(t < n_ctx_tiles)
    def _():
        op_ref[...] = out

    @pl.when(t >= n_ctx_tiles)
    def _():
        os_ref[...] = out


def _combine(x1, yg, wts, mod, n_ctx, tiles_per_lat_batch):
    n, d = x1.shape
    tm = TOK_TILE
    n_ctx_tiles = n_ctx // tm
    last = n_ctx_tiles - 1

    def mod_row(t):
        return jnp.where(t < n_ctx_tiles, 0, 1 + (t - n_ctx_tiles) // tiles_per_lat_batch)

    return pl.pallas_call(
        functools.partial(_combine_kernel, n_ctx_tiles),
        out_shape=(jax.ShapeDtypeStruct((n_ctx, d), F32), jax.ShapeDtypeStruct((n - n_ctx, d), F32)),
        grid=(n // tm,),
        in_specs=[pl.BlockSpec((tm, d), lambda t: (t, 0)),
                  pl.BlockSpec((TOP_K, tm, d // 2), lambda t: (0, t, 0)),
                  pl.BlockSpec((tm, TOP_K), lambda t: (t, 0)),
                  pl.BlockSpec((1,) + mod.shape[1:], lambda t: (mod_row(t), 0, 0))],
        out_specs=(pl.BlockSpec((tm, d), lambda t: (jnp.minimum(t, last), 0)),
                   pl.BlockSpec((tm, d), lambda t: (jnp.maximum(t - n_ctx_tiles, 0), 0))),
        compiler_params=_cparams(1),
        name="combine",
    )(x1, yg, wts, mod)


def kernel(x_prompt, x_sample, cache_k, cache_v, c, c_ctx, g_norm1, w_ada, b_ada, w_in, g_q, g_k, rpb,
           w_fmap, w_amap, w_gate, b_gate, w_out, g_norm2, w_router, b_router, w1, b1, w2, b2):
    batch, seq, d = x_prompt.shape
    dec_batch, dec_seq, _ = x_sample.shape
    assert w_ada.shape[0] == 1, "single-layer trunk"
    rows = dec_seq // GRID_W
    da = N_HEADS * HEAD_DIM
    n_ctx = batch * seq
    n_lat = dec_batch * dec_seq
    n = n_ctx + n_lat
    ne = w_router.shape[2]
    assert TOK_TILE % seq == 0 and n_ctx % TOK_TILE == 0 and dec_seq % TOK_TILE == 0
    assert rows % (ATT_TILE // GRID_W) == 0 and rows >= 3 * (ATT_TILE // GRID_W) and rows >= WIN_H_MAX
    assert seq == ATT_TILE and cache_k.shape[3] == ATT_TILE
    tiles_per_lat_batch = dec_seq // TOK_TILE

    n_mod_rows = 8
    cvecs = jnp.zeros((n_mod_rows, d), F32).at[0].set(c_ctx).at[1:1 + dec_batch].set(c)
    mod = _ada(cvecs, w_ada[0], b_ada[0]).reshape(n_mod_rows, 6, d)

    xp = x_prompt.reshape(n_ctx, d)
    xs = x_sample.reshape(n_lat, d)
    g1 = g_norm1[0].reshape(1, d)
    g2 = g_norm2[0].reshape(1, d)
    hsum = jnp.asarray(np.kron(np.eye(N_HEADS), np.full((HEAD_DIM, HEAD_DIM), 1.0 / HEAD_DIM)), BF16)
    gq_t = jnp.tile(g_q[0], N_HEADS).reshape(1, da)
    gk_t = jnp.tile(g_k[0], N_HEADS).reshape(1, da)

    u, q, k, v, new_k, new_v = _proj(xp, xs, mod, g1, w_in[0].astype(BF16), hsum, gq_t, gk_t,
                                      batch, seq, tiles_per_lat_batch)

    aa_c = _attn_ctx(q, k, v, batch, seq)
    bias = _window_bias(rpb[0], rows)
    past = cache_k.shape[3]
    kc = cache_k[:, 0].transpose(0, 2, 1, 3).reshape(dec_batch, past, da).astype(BF16)
    vc = cache_v[:, 0].transpose(0, 2, 1, 3).reshape(dec_batch, past, da).astype(BF16)
    aa_l = _attn_lat(q, k, v, kc, vc, bias, n_ctx, dec_batch, rows)

    w1c, ct_c, st_c = _dft_tables(seq)
    _, ct_l, st_l = _dft_tables(dec_seq)
    w1c = jnp.asarray(w1c).astype(BF16)
    af_c = _fourier(u, w1c, jnp.asarray(ct_c).astype(BF16), jnp.asarray(st_c).astype(BF16),
                    batch, seq, 0, seq)
    af_l = _fourier(u, w1c, jnp.asarray(ct_l).astype(BF16), jnp.asarray(st_l).astype(BF16),
                    dec_batch, dec_seq, n_ctx // dec_seq, TOK_TILE)

    x1, h2, idx, wts, rank, counts = _merge(
        xp, xs, mod, g1, g2, af_c, af_l, aa_c, aa_l,
        w_fmap[0].astype(BF16), w_amap[0].astype(BF16), w_gate[0].astype(BF16), b_gate[0].reshape(1, -1),
        w_out[0].astype(BF16), w_router[0], b_router[0].reshape(1, ne), tiles_per_lat_batch)

    tm = MOE_TILE
    max_tiles = (n * TOP_K) // tm + ne
    cnt = counts[0].astype(jnp.int32)
    tiles_e = (cnt + tm - 1) // tm
    tile_end = jnp.cumsum(tiles_e)
    pad_off = (tile_end - tiles_e) * tm
    n_tiles = tile_end[-1:]
    tile_ids = jnp.arange(max_tiles, dtype=jnp.int32)
    tile_expert = jnp.sum((tile_ids[:, None] >= tile_end[None, :]).astype(jnp.int32), axis=1)
    last_e = jnp.sum((n_tiles - 1 >= tile_end).astype(jnp.int32))
    tile_expert = jnp.minimum(tile_expert, last_e).astype(jnp.int32)
    pos = jnp.sum(jnp.where(idx[:, :, None] == jnp.arange(ne)[None, None, :], pad_off[None, None, :], 0),
                  axis=-1) + rank

    p_rows = max_tiles * tm
    n_workers = SC_CORES * SC_SUBCORES

    def index_lists(pos_rows):
        rows = pos_rows.shape[0]
        assert rows % (n_workers * SC_CHUNK * 2) == 0
        chunks = rows // (n_workers * SC_CHUNK)
        lists = pos_rows.astype(jnp.int32).reshape(n_workers, chunks, SC_CHUNK, TOP_K).transpose(0, 1, 3, 2)
        return lists.reshape(n_workers, chunks * TOP_K, SC_CHUNK)

    pos_lists = index_lists(pos)
    xs_sorted = _sc_dispatch(h2, pos_lists, p_rows)
    e_ids = jnp.arange(ne, dtype=jnp.int32)
    later_used = (e_ids[None, :] > e_ids[:, None]) & (tiles_e[None, :] > 0)
    next_of_e = jnp.min(jnp.where(later_used, e_ids[None, :], ne), axis=1)
    next_of_e = jnp.where(next_of_e == ne, -1, next_of_e)
    is_e = tile_expert[:, None] == e_ids[None, :]
    next_expert = jnp.sum(jnp.where(is_e, next_of_e[None, :], 0), axis=1)
    rows_left = jnp.sum(jnp.where(is_e, (cnt + pad_off)[None, :], 0), axis=1) - tile_ids * tm
    rows_valid = jnp.where(tile_ids < n_tiles[0], jnp.clip(rows_left, 0, tm), 0)
    run_of_e = jnp.cumsum((tiles_e > 0).astype(jnp.int32)) - 1
    slots = jnp.sum(jnp.where(is_e, run_of_e[None, :], 0), axis=1) % 2
    y_sorted = _moe(tile_expert, n_tiles.astype(jnp.int32), next_expert.astype(jnp.int32),
                    rows_valid.astype(jnp.int32), slots.astype(jnp.int32), xs_sorted,
                    w1[0], b1[0], w2[0], b2[0])
    yg = _sc_gather(y_sorted, pos_lists, n)
    y_p, y_s = _combine(x1, yg, wts, mod, n_ctx, tiles_per_lat_batch)
    return (y_p.reshape(batch, seq, d), y_s.reshape(dec_batch, dec_seq, d), new_k, new_v)
```

```python
import functools
import math

import numpy as np
import jax
import jax.numpy as jnp
from jax import lax
from jax.experimental import pallas as pl
from jax.experimental.pallas import tpu as pltpu
from jax.experimental.pallas import tpu_sc as plsc

F32 = jnp.float32
BF16 = jnp.bfloat16

GRID_W = 64
N_HEADS = 8
HEAD_DIM = 64
N_FGROUPS = 4
FGROUP_DIM = 128
WIN_H_MAX = 8
WIN_W = 16
TOP_K = 4
SWIGLU_LIMIT = 7.0
SWIGLU_ALPHA = 1.702
EPS = 1e-6
LOG2E = math.log2(math.e)
QK_PRESCALE = HEAD_DIM ** -0.5 * LOG2E

LANES = 128
TOK_TILE = 512
ATT_TILE = 256
MOE_TILE = 1024
MOE_SUB = 256
MERGE_PARTS = 2
VMEM_LIMIT = 56 * 1024 * 1024
NEG_BIG = -1e30

SC_CORES = 2
SC_SUBCORES = 16
SC_CHUNK = 64


def _cparams(n_axes, vmem=VMEM_LIMIT):
    return pltpu.CompilerParams(dimension_semantics=("arbitrary",) * n_axes, vmem_limit_bytes=vmem)


def _rms(x):
    return x * lax.rsqrt(jnp.mean(x * x, axis=-1, keepdims=True) + EPS)


def _pack_halves(x):
    c = x.shape[1] // 2
    lo = lax.bitcast_convert_type(x[:, :c].astype(BF16).astype(F32), jnp.uint32)
    hi = lax.bitcast_convert_type(x[:, c:].astype(BF16).astype(F32), jnp.uint32)
    return lax.bitcast_convert_type(hi | (lo >> 16), jnp.int32)


def _unpack_halves(w):
    u = lax.bitcast_convert_type(w, jnp.uint32)
    lo = lax.bitcast_convert_type(u << 16, F32)
    hi = lax.bitcast_convert_type(u & jnp.uint32(0xFFFF0000), F32)
    return lo, hi


def _ada_kernel(c_ref, w_ref, b_ref, o_ref):
    cv = c_ref[...]
    s = cv * jax.nn.sigmoid(cv)
    w = w_ref[...]
    s_hi = s.astype(BF16)
    s_lo = (s - s_hi.astype(F32)).astype(BF16)
    w_hi = w.astype(BF16)
    w_lo = (w - w_hi.astype(F32)).astype(BF16)
    o_ref[...] = (jnp.dot(s_hi, w_hi, preferred_element_type=F32)
                  + jnp.dot(s_hi, w_lo, preferred_element_type=F32)
                  + jnp.dot(s_lo, w_hi, preferred_element_type=F32) + b_ref[...])


def _ada(cvecs, w_ada, b_ada):
    rows, d = cvecs.shape
    n = w_ada.shape[1]
    blk = 1024
    return pl.pallas_call(
        _ada_kernel,
        out_shape=jax.ShapeDtypeStruct((rows, n), F32),
        grid=(n // blk,),
        in_specs=[pl.BlockSpec((rows, d), lambda j: (0, 0)),
                  pl.BlockSpec((d, blk), lambda j: (0, j)),
                  pl.BlockSpec((1, blk), lambda j: (0, j))],
        out_specs=pl.BlockSpec((rows, blk), lambda j: (0, j)),
        compiler_params=_cparams(1),
        name="ada",
    )(cvecs, w_ada, b_ada.reshape(1, n))


def _proj_kernel(n_ctx_tiles, xp_ref, xs_ref, mod_ref, g1_ref, win_ref, hsum_ref, gq_ref, gk_ref,
                 u_ref, q_ref, k_ref, v_ref, nk_ref, nv_ref):
    t = pl.program_id(0)
    is_ctx = t < n_ctx_tiles
    sh1 = mod_ref[0, 0:1, :]
    sc1 = mod_ref[0, 1:2, :]
    da = q_ref.shape[1]
    df = u_ref.shape[1]
    nb, _, nh, s, dh = nk_ref.shape
    parts = [slice(b * s, (b + 1) * s) for b in range(nb)]
    hs = [(_rms(jnp.where(is_ctx, xp_ref[rs, :], xs_ref[rs, :])) * g1_ref[...] * (1.0 + sc1) + sh1).astype(BF16)
          for rs in parts]
    projs = [jnp.dot(h, win_ref[...], preferred_element_type=F32) for h in hs]
    kns, vs = [], []
    for rs, proj in zip(parts, projs):
        u_ref[rs, :] = proj[:, :df].astype(BF16)
        q = proj[:, df:df + da]
        k = proj[:, df + da:df + 2 * da]
        v = proj[:, df + 2 * da:]
        msq = jnp.dot((q * q).astype(BF16), hsum_ref[...], preferred_element_type=F32)
        msk = jnp.dot((k * k).astype(BF16), hsum_ref[...], preferred_element_type=F32)
        qn = q * lax.rsqrt(msq + EPS) * gq_ref[...]
        kn = k * lax.rsqrt(msk + EPS) * gk_ref[...]
        q_ref[rs, :] = (qn * QK_PRESCALE).astype(BF16)
        k_ref[rs, :] = kn.astype(BF16)
        v_ref[rs, :] = v.astype(BF16)
        kns.append(kn)
        vs.append(v)

    @pl.when(is_ctx)
    def _():
        for b in range(nb):
            for hd in range(nh):
                nk_ref[b, 0, hd] = kns[b][:, hd * dh:(hd + 1) * dh]
                nv_ref[b, 0, hd] = vs[b][:, hd * dh:(hd + 1) * dh]


def _proj(xp, xs, mod, g1, w_in_b, hsum, gq_t, gk_t, batch, seq, tiles_per_lat_batch):
    n_ctx, d = xp.shape
    n_lat = xs.shape[0]
    tm = TOK_TILE
    n_ctx_tiles = n_ctx // tm
    n_tiles = (n_ctx + n_lat) // tm
    n = n_ctx + n_lat
    d_in = w_in_b.shape[1]
    da = N_HEADS * HEAD_DIM
    df = d_in - 3 * da
    bpt = tm // seq
    last = n_ctx_tiles - 1

    def mod_row(t):
        return jnp.where(t < n_ctx_tiles, 0, 1 + (t - n_ctx_tiles) // tiles_per_lat_batch)

    tok = lambda w: pl.BlockSpec((tm, w), lambda t: (t, 0))
    full = lambda a: pl.BlockSpec(a.shape, lambda t: (0,) * a.ndim)
    kv_spec = pl.BlockSpec((bpt, 1, N_HEADS, seq, HEAD_DIM), lambda t: (jnp.minimum(t, last), 0, 0, 0, 0))
    kv_shape = jax.ShapeDtypeStruct((batch, 1, N_HEADS, seq, HEAD_DIM), F32)
    return pl.pallas_call(
        functools.partial(_proj_kernel, n_ctx_tiles),
        out_shape=(jax.ShapeDtypeStruct((n, df), BF16),) + (jax.ShapeDtypeStruct((n, da), BF16),) * 3
        + (kv_shape, kv_shape),
        grid=(n_tiles,),
        in_specs=[pl.BlockSpec((tm, d), lambda t: (jnp.minimum(t, last), 0)),
                  pl.BlockSpec((tm, d), lambda t: (jnp.maximum(t - n_ctx_tiles, 0), 0)),
                  pl.BlockSpec((1,) + mod.shape[1:], lambda t: (mod_row(t), 0, 0)),
                  full(g1), full(w_in_b), full(hsum), full(gq_t), full(gk_t)],
        out_specs=(tok(df), tok(da), tok(da), tok(da), kv_spec, kv_spec),
        compiler_params=_cparams(1),
        name="proj",
    )(xp, xs, mod, g1, w_in_b, hsum, gq_t, gk_t)


def _softmax_pv(q2, key_blocks, val_blocks, bias_blocks):
    lane = lax.broadcasted_iota(jnp.int32, (1, LANES), 1)
    masks = [lane < HEAD_DIM, lane >= HEAD_DIM]

    def head_scores(half):
        qh = jnp.where(masks[half], q2, jnp.zeros_like(q2))
        scores = []
        for kb, bb in zip(key_blocks, bias_blocks):
            s = lax.dot_general(qh, kb, (((1,), (1,)), ((), ())), preferred_element_type=F32)
            if bb is not None:
                s = s + bb[half]
            scores.append(s)
        return scores

    def head_out(scores):
        m = jnp.max(functools.reduce(jnp.maximum, scores), axis=-1, keepdims=True)
        ps = [jnp.exp2(s - m) for s in scores]
        denom = jnp.sum(functools.reduce(lambda a, b: a + b, ps), axis=-1, keepdims=True)
        o = functools.reduce(lambda a, b: a + b,
                             [jnp.dot(p.astype(BF16), vb, preferred_element_type=F32)
                              for p, vb in zip(ps, val_blocks)])
        return o / denom

    if len(key_blocks) > 1:
        outs = [head_out(s) for s in [head_scores(0), head_scores(1)]]
    else:
        outs = [head_out(head_scores(half)) for half in range(2)]
    return jnp.where(masks[0], outs[0], outs[1])


def _attn_ctx_kernel(q_ref, k_ref, v_ref, o_ref):
    for p in range(q_ref.shape[1] // LANES):
        sl = slice(p * LANES, (p + 1) * LANES)
        o = _softmax_pv(q_ref[:, sl], [k_ref[:, sl]], [v_ref[:, sl]], [None])
        o_ref[:, sl] = o.astype(BF16)


def _attn_ctx(q, k, v, batch, seq):
    da = q.shape[1]
    spec = pl.BlockSpec((seq, da), lambda b: (b, 0))
    return pl.pallas_call(
        _attn_ctx_kernel,
        out_shape=jax.ShapeDtypeStruct((batch * seq, da), BF16),
        grid=(batch,),
        in_specs=[spec, spec, spec],
        out_specs=spec,
        compiler_params=_cparams(1),
        name="attn_ctx",
    )(q, k, v)


def _attn_lat_kernel(q_ref, k0_ref, k1_ref, k2_ref, v0_ref, v1_ref, v2_ref, kc_ref, vc_ref, bias_ref, o_ref):
    tk = k0_ref.shape[0]
    for p in range(q_ref.shape[1] // LANES):
        sl = slice(p * LANES, (p + 1) * LANES)
        keys = [k0_ref[:, sl], k1_ref[:, sl], k2_ref[:, sl], kc_ref[0, :, sl]]
        vals = [v0_ref[:, sl], v1_ref[:, sl], v2_ref[:, sl], vc_ref[0, :, sl]]
        biases = [[bias_ref[0, 2 * p + half, :, d * tk:(d + 1) * tk] for half in range(2)] for d in range(3)]
        o = _softmax_pv(q_ref[:, sl], keys, vals, biases + [None])
        o_ref[:, sl] = o.astype(BF16)


def _attn_lat(q, k, v, kc, vc, bias, n_ctx, dec_batch, rows):
    da = q.shape[1]
    tq = ATT_TILE
    rows_per_tile = tq // GRID_W
    tiles = rows // rows_per_tile
    base = n_ctx // tq
    n_lat = dec_batch * rows * GRID_W
    max_start = tiles - 3

    def qmap(j, b):
        return (base + b * tiles + j, 0)

    def kmap(d):
        return lambda j, b: (base + b * tiles + jnp.clip(j - 1, 0, max_start) + d, 0)

    def bmap(j, b):
        return (jnp.where(j == 0, 0, jnp.where(j == tiles - 1, 2, 1)), 0, 0, 0)

    blk = lambda m: pl.BlockSpec((tq, da), m)
    cspec = pl.BlockSpec((1,) + kc.shape[1:], lambda j, b: (b, 0, 0))
    return pl.pallas_call(
        _attn_lat_kernel,
        out_shape=jax.ShapeDtypeStruct((n_lat, da), BF16),
        grid=(tiles, dec_batch),
        in_specs=[blk(qmap)] + [blk(kmap(d)) for d in range(3)] + [blk(kmap(d)) for d in range(3)]
        + [cspec, cspec, pl.BlockSpec((1,) + bias.shape[1:], bmap)],
        out_specs=pl.BlockSpec((tq, da), lambda j, b: (b * tiles + j, 0)),
        compiler_params=_cparams(2),
        name="attn_lat",
    )(q, k, k, k, v, v, v, kc, vc, bias)


def _window_tables(rows):
    rpt = ATT_TILE // GRID_W
    krows = 3 * rpt
    tiles = rows // rpt
    kh = min(WIN_H_MAX, rows)
    rpb_w = 2 * WIN_W - 1
    arow = -np.ones((3, rpt, krows), np.int64)
    for var, j in enumerate((0, 1, tiles - 1)):
        r0 = j * rpt
        k0 = int(np.clip(j - 1, 0, tiles - 3)) * rpt
        for qi in range(rpt):
            qrow = r0 + qi
            rs = int(np.clip(qrow - kh // 2, 0, rows - kh))
            for ki in range(krows):
                krow = k0 + ki
                if rs <= krow < rs + kh:
                    arow[var, qi, ki] = krow - qrow + WIN_H_MAX - 1
    ec = np.zeros((GRID_W, GRID_W, rpb_w), np.float32)
    for qc in range(GRID_W):
        cs = int(np.clip(qc - WIN_W // 2, 0, GRID_W - WIN_W))
        for kc in range(cs, cs + WIN_W):
            ec[qc, kc, int(np.clip(kc - qc, -(WIN_W - 1), WIN_W - 1)) + WIN_W - 1] = 1.0
    return arow.tolist(), ec


def _bias_kernel(arow, u_ref, o_ref):
    var_id = pl.program_id(0)
    for var, table in enumerate(arow):
        @pl.when(var_id == var)
        def _(table=table):
            for qi, row in enumerate(table):
                for ki, a in enumerate(row):
                    blk = u_ref[0, a] if a >= 0 else jnp.full((GRID_W, GRID_W), NEG_BIG, F32)
                    o_ref[0, 0, qi * GRID_W:(qi + 1) * GRID_W, ki * GRID_W:(ki + 1) * GRID_W] = blk


def _window_bias(rpb, rows):
    arow, ec = _window_tables(rows)
    nh = rpb.shape[0]
    col_blocks = jnp.einsum("hab,xyb->haxy", rpb, ec, precision=lax.Precision.HIGHEST)
    col_blocks = jnp.where(ec.sum(-1) > 0.5, col_blocks * LOG2E, NEG_BIG)
    return pl.pallas_call(
        functools.partial(_bias_kernel, arow),
        out_shape=jax.ShapeDtypeStruct((len(arow), nh, ATT_TILE, 3 * ATT_TILE), F32),
        grid=(len(arow), nh),
        in_specs=[pl.BlockSpec((1,) + col_blocks.shape[1:], lambda v, h: (h, 0, 0, 0))],
        out_specs=pl.BlockSpec((1, 1, ATT_TILE, 3 * ATT_TILE), lambda v, h: (v, h, 0, 0)),
        compiler_params=_cparams(2),
        name="window_bias",
    )(col_blocks)


def _fourier_kernel(u_ref, w1_ref, ct_ref, st_ref, o_ref, p_scr, q_scr):
    @pl.when(pl.program_id(1) == 0)
    def _():
        pq = jnp.dot(u_ref[...], w1_ref[...], preferred_element_type=F32)
        df = p_scr.shape[1]
        p_scr[...] = pq[:, :df].astype(BF16)
        q_scr[...] = pq[:, df:].astype(BF16)

    o = (jnp.dot(ct_ref[...], p_scr[...], preferred_element_type=F32)
         - jnp.dot(st_ref[...], q_scr[...], preferred_element_type=F32))
    o_ref[...] = o.astype(BF16)


def _fourier(u, w1, ct, st, batch, t_len, first_block, row_tile):
    df = u.shape[1]
    steps = t_len // row_tile
    return pl.pallas_call(
        _fourier_kernel,
        out_shape=jax.ShapeDtypeStruct((batch * t_len, df), BF16),
        grid=(batch, steps),
        in_specs=[pl.BlockSpec((t_len, df), lambda b, i: (first_block + b, 0)),
                  pl.BlockSpec(w1.shape, lambda b, i: (0, 0)),
                  pl.BlockSpec((row_tile, t_len), lambda b, i: (i, 0)),
                  pl.BlockSpec((row_tile, t_len), lambda b, i: (i, 0))],
        out_specs=pl.BlockSpec((row_tile, df), lambda b, i: (b * steps + i, 0)),
        scratch_shapes=[pltpu.VMEM((t_len, df), BF16), pltpu.VMEM((t_len, df), BF16)],
        compiler_params=_cparams(2),
        name=f"fourier_{t_len}",
    )(u, w1, ct, st)


def _dft_tables(t_len):
    c = FGROUP_DIM
    jk = np.outer(np.arange(c), np.arange(c)) % c
    ang = 2.0 * np.pi * jk / c
    eye = np.eye(N_FGROUPS)
    w1 = np.concatenate([np.kron(eye, np.cos(ang)), np.kron(eye, np.sin(ang))], axis=1) / np.sqrt(c)
    tt = np.outer(np.arange(t_len), np.arange(t_len)) % t_len
    angt = 2.0 * np.pi * tt / t_len
    return (w1.astype(np.float32), (np.cos(angt) / np.sqrt(t_len)).astype(np.float32),
            (np.sin(angt) / np.sqrt(t_len)).astype(np.float32))


def _merge_kernel(n_ctx_tiles, xp_ref, xs_ref, mod_ref, g1_ref, g2_ref, afc_ref, afl_ref, aac_ref, aal_ref,
                  wf_ref, wa_ref, wg_ref, bg_ref, wo_ref, wrh_ref, wrl_ref, br_ref,
                  x1_ref, h2_ref, idx_ref, wts_ref, rank_ref, cnt_ref, carry):
    t = pl.program_id(0)
    is_ctx = t < n_ctx_tiles

    @pl.when(t == 0)
    def _():
        carry[...] = jnp.zeros_like(carry)

    sh1 = mod_ref[0, 0:1, :]
    sc1 = mod_ref[0, 1:2, :]
    gt1 = mod_ref[0, 2:3, :]
    sh2 = mod_ref[0, 3:4, :]
    sc2 = mod_ref[0, 4:5, :]
    d = xp_ref.shape[1]
    part_rows = xp_ref.shape[0] // MERGE_PARTS
    parts = [slice(p * part_rows, (p + 1) * part_rows) for p in range(MERGE_PARTS)]
    xs_ = [jnp.where(is_ctx, xp_ref[rs, :], xs_ref[rs, :]) for rs in parts]
    hbs = [(_rms(x) * g1_ref[...] * (1.0 + sc1) + sh1).astype(BF16) for x in xs_]
    gates = [jax.nn.sigmoid(jnp.dot(hb, wg_ref[...], preferred_element_type=F32) + bg_ref[...]) for hb in hbs]
    fas = [jnp.dot(jnp.where(is_ctx, afc_ref[rs, :], afl_ref[rs, :]), wf_ref[...], preferred_element_type=F32)
           for rs in parts]
    fbs = [jnp.dot(jnp.where(is_ctx, aac_ref[rs, :], aal_ref[rs, :]), wa_ref[...], preferred_element_type=F32)
           for rs in parts]
    mixes = [(g[:, :d] * fa + g[:, d:] * fb).astype(BF16) for g, fa, fb in zip(gates, fas, fbs)]
    x1s = [x + gt1 * jnp.dot(mix, wo_ref[...], preferred_element_type=F32) for x, mix in zip(xs_, mixes)]
    h2s = []
    for rs, x1 in zip(parts, x1s):
        x1_ref[rs, :] = x1
        h2 = _rms(x1) * g2_ref[...] * (1.0 + sc2) + sh2
        h2_ref[rs, :] = _pack_halves(h2)
        h2s.append(h2)
    logit_parts = []
    for h2 in h2s:
        h2_hi = h2.astype(BF16)
        h2_lo = (h2 - h2_hi.astype(F32)).astype(BF16)
        logit_parts.append(jnp.dot(h2_hi, wrh_ref[...], preferred_element_type=F32)
                           + jnp.dot(h2_hi, wrl_ref[...], preferred_element_type=F32)
                           + jnp.dot(h2_lo, wrh_ref[...], preferred_element_type=F32))

    logits = jnp.concatenate(logit_parts, axis=0) + br_ref[...]
    idx_o, wts_o, rank_o, counts = _route(logits, carry[...])
    idx_ref[...] = idx_o
    wts_ref[...] = wts_o
    rank_ref[...] = rank_o
    carry[...] = counts
    cnt_ref[...] = counts


def _route(logits, counts):
    tm, ne = logits.shape
    col = lax.broadcasted_iota(jnp.int32, (tm, ne), 1)
    colk = lax.broadcasted_iota(jnp.int32, (tm, TOP_K), 1)
    lg = logits
    vals, idxs = [], []
    for _ in range(TOP_K):
        m = jnp.max(lg, axis=-1, keepdims=True)
        am = jnp.min(jnp.where(lg == m, col, ne), axis=-1, keepdims=True)
        vals.append(m)
        idxs.append(am)
        lg = jnp.where(col == am, -jnp.inf, lg)
    es = [jnp.exp(v - vals[0]) for v in vals]
    den = functools.reduce(lambda a, b: a + b, es)
    onehot = functools.reduce(lambda a, b: a + b, [(col == am).astype(F32) for am in idxs])
    r_i = lax.broadcasted_iota(jnp.int32, (tm, tm), 0)
    c_i = lax.broadcasted_iota(jnp.int32, (tm, tm), 1)
    lower = (r_i > c_i).astype(BF16)
    before = jnp.dot(lower, onehot.astype(BF16), preferred_element_type=F32) + counts
    idx_o = jnp.zeros((tm, TOP_K), jnp.int32)
    wts_o = jnp.zeros((tm, TOP_K), F32)
    rank_o = jnp.zeros((tm, TOP_K), jnp.int32)
    for kk in range(TOP_K):
        rk = jnp.sum(jnp.where(col == idxs[kk], before, 0.0), axis=-1, keepdims=True).astype(jnp.int32)
        idx_o = jnp.where(colk == kk, idxs[kk], idx_o)
        wts_o = jnp.where(colk == kk, es[kk] / den, wts_o)
        rank_o = jnp.where(colk == kk, rk, rank_o)
    return idx_o, wts_o, rank_o, counts + jnp.sum(onehot, axis=0, keepdims=True)


def _merge(xp, xs, mod, g1, g2, af_c, af_l, aa_c, aa_l, wf, wa, wg, bg, wo, wr, br, tiles_per_lat_batch):
    n_ctx, d = xp.shape
    n = n_ctx + xs.shape[0]
    tm = TOK_TILE
    n_ctx_tiles = n_ctx // tm
    last = n_ctx_tiles - 1
    ne = wr.shape[1]
    da = aa_c.shape[1]
    df = af_c.shape[1]

    def mod_row(t):
        return jnp.where(t < n_ctx_tiles, 0, 1 + (t - n_ctx_tiles) // tiles_per_lat_batch)

    cmap = lambda t: (jnp.minimum(t, last), 0)
    lmap = lambda t: (jnp.maximum(t - n_ctx_tiles, 0), 0)
    full = lambda a: pl.BlockSpec(a.shape, lambda t: (0,) * a.ndim)
    tok = lambda w: pl.BlockSpec((tm, w), lambda t: (t, 0))
    wr_hi = wr.astype(BF16)
    wr_lo = (wr - wr_hi.astype(F32)).astype(BF16)
    return pl.pallas_call(
        functools.partial(_merge_kernel, n_ctx_tiles),
        out_shape=(jax.ShapeDtypeStruct((n, d), F32), jax.ShapeDtypeStruct((n, d // 2), jnp.int32),
                   jax.ShapeDtypeStruct((n, TOP_K), jnp.int32), jax.ShapeDtypeStruct((n, TOP_K), F32),
                   jax.ShapeDtypeStruct((n, TOP_K), jnp.int32), jax.ShapeDtypeStruct((1, ne), F32)),
        grid=(n // tm,),
        in_specs=[pl.BlockSpec((tm, d), cmap), pl.BlockSpec((tm, d), lmap),
                  pl.BlockSpec((1,) + mod.shape[1:], lambda t: (mod_row(t), 0, 0)),
                  full(g1), full(g2),
                  pl.BlockSpec((tm, df), cmap), pl.BlockSpec((tm, df), lmap),
                  pl.BlockSpec((tm, da), cmap), pl.BlockSpec((tm, da), lmap),
                  full(wf), full(wa), full(wg), full(bg), full(wo), full(wr_hi), full(wr_lo), full(br)],
        out_specs=(tok(d), tok(d // 2), tok(TOP_K), tok(TOP_K), tok(TOP_K),
                   pl.BlockSpec((1, ne), lambda t: (0, 0))),
        scratch_shapes=[pltpu.VMEM((1, ne), F32)],
        compiler_params=_cparams(1),
        name="merge",
    )(xp, xs, mod, g1, g2, af_c, af_l, aa_c, aa_l, wf, wa, wg, bg, wo, wr_hi, wr_lo, br)


def _moe_kernel(te_ref, nt_ref, nx_ref, rv_ref, sl_ref, x_ref, w1_hbm, b1_ref, w2_hbm, b2_ref, y_ref,
                w1s, w2s, sem):
    i = pl.program_id(0)
    e = te_ref[i]
    prev = te_ref[jnp.maximum(i - 1, 0)]
    first_of_run = (i == 0) | (e != prev)
    slot = sl_ref[i]

    def stage(expert, s):
        return (pltpu.make_async_copy(w1_hbm.at[expert], w1s.at[s], sem.at[0, s]),
                pltpu.make_async_copy(w2_hbm.at[expert], w2s.at[s], sem.at[1, s]))

    @pl.when(i == 0)
    def _():
        for cp in stage(e, slot):
            cp.start()

    @pl.when(first_of_run)
    def _():
        for cp in stage(e, slot):
            cp.wait()

        @pl.when(nx_ref[i] >= 0)
        def _():
            for cp in stage(nx_ref[i], 1 - slot):
                cp.start()

    def expert_rows(rs):
        dff = w2s.shape[1]
        x_lo, x_hi = _unpack_halves(x_ref[rs, :])
        x = jnp.concatenate([x_lo.astype(BF16), x_hi.astype(BF16)], axis=1)
        gu = jnp.dot(x, w1s[slot].astype(BF16), preferred_element_type=F32) + b1_ref[0]
        gate = jnp.minimum(gu[:, :dff], SWIGLU_LIMIT)
        up = jnp.clip(gu[:, dff:], -SWIGLU_LIMIT, SWIGLU_LIMIT)
        glu = gate * jax.nn.sigmoid(SWIGLU_ALPHA * gate)
        act = ((up + 1.0) * glu).astype(BF16)
        y = jnp.dot(act, w2s[slot].astype(BF16), preferred_element_type=F32) + b2_ref[0]
        y_ref[rs, :] = _pack_halves(y)

    tm = x_ref.shape[0]
    rv = rv_ref[i]

    @pl.when(rv == tm)
    def _():
        expert_rows(slice(0, tm))

    for sub in range(tm // MOE_SUB):
        @pl.when((rv < tm) & (rv > sub * MOE_SUB))
        def _(sub=sub):
            expert_rows(slice(sub * MOE_SUB, (sub + 1) * MOE_SUB))


def _moe(tile_expert, n_tiles, next_expert, rows_valid, slots, xs_sorted, w1, b1, w2, b2):
    p, dh = xs_sorted.shape
    ne, d, dff2 = w1.shape
    dff = w2.shape[1]
    tm = MOE_TILE
    max_tiles = p // tm

    def row_map(i, te, nt, nx, rv, sl):
        return (jnp.minimum(i, nt[0] - 1), 0)

    def b_map(i, te, nt, nx, rv, sl):
        return (te[i], 0, 0)

    grid_spec = pltpu.PrefetchScalarGridSpec(
        num_scalar_prefetch=5,
        grid=(max_tiles,),
        in_specs=[pl.BlockSpec((tm, dh), row_map),
                  pl.BlockSpec(memory_space=pl.ANY),
                  pl.BlockSpec((1, 1, dff2), b_map),
                  pl.BlockSpec(memory_space=pl.ANY),
                  pl.BlockSpec((1, 1, d), b_map)],
        out_specs=pl.BlockSpec((tm, dh), row_map),
        scratch_shapes=[pltpu.VMEM((2, d, dff2), F32), pltpu.VMEM((2, dff, d), F32),
                        pltpu.SemaphoreType.DMA((2, 2))],
    )
    return pl.pallas_call(
        _moe_kernel,
        out_shape=jax.ShapeDtypeStruct((p, dh), jnp.int32),
        grid_spec=grid_spec,
        compiler_params=_cparams(1),
        name="moe",
    )(tile_expert, n_tiles, next_expert, rows_valid, slots, xs_sorted,
      w1, b1.reshape(ne, 1, dff2), w2, b2.reshape(ne, 1, d))


def _sc_mesh():
    return plsc.VectorSubcoreMesh(core_axis_name="c", subcore_axis_name="s",
                                  num_cores=SC_CORES, num_subcores=SC_SUBCORES)


def _sc_dispatch(h, idx, p_rows):
    n, d = h.shape
    nw, items, chunk = idx.shape
    tok_w = n // nw
    n_chunks = items // TOP_K

    @functools.partial(
        pl.kernel, mesh=_sc_mesh(),
        out_type=jax.ShapeDtypeStruct((p_rows, d), h.dtype),
        scratch_types=[pltpu.VMEM((items, chunk), jnp.int32),
                       pltpu.VMEM((2, chunk, d), h.dtype),
                       pltpu.SemaphoreType.DMA((2,)), pltpu.SemaphoreType.DMA((2,))],
        name="sc_dispatch",
    )
    def run(h_hbm, idx_hbm, out_hbm, idx_v, rows_v, lsem, ssem):
        wid = lax.axis_index("s") * SC_CORES + lax.axis_index("c")
        base = wid * tok_w
        pltpu.sync_copy(idx_hbm.at[wid], idx_v)

        def load(j, slot):
            return pltpu.make_async_copy(h_hbm.at[pl.ds(base + j * chunk, chunk)], rows_v.at[slot], lsem.at[slot])

        def scat(j, kk, slot):
            return pltpu.make_async_copy(rows_v.at[slot], out_hbm.at[idx_v.at[j * TOP_K + kk]], ssem.at[slot])

        load(0, 0).start()

        @pl.loop(0, n_chunks, step=2)
        def _(j0):
            for b in range(2):
                j = j0 + b

                @pl.when(j >= 1)
                def _():
                    for kk in range(TOP_K):
                        scat(j - 1, kk, 1 - b).wait()

                @pl.when(j + 1 < n_chunks)
                def _():
                    load(j + 1, 1 - b).start()

                load(j, b).wait()
                for kk in range(TOP_K):
                    scat(j, kk, b).start()

        for kk in range(TOP_K):
            scat(n_chunks - 1, kk, (n_chunks - 1) % 2).wait()

    return run(h, idx)


def _sc_gather(y, idx, n):
    d = y.shape[1]
    nw, items, chunk = idx.shape
    tok_w = n // nw

    @functools.partial(
        pl.kernel, mesh=_sc_mesh(),
        out_type=jax.ShapeDtypeStruct((TOP_K, n, d), y.dtype),
        scratch_types=[pltpu.VMEM((items, chunk), jnp.int32),
                       pltpu.VMEM((2, chunk, d), y.dtype),
                       pltpu.SemaphoreType.DMA((2,)), pltpu.SemaphoreType.DMA((2,))],
        name="sc_gather",
    )
    def run(y_hbm, idx_hbm, out_hbm, idx_v, rows_v, gsem, wsem):
        wid = lax.axis_index("s") * SC_CORES + lax.axis_index("c")
        base = wid * tok_w
        pltpu.sync_copy(idx_hbm.at[wid], idx_v)

        def gather(i, slot):
            return pltpu.make_async_copy(y_hbm.at[idx_v.at[i]], rows_v.at[slot], gsem.at[slot])

        def write(i, slot):
            dst = out_hbm.at[i % TOP_K, pl.ds(base + (i // TOP_K) * chunk, chunk)]
            return pltpu.make_async_copy(rows_v.at[slot], dst, wsem.at[slot])

        gather(0, 0).start()

        @pl.loop(0, items, step=2)
        def _(i0):
            for b in range(2):
                i = i0 + b

                @pl.when(i >= 1)
                def _():
                    write(i - 1, 1 - b).wait()

                @pl.when(i + 1 < items)
                def _():
                    gather(i + 1, 1 - b).start()

                gather(i, b).wait()
                write(i, b).start()

        write(items - 1, (items - 1) % 2).wait()

    return run(y, idx)


def _combine_kernel(n_ctx_tiles, x1_ref, yg_ref, wts_ref, mod_ref, op_ref, os_ref):
    t = pl.program_id(0)
    gt2 = mod_ref[0, 5:6, :]
    w = wts_ref[...]
    acc_lo, acc_hi = None, None
    for kk in range(TOP_K):
        y_lo, y_hi = _unpack_halves(yg_ref[kk])
        wk = w[:, kk:kk + 1]
        acc_lo = wk * y_lo if acc_lo is None else acc_lo + wk * y_lo
        acc_hi = wk * y_hi if acc_hi is None else acc_hi + wk * y_hi
    out = x1_ref[...] + gt2 * jnp.concatenate([acc_lo, acc_hi], axis=1)

    @pl.when(t < n_ctx_tiles)
    def _():
        op_ref[...] = out

    @pl.when(t >= n_ctx_tiles)
    def _():
        os_ref[...] = out


def _combine(x1, yg, wts, mod, n_ctx, tiles_per_lat_batch):
    n, d = x1.shape
    tm = TOK_TILE
    n_ctx_tiles = n_ctx // tm
    last = n_ctx_tiles - 1

    def mod_row(t):
        return jnp.where(t < n_ctx_tiles, 0, 1 + (t - n_ctx_tiles) // tiles_per_lat_batch)

    return pl.pallas_call(
        functools.partial(_combine_kernel, n_ctx_tiles),
        out_shape=(jax.ShapeDtypeStruct((n_ctx, d), F32), jax.ShapeDtypeStruct((n - n_ctx, d), F32)),
        grid=(n // tm,),
        in_specs=[pl.BlockSpec((tm, d), lambda t: (t, 0)),
                  pl.BlockSpec((TOP_K, tm, d // 2), lambda t: (0, t, 0)),
                  pl.BlockSpec((tm, TOP_K), lambda t: (t, 0)),
                  pl.BlockSpec((1,) + mod.shape[1:], lambda t: (mod_row(t), 0, 0))],
        out_specs=(pl.BlockSpec((tm, d), lambda t: (jnp.minimum(t, last), 0)),
                   pl.BlockSpec((tm, d), lambda t: (jnp.maximum(t - n_ctx_tiles, 0), 0))),
        compiler_params=_cparams(1),
        name="combine",
    )(x1, yg, wts, mod)


def kernel(x_prompt, x_sample, cache_k, cache_v, c, c_ctx, g_norm1, w_ada, b_ada, w_in, g_q, g_k, rpb,
           w_fmap, w_amap, w_gate, b_gate, w_out, g_norm2, w_router, b_router, w1, b1, w2, b2):
    batch, seq, d = x_prompt.shape
    dec_batch, dec_seq, _ = x_sample.shape
    assert w_ada.shape[0] == 1, "single-layer trunk"
    rows = dec_seq // GRID_W
    da = N_HEADS * HEAD_DIM
    n_ctx = batch * seq
    n_lat = dec_batch * dec_seq
    n = n_ctx + n_lat
    ne = w_router.shape[2]
    assert TOK_TILE % seq == 0 and n_ctx % TOK_TILE == 0 and dec_seq % TOK_TILE == 0
    assert rows % (ATT_TILE // GRID_W) == 0 and rows >= 3 * (ATT_TILE // GRID_W) and rows >= WIN_H_MAX
    assert seq == ATT_TILE and cache_k.shape[3] == ATT_TILE
    tiles_per_lat_batch = dec_seq // TOK_TILE

    n_mod_rows = 8
    cvecs = jnp.zeros((n_mod_rows, d), F32).at[0].set(c_ctx).at[1:1 + dec_batch].set(c)
    mod = _ada(cvecs, w_ada[0], b_ada[0]).reshape(n_mod_rows, 6, d)

    xp = x_prompt.reshape(n_ctx, d)
    xs = x_sample.reshape(n_lat, d)
    g1 = g_norm1[0].reshape(1, d)
    g2 = g_norm2[0].reshape(1, d)
    hsum = jnp.asarray(np.kron(np.eye(N_HEADS), np.full((HEAD_DIM, HEAD_DIM), 1.0 / HEAD_DIM)), BF16)
    gq_t = jnp.tile(g_q[0], N_HEADS).reshape(1, da)
    gk_t = jnp.tile(g_k[0], N_HEADS).reshape(1, da)

    u, q, k, v, new_k, new_v = _proj(xp, xs, mod, g1, w_in[0].astype(BF16), hsum, gq_t, gk_t,
                                      batch, seq, tiles_per_lat_batch)

    aa_c = _attn_ctx(q, k, v, batch, seq)
    bias = _window_bias(rpb[0], rows)
    past = cache_k.shape[3]
    kc = cache_k[:, 0].transpose(0, 2, 1, 3).reshape(dec_batch, past, da).astype(BF16)
    vc = cache_v[:, 0].transpose(0, 2, 1, 3).reshape(dec_batch, past, da).astype(BF16)
    aa_l = _attn_lat(q, k, v, kc, vc, bias, n_ctx, dec_batch, rows)

    w1c, ct_c, st_c = _dft_tables(seq)
    _, ct_l, st_l = _dft_tables(dec_seq)
    w1c = jnp.asarray(w1c).astype(BF16)
    af_c = _fourier(u, w1c, jnp.asarray(ct_c).astype(BF16), jnp.asarray(st_c).astype(BF16),
                    batch, seq, 0, seq)
    af_l = _fourier(u, w1c, jnp.asarray(ct_l).astype(BF16), jnp.asarray(st_l).astype(BF16),
                    dec_batch, dec_seq, n_ctx // dec_seq, TOK_TILE)

    x1, h2, idx, wts, rank, counts = _merge(
        xp, xs, mod, g1, g2, af_c, af_l, aa_c, aa_l,
        w_fmap[0].astype(BF16), w_amap[0].astype(BF16), w_gate[0].astype(BF16), b_gate[0].reshape(1, -1),
        w_out[0].astype(BF16), w_router[0], b_router[0].reshape(1, ne), tiles_per_lat_batch)

    tm = MOE_TILE
    max_tiles = (n * TOP_K) // tm + ne
    cnt = counts[0].astype(jnp.int32)
    tiles_e = (cnt + tm - 1) // tm
    tile_end = jnp.cumsum(tiles_e)
    pad_off = (tile_end - tiles_e) * tm
    n_tiles = tile_end[-1:]
    tile_ids = jnp.arange(max_tiles, dtype=jnp.int32)
    tile_expert = jnp.sum((tile_ids[:, None] >= tile_end[None, :]).astype(jnp.int32), axis=1)
    last_e = jnp.sum((n_tiles - 1 >= tile_end).astype(jnp.int32))
    tile_expert = jnp.minimum(tile_expert, last_e).astype(jnp.int32)
    pos = jnp.sum(jnp.where(idx[:, :, None] == jnp.arange(ne)[None, None, :], pad_off[None, None, :], 0),
                  axis=-1) + rank

    p_rows = max_tiles * tm
    n_workers = SC_CORES * SC_SUBCORES

    def index_lists(pos_rows):
        rows = pos_rows.shape[0]
        assert rows % (n_workers * SC_CHUNK * 2) == 0
        chunks = rows // (n_workers * SC_CHUNK)
        lists = pos_rows.astype(jnp.int32).reshape(n_workers, chunks, SC_CHUNK, TOP_K).transpose(0, 1, 3, 2)
        return lists.reshape(n_workers, chunks * TOP_K, SC_CHUNK)

    pos_lists = index_lists(pos)
    xs_sorted = _sc_dispatch(h2, pos_lists, p_rows)
    e_ids = jnp.arange(ne, dtype=jnp.int32)
    later_used = (e_ids[None, :] > e_ids[:, None]) & (tiles_e[None, :] > 0)
    next_of_e = jnp.min(jnp.where(later_used, e_ids[None, :], ne), axis=1)
    next_of_e = jnp.where(next_of_e == ne, -1, next_of_e)
    is_e = tile_expert[:, None] == e_ids[None, :]
    next_expert = jnp.sum(jnp.where(is_e, next_of_e[None, :], 0), axis=1)
    rows_left = jnp.sum(jnp.where(is_e, (cnt + pad_off)[None, :], 0), axis=1) - tile_ids * tm
    rows_valid = jnp.where(tile_ids < n_tiles[0], jnp.clip(rows_left, 0, tm), 0)
    run_of_e = jnp.cumsum((tiles_e > 0).astype(jnp.int32)) - 1
    slots = jnp.sum(jnp.where(is_e, run_of_e[None, :], 0), axis=1) % 2
    y_sorted = _moe(tile_expert, n_tiles.astype(jnp.int32), next_expert.astype(jnp.int32),
                    rows_valid.astype(jnp.int32), slots.astype(jnp.int32), xs_sorted,
                    w1[0], b1[0], w2[0], b2[0])
    yg = _sc_gather(y_sorted, pos_lists, n)
    y_p, y_s = _combine(x1, yg, wts, mod, n_ctx, tiles_per_lat_batch)
    return (y_p.reshape(batch, seq, d), y_s.reshape(dec_batch, dec_seq, d), new_k, new_v)
```

```python
import functools
import math

import numpy as np
import jax
import jax.numpy as jnp
from jax import lax
from jax.experimental import pallas as pl
from jax.experimental.pallas import tpu as pltpu
from jax.experimental.pallas import tpu_sc as plsc

F32 = jnp.float32
BF16 = jnp.bfloat16

GRID_W = 64
N_HEADS = 8
HEAD_DIM = 64
N_FGROUPS = 4
FGROUP_DIM = 128
WIN_H_MAX = 8
WIN_W = 16
TOP_K = 4
SWIGLU_LIMIT = 7.0
SWIGLU_ALPHA = 1.702
EPS = 1e-6
LOG2E = math.log2(math.e)
QK_PRESCALE = HEAD_DIM ** -0.5 * LOG2E

LANES = 128
SUBLANES = 8
ADA_COLS = 1024
TOK_TILE = 512
COMBINE_TILE = 1024
ATT_TILE = 256
MOE_TILE = 1024
MOE_SUB = 256
MERGE_PARTS = 2
VMEM_LIMIT = 56 * 1024 * 1024
NEG_BIG = -1e30

SC_CORES = 2
SC_SUBCORES = 16
SC_CHUNK = 64


def _cparams(n_axes, vmem=VMEM_LIMIT):
    return pltpu.CompilerParams(dimension_semantics=("arbitrary",) * n_axes, vmem_limit_bytes=vmem)


def _rms(x):
    return x * lax.rsqrt(jnp.mean(x * x, axis=-1, keepdims=True) + EPS)


def _pack_halves(x):
    c = x.shape[1] // 2
    lo = lax.bitcast_convert_type(x[:, :c].astype(BF16).astype(F32), jnp.uint32)
    hi = lax.bitcast_convert_type(x[:, c:].astype(BF16).astype(F32), jnp.uint32)
    return lax.bitcast_convert_type(hi | (lo >> 16), jnp.int32)


def _unpack_halves(w):
    u = lax.bitcast_convert_type(w, jnp.uint32)
    lo = lax.bitcast_convert_type(u << 16, F32)
    hi = lax.bitcast_convert_type(u & jnp.uint32(0xFFFF0000), F32)
    return lo, hi


def _ada_kernel(c_ref, w_ref, b_ref, o_ref):
    cv = c_ref[...]
    s = cv * jax.nn.sigmoid(cv)
    w = w_ref[...]
    s_hi = s.astype(BF16)
    s_lo = (s - s_hi.astype(F32)).astype(BF16)
    w_hi = w.astype(BF16)
    w_lo = (w - w_hi.astype(F32)).astype(BF16)
    o_ref[...] = (jnp.dot(s_hi, w_hi, preferred_element_type=F32)
                  + jnp.dot(s_hi, w_lo, preferred_element_type=F32)
                  + jnp.dot(s_lo, w_hi, preferred_element_type=F32) + b_ref[...])


def _ada(cvecs, w_ada, b_ada):
    rows, d = cvecs.shape
    n = w_ada.shape[1]
    blk = ADA_COLS
    return pl.pallas_call(
        _ada_kernel,
        out_shape=jax.ShapeDtypeStruct((rows, n), F32),
        grid=(n // blk,),
        in_specs=[pl.BlockSpec((rows, d), lambda j: (0, 0)),
                  pl.BlockSpec((d, blk), lambda j: (0, j)),
                  pl.BlockSpec((1, blk), lambda j: (0, j))],
        out_specs=pl.BlockSpec((rows, blk), lambda j: (0, j)),
        compiler_params=_cparams(1),
        name="ada",
    )(cvecs, w_ada, b_ada.reshape(1, n))


def _proj_kernel(n_ctx_tiles, xp_ref, xs_ref, mod_ref, g1_ref, win_ref, hsum_ref, gq_ref, gk_ref,
                 u_ref, q_ref, k_ref, v_ref, nk_ref, nv_ref):
    t = pl.program_id(0)
    is_ctx = t < n_ctx_tiles
    sh1 = mod_ref[0, 0:1, :]
    sc1 = mod_ref[0, 1:2, :]
    da = q_ref.shape[1]
    df = u_ref.shape[1]
    nb, _, nh, s, dh = nk_ref.shape
    parts = [slice(b * s, (b + 1) * s) for b in range(nb)]
    hs = [(_rms(jnp.where(is_ctx, xp_ref[rs, :], xs_ref[rs, :])) * g1_ref[...] * (1.0 + sc1) + sh1).astype(BF16)
          for rs in parts]
    projs = [jnp.dot(h, win_ref[...], preferred_element_type=F32) for h in hs]
    kns, vs = [], []
    for rs, proj in zip(parts, projs):
        u_ref[rs, :] = proj[:, :df].astype(BF16)
        q = proj[:, df:df + da]
        k = proj[:, df + da:df + 2 * da]
        v = proj[:, df + 2 * da:]
        msq = jnp.dot((q * q).astype(BF16), hsum_ref[...], preferred_element_type=F32)
        msk = jnp.dot((k * k).astype(BF16), hsum_ref[...], preferred_element_type=F32)
        qn = q * lax.rsqrt(msq + EPS) * gq_ref[...]
        kn = k * lax.rsqrt(msk + EPS) * gk_ref[...]
        q_ref[rs, :] = (qn * QK_PRESCALE).astype(BF16)
        k_ref[rs, :] = kn.astype(BF16)
        v_ref[rs, :] = v.astype(BF16)
        kns.append(kn)
        vs.append(v)

    @pl.when(is_ctx)
    def _():
        for b in range(nb):
            for hd in range(nh):
                nk_ref[b, 0, hd] = kns[b][:, hd * dh:(hd + 1) * dh]
                nv_ref[b, 0, hd] = vs[b][:, hd * dh:(hd + 1) * dh]


def _proj(xp, xs, mod, g1, w_in_b, hsum, gq_t, gk_t, batch, seq, tiles_per_lat_batch):
    n_ctx, d = xp.shape
    n_lat = xs.shape[0]
    tm = TOK_TILE
    n_ctx_tiles = n_ctx // tm
    n_tiles = (n_ctx + n_lat) // tm
    n = n_ctx + n_lat
    d_in = w_in_b.shape[1]
    da = N_HEADS * HEAD_DIM
    df = d_in - 3 * da
    bpt = tm // seq
    last = n_ctx_tiles - 1

    def mod_row(t):
        return jnp.where(t < n_ctx_tiles, 0, 1 + (t - n_ctx_tiles) // tiles_per_lat_batch)

    tok = lambda w: pl.BlockSpec((tm, w), lambda t: (t, 0))
    full = lambda a: pl.BlockSpec(a.shape, lambda t: (0,) * a.ndim)
    kv_spec = pl.BlockSpec((bpt, 1, N_HEADS, seq, HEAD_DIM), lambda t: (jnp.minimum(t, last), 0, 0, 0, 0))
    kv_shape = jax.ShapeDtypeStruct((batch, 1, N_HEADS, seq, HEAD_DIM), F32)
    return pl.pallas_call(
        functools.partial(_proj_kernel, n_ctx_tiles),
        out_shape=(jax.ShapeDtypeStruct((n, df), BF16),) + (jax.ShapeDtypeStruct((n, da), BF16),) * 3
        + (kv_shape, kv_shape),
        grid=(n_tiles,),
        in_specs=[pl.BlockSpec((tm, d), lambda t: (jnp.minimum(t, last), 0)),
                  pl.BlockSpec((tm, d), lambda t: (jnp.maximum(t - n_ctx_tiles, 0), 0)),
                  pl.BlockSpec((1,) + mod.shape[1:], lambda t: (mod_row(t), 0, 0)),
                  full(g1), full(w_in_b), full(hsum), full(gq_t), full(gk_t)],
        out_specs=(tok(df), tok(da), tok(da), tok(da), kv_spec, kv_spec),
        compiler_params=_cparams(1),
        name="proj",
    )(xp, xs, mod, g1, w_in_b, hsum, gq_t, gk_t)


def _softmax_pv(q2, key_blocks, val_blocks, bias_blocks):
    lane = lax.broadcasted_iota(jnp.int32, (1, LANES), 1)
    masks = [lane < HEAD_DIM, lane >= HEAD_DIM]

    def head_scores(half):
        qh = jnp.where(masks[half], q2, jnp.zeros_like(q2))
        scores = []
        for kb, bb in zip(key_blocks, bias_blocks):
            s = lax.dot_general(qh, kb, (((1,), (1,)), ((), ())), preferred_element_type=F32)
            if bb is not None:
                s = s + bb[half]
            scores.append(s)
        return scores

    def head_out(scores):
        m = jnp.max(functools.reduce(jnp.maximum, scores), axis=-1, keepdims=True)
        ps = [jnp.exp2(s - m) for s in scores]
        denom = jnp.sum(functools.reduce(lambda a, b: a + b, ps), axis=-1, keepdims=True)
        o = functools.reduce(lambda a, b: a + b,
                             [jnp.dot(p.astype(BF16), vb, preferred_element_type=F32)
                              for p, vb in zip(ps, val_blocks)])
        return o / denom

    if len(key_blocks) > 1:
        outs = [head_out(s) for s in [head_scores(0), head_scores(1)]]
    else:
        outs = [head_out(head_scores(half)) for half in range(2)]
    return jnp.where(masks[0], outs[0], outs[1])


def _attn_ctx_kernel(q_ref, k_ref, v_ref, o_ref):
    for p in range(q_ref.shape[1] // LANES):
        sl = slice(p * LANES, (p + 1) * LANES)
        o = _softmax_pv(q_ref[:, sl], [k_ref[:, sl]], [v_ref[:, sl]], [None])
        o_ref[:, sl] = o.astype(BF16)


def _attn_ctx(q, k, v, batch, seq):
    da = q.shape[1]
    spec = pl.BlockSpec((seq, da), lambda b: (b, 0))
    return pl.pallas_call(
        _attn_ctx_kernel,
        out_shape=jax.ShapeDtypeStruct((batch * seq, da), BF16),
        grid=(batch,),
        in_specs=[spec, spec, spec],
        out_specs=spec,
        compiler_params=_cparams(1),
        name="attn_ctx",
    )(q, k, v)


def _attn_lat_kernel(q_ref, k0_ref, k1_ref, k2_ref, v0_ref, v1_ref, v2_ref, kc_ref, vc_ref, bias_ref, o_ref):
    tk = k0_ref.shape[0]
    for p in range(q_ref.shape[1] // LANES):
        sl = slice(p * LANES, (p + 1) * LANES)
        keys = [k0_ref[:, sl], k1_ref[:, sl], k2_ref[:, sl], kc_ref[0, :, sl]]
        vals = [v0_ref[:, sl], v1_ref[:, sl], v2_ref[:, sl], vc_ref[0, :, sl]]
        biases = [[bias_ref[0, 2 * p + half, :, d * tk:(d + 1) * tk] for half in range(2)] for d in range(3)]
        o = _softmax_pv(q_ref[:, sl], keys, vals, biases + [None])
        o_ref[:, sl] = o.astype(BF16)


def _attn_lat(q, k, v, kc, vc, bias, n_ctx, dec_batch, rows):
    da = q.shape[1]
    tq = ATT_TILE
    rows_per_tile = tq // GRID_W
    tiles = rows // rows_per_tile
    base = n_ctx // tq
    n_lat = dec_batch * rows * GRID_W
    max_start = tiles - 3

    def qmap(j, b):
        return (base + b * tiles + j, 0)

    def kmap(d):
        return lambda j, b: (base + b * tiles + jnp.clip(j - 1, 0, max_start) + d, 0)

    def bmap(j, b):
        return (jnp.where(j == 0, 0, jnp.where(j == tiles - 1, 2, 1)), 0, 0, 0)

    blk = lambda m: pl.BlockSpec((tq, da), m)
    cspec = pl.BlockSpec((1,) + kc.shape[1:], lambda j, b: (b, 0, 0))
    return pl.pallas_call(
        _attn_lat_kernel,
        out_shape=jax.ShapeDtypeStruct((n_lat, da), BF16),
        grid=(tiles, dec_batch),
        in_specs=[blk(qmap)] + [blk(kmap(d)) for d in range(3)] + [blk(kmap(d)) for d in range(3)]
        + [cspec, cspec, pl.BlockSpec((1,) + bias.shape[1:], bmap)],
        out_specs=pl.BlockSpec((tq, da), lambda j, b: (b * tiles + j, 0)),
        compiler_params=_cparams(2),
        name="attn_lat",
    )(q, k, k, k, v, v, v, kc, vc, bias)


def _window_tables(rows):
    rpt = ATT_TILE // GRID_W
    krows = 3 * rpt
    tiles = rows // rpt
    kh = min(WIN_H_MAX, rows)
    rpb_w = 2 * WIN_W - 1
    arow = -np.ones((3, rpt, krows), np.int64)
    for var, j in enumerate((0, 1, tiles - 1)):
        r0 = j * rpt
        k0 = int(np.clip(j - 1, 0, tiles - 3)) * rpt
        for qi in range(rpt):
            qrow = r0 + qi
            rs = int(np.clip(qrow - kh // 2, 0, rows - kh))
            for ki in range(krows):
                krow = k0 + ki
                if rs <= krow < rs + kh:
                    arow[var, qi, ki] = krow - qrow + WIN_H_MAX - 1
    ec = np.zeros((GRID_W, GRID_W, rpb_w), np.float32)
    for qc in range(GRID_W):
        cs = int(np.clip(qc - WIN_W // 2, 0, GRID_W - WIN_W))
        for kc in range(cs, cs + WIN_W):
            ec[qc, kc, int(np.clip(kc - qc, -(WIN_W - 1), WIN_W - 1)) + WIN_W - 1] = 1.0
    return arow.tolist(), ec


def _bias_kernel(arow, u_ref, o_ref):
    var_id = pl.program_id(0)
    for var, table in enumerate(arow):
        @pl.when(var_id == var)
        def _(table=table):
            for qi, row in enumerate(table):
                for ki, a in enumerate(row):
                    blk = u_ref[0, a] if a >= 0 else jnp.full((GRID_W, GRID_W), NEG_BIG, F32)
                    o_ref[0, 0, qi * GRID_W:(qi + 1) * GRID_W, ki * GRID_W:(ki + 1) * GRID_W] = blk


def _window_bias(rpb, rows):
    arow, ec = _window_tables(rows)
    nh = rpb.shape[0]
    col_blocks = jnp.einsum("hab,xyb->haxy", rpb, ec, precision=lax.Precision.HIGHEST)
    col_blocks = jnp.where(ec.sum(-1) > 0.5, col_blocks * LOG2E, NEG_BIG)
    return pl.pallas_call(
        functools.partial(_bias_kernel, arow),
        out_shape=jax.ShapeDtypeStruct((len(arow), nh, ATT_TILE, 3 * ATT_TILE), F32),
        grid=(len(arow), nh),
        in_specs=[pl.BlockSpec((1,) + col_blocks.shape[1:], lambda v, h: (h, 0, 0, 0))],
        out_specs=pl.BlockSpec((1, 1, ATT_TILE, 3 * ATT_TILE), lambda v, h: (v, h, 0, 0)),
        compiler_params=_cparams(2),
        name="window_bias",
    )(col_blocks)


def _fourier_kernel(u_ref, w1_ref, ct_ref, st_ref, o_ref, p_scr, q_scr):
    @pl.when(pl.program_id(1) == 0)
    def _():
        pq = jnp.dot(u_ref[...], w1_ref[...], preferred_element_type=F32)
        df = p_scr.shape[1]
        p_scr[...] = pq[:, :df].astype(BF16)
        q_scr[...] = pq[:, df:].astype(BF16)

    o = (jnp.dot(ct_ref[...], p_scr[...], preferred_element_type=F32)
         - jnp.dot(st_ref[...], q_scr[...], preferred_element_type=F32))
    o_ref[...] = o.astype(BF16)


def _fourier(u, w1, ct, st, batch, t_len, first_block, row_tile):
    df = u.shape[1]
    steps = t_len // row_tile
    return pl.pallas_call(
        _fourier_kernel,
        out_shape=jax.ShapeDtypeStruct((batch * t_len, df), BF16),
        grid=(batch, steps),
        in_specs=[pl.BlockSpec((t_len, df), lambda b, i: (first_block + b, 0)),
                  pl.BlockSpec(w1.shape, lambda b, i: (0, 0)),
                  pl.BlockSpec((row_tile, t_len), lambda b, i: (i, 0)),
                  pl.BlockSpec((row_tile, t_len), lambda b, i: (i, 0))],
        out_specs=pl.BlockSpec((row_tile, df), lambda b, i: (b * steps + i, 0)),
        scratch_shapes=[pltpu.VMEM((t_len, df), BF16), pltpu.VMEM((t_len, df), BF16)],
        compiler_params=_cparams(2),
        name=f"fourier_{t_len}",
    )(u, w1, ct, st)


def _dft_tables(t_len):
    c = FGROUP_DIM
    jk = np.outer(np.arange(c), np.arange(c)) % c
    ang = 2.0 * np.pi * jk / c
    eye = np.eye(N_FGROUPS)
    w1 = np.concatenate([np.kron(eye, np.cos(ang)), np.kron(eye, np.sin(ang))], axis=1) / np.sqrt(c)
    tt = np.outer(np.arange(t_len), np.arange(t_len)) % t_len
    angt = 2.0 * np.pi * tt / t_len
    return (w1.astype(np.float32), (np.cos(angt) / np.sqrt(t_len)).astype(np.float32),
            (np.sin(angt) / np.sqrt(t_len)).astype(np.float32))


def _merge_kernel(n_ctx_tiles, xp_ref, xs_ref, mod_ref, g1_ref, g2_ref, afc_ref, afl_ref, aac_ref, aal_ref,
                  wf_ref, wa_ref, wg_ref, bg_ref, wo_ref, wrh_ref, wrl_ref, br_ref,
                  x1_ref, h2_ref, idx_ref, wts_ref, rank_ref, cnt_ref, carry):
    t = pl.program_id(0)
    is_ctx = t < n_ctx_tiles

    @pl.when(t == 0)
    def _():
        carry[...] = jnp.zeros_like(carry)

    sh1 = mod_ref[0, 0:1, :]
    sc1 = mod_ref[0, 1:2, :]
    gt1 = mod_ref[0, 2:3, :]
    sh2 = mod_ref[0, 3:4, :]
    sc2 = mod_ref[0, 4:5, :]
    d = xp_ref.shape[1]
    part_rows = xp_ref.shape[0] // MERGE_PARTS
    parts = [slice(p * part_rows, (p + 1) * part_rows) for p in range(MERGE_PARTS)]
    xs_ = [jnp.where(is_ctx, xp_ref[rs, :], xs_ref[rs, :]) for rs in parts]
    hbs = [(_rms(x) * g1_ref[...] * (1.0 + sc1) + sh1).astype(BF16) for x in xs_]
    gates = [jax.nn.sigmoid(jnp.dot(hb, wg_ref[...], preferred_element_type=F32) + bg_ref[...]) for hb in hbs]
    fas = [jnp.dot(jnp.where(is_ctx, afc_ref[rs, :], afl_ref[rs, :]), wf_ref[...], preferred_element_type=F32)
           for rs in parts]
    fbs = [jnp.dot(jnp.where(is_ctx, aac_ref[rs, :], aal_ref[rs, :]), wa_ref[...], preferred_element_type=F32)
           for rs in parts]
    mixes = [(g[:, :d] * fa + g[:, d:] * fb).astype(BF16) for g, fa, fb in zip(gates, fas, fbs)]
    x1s = [x + gt1 * jnp.dot(mix, wo_ref[...], preferred_element_type=F32) for x, mix in zip(xs_, mixes)]
    h2s = []
    for rs, x1 in zip(parts, x1s):
        x1_ref[rs, :] = x1
        h2 = _rms(x1) * g2_ref[...] * (1.0 + sc2) + sh2
        h2_ref[rs, :] = _pack_halves(h2)
        h2s.append(h2)
    logit_parts = []
    for h2 in h2s:
        h2_hi = h2.astype(BF16)
        h2_lo = (h2 - h2_hi.astype(F32)).astype(BF16)
        logit_parts.append(jnp.dot(h2_hi, wrh_ref[...], preferred_element_type=F32)
                           + jnp.dot(h2_hi, wrl_ref[...], preferred_element_type=F32)
                           + jnp.dot(h2_lo, wrh_ref[...], preferred_element_type=F32))

    logits = jnp.concatenate(logit_parts, axis=0) + br_ref[...]
    idx_o, wts_o, rank_o, counts = _route(logits, carry[...])
    idx_ref[...] = idx_o
    wts_ref[...] = wts_o
    rank_ref[...] = rank_o
    carry[...] = counts
    cnt_ref[...] = counts


def _route(logits, counts):
    tm, ne = logits.shape
    col = lax.broadcasted_iota(jnp.int32, (tm, ne), 1)
    colk = lax.broadcasted_iota(jnp.int32, (tm, TOP_K), 1)
    lg = logits
    vals, idxs = [], []
    for _ in range(TOP_K):
        m = jnp.max(lg, axis=-1, keepdims=True)
        am = jnp.min(jnp.where(lg == m, col, ne), axis=-1, keepdims=True)
        vals.append(m)
        idxs.append(am)
        lg = jnp.where(col == am, -jnp.inf, lg)
    es = [jnp.exp(v - vals[0]) for v in vals]
    den = functools.reduce(lambda a, b: a + b, es)
    onehot = functools.reduce(lambda a, b: a + b, [(col == am).astype(F32) for am in idxs])
    r_i = lax.broadcasted_iota(jnp.int32, (tm, tm), 0)
    c_i = lax.broadcasted_iota(jnp.int32, (tm, tm), 1)
    lower = (r_i > c_i).astype(BF16)
    before = jnp.dot(lower, onehot.astype(BF16), preferred_element_type=F32) + counts
    idx_o = jnp.zeros((tm, TOP_K), jnp.int32)
    wts_o = jnp.zeros((tm, TOP_K), F32)
    rank_o = jnp.zeros((tm, TOP_K), jnp.int32)
    for kk in range(TOP_K):
        rk = jnp.sum(jnp.where(col == idxs[kk], before, 0.0), axis=-1, keepdims=True).astype(jnp.int32)
        idx_o = jnp.where(colk == kk, idxs[kk], idx_o)
        wts_o = jnp.where(colk == kk, es[kk] / den, wts_o)
        rank_o = jnp.where(colk == kk, rk, rank_o)
    return idx_o, wts_o, rank_o, counts + jnp.sum(onehot, axis=0, keepdims=True)


def _merge(xp, xs, mod, g1, g2, af_c, af_l, aa_c, aa_l, wf, wa, wg, bg, wo, wr, br, tiles_per_lat_batch):
    n_ctx, d = xp.shape
    n = n_ctx + xs.shape[0]
    tm = TOK_TILE
    n_ctx_tiles = n_ctx // tm
    last = n_ctx_tiles - 1
    ne = wr.shape[1]
    da = aa_c.shape[1]
    df = af_c.shape[1]

    def mod_row(t):
        return jnp.where(t < n_ctx_tiles, 0, 1 + (t - n_ctx_tiles) // tiles_per_lat_batch)

    cmap = lambda t: (jnp.minimum(t, last), 0)
    lmap = lambda t: (jnp.maximum(t - n_ctx_tiles, 0), 0)
    full = lambda a: pl.BlockSpec(a.shape, lambda t: (0,) * a.ndim)
    tok = lambda w: pl.BlockSpec((tm, w), lambda t: (t, 0))
    wr_hi = wr.astype(BF16)
    wr_lo = (wr - wr_hi.astype(F32)).astype(BF16)
    return pl.pallas_call(
        functools.partial(_merge_kernel, n_ctx_tiles),
        out_shape=(jax.ShapeDtypeStruct((n, d), F32), jax.ShapeDtypeStruct((n, d // 2), jnp.int32),
                   jax.ShapeDtypeStruct((n, TOP_K), jnp.int32), jax.ShapeDtypeStruct((n, TOP_K), F32),
                   jax.ShapeDtypeStruct((n, TOP_K), jnp.int32), jax.ShapeDtypeStruct((1, ne), F32)),
        grid=(n // tm,),
        in_specs=[pl.BlockSpec((tm, d), cmap), pl.BlockSpec((tm, d), lmap),
                  pl.BlockSpec((1,) + mod.shape[1:], lambda t: (mod_row(t), 0, 0)),
                  full(g1), full(g2),
                  pl.BlockSpec((tm, df), cmap), pl.BlockSpec((tm, df), lmap),
                  pl.BlockSpec((tm, da), cmap), pl.BlockSpec((tm, da), lmap),
                  full(wf), full(wa), full(wg), full(bg), full(wo), full(wr_hi), full(wr_lo), full(br)],
        out_specs=(tok(d), tok(d // 2), tok(TOP_K), tok(TOP_K), tok(TOP_K),
                   pl.BlockSpec((1, ne), lambda t: (0, 0))),
        scratch_shapes=[pltpu.VMEM((1, ne), F32)],
        compiler_params=_cparams(1),
        name="merge",
    )(xp, xs, mod, g1, g2, af_c, af_l, aa_c, aa_l, wf, wa, wg, bg, wo, wr_hi, wr_lo, br)


def _moe_kernel(te_ref, nt_ref, nx_ref, rv_ref, sl_ref, x_ref, w1_hbm, b1_ref, w2_hbm, b2_ref, y_ref,
                w1s, w2s, sem):
    i = pl.program_id(0)
    e = te_ref[i]
    prev = te_ref[jnp.maximum(i - 1, 0)]
    first_of_run = (i == 0) | (e != prev)
    slot = sl_ref[i]

    def stage(expert, s):
        return (pltpu.make_async_copy(w1_hbm.at[expert], w1s.at[s], sem.at[0, s]),
                pltpu.make_async_copy(w2_hbm.at[expert], w2s.at[s], sem.at[1, s]))

    @pl.when(i == 0)
    def _():
        for cp in stage(e, slot):
            cp.start()

    @pl.when(first_of_run)
    def _():
        for cp in stage(e, slot):
            cp.wait()

        @pl.when(nx_ref[i] >= 0)
        def _():
            for cp in stage(nx_ref[i], 1 - slot):
                cp.start()

    def expert_rows(rs):
        dff = w2s.shape[1]
        x_lo, x_hi = _unpack_halves(x_ref[rs, :])
        x = jnp.concatenate([x_lo.astype(BF16), x_hi.astype(BF16)], axis=1)
        gu = jnp.dot(x, w1s[slot].astype(BF16), preferred_element_type=F32) + b1_ref[0]
        gate = jnp.minimum(gu[:, :dff], SWIGLU_LIMIT)
        up = jnp.clip(gu[:, dff:], -SWIGLU_LIMIT, SWIGLU_LIMIT)
        glu = gate * jax.nn.sigmoid(SWIGLU_ALPHA * gate)
        act = ((up + 1.0) * glu).astype(BF16)
        y = jnp.dot(act, w2s[slot].astype(BF16), preferred_element_type=F32) + b2_ref[0]
        y_ref[rs, :] = _pack_halves(y)

    tm = x_ref.shape[0]
    rv = rv_ref[i]

    @pl.when(rv == tm)
    def _():
        expert_rows(slice(0, tm))

    for sub in range(tm // MOE_SUB):
        @pl.when((rv < tm) & (rv > sub * MOE_SUB))
        def _(sub=sub):
            expert_rows(slice(sub * MOE_SUB, (sub + 1) * MOE_SUB))


def _moe(tile_expert, n_tiles, next_expert, rows_valid, slots, xs_sorted, w1, b1, w2, b2):
    p, dh = xs_sorted.shape
    ne, d, dff2 = w1.shape
    dff = w2.shape[1]
    tm = MOE_TILE
    max_tiles = p // tm

    def row_map(i, te, nt, nx, rv, sl):
        return (jnp.minimum(i, nt[0] - 1), 0)

    def b_map(i, te, nt, nx, rv, sl):
        return (te[i], 0, 0)

    grid_spec = pltpu.PrefetchScalarGridSpec(
        num_scalar_prefetch=5,
        grid=(max_tiles,),
        in_specs=[pl.BlockSpec((tm, dh), row_map),
                  pl.BlockSpec(memory_space=pl.ANY),
                  pl.BlockSpec((1, 1, dff2), b_map),
                  pl.BlockSpec(memory_space=pl.ANY),
                  pl.BlockSpec((1, 1, d), b_map)],
        out_specs=pl.BlockSpec((tm, dh), row_map),
        scratch_shapes=[pltpu.VMEM((2, d, dff2), F32), pltpu.VMEM((2, dff, d), F32),
                        pltpu.SemaphoreType.DMA((2, 2))],
    )
    return pl.pallas_call(
        _moe_kernel,
        out_shape=jax.ShapeDtypeStruct((p, dh), jnp.int32),
        grid_spec=grid_spec,
        compiler_params=_cparams(1),
        name="moe",
    )(tile_expert, n_tiles, next_expert, rows_valid, slots, xs_sorted,
      w1, b1.reshape(ne, 1, dff2), w2, b2.reshape(ne, 1, d))


def _sc_mesh():
    return plsc.VectorSubcoreMesh(core_axis_name="c", subcore_axis_name="s",
                                  num_cores=SC_CORES, num_subcores=SC_SUBCORES)


def _sc_dispatch(h, idx, p_rows):
    n, d = h.shape
    nw, items, chunk = idx.shape
    tok_w = n // nw
    n_chunks = items // TOP_K

    @functools.partial(
        pl.kernel, mesh=_sc_mesh(),
        out_type=jax.ShapeDtypeStruct((p_rows, d), h.dtype),
        scratch_types=[pltpu.VMEM((items, chunk), jnp.int32),
                       pltpu.VMEM((2, chunk, d), h.dtype),
                       pltpu.SemaphoreType.DMA((2,)), pltpu.SemaphoreType.DMA((2,))],
        name="sc_dispatch",
    )
    def run(h_hbm, idx_hbm, out_hbm, idx_v, rows_v, lsem, ssem):
        wid = lax.axis_index("s") * SC_CORES + lax.axis_index("c")
        base = wid * tok_w
        pltpu.sync_copy(idx_hbm.at[wid], idx_v)

        def load(j, slot):
            return pltpu.make_async_copy(h_hbm.at[pl.ds(base + j * chunk, chunk)], rows_v.at[slot], lsem.at[slot])

        def scat(j, kk, slot):
            return pltpu.make_async_copy(rows_v.at[slot], out_hbm.at[idx_v.at[j * TOP_K + kk]], ssem.at[slot])

        load(0, 0).start()

        @pl.loop(0, n_chunks, step=2)
        def _(j0):
            for b in range(2):
                j = j0 + b

                @pl.when(j >= 1)
                def _():
                    for kk in range(TOP_K):
                        scat(j - 1, kk, 1 - b).wait()

                @pl.when(j + 1 < n_chunks)
                def _():
                    load(j + 1, 1 - b).start()

                load(j, b).wait()
                for kk in range(TOP_K):
                    scat(j, kk, b).start()

        for kk in range(TOP_K):
            scat(n_chunks - 1, kk, (n_chunks - 1) % 2).wait()

    return run(h, idx)


def _sc_gather(y, idx, n):
    d = y.shape[1]
    nw, items, chunk = idx.shape
    tok_w = n // nw

    @functools.partial(
        pl.kernel, mesh=_sc_mesh(),
        out_type=jax.ShapeDtypeStruct((TOP_K, n, d), y.dtype),
        scratch_types=[pltpu.VMEM((items, chunk), jnp.int32),
                       pltpu.VMEM((2, chunk, d), y.dtype),
                       pltpu.SemaphoreType.DMA((2,)), pltpu.SemaphoreType.DMA((2,))],
        name="sc_gather",
    )
    def run(y_hbm, idx_hbm, out_hbm, idx_v, rows_v, gsem, wsem):
        wid = lax.axis_index("s") * SC_CORES + lax.axis_index("c")
        base = wid * tok_w
        pltpu.sync_copy(idx_hbm.at[wid], idx_v)

        def gather(i, slot):
            return pltpu.make_async_copy(y_hbm.at[idx_v.at[i]], rows_v.at[slot], gsem.at[slot])

        def write(i, slot):
            dst = out_hbm.at[i % TOP_K, pl.ds(base + (i // TOP_K) * chunk, chunk)]
            return pltpu.make_async_copy(rows_v.at[slot], dst, wsem.at[slot])

        gather(0, 0).start()

        @pl.loop(0, items, step=2)
        def _(i0):
            for b in range(2):
                i = i0 + b

                @pl.when(i >= 1)
                def _():
                    write(i - 1, 1 - b).wait()

                @pl.when(i + 1 < items)
                def _():
                    gather(i + 1, 1 - b).start()

                gather(i, b).wait()
                write(i, b).start()

        write(items - 1, (items - 1) % 2).wait()

    return run(y, idx)


def _combine_kernel(n_ctx_tiles, x1_ref, yg_ref, wts_ref, mod_ref, op_ref, os_ref):
    t = pl.program_id(0)
    gt2 = mod_ref[0, 5:6, :]
    w = wts_ref[...]
    acc_lo, acc_hi = None, None
    for kk in range(TOP_K):
        y_lo, y_hi = _unpack_halves(yg_ref[kk])
        wk = w[:, kk:kk + 1]
        acc_lo = wk * y_lo if acc_lo is None else acc_lo + wk * y_lo
        acc_hi = wk * y_hi if acc_hi is None else acc_hi + wk * y_hi
    out = x1_ref[...] + gt2 * jnp.concatenate([acc_lo, acc_hi], axis=1)

    @pl.when(t < n_ctx_tiles)
    def _():
        op_ref[...] = out

    @pl.when(t >= n_ctx_tiles)
    def _():
        os_ref[...] = out


def _combine(x1, yg, wts, mod, n_ctx, lat_seq):
    n, d = x1.shape
    tm = COMBINE_TILE
    assert n_ctx % tm == 0 and lat_seq % tm == 0
    tiles_per_lat_batch = lat_seq // tm
    n_ctx_tiles = n_ctx // tm
    last = n_ctx_tiles - 1

    def mod_row(t):
        return jnp.where(t < n_ctx_tiles, 0, 1 + (t - n_ctx_tiles) // tiles_per_lat_batch)

    return pl.pallas_call(
        functools.partial(_combine_kernel, n_ctx_tiles),
        out_shape=(jax.ShapeDtypeStruct((n_ctx, d), F32), jax.ShapeDtypeStruct((n - n_ctx, d), F32)),
        grid=(n // tm,),
        in_specs=[pl.BlockSpec((tm, d), lambda t: (t, 0)),
                  pl.BlockSpec((TOP_K, tm, d // 2), lambda t: (0, t, 0)),
                  pl.BlockSpec((tm, TOP_K), lambda t: (t, 0)),
                  pl.BlockSpec((1,) + mod.shape[1:], lambda t: (mod_row(t), 0, 0))],
        out_specs=(pl.BlockSpec((tm, d), lambda t: (jnp.minimum(t, last), 0)),
                   pl.BlockSpec((tm, d), lambda t: (jnp.maximum(t - n_ctx_tiles, 0), 0))),
        compiler_params=_cparams(1),
        name="combine",
    )(x1, yg, wts, mod)


def kernel(x_prompt, x_sample, cache_k, cache_v, c, c_ctx, g_norm1, w_ada, b_ada, w_in, g_q, g_k, rpb,
           w_fmap, w_amap, w_gate, b_gate, w_out, g_norm2, w_router, b_router, w1, b1, w2, b2):
    batch, seq, d = x_prompt.shape
    dec_batch, dec_seq, _ = x_sample.shape
    assert w_ada.shape[0] == 1, "single-layer trunk"
    rows = dec_seq // GRID_W
    da = N_HEADS * HEAD_DIM
    n_ctx = batch * seq
    n_lat = dec_batch * dec_seq
    n = n_ctx + n_lat
    ne = w_router.shape[2]
    assert TOK_TILE % seq == 0 and n_ctx % TOK_TILE == 0 and dec_seq % TOK_TILE == 0
    assert rows % (ATT_TILE // GRID_W) == 0 and rows >= 3 * (ATT_TILE // GRID_W) and rows >= WIN_H_MAX
    assert seq == ATT_TILE and cache_k.shape[3] == ATT_TILE
    tiles_per_lat_batch = dec_seq // TOK_TILE

    n_mod_rows = -(-(1 + dec_batch) // SUBLANES) * SUBLANES
    cvecs = jnp.zeros((n_mod_rows, d), F32).at[0].set(c_ctx).at[1:1 + dec_batch].set(c)
    mod = _ada(cvecs, w_ada[0], b_ada[0]).reshape(n_mod_rows, 6, d)

    xp = x_prompt.reshape(n_ctx, d)
    xs = x_sample.reshape(n_lat, d)
    g1 = g_norm1[0].reshape(1, d)
    g2 = g_norm2[0].reshape(1, d)
    hsum = jnp.asarray(np.kron(np.eye(N_HEADS), np.full((HEAD_DIM, HEAD_DIM), 1.0 / HEAD_DIM)), BF16)
    gq_t = jnp.tile(g_q[0], N_HEADS).reshape(1, da)
    gk_t = jnp.tile(g_k[0], N_HEADS).reshape(1, da)

    u, q, k, v, new_k, new_v = _proj(xp, xs, mod, g1, w_in[0].astype(BF16), hsum, gq_t, gk_t,
                                      batch, seq, tiles_per_lat_batch)

    aa_c = _attn_ctx(q, k, v, batch, seq)
    bias = _window_bias(rpb[0], rows)
    past = cache_k.shape[3]
    kc = cache_k[:, 0].transpose(0, 2, 1, 3).reshape(dec_batch, past, da).astype(BF16)
    vc = cache_v[:, 0].transpose(0, 2, 1, 3).reshape(dec_batch, past, da).astype(BF16)
    aa_l = _attn_lat(q, k, v, kc, vc, bias, n_ctx, dec_batch, rows)

    w1c, ct_c, st_c = _dft_tables(seq)
    _, ct_l, st_l = _dft_tables(dec_seq)
    w1c = jnp.asarray(w1c).astype(BF16)
    af_c = _fourier(u, w1c, jnp.asarray(ct_c).astype(BF16), jnp.asarray(st_c).astype(BF16),
                    batch, seq, 0, seq)
    af_l = _fourier(u, w1c, jnp.asarray(ct_l).astype(BF16), jnp.asarray(st_l).astype(BF16),
                    dec_batch, dec_seq, n_ctx // dec_seq, TOK_TILE)

    x1, h2, idx, wts, rank, counts = _merge(
        xp, xs, mod, g1, g2, af_c, af_l, aa_c, aa_l,
        w_fmap[0].astype(BF16), w_amap[0].astype(BF16), w_gate[0].astype(BF16), b_gate[0].reshape(1, -1),
        w_out[0].astype(BF16), w_router[0], b_router[0].reshape(1, ne), tiles_per_lat_batch)

    tm = MOE_TILE
    max_tiles = (n * TOP_K) // tm + ne
    cnt = counts[0].astype(jnp.int32)
    tiles_e = (cnt + tm - 1) // tm
    tile_end = jnp.cumsum(tiles_e)
    pad_off = (tile_end - tiles_e) * tm
    n_tiles = tile_end[-1:]
    tile_ids = jnp.arange(max_tiles, dtype=jnp.int32)
    tile_expert = jnp.sum((tile_ids[:, None] >= tile_end[None, :]).astype(jnp.int32), axis=1)
    last_e = jnp.sum((n_tiles - 1 >= tile_end).astype(jnp.int32))
    tile_expert = jnp.minimum(tile_expert, last_e).astype(jnp.int32)
    pos = jnp.sum(jnp.where(idx[:, :, None] == jnp.arange(ne)[None, None, :], pad_off[None, None, :], 0),
                  axis=-1) + rank

    p_rows = max_tiles * tm
    n_workers = SC_CORES * SC_SUBCORES

    def index_lists(pos_rows):
        rows = pos_rows.shape[0]
        assert rows % (n_workers * SC_CHUNK * 2) == 0
        chunks = rows // (n_workers * SC_CHUNK)
        lists = pos_rows.astype(jnp.int32).reshape(n_workers, chunks, SC_CHUNK, TOP_K).transpose(0, 1, 3, 2)
        return lists.reshape(n_workers, chunks * TOP_K, SC_CHUNK)

    pos_lists = index_lists(pos)
    xs_sorted = _sc_dispatch(h2, pos_lists, p_rows)
    e_ids = jnp.arange(ne, dtype=jnp.int32)
    later_used = (e_ids[None, :] > e_ids[:, None]) & (tiles_e[None, :] > 0)
    next_of_e = jnp.min(jnp.where(later_used, e_ids[None, :], ne), axis=1)
    next_of_e = jnp.where(next_of_e == ne, -1, next_of_e)
    is_e = tile_expert[:, None] == e_ids[None, :]
    next_expert = jnp.sum(jnp.where(is_e, next_of_e[None, :], 0), axis=1)
    rows_left = jnp.sum(jnp.where(is_e, (cnt + pad_off)[None, :], 0), axis=1) - tile_ids * tm
    rows_valid = jnp.where(tile_ids < n_tiles[0], jnp.clip(rows_left, 0, tm), 0)
    run_of_e = jnp.cumsum((tiles_e > 0).astype(jnp.int32)) - 1
    slots = jnp.sum(jnp.where(is_e, run_of_e[None, :], 0), axis=1) % 2
    y_sorted = _moe(tile_expert, n_tiles.astype(jnp.int32), next_expert.astype(jnp.int32),
                    rows_valid.astype(jnp.int32), slots.astype(jnp.int32), xs_sorted,
                    w1[0], b1[0], w2[0], b2[0])
    yg = _sc_gather(y_sorted, pos_lists, n)
    y_p, y_s = _combine(x1, yg, wts, mod, n_ctx, dec_seq)
    return (y_p.reshape(batch, seq, d), y_s.reshape(dec_batch, dec_seq, d), new_k, new_v)
```

```python
import functools
import math

import numpy as np
import jax
import jax.numpy as jnp
from jax import lax
from jax.experimental import pallas as pl
from jax.experimental.pallas import tpu as pltpu
from jax.experimental.pallas import tpu_sc as plsc

F32 = jnp.float32
BF16 = jnp.bfloat16

GRID_W = 64
N_HEADS = 8
HEAD_DIM = 64
N_FGROUPS = 4
FGROUP_DIM = 128
WIN_H_MAX = 8
WIN_W = 16
TOP_K = 4
SWIGLU_LIMIT = 7.0
SWIGLU_ALPHA = 1.702
EPS = 1e-6
LOG2E = math.log2(math.e)
QK_PRESCALE = HEAD_DIM ** -0.5 * LOG2E

LANES = 128
SUBLANES = 8
ADA_COLS = 1024
TOK_TILE = 512
COMBINE_TILE = 1024
ATT_TILE = 256
MOE_TILE = 1024
MOE_SUB = 256
MERGE_PARTS = 2
VMEM_LIMIT = 56 * 1024 * 1024
NEG_BIG = -1e30

SC_CORES = 2
SC_SUBCORES = 16
SC_CHUNK = 64


def _cparams(n_axes, vmem=VMEM_LIMIT):
    return pltpu.CompilerParams(dimension_semantics=("arbitrary",) * n_axes, vmem_limit_bytes=vmem)


def _rms(x):
    return x * lax.rsqrt(jnp.mean(x * x, axis=-1, keepdims=True) + EPS)


def _pack_halves(x):
    c = x.shape[1] // 2
    lo = lax.bitcast_convert_type(x[:, :c].astype(BF16).astype(F32), jnp.uint32)
    hi = lax.bitcast_convert_type(x[:, c:].astype(BF16).astype(F32), jnp.uint32)
    return lax.bitcast_convert_type(hi | (lo >> 16), jnp.int32)


def _unpack_halves(w):
    u = lax.bitcast_convert_type(w, jnp.uint32)
    lo = lax.bitcast_convert_type(u << 16, F32)
    hi = lax.bitcast_convert_type(u & jnp.uint32(0xFFFF0000), F32)
    return lo, hi


def _ada_kernel(c_ref, w_ref, b_ref, o_ref):
    cv = c_ref[...]
    s = cv * jax.nn.sigmoid(cv)
    w = w_ref[...]
    s_hi = s.astype(BF16)
    s_lo = (s - s_hi.astype(F32)).astype(BF16)
    w_hi = w.astype(BF16)
    w_lo = (w - w_hi.astype(F32)).astype(BF16)
    o_ref[...] = (jnp.dot(s_hi, w_hi, preferred_element_type=F32)
                  + jnp.dot(s_hi, w_lo, preferred_element_type=F32)
                  + jnp.dot(s_lo, w_hi, preferred_element_type=F32) + b_ref[...])


def _ada(cvecs, w_ada, b_ada):
    rows, d = cvecs.shape
    n = w_ada.shape[1]
    blk = ADA_COLS
    return pl.pallas_call(
        _ada_kernel,
        out_shape=jax.ShapeDtypeStruct((rows, n), F32),
        grid=(n // blk,),
        in_specs=[pl.BlockSpec((rows, d), lambda j: (0, 0)),
                  pl.BlockSpec((d, blk), lambda j: (0, j)),
                  pl.BlockSpec((1, blk), lambda j: (0, j))],
        out_specs=pl.BlockSpec((rows, blk), lambda j: (0, j)),
        compiler_params=_cparams(1),
        name="ada",
    )(cvecs, w_ada, b_ada.reshape(1, n))


def _proj_kernel(n_ctx_tiles, xp_ref, xs_ref, mod_ref, g1_ref, win_ref, hsum_ref, gq_ref, gk_ref,
                 u_ref, q_ref, k_ref, v_ref, nk_ref, nv_ref):
    t = pl.program_id(0)
    is_ctx = t < n_ctx_tiles
    sh1 = mod_ref[0, 0:1, :]
    sc1 = mod_ref[0, 1:2, :]
    da = q_ref.shape[1]
    df = u_ref.shape[1]
    nb, _, nh, s, dh = nk_ref.shape
    parts = [slice(b * s, (b + 1) * s) for b in range(nb)]
    hs = [(_rms(jnp.where(is_ctx, xp_ref[rs, :], xs_ref[rs, :])) * g1_ref[...] * (1.0 + sc1) + sh1).astype(BF16)
          for rs in parts]
    projs = [jnp.dot(h, win_ref[...], preferred_element_type=F32) for h in hs]
    kns, vs = [], []
    for rs, proj in zip(parts, projs):
        u_ref[rs, :] = proj[:, :df].astype(BF16)
        q = proj[:, df:df + da]
        k = proj[:, df + da:df + 2 * da]
        v = proj[:, df + 2 * da:]
        msq = jnp.dot((q * q).astype(BF16), hsum_ref[...], preferred_element_type=F32)
        msk = jnp.dot((k * k).astype(BF16), hsum_ref[...], preferred_element_type=F32)
        qn = q * lax.rsqrt(msq + EPS) * gq_ref[...]
        kn = k * lax.rsqrt(msk + EPS) * gk_ref[...]
        q_ref[rs, :] = (qn * QK_PRESCALE).astype(BF16)
        k_ref[rs, :] = kn.astype(BF16)
        v_ref[rs, :] = v.astype(BF16)
        kns.append(kn)
        vs.append(v)

    @pl.when(is_ctx)
    def _():
        for b in range(nb):
            for hd in range(nh):
                nk_ref[b, 0, hd] = kns[b][:, hd * dh:(hd + 1) * dh]
                nv_ref[b, 0, hd] = vs[b][:, hd * dh:(hd + 1) * dh]


def _proj(xp, xs, mod, g1, w_in_b, hsum, gq_t, gk_t, batch, seq, tiles_per_lat_batch):
    n_ctx, d = xp.shape
    n_lat = xs.shape[0]
    tm = TOK_TILE
    n_ctx_tiles = n_ctx // tm
    n_tiles = (n_ctx + n_lat) // tm
    n = n_ctx + n_lat
    d_in = w_in_b.shape[1]
    da = N_HEADS * HEAD_DIM
    df = d_in - 3 * da
    bpt = tm // seq
    last = n_ctx_tiles - 1

    def mod_row(t):
        return jnp.where(t < n_ctx_tiles, 0, 1 + (t - n_ctx_tiles) // tiles_per_lat_batch)

    tok = lambda w: pl.BlockSpec((tm, w), lambda t: (t, 0))
    full = lambda a: pl.BlockSpec(a.shape, lambda t: (0,) * a.ndim)
    kv_spec = pl.BlockSpec((bpt, 1, N_HEADS, seq, HEAD_DIM), lambda t: (jnp.minimum(t, last), 0, 0, 0, 0))
    kv_shape = jax.ShapeDtypeStruct((batch, 1, N_HEADS, seq, HEAD_DIM), F32)
    return pl.pallas_call(
        functools.partial(_proj_kernel, n_ctx_tiles),
        out_shape=(jax.ShapeDtypeStruct((n, df), BF16),) + (jax.ShapeDtypeStruct((n, da), BF16),) * 3
        + (kv_shape, kv_shape),
        grid=(n_tiles,),
        in_specs=[pl.BlockSpec((tm, d), lambda t: (jnp.minimum(t, last), 0)),
                  pl.BlockSpec((tm, d), lambda t: (jnp.maximum(t - n_ctx_tiles, 0), 0)),
                  pl.BlockSpec((1,) + mod.shape[1:], lambda t: (mod_row(t), 0, 0)),
                  full(g1), full(w_in_b), full(hsum), full(gq_t), full(gk_t)],
        out_specs=(tok(df), tok(da), tok(da), tok(da), kv_spec, kv_spec),
        compiler_params=_cparams(1),
        name="proj",
    )(xp, xs, mod, g1, w_in_b, hsum, gq_t, gk_t)


def _softmax_pv(q2, key_blocks, val_blocks, bias_blocks):
    lane = lax.broadcasted_iota(jnp.int32, (1, LANES), 1)
    masks = [lane < HEAD_DIM, lane >= HEAD_DIM]

    def head_scores(half):
        qh = jnp.where(masks[half], q2, jnp.zeros_like(q2))
        scores = []
        for kb, bb in zip(key_blocks, bias_blocks):
            s = lax.dot_general(qh, kb, (((1,), (1,)), ((), ())), preferred_element_type=F32)
            if bb is not None:
                s = s + bb[half]
            scores.append(s)
        return scores

    def head_out(scores):
        m = jnp.max(functools.reduce(jnp.maximum, scores), axis=-1, keepdims=True)
        ps = [jnp.exp2(s - m) for s in scores]
        denom = jnp.sum(functools.reduce(lambda a, b: a + b, ps), axis=-1, keepdims=True)
        o = functools.reduce(lambda a, b: a + b,
                             [jnp.dot(p.astype(BF16), vb, preferred_element_type=F32)
                              for p, vb in zip(ps, val_blocks)])
        return o / denom

    if len(key_blocks) > 1:
        outs = [head_out(s) for s in [head_scores(0), head_scores(1)]]
    else:
        outs = [head_out(head_scores(half)) for half in range(2)]
    return jnp.where(masks[0], outs[0], outs[1])


def _attn_ctx_kernel(q_ref, k_ref, v_ref, o_ref):
    for p in range(q_ref.shape[1] // LANES):
        sl = slice(p * LANES, (p + 1) * LANES)
        o = _softmax_pv(q_ref[:, sl], [k_ref[:, sl]], [v_ref[:, sl]], [None])
        o_ref[:, sl] = o.astype(BF16)


def _attn_ctx(q, k, v, batch, seq):
    da = q.shape[1]
    spec = pl.BlockSpec((seq, da), lambda b: (b, 0))
    return pl.pallas_call(
        _attn_ctx_kernel,
        out_shape=jax.ShapeDtypeStruct((batch * seq, da), BF16),
        grid=(batch,),
        in_specs=[spec, spec, spec],
        out_specs=spec,
        compiler_params=_cparams(1),
        name="attn_ctx",
    )(q, k, v)


def _attn_lat_kernel(q_ref, k0_ref, k1_ref, k2_ref, v0_ref, v1_ref, v2_ref, kc_ref, vc_ref, bias_ref, o_ref):
    tk = k0_ref.shape[0]
    for p in range(q_ref.shape[1] // LANES):
        sl = slice(p * LANES, (p + 1) * LANES)
        keys = [k0_ref[:, sl], k1_ref[:, sl], k2_ref[:, sl], kc_ref[0, :, sl]]
        vals = [v0_ref[:, sl], v1_ref[:, sl], v2_ref[:, sl], vc_ref[0, :, sl]]
        biases = [[bias_ref[0, 2 * p + half, :, d * tk:(d + 1) * tk] for half in range(2)] for d in range(3)]
        o = _softmax_pv(q_ref[:, sl], keys, vals, biases + [None])
        o_ref[:, sl] = o.astype(BF16)


def _attn_lat(q, k, v, kc, vc, bias, n_ctx, dec_batch, rows):
    da = q.shape[1]
    tq = ATT_TILE
    rows_per_tile = tq // GRID_W
    tiles = rows // rows_per_tile
    base = n_ctx // tq
    n_lat = dec_batch * rows * GRID_W
    max_start = tiles - 3

    def qmap(j, b):
        return (base + b * tiles + j, 0)

    def kmap(d):
        return lambda j, b: (base + b * tiles + jnp.clip(j - 1, 0, max_start) + d, 0)

    def bmap(j, b):
        return (jnp.where(j == 0, 0, jnp.where(j == tiles - 1, 2, 1)), 0, 0, 0)

    blk = lambda m: pl.BlockSpec((tq, da), m)
    cspec = pl.BlockSpec((1,) + kc.shape[1:], lambda j, b: (b, 0, 0))
    return pl.pallas_call(
        _attn_lat_kernel,
        out_shape=jax.ShapeDtypeStruct((n_lat, da), BF16),
        grid=(tiles, dec_batch),
        in_specs=[blk(qmap)] + [blk(kmap(d)) for d in range(3)] + [blk(kmap(d)) for d in range(3)]
        + [cspec, cspec, pl.BlockSpec((1,) + bias.shape[1:], bmap)],
        out_specs=pl.BlockSpec((tq, da), lambda j, b: (b * tiles + j, 0)),
        compiler_params=_cparams(2),
        name="attn_lat",
    )(q, k, k, k, v, v, v, kc, vc, bias)


def _window_tables(rows):
    rpt = ATT_TILE // GRID_W
    krows = 3 * rpt
    tiles = rows // rpt
    kh = min(WIN_H_MAX, rows)
    rpb_w = 2 * WIN_W - 1
    arow = -np.ones((3, rpt, krows), np.int64)
    for var, j in enumerate((0, 1, tiles - 1)):
        r0 = j * rpt
        k0 = int(np.clip(j - 1, 0, tiles - 3)) * rpt
        for qi in range(rpt):
            qrow = r0 + qi
            rs = int(np.clip(qrow - kh // 2, 0, rows - kh))
            for ki in range(krows):
                krow = k0 + ki
                if rs <= krow < rs + kh:
                    arow[var, qi, ki] = krow - qrow + WIN_H_MAX - 1
    ec = np.zeros((GRID_W, GRID_W, rpb_w), np.float32)
    for qc in range(GRID_W):
        cs = int(np.clip(qc - WIN_W // 2, 0, GRID_W - WIN_W))
        for kc in range(cs, cs + WIN_W):
            ec[qc, kc, int(np.clip(kc - qc, -(WIN_W - 1), WIN_W - 1)) + WIN_W - 1] = 1.0
    return arow.tolist(), ec


def _bias_kernel(arow, u_ref, o_ref):
    var_id = pl.program_id(0)
    for var, table in enumerate(arow):
        @pl.when(var_id == var)
        def _(table=table):
            for qi, row in enumerate(table):
                for ki, a in enumerate(row):
                    blk = u_ref[0, a] if a >= 0 else jnp.full((GRID_W, GRID_W), NEG_BIG, F32)
                    o_ref[0, 0, qi * GRID_W:(qi + 1) * GRID_W, ki * GRID_W:(ki + 1) * GRID_W] = blk


def _window_bias(rpb, rows):
    arow, ec = _window_tables(rows)
    nh = rpb.shape[0]
    col_blocks = jnp.einsum("hab,xyb->haxy", rpb, ec, precision=lax.Precision.HIGHEST)
    col_blocks = jnp.where(ec.sum(-1) > 0.5, col_blocks * LOG2E, NEG_BIG)
    return pl.pallas_call(
        functools.partial(_bias_kernel, arow),
        out_shape=jax.ShapeDtypeStruct((len(arow), nh, ATT_TILE, 3 * ATT_TILE), F32),
        grid=(len(arow), nh),
        in_specs=[pl.BlockSpec((1,) + col_blocks.shape[1:], lambda v, h: (h, 0, 0, 0))],
        out_specs=pl.BlockSpec((1, 1, ATT_TILE, 3 * ATT_TILE), lambda v, h: (v, h, 0, 0)),
        compiler_params=_cparams(2),
        name="window_bias",
    )(col_blocks)


def _fourier_kernel(u_ref, w1_ref, ct_ref, st_ref, o_ref, p_scr, q_scr):
    @pl.when(pl.program_id(1) == 0)
    def _():
        pq = jnp.dot(u_ref[...], w1_ref[...], preferred_element_type=F32)
        df = p_scr.shape[1]
        p_scr[...] = pq[:, :df].astype(BF16)
        q_scr[...] = pq[:, df:].astype(BF16)

    o = (jnp.dot(ct_ref[...], p_scr[...], preferred_element_type=F32)
         - jnp.dot(st_ref[...], q_scr[...], preferred_element_type=F32))
    o_ref[...] = o.astype(BF16)


def _fourier(u, w1, ct, st, batch, t_len, first_block, row_tile):
    df = u.shape[1]
    steps = t_len // row_tile
    return pl.pallas_call(
        _fourier_kernel,
        out_shape=jax.ShapeDtypeStruct((batch * t_len, df), BF16),
        grid=(batch, steps),
        in_specs=[pl.BlockSpec((t_len, df), lambda b, i: (first_block + b, 0)),
                  pl.BlockSpec(w1.shape, lambda b, i: (0, 0)),
                  pl.BlockSpec((row_tile, t_len), lambda b, i: (i, 0)),
                  pl.BlockSpec((row_tile, t_len), lambda b, i: (i, 0))],
        out_specs=pl.BlockSpec((row_tile, df), lambda b, i: (b * steps + i, 0)),
        scratch_shapes=[pltpu.VMEM((t_len, df), BF16), pltpu.VMEM((t_len, df), BF16)],
        compiler_params=_cparams(2),
        name=f"fourier_{t_len}",
    )(u, w1, ct, st)


def _dft_tables(t_len):
    c = FGROUP_DIM
    jk = np.outer(np.arange(c), np.arange(c)) % c
    ang = 2.0 * np.pi * jk / c
    eye = np.eye(N_FGROUPS)
    w1 = np.concatenate([np.kron(eye, np.cos(ang)), np.kron(eye, np.sin(ang))], axis=1) / np.sqrt(c)
    tt = np.outer(np.arange(t_len), np.arange(t_len)) % t_len
    angt = 2.0 * np.pi * tt / t_len
    return (w1.astype(np.float32), (np.cos(angt) / np.sqrt(t_len)).astype(np.float32),
            (np.sin(angt) / np.sqrt(t_len)).astype(np.float32))


def _merge_kernel(n_ctx_tiles, xp_ref, xs_ref, mod_ref, g1_ref, g2_ref, afc_ref, afl_ref, aac_ref, aal_ref,
                  wf_ref, wa_ref, wg_ref, bg_ref, wo_ref, wrt_ref, br_ref,
                  x1_ref, h2_ref, idx_ref, wts_ref, rank_ref, cnt_ref, carry):
    t = pl.program_id(0)
    is_ctx = t < n_ctx_tiles

    @pl.when(t == 0)
    def _():
        carry[...] = jnp.zeros_like(carry)

    sh1 = mod_ref[0, 0:1, :]
    sc1 = mod_ref[0, 1:2, :]
    gt1 = mod_ref[0, 2:3, :]
    sh2 = mod_ref[0, 3:4, :]
    sc2 = mod_ref[0, 4:5, :]
    d = xp_ref.shape[1]
    part_rows = xp_ref.shape[0] // MERGE_PARTS
    parts = [slice(p * part_rows, (p + 1) * part_rows) for p in range(MERGE_PARTS)]
    xs_ = [jnp.where(is_ctx, xp_ref[rs, :], xs_ref[rs, :]) for rs in parts]
    hbs = [(_rms(x) * g1_ref[...] * (1.0 + sc1) + sh1).astype(BF16) for x in xs_]
    gates = [jax.nn.sigmoid(jnp.dot(hb, wg_ref[...], preferred_element_type=F32) + bg_ref[...]) for hb in hbs]
    fas = [jnp.dot(jnp.where(is_ctx, afc_ref[rs, :], afl_ref[rs, :]), wf_ref[...], preferred_element_type=F32)
           for rs in parts]
    fbs = [jnp.dot(jnp.where(is_ctx, aac_ref[rs, :], aal_ref[rs, :]), wa_ref[...], preferred_element_type=F32)
           for rs in parts]
    mixes = [(g[:, :d] * fa + g[:, d:] * fb).astype(BF16) for g, fa, fb in zip(gates, fas, fbs)]
    x1s = [x + gt1 * jnp.dot(mix, wo_ref[...], preferred_element_type=F32) for x, mix in zip(xs_, mixes)]
    h2s = []
    for rs, x1 in zip(parts, x1s):
        x1_ref[rs, :] = x1
        h2 = _rms(x1) * g2_ref[...] * (1.0 + sc2) + sh2
        h2_ref[rs, :] = _pack_halves(h2)
        h2s.append(h2)
    h2 = jnp.concatenate(h2s, axis=0)
    h2_hi = h2.astype(BF16)
    h2_lo = (h2 - h2_hi.astype(F32)).astype(BF16)
    ne = br_ref.shape[0]
    nt_dims = (((1,), (1,)), ((), ()))
    both = lax.dot_general(wrt_ref[...], h2_hi, nt_dims, preferred_element_type=F32)
    cross = lax.dot_general(wrt_ref[:ne, :], h2_lo, nt_dims, preferred_element_type=F32)
    logits_t = both[:ne, :] + both[ne:, :] + cross + br_ref[...]
    idx_o, wts_o, rank_o, counts = _route(logits_t, carry[...])
    idx_ref[...] = idx_o
    wts_ref[...] = wts_o
    rank_ref[...] = rank_o
    carry[...] = counts
    cnt_ref[...] = counts


def _route(logits_t, counts):
    ne, tm = logits_t.shape
    row = lax.broadcasted_iota(jnp.int32, (ne, tm), 0)
    lg = logits_t
    vals, idxs = [], []
    for _ in range(TOP_K):
        m = jnp.max(lg, axis=0, keepdims=True)
        am = jnp.min(jnp.where(lg == m, row, ne), axis=0, keepdims=True)
        vals.append(m)
        idxs.append(am)
        lg = jnp.where(row == am, -jnp.inf, lg)
    es = [jnp.exp(v - vals[0]) for v in vals]
    den = functools.reduce(lambda a, b: a + b, es)
    onehot = functools.reduce(lambda a, b: a + b, [(row == am).astype(F32) for am in idxs])
    s_i = lax.broadcasted_iota(jnp.int32, (tm, tm), 0)
    t_i = lax.broadcasted_iota(jnp.int32, (tm, tm), 1)
    earlier = (s_i < t_i).astype(BF16)
    before = jnp.dot(onehot.astype(BF16), earlier, preferred_element_type=F32) + counts
    out_row = lax.broadcasted_iota(jnp.int32, (SUBLANES, tm), 0)
    idx_o = jnp.zeros((SUBLANES, tm), jnp.int32)
    wts_o = jnp.zeros((SUBLANES, tm), F32)
    rank_o = jnp.zeros((SUBLANES, tm), jnp.int32)
    for kk in range(TOP_K):
        rk = jnp.sum(jnp.where(row == idxs[kk], before, 0.0), axis=0, keepdims=True).astype(jnp.int32)
        idx_o = jnp.where(out_row == kk, idxs[kk], idx_o)
        wts_o = jnp.where(out_row == kk, es[kk] / den, wts_o)
        rank_o = jnp.where(out_row == kk, rk, rank_o)
    return idx_o, wts_o, rank_o, counts + jnp.sum(onehot, axis=1, keepdims=True)


def _merge(xp, xs, mod, g1, g2, af_c, af_l, aa_c, aa_l, wf, wa, wg, bg, wo, wr, br, tiles_per_lat_batch):
    n_ctx, d = xp.shape
    n = n_ctx + xs.shape[0]
    tm = TOK_TILE
    n_ctx_tiles = n_ctx // tm
    last = n_ctx_tiles - 1
    ne = wr.shape[1]
    da = aa_c.shape[1]
    df = af_c.shape[1]

    def mod_row(t):
        return jnp.where(t < n_ctx_tiles, 0, 1 + (t - n_ctx_tiles) // tiles_per_lat_batch)

    cmap = lambda t: (jnp.minimum(t, last), 0)
    lmap = lambda t: (jnp.maximum(t - n_ctx_tiles, 0), 0)
    full = lambda a: pl.BlockSpec(a.shape, lambda t: (0,) * a.ndim)
    tok = lambda w: pl.BlockSpec((tm, w), lambda t: (t, 0))
    per_choice = lambda dt: jax.ShapeDtypeStruct((SUBLANES, n), dt)
    choice_spec = pl.BlockSpec((SUBLANES, tm), lambda t: (0, t))
    wr_hi = wr.astype(BF16)
    wr_lo = (wr - wr_hi.astype(F32)).astype(BF16)
    wr_t = jnp.concatenate([wr_hi.T, wr_lo.T], axis=0)
    return pl.pallas_call(
        functools.partial(_merge_kernel, n_ctx_tiles),
        out_shape=(jax.ShapeDtypeStruct((n, d), F32), jax.ShapeDtypeStruct((n, d // 2), jnp.int32),
                   per_choice(jnp.int32), per_choice(F32), per_choice(jnp.int32),
                   jax.ShapeDtypeStruct((ne, 1), F32)),
        grid=(n // tm,),
        in_specs=[pl.BlockSpec((tm, d), cmap), pl.BlockSpec((tm, d), lmap),
                  pl.BlockSpec((1,) + mod.shape[1:], lambda t: (mod_row(t), 0, 0)),
                  full(g1), full(g2),
                  pl.BlockSpec((tm, df), cmap), pl.BlockSpec((tm, df), lmap),
                  pl.BlockSpec((tm, da), cmap), pl.BlockSpec((tm, da), lmap),
                  full(wf), full(wa), full(wg), full(bg), full(wo), full(wr_t), full(br)],
        out_specs=(tok(d), tok(d // 2), choice_spec, choice_spec, choice_spec,
                   pl.BlockSpec((ne, 1), lambda t: (0, 0))),
        scratch_shapes=[pltpu.VMEM((ne, 1), F32)],
        compiler_params=_cparams(1),
        name="merge",
    )(xp, xs, mod, g1, g2, af_c, af_l, aa_c, aa_l, wf, wa, wg, bg, wo, wr_t, br)


def _moe_kernel(te_ref, nt_ref, nx_ref, rv_ref, sl_ref, x_ref, w1_hbm, b1_ref, w2_hbm, b2_ref, y_ref,
                w1s, w2s, sem):
    i = pl.program_id(0)
    e = te_ref[i]
    prev = te_ref[jnp.maximum(i - 1, 0)]
    first_of_run = (i == 0) | (e != prev)
    slot = sl_ref[i]

    def stage(expert, s):
        return (pltpu.make_async_copy(w1_hbm.at[expert], w1s.at[s], sem.at[0, s]),
                pltpu.make_async_copy(w2_hbm.at[expert], w2s.at[s], sem.at[1, s]))

    @pl.when(i == 0)
    def _():
        for cp in stage(e, slot):
            cp.start()

    @pl.when(first_of_run)
    def _():
        for cp in stage(e, slot):
            cp.wait()

        @pl.when(nx_ref[i] >= 0)
        def _():
            for cp in stage(nx_ref[i], 1 - slot):
                cp.start()

    def expert_rows(rs):
        dff = w2s.shape[1]
        x_lo, x_hi = _unpack_halves(x_ref[rs, :])
        x = jnp.concatenate([x_lo.astype(BF16), x_hi.astype(BF16)], axis=1)
        gu = jnp.dot(x, w1s[slot].astype(BF16), preferred_element_type=F32) + b1_ref[0]
        gate = jnp.minimum(gu[:, :dff], SWIGLU_LIMIT)
        up = jnp.clip(gu[:, dff:], -SWIGLU_LIMIT, SWIGLU_LIMIT)
        glu = gate * jax.nn.sigmoid(SWIGLU_ALPHA * gate)
        act = ((up + 1.0) * glu).astype(BF16)
        y = jnp.dot(act, w2s[slot].astype(BF16), preferred_element_type=F32) + b2_ref[0]
        y_ref[rs, :] = _pack_halves(y)

    tm = x_ref.shape[0]
    rv = rv_ref[i]

    @pl.when(rv == tm)
    def _():
        expert_rows(slice(0, tm))

    for sub in range(tm // MOE_SUB):
        @pl.when((rv < tm) & (rv > sub * MOE_SUB))
        def _(sub=sub):
            expert_rows(slice(sub * MOE_SUB, (sub + 1) * MOE_SUB))


def _moe(tile_expert, n_tiles, next_expert, rows_valid, slots, xs_sorted, w1, b1, w2, b2):
    p, dh = xs_sorted.shape
    ne, d, dff2 = w1.shape
    dff = w2.shape[1]
    tm = MOE_TILE
    max_tiles = p // tm

    def row_map(i, te, nt, nx, rv, sl):
        return (jnp.minimum(i, nt[0] - 1), 0)

    def b_map(i, te, nt, nx, rv, sl):
        return (te[i], 0, 0)

    grid_spec = pltpu.PrefetchScalarGridSpec(
        num_scalar_prefetch=5,
        grid=(max_tiles,),
        in_specs=[pl.BlockSpec((tm, dh), row_map),
                  pl.BlockSpec(memory_space=pl.ANY),
                  pl.BlockSpec((1, 1, dff2), b_map),
                  pl.BlockSpec(memory_space=pl.ANY),
                  pl.BlockSpec((1, 1, d), b_map)],
        out_specs=pl.BlockSpec((tm, dh), row_map),
        scratch_shapes=[pltpu.VMEM((2, d, dff2), F32), pltpu.VMEM((2, dff, d), F32),
                        pltpu.SemaphoreType.DMA((2, 2))],
    )
    return pl.pallas_call(
        _moe_kernel,
        out_shape=jax.ShapeDtypeStruct((p, dh), jnp.int32),
        grid_spec=grid_spec,
        compiler_params=_cparams(1),
        name="moe",
    )(tile_expert, n_tiles, next_expert, rows_valid, slots, xs_sorted,
      w1, b1.reshape(ne, 1, dff2), w2, b2.reshape(ne, 1, d))


def _sc_mesh():
    return plsc.VectorSubcoreMesh(core_axis_name="c", subcore_axis_name="s",
                                  num_cores=SC_CORES, num_subcores=SC_SUBCORES)


def _sc_dispatch(h, idx, p_rows):
    n, d = h.shape
    nw, items, chunk = idx.shape
    tok_w = n // nw
    n_chunks = items // TOP_K

    @functools.partial(
        pl.kernel, mesh=_sc_mesh(),
        out_type=jax.ShapeDtypeStruct((p_rows, d), h.dtype),
        scratch_types=[pltpu.VMEM((items, chunk), jnp.int32),
                       pltpu.VMEM((2, chunk, d), h.dtype),
                       pltpu.SemaphoreType.DMA((2,)), pltpu.SemaphoreType.DMA((2,))],
        name="sc_dispatch",
    )
    def run(h_hbm, idx_hbm, out_hbm, idx_v, rows_v, lsem, ssem):
        wid = lax.axis_index("s") * SC_CORES + lax.axis_index("c")
        base = wid * tok_w
        pltpu.sync_copy(idx_hbm.at[wid], idx_v)

        def load(j, slot):
            return pltpu.make_async_copy(h_hbm.at[pl.ds(base + j * chunk, chunk)], rows_v.at[slot], lsem.at[slot])

        def scat(j, kk, slot):
            return pltpu.make_async_copy(rows_v.at[slot], out_hbm.at[idx_v.at[j * TOP_K + kk]], ssem.at[slot])

        load(0, 0).start()

        @pl.loop(0, n_chunks, step=2)
        def _(j0):
            for b in range(2):
                j = j0 + b

                @pl.when(j >= 1)
                def _():
                    for kk in range(TOP_K):
                        scat(j - 1, kk, 1 - b).wait()

                @pl.when(j + 1 < n_chunks)
                def _():
                    load(j + 1, 1 - b).start()

                load(j, b).wait()
                for kk in range(TOP_K):
                    scat(j, kk, b).start()

        for kk in range(TOP_K):
            scat(n_chunks - 1, kk, (n_chunks - 1) % 2).wait()

    return run(h, idx)


def _sc_gather(y, idx, n):
    d = y.shape[1]
    nw, items, chunk = idx.shape
    tok_w = n // nw

    @functools.partial(
        pl.kernel, mesh=_sc_mesh(),
        out_type=jax.ShapeDtypeStruct((TOP_K, n, d), y.dtype),
        scratch_types=[pltpu.VMEM((items, chunk), jnp.int32),
                       pltpu.VMEM((2, chunk, d), y.dtype),
                       pltpu.SemaphoreType.DMA((2,)), pltpu.SemaphoreType.DMA((2,))],
        name="sc_gather",
    )
    def run(y_hbm, idx_hbm, out_hbm, idx_v, rows_v, gsem, wsem):
        wid = lax.axis_index("s") * SC_CORES + lax.axis_index("c")
        base = wid * tok_w
        pltpu.sync_copy(idx_hbm.at[wid], idx_v)

        def gather(i, slot):
            return pltpu.make_async_copy(y_hbm.at[idx_v.at[i]], rows_v.at[slot], gsem.at[slot])

        def write(i, slot):
            dst = out_hbm.at[i % TOP_K, pl.ds(base + (i // TOP_K) * chunk, chunk)]
            return pltpu.make_async_copy(rows_v.at[slot], dst, wsem.at[slot])

        gather(0, 0).start()

        @pl.loop(0, items, step=2)
        def _(i0):
            for b in range(2):
                i = i0 + b

                @pl.when(i >= 1)
                def _():
                    write(i - 1, 1 - b).wait()

                @pl.when(i + 1 < items)
                def _():
                    gather(i + 1, 1 - b).start()

                gather(i, b).wait()
                write(i, b).start()

        write(items - 1, (items - 1) % 2).wait()

    return run(y, idx)


def _combine_kernel(n_ctx_tiles, x1_ref, yg_ref, wts_ref, mod_ref, op_ref, os_ref):
    t = pl.program_id(0)
    gt2 = mod_ref[0, 5:6, :]
    w = wts_ref[...]
    acc_lo, acc_hi = None, None
    for kk in range(TOP_K):
        y_lo, y_hi = _unpack_halves(yg_ref[kk])
        wk = w[:, kk:kk + 1]
        acc_lo = wk * y_lo if acc_lo is None else acc_lo + wk * y_lo
        acc_hi = wk * y_hi if acc_hi is None else acc_hi + wk * y_hi
    out = x1_ref[...] + gt2 * jnp.concatenate([acc_lo, acc_hi], axis=1)

    @pl.when(t < n_ctx_tiles)
    def _():
        op_ref[...] = out

    @pl.when(t >= n_ctx_tiles)
    def _():
        os_ref[...] = out


def _combine(x1, yg, wts, mod, n_ctx, lat_seq):
    n, d = x1.shape
    tm = COMBINE_TILE
    assert n_ctx % tm == 0 and lat_seq % tm == 0
    tiles_per_lat_batch = lat_seq // tm
    n_ctx_tiles = n_ctx // tm
    last = n_ctx_tiles - 1

    def mod_row(t):
        return jnp.where(t < n_ctx_tiles, 0, 1 + (t - n_ctx_tiles) // tiles_per_lat_batch)

    return pl.pallas_call(
        functools.partial(_combine_kernel, n_ctx_tiles),
        out_shape=(jax.ShapeDtypeStruct((n_ctx, d), F32), jax.ShapeDtypeStruct((n - n_ctx, d), F32)),
        grid=(n // tm,),
        in_specs=[pl.BlockSpec((tm, d), lambda t: (t, 0)),
                  pl.BlockSpec((TOP_K, tm, d // 2), lambda t: (0, t, 0)),
                  pl.BlockSpec((tm, TOP_K), lambda t: (t, 0)),
                  pl.BlockSpec((1,) + mod.shape[1:], lambda t: (mod_row(t), 0, 0))],
        out_specs=(pl.BlockSpec((tm, d), lambda t: (jnp.minimum(t, last), 0)),
                   pl.BlockSpec((tm, d), lambda t: (jnp.maximum(t - n_ctx_tiles, 0), 0))),
        compiler_params=_cparams(1),
        name="combine",
    )(x1, yg, wts, mod)


def kernel(x_prompt, x_sample, cache_k, cache_v, c, c_ctx, g_norm1, w_ada, b_ada, w_in, g_q, g_k, rpb,
           w_fmap, w_amap, w_gate, b_gate, w_out, g_norm2, w_router, b_router, w1, b1, w2, b2):
    batch, seq, d = x_prompt.shape
    dec_batch, dec_seq, _ = x_sample.shape
    assert w_ada.shape[0] == 1, "single-layer trunk"
    rows = dec_seq // GRID_W
    da = N_HEADS * HEAD_DIM
    n_ctx = batch * seq
    n_lat = dec_batch * dec_seq
    n = n_ctx + n_lat
    ne = w_router.shape[2]
    assert TOK_TILE % seq == 0 and n_ctx % TOK_TILE == 0 and dec_seq % TOK_TILE == 0
    assert rows % (ATT_TILE // GRID_W) == 0 and rows >= 3 * (ATT_TILE // GRID_W) and rows >= WIN_H_MAX
    assert seq == ATT_TILE and cache_k.shape[3] == ATT_TILE
    tiles_per_lat_batch = dec_seq // TOK_TILE

    n_mod_rows = -(-(1 + dec_batch) // SUBLANES) * SUBLANES
    cvecs = jnp.zeros((n_mod_rows, d), F32).at[0].set(c_ctx).at[1:1 + dec_batch].set(c)
    mod = _ada(cvecs, w_ada[0], b_ada[0]).reshape(n_mod_rows, 6, d)

    xp = x_prompt.reshape(n_ctx, d)
    xs = x_sample.reshape(n_lat, d)
    g1 = g_norm1[0].reshape(1, d)
    g2 = g_norm2[0].reshape(1, d)
    hsum = jnp.asarray(np.kron(np.eye(N_HEADS), np.full((HEAD_DIM, HEAD_DIM), 1.0 / HEAD_DIM)), BF16)
    gq_t = jnp.tile(g_q[0], N_HEADS).reshape(1, da)
    gk_t = jnp.tile(g_k[0], N_HEADS).reshape(1, da)

    u, q, k, v, new_k, new_v = _proj(xp, xs, mod, g1, w_in[0].astype(BF16), hsum, gq_t, gk_t,
                                      batch, seq, tiles_per_lat_batch)

    aa_c = _attn_ctx(q, k, v, batch, seq)
    bias = _window_bias(rpb[0], rows)
    past = cache_k.shape[3]
    kc = cache_k[:, 0].transpose(0, 2, 1, 3).reshape(dec_batch, past, da).astype(BF16)
    vc = cache_v[:, 0].transpose(0, 2, 1, 3).reshape(dec_batch, past, da).astype(BF16)
    aa_l = _attn_lat(q, k, v, kc, vc, bias, n_ctx, dec_batch, rows)

    w1c, ct_c, st_c = _dft_tables(seq)
    _, ct_l, st_l = _dft_tables(dec_seq)
    w1c = jnp.asarray(w1c).astype(BF16)
    af_c = _fourier(u, w1c, jnp.asarray(ct_c).astype(BF16), jnp.asarray(st_c).astype(BF16),
                    batch, seq, 0, seq)
    af_l = _fourier(u, w1c, jnp.asarray(ct_l).astype(BF16), jnp.asarray(st_l).astype(BF16),
                    dec_batch, dec_seq, n_ctx // dec_seq, TOK_TILE)

    x1, h2, idx, wts, rank, counts = _merge(
        xp, xs, mod, g1, g2, af_c, af_l, aa_c, aa_l,
        w_fmap[0].astype(BF16), w_amap[0].astype(BF16), w_gate[0].astype(BF16), b_gate[0].reshape(1, -1),
        w_out[0].astype(BF16), w_router[0], b_router[0].reshape(ne, 1), tiles_per_lat_batch)
    idx, wts, rank = idx[:TOP_K], wts[:TOP_K], rank[:TOP_K]

    tm = MOE_TILE
    max_tiles = (n * TOP_K) // tm + ne
    cnt = counts[:, 0].astype(jnp.int32)
    tiles_e = (cnt + tm - 1) // tm
    tile_end = jnp.cumsum(tiles_e)
    pad_off = (tile_end - tiles_e) * tm
    n_tiles = tile_end[-1:]
    tile_ids = jnp.arange(max_tiles, dtype=jnp.int32)
    tile_expert = jnp.sum((tile_ids[:, None] >= tile_end[None, :]).astype(jnp.int32), axis=1)
    last_e = jnp.sum((n_tiles - 1 >= tile_end).astype(jnp.int32))
    tile_expert = jnp.minimum(tile_expert, last_e).astype(jnp.int32)
    pos = rank
    for e in range(ne):
        pos = pos + jnp.where(idx == e, pad_off[e], 0)

    p_rows = max_tiles * tm
    n_workers = SC_CORES * SC_SUBCORES
    assert n % (n_workers * SC_CHUNK * 2) == 0
    chunks = n // (n_workers * SC_CHUNK)
    pos_lists = pos.astype(jnp.int32).reshape(TOP_K, n_workers, chunks, SC_CHUNK).transpose(1, 2, 0, 3)
    pos_lists = pos_lists.reshape(n_workers, chunks * TOP_K, SC_CHUNK)
    xs_sorted = _sc_dispatch(h2, pos_lists, p_rows)
    e_ids = jnp.arange(ne, dtype=jnp.int32)
    later_used = (e_ids[None, :] > e_ids[:, None]) & (tiles_e[None, :] > 0)
    next_of_e = jnp.min(jnp.where(later_used, e_ids[None, :], ne), axis=1)
    next_of_e = jnp.where(next_of_e == ne, -1, next_of_e)
    is_e = tile_expert[:, None] == e_ids[None, :]
    next_expert = jnp.sum(jnp.where(is_e, next_of_e[None, :], 0), axis=1)
    rows_left = jnp.sum(jnp.where(is_e, (cnt + pad_off)[None, :], 0), axis=1) - tile_ids * tm
    rows_valid = jnp.where(tile_ids < n_tiles[0], jnp.clip(rows_left, 0, tm), 0)
    run_of_e = jnp.cumsum((tiles_e > 0).astype(jnp.int32)) - 1
    slots = jnp.sum(jnp.where(is_e, run_of_e[None, :], 0), axis=1) % 2
    y_sorted = _moe(tile_expert, n_tiles.astype(jnp.int32), next_expert.astype(jnp.int32),
                    rows_valid.astype(jnp.int32), slots.astype(jnp.int32), xs_sorted,
                    w1[0], b1[0], w2[0], b2[0])
    yg = _sc_gather(y_sorted, pos_lists, n)
    y_p, y_s = _combine(x1, yg, wts.T, mod, n_ctx, dec_seq)
    return (y_p.reshape(batch, seq, d), y_s.reshape(dec_batch, dec_seq, d), new_k, new_v)
```

```python
import functools
import math

import numpy as np
import jax
import jax.numpy as jnp
from jax import lax
from jax.experimental import pallas as pl
from jax.experimental.pallas import tpu as pltpu
from jax.experimental.pallas import tpu_sc as plsc

F32 = jnp.float32
BF16 = jnp.bfloat16

GRID_W = 64
N_HEADS = 8
HEAD_DIM = 64
N_FGROUPS = 4
FGROUP_DIM = 128
WIN_H_MAX = 8
WIN_W = 16
TOP_K = 4
SWIGLU_LIMIT = 7.0
SWIGLU_ALPHA = 1.702
EPS = 1e-6
LOG2E = math.log2(math.e)
QK_PRESCALE = HEAD_DIM ** -0.5 * LOG2E

LANES = 128
SUBLANES = 8
ADA_COLS = 1024
POS_COLS = 2048
TOK_TILE = 512
COMBINE_TILE = 1024
ATT_TILE = 256
MOE_TILE = 1024
MOE_SUB = 256
MERGE_PARTS = 2
VMEM_LIMIT = 56 * 1024 * 1024
NEG_BIG = -1e30

SC_CORES = 2
SC_SUBCORES = 16
SC_CHUNK = 64


def _cparams(n_axes, vmem=VMEM_LIMIT):
    return pltpu.CompilerParams(dimension_semantics=("arbitrary",) * n_axes, vmem_limit_bytes=vmem)


def _rms(x):
    return x * lax.rsqrt(jnp.mean(x * x, axis=-1, keepdims=True) + EPS)


def _pack_halves(x):
    c = x.shape[1] // 2
    lo = lax.bitcast_convert_type(x[:, :c].astype(BF16).astype(F32), jnp.uint32)
    hi = lax.bitcast_convert_type(x[:, c:].astype(BF16).astype(F32), jnp.uint32)
    return lax.bitcast_convert_type(hi | (lo >> 16), jnp.int32)


def _unpack_halves(w):
    u = lax.bitcast_convert_type(w, jnp.uint32)
    lo = lax.bitcast_convert_type(u << 16, F32)
    hi = lax.bitcast_convert_type(u & jnp.uint32(0xFFFF0000), F32)
    return lo, hi


def _ada_kernel(c_ref, w_ref, b_ref, o_ref):
    cv = c_ref[...]
    s = cv * jax.nn.sigmoid(cv)
    w = w_ref[...]
    s_hi = s.astype(BF16)
    s_lo = (s - s_hi.astype(F32)).astype(BF16)
    w_hi = w.astype(BF16)
    w_lo = (w - w_hi.astype(F32)).astype(BF16)
    o_ref[...] = (jnp.dot(s_hi, w_hi, preferred_element_type=F32)
                  + jnp.dot(s_hi, w_lo, preferred_element_type=F32)
                  + jnp.dot(s_lo, w_hi, preferred_element_type=F32) + b_ref[...])


def _ada(cvecs, w_ada, b_ada):
    rows, d = cvecs.shape
    n = w_ada.shape[1]
    blk = ADA_COLS
    return pl.pallas_call(
        _ada_kernel,
        out_shape=jax.ShapeDtypeStruct((rows, n), F32),
        grid=(n // blk,),
        in_specs=[pl.BlockSpec((rows, d), lambda j: (0, 0)),
                  pl.BlockSpec((d, blk), lambda j: (0, j)),
                  pl.BlockSpec((1, blk), lambda j: (0, j))],
        out_specs=pl.BlockSpec((rows, blk), lambda j: (0, j)),
        compiler_params=_cparams(1),
        name="ada",
    )(cvecs, w_ada, b_ada.reshape(1, n))


def _proj_kernel(n_ctx_tiles, xp_ref, xs_ref, mod_ref, g1_ref, win_ref, hsum_ref, gq_ref, gk_ref,
                 u_ref, q_ref, k_ref, v_ref, nk_ref, nv_ref):
    t = pl.program_id(0)
    is_ctx = t < n_ctx_tiles
    sh1 = mod_ref[0, 0:1, :]
    sc1 = mod_ref[0, 1:2, :]
    da = q_ref.shape[1]
    df = u_ref.shape[1]
    nb, _, nh, s, dh = nk_ref.shape
    parts = [slice(b * s, (b + 1) * s) for b in range(nb)]
    hs = [(_rms(jnp.where(is_ctx, xp_ref[rs, :], xs_ref[rs, :])) * g1_ref[...] * (1.0 + sc1) + sh1).astype(BF16)
          for rs in parts]
    projs = [jnp.dot(h, win_ref[...], preferred_element_type=F32) for h in hs]
    kns, vs = [], []
    for rs, proj in zip(parts, projs):
        u_ref[rs, :] = proj[:, :df].astype(BF16)
        q = proj[:, df:df + da]
        k = proj[:, df + da:df + 2 * da]
        v = proj[:, df + 2 * da:]
        msq = jnp.dot((q * q).astype(BF16), hsum_ref[...], preferred_element_type=F32)
        msk = jnp.dot((k * k).astype(BF16), hsum_ref[...], preferred_element_type=F32)
        qn = q * lax.rsqrt(msq + EPS) * gq_ref[...]
        kn = k * lax.rsqrt(msk + EPS) * gk_ref[...]
        q_ref[rs, :] = (qn * QK_PRESCALE).astype(BF16)
        k_ref[rs, :] = kn.astype(BF16)
        v_ref[rs, :] = v.astype(BF16)
        kns.append(kn)
        vs.append(v)

    @pl.when(is_ctx)
    def _():
        for b in range(nb):
            for hd in range(nh):
                nk_ref[b, 0, hd] = kns[b][:, hd * dh:(hd + 1) * dh]
                nv_ref[b, 0, hd] = vs[b][:, hd * dh:(hd + 1) * dh]


def _proj(xp, xs, mod, g1, w_in_b, hsum, gq_t, gk_t, batch, seq, tiles_per_lat_batch):
    n_ctx, d = xp.shape
    n_lat = xs.shape[0]
    tm = TOK_TILE
    n_ctx_tiles = n_ctx // tm
    n_tiles = (n_ctx + n_lat) // tm
    n = n_ctx + n_lat
    d_in = w_in_b.shape[1]
    da = N_HEADS * HEAD_DIM
    df = d_in - 3 * da
    bpt = tm // seq
    last = n_ctx_tiles - 1

    def mod_row(t):
        return jnp.where(t < n_ctx_tiles, 0, 1 + (t - n_ctx_tiles) // tiles_per_lat_batch)

    tok = lambda w: pl.BlockSpec((tm, w), lambda t: (t, 0))
    full = lambda a: pl.BlockSpec(a.shape, lambda t: (0,) * a.ndim)
    kv_spec = pl.BlockSpec((bpt, 1, N_HEADS, seq, HEAD_DIM), lambda t: (jnp.minimum(t, last), 0, 0, 0, 0))
    kv_shape = jax.ShapeDtypeStruct((batch, 1, N_HEADS, seq, HEAD_DIM), F32)
    return pl.pallas_call(
        functools.partial(_proj_kernel, n_ctx_tiles),
        out_shape=(jax.ShapeDtypeStruct((n, df), BF16),) + (jax.ShapeDtypeStruct((n, da), BF16),) * 3
        + (kv_shape, kv_shape),
        grid=(n_tiles,),
        in_specs=[pl.BlockSpec((tm, d), lambda t: (jnp.minimum(t, last), 0)),
                  pl.BlockSpec((tm, d), lambda t: (jnp.maximum(t - n_ctx_tiles, 0), 0)),
                  pl.BlockSpec((1,) + mod.shape[1:], lambda t: (mod_row(t), 0, 0)),
                  full(g1), full(w_in_b), full(hsum), full(gq_t), full(gk_t)],
        out_specs=(tok(df), tok(da), tok(da), tok(da), kv_spec, kv_spec),
        compiler_params=_cparams(1),
        name="proj",
    )(xp, xs, mod, g1, w_in_b, hsum, gq_t, gk_t)


def _softmax_pv(q2, key_blocks, val_blocks, bias_blocks):
    lane = lax.broadcasted_iota(jnp.int32, (1, LANES), 1)
    masks = [lane < HEAD_DIM, lane >= HEAD_DIM]

    def head_scores(half):
        qh = jnp.where(masks[half], q2, jnp.zeros_like(q2))
        scores = []
        for kb, bb in zip(key_blocks, bias_blocks):
            s = lax.dot_general(qh, kb, (((1,), (1,)), ((), ())), preferred_element_type=F32)
            if bb is not None:
                s = s + bb[half]
            scores.append(s)
        return scores

    def head_out(scores):
        m = jnp.max(functools.reduce(jnp.maximum, scores), axis=-1, keepdims=True)
        ps = [jnp.exp2(s - m) for s in scores]
        denom = jnp.sum(functools.reduce(lambda a, b: a + b, ps), axis=-1, keepdims=True)
        o = functools.reduce(lambda a, b: a + b,
                             [jnp.dot(p.astype(BF16), vb, preferred_element_type=F32)
                              for p, vb in zip(ps, val_blocks)])
        return o / denom

    if len(key_blocks) > 1:
        outs = [head_out(s) for s in [head_scores(0), head_scores(1)]]
    else:
        outs = [head_out(head_scores(half)) for half in range(2)]
    return jnp.where(masks[0], outs[0], outs[1])


def _attn_ctx_kernel(q_ref, k_ref, v_ref, o_ref):
    for p in range(q_ref.shape[1] // LANES):
        sl = slice(p * LANES, (p + 1) * LANES)
        o = _softmax_pv(q_ref[:, sl], [k_ref[:, sl]], [v_ref[:, sl]], [None])
        o_ref[:, sl] = o.astype(BF16)


def _attn_ctx(q, k, v, batch, seq):
    da = q.shape[1]
    spec = pl.BlockSpec((seq, da), lambda b: (b, 0))
    return pl.pallas_call(
        _attn_ctx_kernel,
        out_shape=jax.ShapeDtypeStruct((batch * seq, da), BF16),
        grid=(batch,),
        in_specs=[spec, spec, spec],
        out_specs=spec,
        compiler_params=_cparams(1),
        name="attn_ctx",
    )(q, k, v)


def _attn_lat_kernel(q_ref, k0_ref, k1_ref, k2_ref, v0_ref, v1_ref, v2_ref, kc_ref, vc_ref, bias_ref, o_ref):
    tk = k0_ref.shape[0]
    for p in range(q_ref.shape[1] // LANES):
        sl = slice(p * LANES, (p + 1) * LANES)
        keys = [k0_ref[:, sl], k1_ref[:, sl], k2_ref[:, sl], kc_ref[0, :, sl]]
        vals = [v0_ref[:, sl], v1_ref[:, sl], v2_ref[:, sl], vc_ref[0, :, sl]]
        biases = [[bias_ref[0, 2 * p + half, :, d * tk:(d + 1) * tk] for half in range(2)] for d in range(3)]
        o = _softmax_pv(q_ref[:, sl], keys, vals, biases + [None])
        o_ref[:, sl] = o.astype(BF16)


def _attn_lat(q, k, v, kc, vc, bias, n_ctx, dec_batch, rows):
    da = q.shape[1]
    tq = ATT_TILE
    rows_per_tile = tq // GRID_W
    tiles = rows // rows_per_tile
    base = n_ctx // tq
    n_lat = dec_batch * rows * GRID_W
    max_start = tiles - 3

    def qmap(j, b):
        return (base + b * tiles + j, 0)

    def kmap(d):
        return lambda j, b: (base + b * tiles + jnp.clip(j - 1, 0, max_start) + d, 0)

    def bmap(j, b):
        return (jnp.where(j == 0, 0, jnp.where(j == tiles - 1, 2, 1)), 0, 0, 0)

    blk = lambda m: pl.BlockSpec((tq, da), m)
    cspec = pl.BlockSpec((1,) + kc.shape[1:], lambda j, b: (b, 0, 0))
    return pl.pallas_call(
        _attn_lat_kernel,
        out_shape=jax.ShapeDtypeStruct((n_lat, da), BF16),
        grid=(tiles, dec_batch),
        in_specs=[blk(qmap)] + [blk(kmap(d)) for d in range(3)] + [blk(kmap(d)) for d in range(3)]
        + [cspec, cspec, pl.BlockSpec((1,) + bias.shape[1:], bmap)],
        out_specs=pl.BlockSpec((tq, da), lambda j, b: (b * tiles + j, 0)),
        compiler_params=_cparams(2),
        name="attn_lat",
    )(q, k, k, k, v, v, v, kc, vc, bias)


def _window_tables(rows):
    rpt = ATT_TILE // GRID_W
    krows = 3 * rpt
    tiles = rows // rpt
    kh = min(WIN_H_MAX, rows)
    rpb_w = 2 * WIN_W - 1
    arow = -np.ones((3, rpt, krows), np.int64)
    for var, j in enumerate((0, 1, tiles - 1)):
        r0 = j * rpt
        k0 = int(np.clip(j - 1, 0, tiles - 3)) * rpt
        for qi in range(rpt):
            qrow = r0 + qi
            rs = int(np.clip(qrow - kh // 2, 0, rows - kh))
            for ki in range(krows):
                krow = k0 + ki
                if rs <= krow < rs + kh:
                    arow[var, qi, ki] = krow - qrow + WIN_H_MAX - 1
    ec = np.zeros((GRID_W, GRID_W, rpb_w), np.float32)
    for qc in range(GRID_W):
        cs = int(np.clip(qc - WIN_W // 2, 0, GRID_W - WIN_W))
        for kc in range(cs, cs + WIN_W):
            ec[qc, kc, int(np.clip(kc - qc, -(WIN_W - 1), WIN_W - 1)) + WIN_W - 1] = 1.0
    return arow.tolist(), ec


def _bias_kernel(arow, u_ref, o_ref):
    var_id = pl.program_id(0)
    for var, table in enumerate(arow):
        @pl.when(var_id == var)
        def _(table=table):
            for qi, row in enumerate(table):
                for ki, a in enumerate(row):
                    blk = u_ref[0, a] if a >= 0 else jnp.full((GRID_W, GRID_W), NEG_BIG, F32)
                    o_ref[0, 0, qi * GRID_W:(qi + 1) * GRID_W, ki * GRID_W:(ki + 1) * GRID_W] = blk


def _window_bias(rpb, rows):
    arow, ec = _window_tables(rows)
    nh = rpb.shape[0]
    col_blocks = jnp.einsum("hab,xyb->haxy", rpb, ec, precision=lax.Precision.HIGHEST)
    col_blocks = jnp.where(ec.sum(-1) > 0.5, col_blocks * LOG2E, NEG_BIG)
    return pl.pallas_call(
        functools.partial(_bias_kernel, arow),
        out_shape=jax.ShapeDtypeStruct((len(arow), nh, ATT_TILE, 3 * ATT_TILE), F32),
        grid=(len(arow), nh),
        in_specs=[pl.BlockSpec((1,) + col_blocks.shape[1:], lambda v, h: (h, 0, 0, 0))],
        out_specs=pl.BlockSpec((1, 1, ATT_TILE, 3 * ATT_TILE), lambda v, h: (v, h, 0, 0)),
        compiler_params=_cparams(2),
        name="window_bias",
    )(col_blocks)


def _fourier_kernel(u_ref, w1_ref, ct_ref, st_ref, o_ref, p_scr, q_scr):
    @pl.when(pl.program_id(1) == 0)
    def _():
        pq = jnp.dot(u_ref[...], w1_ref[...], preferred_element_type=F32)
        df = p_scr.shape[1]
        p_scr[...] = pq[:, :df].astype(BF16)
        q_scr[...] = pq[:, df:].astype(BF16)

    o = (jnp.dot(ct_ref[...], p_scr[...], preferred_element_type=F32)
         - jnp.dot(st_ref[...], q_scr[...], preferred_element_type=F32))
    o_ref[...] = o.astype(BF16)


def _fourier(u, w1, ct, st, batch, t_len, first_block, row_tile):
    df = u.shape[1]
    steps = t_len // row_tile
    return pl.pallas_call(
        _fourier_kernel,
        out_shape=jax.ShapeDtypeStruct((batch * t_len, df), BF16),
        grid=(batch, steps),
        in_specs=[pl.BlockSpec((t_len, df), lambda b, i: (first_block + b, 0)),
                  pl.BlockSpec(w1.shape, lambda b, i: (0, 0)),
                  pl.BlockSpec((row_tile, t_len), lambda b, i: (i, 0)),
                  pl.BlockSpec((row_tile, t_len), lambda b, i: (i, 0))],
        out_specs=pl.BlockSpec((row_tile, df), lambda b, i: (b * steps + i, 0)),
        scratch_shapes=[pltpu.VMEM((t_len, df), BF16), pltpu.VMEM((t_len, df), BF16)],
        compiler_params=_cparams(2),
        name=f"fourier_{t_len}",
    )(u, w1, ct, st)


def _dft_tables(t_len):
    c = FGROUP_DIM
    jk = np.outer(np.arange(c), np.arange(c)) % c
    ang = 2.0 * np.pi * jk / c
    eye = np.eye(N_FGROUPS)
    w1 = np.concatenate([np.kron(eye, np.cos(ang)), np.kron(eye, np.sin(ang))], axis=1) / np.sqrt(c)
    tt = np.outer(np.arange(t_len), np.arange(t_len)) % t_len
    angt = 2.0 * np.pi * tt / t_len
    return (w1.astype(np.float32), (np.cos(angt) / np.sqrt(t_len)).astype(np.float32),
            (np.sin(angt) / np.sqrt(t_len)).astype(np.float32))


def _merge_kernel(n_ctx_tiles, xp_ref, xs_ref, mod_ref, g1_ref, g2_ref, afc_ref, afl_ref, aac_ref, aal_ref,
                  wf_ref, wa_ref, wg_ref, bg_ref, wo_ref, wrt_ref, br_ref,
                  x1_ref, h2_ref, idx_ref, wts_ref, rank_ref, cnt_ref, carry):
    t = pl.program_id(0)
    is_ctx = t < n_ctx_tiles

    @pl.when(t == 0)
    def _():
        carry[...] = jnp.zeros_like(carry)

    sh1 = mod_ref[0, 0:1, :]
    sc1 = mod_ref[0, 1:2, :]
    gt1 = mod_ref[0, 2:3, :]
    sh2 = mod_ref[0, 3:4, :]
    sc2 = mod_ref[0, 4:5, :]
    d = xp_ref.shape[1]
    part_rows = xp_ref.shape[0] // MERGE_PARTS
    parts = [slice(p * part_rows, (p + 1) * part_rows) for p in range(MERGE_PARTS)]
    xs_ = [jnp.where(is_ctx, xp_ref[rs, :], xs_ref[rs, :]) for rs in parts]
    hbs = [(_rms(x) * g1_ref[...] * (1.0 + sc1) + sh1).astype(BF16) for x in xs_]
    gates = [jax.nn.sigmoid(jnp.dot(hb, wg_ref[...], preferred_element_type=F32) + bg_ref[...]) for hb in hbs]
    fas = [jnp.dot(jnp.where(is_ctx, afc_ref[rs, :], afl_ref[rs, :]), wf_ref[...], preferred_element_type=F32)
           for rs in parts]
    fbs = [jnp.dot(jnp.where(is_ctx, aac_ref[rs, :], aal_ref[rs, :]), wa_ref[...], preferred_element_type=F32)
           for rs in parts]
    mixes = [(g[:, :d] * fa + g[:, d:] * fb).astype(BF16) for g, fa, fb in zip(gates, fas, fbs)]
    x1s = [x + gt1 * jnp.dot(mix, wo_ref[...], preferred_element_type=F32) for x, mix in zip(xs_, mixes)]
    h2s = []
    for rs, x1 in zip(parts, x1s):
        x1_ref[rs, :] = x1
        h2 = _rms(x1) * g2_ref[...] * (1.0 + sc2) + sh2
        h2_ref[rs, :] = _pack_halves(h2)
        h2s.append(h2)
    h2 = jnp.concatenate(h2s, axis=0)
    h2_hi = h2.astype(BF16)
    h2_lo = (h2 - h2_hi.astype(F32)).astype(BF16)
    ne = br_ref.shape[0]
    nt_dims = (((1,), (1,)), ((), ()))
    both = lax.dot_general(wrt_ref[...], h2_hi, nt_dims, preferred_element_type=F32)
    cross = lax.dot_general(wrt_ref[:ne, :], h2_lo, nt_dims, preferred_element_type=F32)
    logits_t = both[:ne, :] + both[ne:, :] + cross + br_ref[...]
    idx_o, wts_o, rank_o, counts = _route(logits_t, carry[...])
    idx_ref[...] = idx_o
    wts_ref[...] = wts_o
    rank_ref[...] = rank_o
    carry[...] = counts
    cnt_ref[...] = counts


def _route(logits_t, counts):
    ne, tm = logits_t.shape
    row = lax.broadcasted_iota(jnp.int32, (ne, tm), 0)
    lg = logits_t
    vals, idxs = [], []
    for _ in range(TOP_K):
        m = jnp.max(lg, axis=0, keepdims=True)
        am = jnp.min(jnp.where(lg == m, row, ne), axis=0, keepdims=True)
        vals.append(m)
        idxs.append(am)
        lg = jnp.where(row == am, -jnp.inf, lg)
    es = [jnp.exp(v - vals[0]) for v in vals]
    den = functools.reduce(lambda a, b: a + b, es)
    onehot = functools.reduce(lambda a, b: a + b, [(row == am).astype(F32) for am in idxs])
    s_i = lax.broadcasted_iota(jnp.int32, (tm, tm), 0)
    t_i = lax.broadcasted_iota(jnp.int32, (tm, tm), 1)
    earlier = (s_i < t_i).astype(BF16)
    before = jnp.dot(onehot.astype(BF16), earlier, preferred_element_type=F32) + counts
    out_row = lax.broadcasted_iota(jnp.int32, (SUBLANES, tm), 0)
    idx_o = jnp.zeros((SUBLANES, tm), jnp.int32)
    wts_o = jnp.zeros((SUBLANES, tm), F32)
    rank_o = jnp.zeros((SUBLANES, tm), jnp.int32)
    for kk in range(TOP_K):
        rk = jnp.sum(jnp.where(row == idxs[kk], before, 0.0), axis=0, keepdims=True).astype(jnp.int32)
        idx_o = jnp.where(out_row == kk, idxs[kk], idx_o)
        wts_o = jnp.where(out_row == kk, es[kk] / den, wts_o)
        rank_o = jnp.where(out_row == kk, rk, rank_o)
    return idx_o, wts_o, rank_o, counts + jnp.sum(onehot, axis=1, keepdims=True)


def _merge(xp, xs, mod, g1, g2, af_c, af_l, aa_c, aa_l, wf, wa, wg, bg, wo, wr, br, tiles_per_lat_batch):
    n_ctx, d = xp.shape
    n = n_ctx + xs.shape[0]
    tm = TOK_TILE
    n_ctx_tiles = n_ctx // tm
    last = n_ctx_tiles - 1
    ne = wr.shape[1]
    da = aa_c.shape[1]
    df = af_c.shape[1]

    def mod_row(t):
        return jnp.where(t < n_ctx_tiles, 0, 1 + (t - n_ctx_tiles) // tiles_per_lat_batch)

    cmap = lambda t: (jnp.minimum(t, last), 0)
    lmap = lambda t: (jnp.maximum(t - n_ctx_tiles, 0), 0)
    full = lambda a: pl.BlockSpec(a.shape, lambda t: (0,) * a.ndim)
    tok = lambda w: pl.BlockSpec((tm, w), lambda t: (t, 0))
    per_choice = lambda dt: jax.ShapeDtypeStruct((SUBLANES, n), dt)
    choice_spec = pl.BlockSpec((SUBLANES, tm), lambda t: (0, t))
    wr_hi = wr.astype(BF16)
    wr_lo = (wr - wr_hi.astype(F32)).astype(BF16)
    wr_t = jnp.concatenate([wr_hi.T, wr_lo.T], axis=0)
    return pl.pallas_call(
        functools.partial(_merge_kernel, n_ctx_tiles),
        out_shape=(jax.ShapeDtypeStruct((n, d), F32), jax.ShapeDtypeStruct((n, d // 2), jnp.int32),
                   per_choice(jnp.int32), per_choice(F32), per_choice(jnp.int32),
                   jax.ShapeDtypeStruct((ne, 1), F32)),
        grid=(n // tm,),
        in_specs=[pl.BlockSpec((tm, d), cmap), pl.BlockSpec((tm, d), lmap),
                  pl.BlockSpec((1,) + mod.shape[1:], lambda t: (mod_row(t), 0, 0)),
                  full(g1), full(g2),
                  pl.BlockSpec((tm, df), cmap), pl.BlockSpec((tm, df), lmap),
                  pl.BlockSpec((tm, da), cmap), pl.BlockSpec((tm, da), lmap),
                  full(wf), full(wa), full(wg), full(bg), full(wo), full(wr_t), full(br)],
        out_specs=(tok(d), tok(d // 2), choice_spec, choice_spec, choice_spec,
                   pl.BlockSpec((ne, 1), lambda t: (0, 0))),
        scratch_shapes=[pltpu.VMEM((ne, 1), F32)],
        compiler_params=_cparams(1),
        name="merge",
    )(xp, xs, mod, g1, g2, af_c, af_l, aa_c, aa_l, wf, wa, wg, bg, wo, wr_t, br)


def _pos_kernel(off_ref, idx_ref, rank_ref, o_ref):
    idx = idx_ref[...]
    pos = rank_ref[...]
    for e in range(off_ref.shape[0]):
        pos = pos + jnp.where(idx == e, off_ref[e], 0)
    o_ref[...] = pos


def _positions(first_row, idx, rank):
    rows, n = idx.shape
    blk = POS_COLS
    spec = pl.BlockSpec((rows, blk), lambda j, off: (0, j))
    return pl.pallas_call(
        _pos_kernel,
        out_shape=jax.ShapeDtypeStruct((rows, n), jnp.int32),
        grid_spec=pltpu.PrefetchScalarGridSpec(num_scalar_prefetch=1, grid=(n // blk,),
                                               in_specs=[spec, spec], out_specs=spec),
        compiler_params=_cparams(1),
        name="positions",
    )(first_row, idx, rank)


def _moe_kernel(te_ref, nt_ref, nx_ref, rv_ref, sl_ref, x_ref, w1_hbm, b1_ref, w2_hbm, b2_ref, y_ref,
                w1s, w2s, sem):
    i = pl.program_id(0)
    e = te_ref[i]
    prev = te_ref[jnp.maximum(i - 1, 0)]
    first_of_run = (i == 0) | (e != prev)
    slot = sl_ref[i]

    def stage(expert, s):
        return (pltpu.make_async_copy(w1_hbm.at[expert], w1s.at[s], sem.at[0, s]),
                pltpu.make_async_copy(w2_hbm.at[expert], w2s.at[s], sem.at[1, s]))

    @pl.when(i == 0)
    def _():
        for cp in stage(e, slot):
            cp.start()

    @pl.when(first_of_run)
    def _():
        for cp in stage(e, slot):
            cp.wait()

        @pl.when(nx_ref[i] >= 0)
        def _():
            for cp in stage(nx_ref[i], 1 - slot):
                cp.start()

    def expert_rows(rs):
        dff = w2s.shape[1]
        x_lo, x_hi = _unpack_halves(x_ref[rs, :])
        x = jnp.concatenate([x_lo.astype(BF16), x_hi.astype(BF16)], axis=1)
        gu = jnp.dot(x, w1s[slot].astype(BF16), preferred_element_type=F32) + b1_ref[0]
        gate = jnp.minimum(gu[:, :dff], SWIGLU_LIMIT)
        up = jnp.clip(gu[:, dff:], -SWIGLU_LIMIT, SWIGLU_LIMIT)
        glu = gate * jax.nn.sigmoid(SWIGLU_ALPHA * gate)
        act = ((up + 1.0) * glu).astype(BF16)
        y = jnp.dot(act, w2s[slot].astype(BF16), preferred_element_type=F32) + b2_ref[0]
        y_ref[rs, :] = _pack_halves(y)

    tm = x_ref.shape[0]
    rv = rv_ref[i]

    @pl.when(rv == tm)
    def _():
        expert_rows(slice(0, tm))

    for sub in range(tm // MOE_SUB):
        @pl.when((rv < tm) & (rv > sub * MOE_SUB))
        def _(sub=sub):
            expert_rows(slice(sub * MOE_SUB, (sub + 1) * MOE_SUB))


def _moe(tile_expert, n_tiles, next_expert, rows_valid, slots, xs_sorted, w1, b1, w2, b2):
    p, dh = xs_sorted.shape
    ne, d, dff2 = w1.shape
    dff = w2.shape[1]
    tm = MOE_TILE
    max_tiles = p // tm

    def row_map(i, te, nt, nx, rv, sl):
        return (jnp.minimum(i, nt[0] - 1), 0)

    def b_map(i, te, nt, nx, rv, sl):
        return (te[i], 0, 0)

    grid_spec = pltpu.PrefetchScalarGridSpec(
        num_scalar_prefetch=5,
        grid=(max_tiles,),
        in_specs=[pl.BlockSpec((tm, dh), row_map),
                  pl.BlockSpec(memory_space=pl.ANY),
                  pl.BlockSpec((1, 1, dff2), b_map),
                  pl.BlockSpec(memory_space=pl.ANY),
                  pl.BlockSpec((1, 1, d), b_map)],
        out_specs=pl.BlockSpec((tm, dh), row_map),
        scratch_shapes=[pltpu.VMEM((2, d, dff2), F32), pltpu.VMEM((2, dff, d), F32),
                        pltpu.SemaphoreType.DMA((2, 2))],
    )
    return pl.pallas_call(
        _moe_kernel,
        out_shape=jax.ShapeDtypeStruct((p, dh), jnp.int32),
        grid_spec=grid_spec,
        compiler_params=_cparams(1),
        name="moe",
    )(tile_expert, n_tiles, next_expert, rows_valid, slots, xs_sorted,
      w1, b1.reshape(ne, 1, dff2), w2, b2.reshape(ne, 1, d))


def _sc_mesh():
    return plsc.VectorSubcoreMesh(core_axis_name="c", subcore_axis_name="s",
                                  num_cores=SC_CORES, num_subcores=SC_SUBCORES)


def _sc_dispatch(h, idx, p_rows):
    n, d = h.shape
    nw, items, chunk = idx.shape
    tok_w = n // nw
    n_chunks = items // TOP_K

    @functools.partial(
        pl.kernel, mesh=_sc_mesh(),
        out_type=jax.ShapeDtypeStruct((p_rows, d), h.dtype),
        scratch_types=[pltpu.VMEM((items, chunk), jnp.int32),
                       pltpu.VMEM((2, chunk, d), h.dtype),
                       pltpu.SemaphoreType.DMA((2,)), pltpu.SemaphoreType.DMA((2,))],
        name="sc_dispatch",
    )
    def run(h_hbm, idx_hbm, out_hbm, idx_v, rows_v, lsem, ssem):
        wid = lax.axis_index("s") * SC_CORES + lax.axis_index("c")
        base = wid * tok_w
        pltpu.sync_copy(idx_hbm.at[wid], idx_v)

        def load(j, slot):
            return pltpu.make_async_copy(h_hbm.at[pl.ds(base + j * chunk, chunk)], rows_v.at[slot], lsem.at[slot])

        def scat(j, kk, slot):
            return pltpu.make_async_copy(rows_v.at[slot], out_hbm.at[idx_v.at[j * TOP_K + kk]], ssem.at[slot])

        load(0, 0).start()

        @pl.loop(0, n_chunks, step=2)
        def _(j0):
            for b in range(2):
                j = j0 + b

                @pl.when(j >= 1)
                def _():
                    for kk in range(TOP_K):
                        scat(j - 1, kk, 1 - b).wait()

                @pl.when(j + 1 < n_chunks)
                def _():
                    load(j + 1, 1 - b).start()

                load(j, b).wait()
                for kk in range(TOP_K):
                    scat(j, kk, b).start()

        for kk in range(TOP_K):
            scat(n_chunks - 1, kk, (n_chunks - 1) % 2).wait()

    return run(h, idx)


def _sc_gather(y, idx, n):
    d = y.shape[1]
    nw, items, chunk = idx.shape
    tok_w = n // nw

    @functools.partial(
        pl.kernel, mesh=_sc_mesh(),
        out_type=jax.ShapeDtypeStruct((TOP_K, n, d), y.dtype),
        scratch_types=[pltpu.VMEM((items, chunk), jnp.int32),
                       pltpu.VMEM((2, chunk, d), y.dtype),
                       pltpu.SemaphoreType.DMA((2,)), pltpu.SemaphoreType.DMA((2,))],
        name="sc_gather",
    )
    def run(y_hbm, idx_hbm, out_hbm, idx_v, rows_v, gsem, wsem):
        wid = lax.axis_index("s") * SC_CORES + lax.axis_index("c")
        base = wid * tok_w
        pltpu.sync_copy(idx_hbm.at[wid], idx_v)

        def gather(i, slot):
            return pltpu.make_async_copy(y_hbm.at[idx_v.at[i]], rows_v.at[slot], gsem.at[slot])

        def write(i, slot):
            dst = out_hbm.at[i % TOP_K, pl.ds(base + (i // TOP_K) * chunk, chunk)]
            return pltpu.make_async_copy(rows_v.at[slot], dst, wsem.at[slot])

        gather(0, 0).start()

        @pl.loop(0, items, step=2)
        def _(i0):
            for b in range(2):
                i = i0 + b

                @pl.when(i >= 1)
                def _():
                    write(i - 1, 1 - b).wait()

                @pl.when(i + 1 < items)
                def _():
                    gather(i + 1, 1 - b).start()

                gather(i, b).wait()
                write(i, b).start()

        write(items - 1, (items - 1) % 2).wait()

    return run(y, idx)


def _combine_kernel(n_ctx_tiles, x1_ref, yg_ref, wts_ref, mod_ref, op_ref, os_ref):
    t = pl.program_id(0)
    gt2 = mod_ref[0, 5:6, :]
    w = wts_ref[...]
    acc_lo, acc_hi = None, None
    for kk in range(TOP_K):
        y_lo, y_hi = _unpack_halves(yg_ref[kk])
        wk = w[:, kk:kk + 1]
        acc_lo = wk * y_lo if acc_lo is None else acc_lo + wk * y_lo
        acc_hi = wk * y_hi if acc_hi is None else acc_hi + wk * y_hi
    out = x1_ref[...] + gt2 * jnp.concatenate([acc_lo, acc_hi], axis=1)

    @pl.when(t < n_ctx_tiles)
    def _():
        op_ref[...] = out

    @pl.when(t >= n_ctx_tiles)
    def _():
        os_ref[...] = out


def _combine(x1, yg, wts, mod, n_ctx, lat_seq):
    n, d = x1.shape
    tm = COMBINE_TILE
    assert n_ctx % tm == 0 and lat_seq % tm == 0
    tiles_per_lat_batch = lat_seq // tm
    n_ctx_tiles = n_ctx // tm
    last = n_ctx_tiles - 1

    def mod_row(t):
        return jnp.where(t < n_ctx_tiles, 0, 1 + (t - n_ctx_tiles) // tiles_per_lat_batch)

    return pl.pallas_call(
        functools.partial(_combine_kernel, n_ctx_tiles),
        out_shape=(jax.ShapeDtypeStruct((n_ctx, d), F32), jax.ShapeDtypeStruct((n - n_ctx, d), F32)),
        grid=(n // tm,),
        in_specs=[pl.BlockSpec((tm, d), lambda t: (t, 0)),
                  pl.BlockSpec((TOP_K, tm, d // 2), lambda t: (0, t, 0)),
                  pl.BlockSpec((tm, TOP_K), lambda t: (t, 0)),
                  pl.BlockSpec((1,) + mod.shape[1:], lambda t: (mod_row(t), 0, 0))],
        out_specs=(pl.BlockSpec((tm, d), lambda t: (jnp.minimum(t, last), 0)),
                   pl.BlockSpec((tm, d), lambda t: (jnp.maximum(t - n_ctx_tiles, 0), 0))),
        compiler_params=_cparams(1),
        name="combine",
    )(x1, yg, wts, mod)


def kernel(x_prompt, x_sample, cache_k, cache_v, c, c_ctx, g_norm1, w_ada, b_ada, w_in, g_q, g_k, rpb,
           w_fmap, w_amap, w_gate, b_gate, w_out, g_norm2, w_router, b_router, w1, b1, w2, b2):
    batch, seq, d = x_prompt.shape
    dec_batch, dec_seq, _ = x_sample.shape
    assert w_ada.shape[0] == 1, "single-layer trunk"
    rows = dec_seq // GRID_W
    da = N_HEADS * HEAD_DIM
    n_ctx = batch * seq
    n_lat = dec_batch * dec_seq
    n = n_ctx + n_lat
    ne = w_router.shape[2]
    assert TOK_TILE % seq == 0 and n_ctx % TOK_TILE == 0 and dec_seq % TOK_TILE == 0
    assert rows % (ATT_TILE // GRID_W) == 0 and rows >= 3 * (ATT_TILE // GRID_W) and rows >= WIN_H_MAX
    assert seq == ATT_TILE and cache_k.shape[3] == ATT_TILE
    tiles_per_lat_batch = dec_seq // TOK_TILE

    n_mod_rows = -(-(1 + dec_batch) // SUBLANES) * SUBLANES
    cvecs = jnp.zeros((n_mod_rows, d), F32).at[0].set(c_ctx).at[1:1 + dec_batch].set(c)
    mod = _ada(cvecs, w_ada[0], b_ada[0]).reshape(n_mod_rows, 6, d)

    xp = x_prompt.reshape(n_ctx, d)
    xs = x_sample.reshape(n_lat, d)
    g1 = g_norm1[0].reshape(1, d)
    g2 = g_norm2[0].reshape(1, d)
    hsum = jnp.asarray(np.kron(np.eye(N_HEADS), np.full((HEAD_DIM, HEAD_DIM), 1.0 / HEAD_DIM)), BF16)
    gq_t = jnp.tile(g_q[0], N_HEADS).reshape(1, da)
    gk_t = jnp.tile(g_k[0], N_HEADS).reshape(1, da)

    u, q, k, v, new_k, new_v = _proj(xp, xs, mod, g1, w_in[0].astype(BF16), hsum, gq_t, gk_t,
                                      batch, seq, tiles_per_lat_batch)

    aa_c = _attn_ctx(q, k, v, batch, seq)
    bias = _window_bias(rpb[0], rows)
    past = cache_k.shape[3]
    kc = cache_k[:, 0].transpose(0, 2, 1, 3).reshape(dec_batch, past, da).astype(BF16)
    vc = cache_v[:, 0].transpose(0, 2, 1, 3).reshape(dec_batch, past, da).astype(BF16)
    aa_l = _attn_lat(q, k, v, kc, vc, bias, n_ctx, dec_batch, rows)

    w1c, ct_c, st_c = _dft_tables(seq)
    _, ct_l, st_l = _dft_tables(dec_seq)
    w1c = jnp.asarray(w1c).astype(BF16)
    af_c = _fourier(u, w1c, jnp.asarray(ct_c).astype(BF16), jnp.asarray(st_c).astype(BF16),
                    batch, seq, 0, seq)
    af_l = _fourier(u, w1c, jnp.asarray(ct_l).astype(BF16), jnp.asarray(st_l).astype(BF16),
                    dec_batch, dec_seq, n_ctx // dec_seq, TOK_TILE)

    x1, h2, idx, wts, rank, counts = _merge(
        xp, xs, mod, g1, g2, af_c, af_l, aa_c, aa_l,
        w_fmap[0].astype(BF16), w_amap[0].astype(BF16), w_gate[0].astype(BF16), b_gate[0].reshape(1, -1),
        w_out[0].astype(BF16), w_router[0], b_router[0].reshape(ne, 1), tiles_per_lat_batch)

    tm = MOE_TILE
    max_tiles = (n * TOP_K) // tm + ne
    cnt = counts[:, 0].astype(jnp.int32)
    tiles_e = (cnt + tm - 1) // tm
    tile_end = jnp.cumsum(tiles_e)
    pad_off = (tile_end - tiles_e) * tm
    n_tiles = tile_end[-1:]
    tile_ids = jnp.arange(max_tiles, dtype=jnp.int32)
    tile_expert = jnp.sum((tile_ids[:, None] >= tile_end[None, :]).astype(jnp.int32), axis=1)
    last_e = jnp.sum((n_tiles - 1 >= tile_end).astype(jnp.int32))
    tile_expert = jnp.minimum(tile_expert, last_e).astype(jnp.int32)
    assert n % POS_COLS == 0
    pos = _positions(pad_off.astype(jnp.int32), idx, rank)[:TOP_K]

    p_rows = max_tiles * tm
    n_workers = SC_CORES * SC_SUBCORES
    assert n % (n_workers * SC_CHUNK * 2) == 0
    chunks = n // (n_workers * SC_CHUNK)
    pos_lists = pos.astype(jnp.int32).reshape(TOP_K, n_workers, chunks, SC_CHUNK).transpose(1, 2, 0, 3)
    pos_lists = pos_lists.reshape(n_workers, chunks * TOP_K, SC_CHUNK)
    xs_sorted = _sc_dispatch(h2, pos_lists, p_rows)
    e_ids = jnp.arange(ne, dtype=jnp.int32)
    later_used = (e_ids[None, :] > e_ids[:, None]) & (tiles_e[None, :] > 0)
    next_of_e = jnp.min(jnp.where(later_used, e_ids[None, :], ne), axis=1)
    next_of_e = jnp.where(next_of_e == ne, -1, next_of_e)
    is_e = tile_expert[:, None] == e_ids[None, :]
    next_expert = jnp.sum(jnp.where(is_e, next_of_e[None, :], 0), axis=1)
    rows_left = jnp.sum(jnp.where(is_e, (cnt + pad_off)[None, :], 0), axis=1) - tile_ids * tm
    rows_valid = jnp.where(tile_ids < n_tiles[0], jnp.clip(rows_left, 0, tm), 0)
    run_of_e = jnp.cumsum((tiles_e > 0).astype(jnp.int32)) - 1
    slots = jnp.sum(jnp.where(is_e, run_of_e[None, :], 0), axis=1) % 2
    y_sorted = _moe(tile_expert, n_tiles.astype(jnp.int32), next_expert.astype(jnp.int32),
                    rows_valid.astype(jnp.int32), slots.astype(jnp.int32), xs_sorted,
                    w1[0], b1[0], w2[0], b2[0])
    yg = _sc_gather(y_sorted, pos_lists, n)
    y_p, y_s = _combine(x1, yg, wts[:TOP_K].T, mod, n_ctx, dec_seq)
    return (y_p.reshape(batch, seq, d), y_s.reshape(dec_batch, dec_seq, d), new_k, new_v)
```

```python
import functools
import math

import numpy as np
import jax
import jax.numpy as jnp
from jax import lax
from jax.experimental import pallas as pl
from jax.experimental.pallas import tpu as pltpu
from jax.experimental.pallas import tpu_sc as plsc

F32 = jnp.float32
BF16 = jnp.bfloat16

GRID_W = 64
N_HEADS = 8
HEAD_DIM = 64
N_FGROUPS = 4
FGROUP_DIM = 128
WIN_H_MAX = 8
WIN_W = 16
TOP_K = 4
SWIGLU_LIMIT = 7.0
SWIGLU_ALPHA = 1.702
EPS = 1e-6
LOG2E = math.log2(math.e)
QK_PRESCALE = HEAD_DIM ** -0.5 * LOG2E

LANES = 128
SUBLANES = 8
ADA_COLS = 2048
POS_COLS = 2048
TOK_TILE = 512
COMBINE_TILE = 1024
ATT_TILE = 256
MOE_TILE = 1024
MOE_SUB = 256
MERGE_PARTS = 2
VMEM_LIMIT = 56 * 1024 * 1024
NEG_BIG = -1e30

SC_CORES = 2
SC_SUBCORES = 16
SC_CHUNK = 64


def _cparams(n_axes, vmem=VMEM_LIMIT):
    return pltpu.CompilerParams(dimension_semantics=("arbitrary",) * n_axes, vmem_limit_bytes=vmem)


def _rms(x):
    return x * lax.rsqrt(jnp.mean(x * x, axis=-1, keepdims=True) + EPS)


def _pack_halves(x):
    c = x.shape[1] // 2
    lo = lax.bitcast_convert_type(x[:, :c].astype(BF16).astype(F32), jnp.uint32)
    hi = lax.bitcast_convert_type(x[:, c:].astype(BF16).astype(F32), jnp.uint32)
    return lax.bitcast_convert_type(hi | (lo >> 16), jnp.int32)


def _unpack_halves(w):
    u = lax.bitcast_convert_type(w, jnp.uint32)
    lo = lax.bitcast_convert_type(u << 16, F32)
    hi = lax.bitcast_convert_type(u & jnp.uint32(0xFFFF0000), F32)
    return lo, hi


def _ada_kernel(c_ref, w_ref, b_ref, o_ref):
    cv = c_ref[...]
    s = cv * jax.nn.sigmoid(cv)
    w = w_ref[...]
    s_hi = s.astype(BF16)
    s_lo = (s - s_hi.astype(F32)).astype(BF16)
    w_hi = w.astype(BF16)
    w_lo = (w - w_hi.astype(F32)).astype(BF16)
    o_ref[...] = (jnp.dot(s_hi, w_hi, preferred_element_type=F32)
                  + jnp.dot(s_hi, w_lo, preferred_element_type=F32)
                  + jnp.dot(s_lo, w_hi, preferred_element_type=F32) + b_ref[...])


def _ada(cvecs, w_ada, b_ada):
    rows, d = cvecs.shape
    n = w_ada.shape[1]
    blk = ADA_COLS
    return pl.pallas_call(
        _ada_kernel,
        out_shape=jax.ShapeDtypeStruct((rows, n), F32),
        grid=(n // blk,),
        in_specs=[pl.BlockSpec((rows, d), lambda j: (0, 0)),
                  pl.BlockSpec((d, blk), lambda j: (0, j)),
                  pl.BlockSpec((1, blk), lambda j: (0, j))],
        out_specs=pl.BlockSpec((rows, blk), lambda j: (0, j)),
        compiler_params=_cparams(1),
        name="ada",
    )(cvecs, w_ada, b_ada.reshape(1, n))


def _proj_kernel(n_ctx_tiles, xp_ref, xs_ref, mod_ref, g1_ref, win_ref, hsum_ref, gq_ref, gk_ref,
                 u_ref, q_ref, k_ref, v_ref, nk_ref, nv_ref):
    t = pl.program_id(0)
    is_ctx = t < n_ctx_tiles
    sh1 = mod_ref[0, 0:1, :]
    sc1 = mod_ref[0, 1:2, :]
    da = q_ref.shape[1]
    df = u_ref.shape[1]
    nb, _, nh, s, dh = nk_ref.shape
    parts = [slice(b * s, (b + 1) * s) for b in range(nb)]
    hs = [(_rms(jnp.where(is_ctx, xp_ref[rs, :], xs_ref[rs, :])) * g1_ref[...] * (1.0 + sc1) + sh1).astype(BF16)
          for rs in parts]
    projs = [jnp.dot(h, win_ref[...], preferred_element_type=F32) for h in hs]
    kns, vs = [], []
    for rs, proj in zip(parts, projs):
        u_ref[rs, :] = proj[:, :df].astype(BF16)
        q = proj[:, df:df + da]
        k = proj[:, df + da:df + 2 * da]
        v = proj[:, df + 2 * da:]
        msq = jnp.dot((q * q).astype(BF16), hsum_ref[...], preferred_element_type=F32)
        msk = jnp.dot((k * k).astype(BF16), hsum_ref[...], preferred_element_type=F32)
        qn = q * lax.rsqrt(msq + EPS) * gq_ref[...]
        kn = k * lax.rsqrt(msk + EPS) * gk_ref[...]
        q_ref[rs, :] = (qn * QK_PRESCALE).astype(BF16)
        k_ref[rs, :] = kn.astype(BF16)
        v_ref[rs, :] = v.astype(BF16)
        kns.append(kn)
        vs.append(v)

    @pl.when(is_ctx)
    def _():
        for b in range(nb):
            for hd in range(nh):
                nk_ref[b, 0, hd] = kns[b][:, hd * dh:(hd + 1) * dh]
                nv_ref[b, 0, hd] = vs[b][:, hd * dh:(hd + 1) * dh]


def _proj(xp, xs, mod, g1, w_in_b, hsum, gq_t, gk_t, batch, seq, tiles_per_lat_batch):
    n_ctx, d = xp.shape
    n_lat = xs.shape[0]
    tm = TOK_TILE
    n_ctx_tiles = n_ctx // tm
    n_tiles = (n_ctx + n_lat) // tm
    n = n_ctx + n_lat
    d_in = w_in_b.shape[1]
    da = N_HEADS * HEAD_DIM
    df = d_in - 3 * da
    bpt = tm // seq
    last = n_ctx_tiles - 1

    def mod_row(t):
        return jnp.where(t < n_ctx_tiles, 0, 1 + (t - n_ctx_tiles) // tiles_per_lat_batch)

    tok = lambda w: pl.BlockSpec((tm, w), lambda t: (t, 0))
    full = lambda a: pl.BlockSpec(a.shape, lambda t: (0,) * a.ndim)
    kv_spec = pl.BlockSpec((bpt, 1, N_HEADS, seq, HEAD_DIM), lambda t: (jnp.minimum(t, last), 0, 0, 0, 0))
    kv_shape = jax.ShapeDtypeStruct((batch, 1, N_HEADS, seq, HEAD_DIM), F32)
    return pl.pallas_call(
        functools.partial(_proj_kernel, n_ctx_tiles),
        out_shape=(jax.ShapeDtypeStruct((n, df), BF16),) + (jax.ShapeDtypeStruct((n, da), BF16),) * 3
        + (kv_shape, kv_shape),
        grid=(n_tiles,),
        in_specs=[pl.BlockSpec((tm, d), lambda t: (jnp.minimum(t, last), 0)),
                  pl.BlockSpec((tm, d), lambda t: (jnp.maximum(t - n_ctx_tiles, 0), 0)),
                  pl.BlockSpec((1,) + mod.shape[1:], lambda t: (mod_row(t), 0, 0)),
                  full(g1), full(w_in_b), full(hsum), full(gq_t), full(gk_t)],
        out_specs=(tok(df), tok(da), tok(da), tok(da), kv_spec, kv_spec),
        compiler_params=_cparams(1),
        name="proj",
    )(xp, xs, mod, g1, w_in_b, hsum, gq_t, gk_t)


def _softmax_pv(q2, key_blocks, val_blocks, bias_blocks):
    lane = lax.broadcasted_iota(jnp.int32, (1, LANES), 1)
    masks = [lane < HEAD_DIM, lane >= HEAD_DIM]

    def head_scores(half):
        qh = jnp.where(masks[half], q2, jnp.zeros_like(q2))
        scores = []
        for kb, bb in zip(key_blocks, bias_blocks):
            s = lax.dot_general(qh, kb, (((1,), (1,)), ((), ())), preferred_element_type=F32)
            if bb is not None:
                s = s + bb[half]
            scores.append(s)
        return scores

    def head_out(scores):
        m = jnp.max(functools.reduce(jnp.maximum, scores), axis=-1, keepdims=True)
        ps = [jnp.exp2(s - m) for s in scores]
        denom = jnp.sum(functools.reduce(lambda a, b: a + b, ps), axis=-1, keepdims=True)
        o = functools.reduce(lambda a, b: a + b,
                             [jnp.dot(p.astype(BF16), vb, preferred_element_type=F32)
                              for p, vb in zip(ps, val_blocks)])
        return o / denom

    if len(key_blocks) > 1:
        outs = [head_out(s) for s in [head_scores(0), head_scores(1)]]
    else:
        outs = [head_out(head_scores(half)) for half in range(2)]
    return jnp.where(masks[0], outs[0], outs[1])


def _attn_ctx_kernel(q_ref, k_ref, v_ref, o_ref):
    for p in range(q_ref.shape[1] // LANES):
        sl = slice(p * LANES, (p + 1) * LANES)
        o = _softmax_pv(q_ref[:, sl], [k_ref[:, sl]], [v_ref[:, sl]], [None])
        o_ref[:, sl] = o.astype(BF16)


def _attn_ctx(q, k, v, batch, seq):
    da = q.shape[1]
    spec = pl.BlockSpec((seq, da), lambda b: (b, 0))
    return pl.pallas_call(
        _attn_ctx_kernel,
        out_shape=jax.ShapeDtypeStruct((batch * seq, da), BF16),
        grid=(batch,),
        in_specs=[spec, spec, spec],
        out_specs=spec,
        compiler_params=_cparams(1),
        name="attn_ctx",
    )(q, k, v)


def _attn_lat_kernel(q_ref, k0_ref, k1_ref, k2_ref, v0_ref, v1_ref, v2_ref, kc_ref, vc_ref, bias_ref, o_ref):
    tk = k0_ref.shape[0]
    for p in range(q_ref.shape[1] // LANES):
        sl = slice(p * LANES, (p + 1) * LANES)
        keys = [k0_ref[:, sl], k1_ref[:, sl], k2_ref[:, sl], kc_ref[0, :, sl]]
        vals = [v0_ref[:, sl], v1_ref[:, sl], v2_ref[:, sl], vc_ref[0, :, sl]]
        biases = [[bias_ref[0, 2 * p + half, :, d * tk:(d + 1) * tk] for half in range(2)] for d in range(3)]
        o = _softmax_pv(q_ref[:, sl], keys, vals, biases + [None])
        o_ref[:, sl] = o.astype(BF16)


def _attn_lat(q, k, v, kc, vc, bias, n_ctx, dec_batch, rows):
    da = q.shape[1]
    tq = ATT_TILE
    rows_per_tile = tq // GRID_W
    tiles = rows // rows_per_tile
    base = n_ctx // tq
    n_lat = dec_batch * rows * GRID_W
    max_start = tiles - 3

    def qmap(j, b):
        return (base + b * tiles + j, 0)

    def kmap(d):
        return lambda j, b: (base + b * tiles + jnp.clip(j - 1, 0, max_start) + d, 0)

    def bmap(j, b):
        return (jnp.where(j == 0, 0, jnp.where(j == tiles - 1, 2, 1)), 0, 0, 0)

    blk = lambda m: pl.BlockSpec((tq, da), m)
    cspec = pl.BlockSpec((1,) + kc.shape[1:], lambda j, b: (b, 0, 0))
    return pl.pallas_call(
        _attn_lat_kernel,
        out_shape=jax.ShapeDtypeStruct((n_lat, da), BF16),
        grid=(tiles, dec_batch),
        in_specs=[blk(qmap)] + [blk(kmap(d)) for d in range(3)] + [blk(kmap(d)) for d in range(3)]
        + [cspec, cspec, pl.BlockSpec((1,) + bias.shape[1:], bmap)],
        out_specs=pl.BlockSpec((tq, da), lambda j, b: (b * tiles + j, 0)),
        compiler_params=_cparams(2),
        name="attn_lat",
    )(q, k, k, k, v, v, v, kc, vc, bias)


def _window_tables(rows):
    rpt = ATT_TILE // GRID_W
    krows = 3 * rpt
    tiles = rows // rpt
    kh = min(WIN_H_MAX, rows)
    rpb_w = 2 * WIN_W - 1
    arow = -np.ones((3, rpt, krows), np.int64)
    for var, j in enumerate((0, 1, tiles - 1)):
        r0 = j * rpt
        k0 = int(np.clip(j - 1, 0, tiles - 3)) * rpt
        for qi in range(rpt):
            qrow = r0 + qi
            rs = int(np.clip(qrow - kh // 2, 0, rows - kh))
            for ki in range(krows):
                krow = k0 + ki
                if rs <= krow < rs + kh:
                    arow[var, qi, ki] = krow - qrow + WIN_H_MAX - 1
    ec = np.zeros((GRID_W, GRID_W, rpb_w), np.float32)
    for qc in range(GRID_W):
        cs = int(np.clip(qc - WIN_W // 2, 0, GRID_W - WIN_W))
        for kc in range(cs, cs + WIN_W):
            ec[qc, kc, int(np.clip(kc - qc, -(WIN_W - 1), WIN_W - 1)) + WIN_W - 1] = 1.0
    return arow.tolist(), ec


def _bias_kernel(arow, u_ref, o_ref):
    var_id = pl.program_id(0)
    for var, table in enumerate(arow):
        @pl.when(var_id == var)
        def _(table=table):
            for hd in range(u_ref.shape[0]):
                for qi, row in enumerate(table):
                    for ki, a in enumerate(row):
                        blk = u_ref[hd, a] if a >= 0 else jnp.full((GRID_W, GRID_W), NEG_BIG, F32)
                        o_ref[0, hd, qi * GRID_W:(qi + 1) * GRID_W, ki * GRID_W:(ki + 1) * GRID_W] = blk


def _window_bias(rpb, rows):
    arow, ec = _window_tables(rows)
    nh = rpb.shape[0]
    col_blocks = jnp.einsum("hab,xyb->haxy", rpb, ec, precision=lax.Precision.HIGHEST)
    col_blocks = jnp.where(ec.sum(-1) > 0.5, col_blocks * LOG2E, NEG_BIG)
    return pl.pallas_call(
        functools.partial(_bias_kernel, arow),
        out_shape=jax.ShapeDtypeStruct((len(arow), nh, ATT_TILE, 3 * ATT_TILE), F32),
        grid=(len(arow),),
        in_specs=[pl.BlockSpec(col_blocks.shape, lambda v: (0, 0, 0, 0))],
        out_specs=pl.BlockSpec((1, nh, ATT_TILE, 3 * ATT_TILE), lambda v: (v, 0, 0, 0)),
        compiler_params=_cparams(1),
        name="window_bias",
    )(col_blocks)


def _fourier_kernel(u_ref, w1_ref, ct_ref, st_ref, o_ref, p_scr, q_scr):
    @pl.when(pl.program_id(1) == 0)
    def _():
        pq = jnp.dot(u_ref[...], w1_ref[...], preferred_element_type=F32)
        df = p_scr.shape[1]
        p_scr[...] = pq[:, :df].astype(BF16)
        q_scr[...] = pq[:, df:].astype(BF16)

    o = (jnp.dot(ct_ref[...], p_scr[...], preferred_element_type=F32)
         - jnp.dot(st_ref[...], q_scr[...], preferred_element_type=F32))
    o_ref[...] = o.astype(BF16)


def _fourier(u, w1, ct, st, batch, t_len, first_block, row_tile):
    df = u.shape[1]
    steps = t_len // row_tile
    return pl.pallas_call(
        _fourier_kernel,
        out_shape=jax.ShapeDtypeStruct((batch * t_len, df), BF16),
        grid=(batch, steps),
        in_specs=[pl.BlockSpec((t_len, df), lambda b, i: (first_block + b, 0)),
                  pl.BlockSpec(w1.shape, lambda b, i: (0, 0)),
                  pl.BlockSpec((row_tile, t_len), lambda b, i: (i, 0)),
                  pl.BlockSpec((row_tile, t_len), lambda b, i: (i, 0))],
        out_specs=pl.BlockSpec((row_tile, df), lambda b, i: (b * steps + i, 0)),
        scratch_shapes=[pltpu.VMEM((t_len, df), BF16), pltpu.VMEM((t_len, df), BF16)],
        compiler_params=_cparams(2),
        name=f"fourier_{t_len}",
    )(u, w1, ct, st)


def _dft_tables(t_len):
    c = FGROUP_DIM
    jk = np.outer(np.arange(c), np.arange(c)) % c
    ang = 2.0 * np.pi * jk / c
    eye = np.eye(N_FGROUPS)
    w1 = np.concatenate([np.kron(eye, np.cos(ang)), np.kron(eye, np.sin(ang))], axis=1) / np.sqrt(c)
    tt = np.outer(np.arange(t_len), np.arange(t_len)) % t_len
    angt = 2.0 * np.pi * tt / t_len
    return (w1.astype(np.float32), (np.cos(angt) / np.sqrt(t_len)).astype(np.float32),
            (np.sin(angt) / np.sqrt(t_len)).astype(np.float32))


def _merge_kernel(n_ctx_tiles, xp_ref, xs_ref, mod_ref, g1_ref, g2_ref, afc_ref, afl_ref, aac_ref, aal_ref,
                  wf_ref, wa_ref, wg_ref, bg_ref, wo_ref, wrt_ref, br_ref,
                  x1_ref, h2_ref, idx_ref, wts_ref, rank_ref, cnt_ref, carry):
    t = pl.program_id(0)
    is_ctx = t < n_ctx_tiles

    @pl.when(t == 0)
    def _():
        carry[...] = jnp.zeros_like(carry)

    sh1 = mod_ref[0, 0:1, :]
    sc1 = mod_ref[0, 1:2, :]
    gt1 = mod_ref[0, 2:3, :]
    sh2 = mod_ref[0, 3:4, :]
    sc2 = mod_ref[0, 4:5, :]
    d = xp_ref.shape[1]
    part_rows = xp_ref.shape[0] // MERGE_PARTS
    parts = [slice(p * part_rows, (p + 1) * part_rows) for p in range(MERGE_PARTS)]
    xs_ = [jnp.where(is_ctx, xp_ref[rs, :], xs_ref[rs, :]) for rs in parts]
    hbs = [(_rms(x) * g1_ref[...] * (1.0 + sc1) + sh1).astype(BF16) for x in xs_]
    gates = [jax.nn.sigmoid(jnp.dot(hb, wg_ref[...], preferred_element_type=F32) + bg_ref[...]) for hb in hbs]
    fas = [jnp.dot(jnp.where(is_ctx, afc_ref[rs, :], afl_ref[rs, :]), wf_ref[...], preferred_element_type=F32)
           for rs in parts]
    fbs = [jnp.dot(jnp.where(is_ctx, aac_ref[rs, :], aal_ref[rs, :]), wa_ref[...], preferred_element_type=F32)
           for rs in parts]
    mixes = [(g[:, :d] * fa + g[:, d:] * fb).astype(BF16) for g, fa, fb in zip(gates, fas, fbs)]
    x1s = [x + gt1 * jnp.dot(mix, wo_ref[...], preferred_element_type=F32) for x, mix in zip(xs_, mixes)]
    h2s = []
    for rs, x1 in zip(parts, x1s):
        x1_ref[rs, :] = x1
        h2 = _rms(x1) * g2_ref[...] * (1.0 + sc2) + sh2
        h2_ref[rs, :] = _pack_halves(h2)
        h2s.append(h2)
    h2 = jnp.concatenate(h2s, axis=0)
    h2_hi = h2.astype(BF16)
    h2_lo = (h2 - h2_hi.astype(F32)).astype(BF16)
    ne = br_ref.shape[0]
    nt_dims = (((1,), (1,)), ((), ()))
    both = lax.dot_general(wrt_ref[...], h2_hi, nt_dims, preferred_element_type=F32)
    cross = lax.dot_general(wrt_ref[:ne, :], h2_lo, nt_dims, preferred_element_type=F32)
    logits_t = both[:ne, :] + both[ne:, :] + cross + br_ref[...]
    idx_o, wts_o, rank_o, counts = _route(logits_t, carry[...])
    idx_ref[...] = idx_o
    wts_ref[...] = wts_o
    rank_ref[...] = rank_o
    carry[...] = counts
    cnt_ref[...] = counts


def _route(logits_t, counts):
    ne, tm = logits_t.shape
    row = lax.broadcasted_iota(jnp.int32, (ne, tm), 0)
    lg = logits_t
    vals, idxs = [], []
    for _ in range(TOP_K):
        m = jnp.max(lg, axis=0, keepdims=True)
        am = jnp.min(jnp.where(lg == m, row, ne), axis=0, keepdims=True)
        vals.append(m)
        idxs.append(am)
        lg = jnp.where(row == am, -jnp.inf, lg)
    es = [jnp.exp(v - vals[0]) for v in vals]
    den = functools.reduce(lambda a, b: a + b, es)
    onehot = functools.reduce(lambda a, b: a + b, [(row == am).astype(F32) for am in idxs])
    s_i = lax.broadcasted_iota(jnp.int32, (tm, tm), 0)
    t_i = lax.broadcasted_iota(jnp.int32, (tm, tm), 1)
    earlier = (s_i < t_i).astype(BF16)
    before = jnp.dot(onehot.astype(BF16), earlier, preferred_element_type=F32) + counts
    out_row = lax.broadcasted_iota(jnp.int32, (SUBLANES, tm), 0)
    idx_o = jnp.zeros((SUBLANES, tm), jnp.int32)
    wts_o = jnp.zeros((SUBLANES, tm), F32)
    rank_o = jnp.zeros((SUBLANES, tm), jnp.int32)
    for kk in range(TOP_K):
        rk = jnp.sum(jnp.where(row == idxs[kk], before, 0.0), axis=0, keepdims=True).astype(jnp.int32)
        idx_o = jnp.where(out_row == kk, idxs[kk], idx_o)
        wts_o = jnp.where(out_row == kk, es[kk] / den, wts_o)
        rank_o = jnp.where(out_row == kk, rk, rank_o)
    return idx_o, wts_o, rank_o, counts + jnp.sum(onehot, axis=1, keepdims=True)


def _merge(xp, xs, mod, g1, g2, af_c, af_l, aa_c, aa_l, wf, wa, wg, bg, wo, wr, br, tiles_per_lat_batch):
    n_ctx, d = xp.shape
    n = n_ctx + xs.shape[0]
    tm = TOK_TILE
    n_ctx_tiles = n_ctx // tm
    last = n_ctx_tiles - 1
    ne = wr.shape[1]
    da = aa_c.shape[1]
    df = af_c.shape[1]

    def mod_row(t):
        return jnp.where(t < n_ctx_tiles, 0, 1 + (t - n_ctx_tiles) // tiles_per_lat_batch)

    cmap = lambda t: (jnp.minimum(t, last), 0)
    lmap = lambda t: (jnp.maximum(t - n_ctx_tiles, 0), 0)
    full = lambda a: pl.BlockSpec(a.shape, lambda t: (0,) * a.ndim)
    tok = lambda w: pl.BlockSpec((tm, w), lambda t: (t, 0))
    per_choice = lambda dt: jax.ShapeDtypeStruct((SUBLANES, n), dt)
    choice_spec = pl.BlockSpec((SUBLANES, tm), lambda t: (0, t))
    wr_hi = wr.astype(BF16)
    wr_lo = (wr - wr_hi.astype(F32)).astype(BF16)
    wr_t = jnp.concatenate([wr_hi.T, wr_lo.T], axis=0)
    return pl.pallas_call(
        functools.partial(_merge_kernel, n_ctx_tiles),
        out_shape=(jax.ShapeDtypeStruct((n, d), F32), jax.ShapeDtypeStruct((n, d // 2), jnp.int32),
                   per_choice(jnp.int32), per_choice(F32), per_choice(jnp.int32),
                   jax.ShapeDtypeStruct((ne, 1), F32)),
        grid=(n // tm,),
        in_specs=[pl.BlockSpec((tm, d), cmap), pl.BlockSpec((tm, d), lmap),
                  pl.BlockSpec((1,) + mod.shape[1:], lambda t: (mod_row(t), 0, 0)),
                  full(g1), full(g2),
                  pl.BlockSpec((tm, df), cmap), pl.BlockSpec((tm, df), lmap),
                  pl.BlockSpec((tm, da), cmap), pl.BlockSpec((tm, da), lmap),
                  full(wf), full(wa), full(wg), full(bg), full(wo), full(wr_t), full(br)],
        out_specs=(tok(d), tok(d // 2), choice_spec, choice_spec, choice_spec,
                   pl.BlockSpec((ne, 1), lambda t: (0, 0))),
        scratch_shapes=[pltpu.VMEM((ne, 1), F32)],
        compiler_params=_cparams(1),
        name="merge",
    )(xp, xs, mod, g1, g2, af_c, af_l, aa_c, aa_l, wf, wa, wg, bg, wo, wr_t, br)


def _pos_kernel(off_ref, idx_ref, rank_ref, o_ref):
    idx = idx_ref[...]
    pos = rank_ref[...]
    for e in range(off_ref.shape[0]):
        pos = pos + jnp.where(idx == e, off_ref[e], 0)
    o_ref[...] = pos


def _positions(first_row, idx, rank):
    rows, n = idx.shape
    blk = POS_COLS
    spec = pl.BlockSpec((rows, blk), lambda j, off: (0, j))
    return pl.pallas_call(
        _pos_kernel,
        out_shape=jax.ShapeDtypeStruct((rows, n), jnp.int32),
        grid_spec=pltpu.PrefetchScalarGridSpec(num_scalar_prefetch=1, grid=(n // blk,),
                                               in_specs=[spec, spec], out_specs=spec),
        compiler_params=_cparams(1),
        name="positions",
    )(first_row, idx, rank)


def _moe_kernel(te_ref, nt_ref, nx_ref, rv_ref, sl_ref, x_ref, w1_hbm, b1_ref, w2_hbm, b2_ref, y_ref,
                w1s, w2s, sem):
    i = pl.program_id(0)
    e = te_ref[i]
    prev = te_ref[jnp.maximum(i - 1, 0)]
    first_of_run = (i == 0) | (e != prev)
    slot = sl_ref[i]

    def stage(expert, s):
        return (pltpu.make_async_copy(w1_hbm.at[expert], w1s.at[s], sem.at[0, s]),
                pltpu.make_async_copy(w2_hbm.at[expert], w2s.at[s], sem.at[1, s]))

    @pl.when(i == 0)
    def _():
        for cp in stage(e, slot):
            cp.start()

    @pl.when(first_of_run)
    def _():
        for cp in stage(e, slot):
            cp.wait()

        @pl.when(nx_ref[i] >= 0)
        def _():
            for cp in stage(nx_ref[i], 1 - slot):
                cp.start()

    def expert_rows(rs):
        dff = w2s.shape[1]
        x_lo, x_hi = _unpack_halves(x_ref[rs, :])
        x = jnp.concatenate([x_lo.astype(BF16), x_hi.astype(BF16)], axis=1)
        gu = jnp.dot(x, w1s[slot].astype(BF16), preferred_element_type=F32) + b1_ref[0]
        gate = jnp.minimum(gu[:, :dff], SWIGLU_LIMIT)
        up = jnp.clip(gu[:, dff:], -SWIGLU_LIMIT, SWIGLU_LIMIT)
        glu = gate * jax.nn.sigmoid(SWIGLU_ALPHA * gate)
        act = ((up + 1.0) * glu).astype(BF16)
        y = jnp.dot(act, w2s[slot].astype(BF16), preferred_element_type=F32) + b2_ref[0]
        y_ref[rs, :] = _pack_halves(y)

    tm = x_ref.shape[0]
    rv = rv_ref[i]

    @pl.when(rv == tm)
    def _():
        expert_rows(slice(0, tm))

    for sub in range(tm // MOE_SUB):
        @pl.when((rv < tm) & (rv > sub * MOE_SUB))
        def _(sub=sub):
            expert_rows(slice(sub * MOE_SUB, (sub + 1) * MOE_SUB))


def _moe(tile_expert, n_tiles, next_expert, rows_valid, slots, xs_sorted, w1, b1, w2, b2):
    p, dh = xs_sorted.shape
    ne, d, dff2 = w1.shape
    dff = w2.shape[1]
    tm = MOE_TILE
    max_tiles = p // tm

    def row_map(i, te, nt, nx, rv, sl):
        return (jnp.minimum(i, nt[0] - 1), 0)

    def b_map(i, te, nt, nx, rv, sl):
        return (te[i], 0, 0)

    grid_spec = pltpu.PrefetchScalarGridSpec(
        num_scalar_prefetch=5,
        grid=(max_tiles,),
        in_specs=[pl.BlockSpec((tm, dh), row_map),
                  pl.BlockSpec(memory_space=pl.ANY),
                  pl.BlockSpec((1, 1, dff2), b_map),
                  pl.BlockSpec(memory_space=pl.ANY),
                  pl.BlockSpec((1, 1, d), b_map)],
        out_specs=pl.BlockSpec((tm, dh), row_map),
        scratch_shapes=[pltpu.VMEM((2, d, dff2), F32), pltpu.VMEM((2, dff, d), F32),
                        pltpu.SemaphoreType.DMA((2, 2))],
    )
    return pl.pallas_call(
        _moe_kernel,
        out_shape=jax.ShapeDtypeStruct((p, dh), jnp.int32),
        grid_spec=grid_spec,
        compiler_params=_cparams(1),
        name="moe",
    )(tile_expert, n_tiles, next_expert, rows_valid, slots, xs_sorted,
      w1, b1.reshape(ne, 1, dff2), w2, b2.reshape(ne, 1, d))


def _sc_mesh():
    return plsc.VectorSubcoreMesh(core_axis_name="c", subcore_axis_name="s",
                                  num_cores=SC_CORES, num_subcores=SC_SUBCORES)


def _sc_dispatch(h, idx, p_rows):
    n, d = h.shape
    nw, items, chunk = idx.shape
    tok_w = n // nw
    n_chunks = items // TOP_K

    @functools.partial(
        pl.kernel, mesh=_sc_mesh(),
        out_type=jax.ShapeDtypeStruct((p_rows, d), h.dtype),
        scratch_types=[pltpu.VMEM((items, chunk), jnp.int32),
                       pltpu.VMEM((2, chunk, d), h.dtype),
                       pltpu.SemaphoreType.DMA((2,)), pltpu.SemaphoreType.DMA((2,))],
        name="sc_dispatch",
    )
    def run(h_hbm, idx_hbm, out_hbm, idx_v, rows_v, lsem, ssem):
        wid = lax.axis_index("s") * SC_CORES + lax.axis_index("c")
        base = wid * tok_w
        pltpu.sync_copy(idx_hbm.at[wid], idx_v)

        def load(j, slot):
            return pltpu.make_async_copy(h_hbm.at[pl.ds(base + j * chunk, chunk)], rows_v.at[slot], lsem.at[slot])

        def scat(j, kk, slot):
            return pltpu.make_async_copy(rows_v.at[slot], out_hbm.at[idx_v.at[j * TOP_K + kk]], ssem.at[slot])

        load(0, 0).start()

        @pl.loop(0, n_chunks, step=2)
        def _(j0):
            for b in range(2):
                j = j0 + b

                @pl.when(j >= 1)
                def _():
                    for kk in range(TOP_K):
                        scat(j - 1, kk, 1 - b).wait()

                @pl.when(j + 1 < n_chunks)
                def _():
                    load(j + 1, 1 - b).start()

                load(j, b).wait()
                for kk in range(TOP_K):
                    scat(j, kk, b).start()

        for kk in range(TOP_K):
            scat(n_chunks - 1, kk, (n_chunks - 1) % 2).wait()

    return run(h, idx)


def _sc_gather(y, idx, n):
    d = y.shape[1]
    nw, items, chunk = idx.shape
    tok_w = n // nw

    @functools.partial(
        pl.kernel, mesh=_sc_mesh(),
        out_type=jax.ShapeDtypeStruct((TOP_K, n, d), y.dtype),
        scratch_types=[pltpu.VMEM((items, chunk), jnp.int32),
                       pltpu.VMEM((2, chunk, d), y.dtype),
                       pltpu.SemaphoreType.DMA((2,)), pltpu.SemaphoreType.DMA((2,))],
        name="sc_gather",
    )
    def run(y_hbm, idx_hbm, out_hbm, idx_v, rows_v, gsem, wsem):
        wid = lax.axis_index("s") * SC_CORES + lax.axis_index("c")
        base = wid * tok_w
        pltpu.sync_copy(idx_hbm.at[wid], idx_v)

        def gather(i, slot):
            return pltpu.make_async_copy(y_hbm.at[idx_v.at[i]], rows_v.at[slot], gsem.at[slot])

        def write(i, slot):
            dst = out_hbm.at[i % TOP_K, pl.ds(base + (i // TOP_K) * chunk, chunk)]
            return pltpu.make_async_copy(rows_v.at[slot], dst, wsem.at[slot])

        gather(0, 0).start()

        @pl.loop(0, items, step=2)
        def _(i0):
            for b in range(2):
                i = i0 + b

                @pl.when(i >= 1)
                def _():
                    write(i - 1, 1 - b).wait()

                @pl.when(i + 1 < items)
                def _():
                    gather(i + 1, 1 - b).start()

                gather(i, b).wait()
                write(i, b).start()

        write(items - 1, (items - 1) % 2).wait()

    return run(y, idx)


def _combine_kernel(n_ctx_tiles, x1_ref, yg_ref, wts_ref, mod_ref, op_ref, os_ref):
    t = pl.program_id(0)
    gt2 = mod_ref[0, 5:6, :]
    w = wts_ref[...]
    acc_lo, acc_hi = None, None
    for kk in range(TOP_K):
        y_lo, y_hi = _unpack_halves(yg_ref[kk])
        wk = w[:, kk:kk + 1]
        acc_lo = wk * y_lo if acc_lo is None else acc_lo + wk * y_lo
        acc_hi = wk * y_hi if acc_hi is None else acc_hi + wk * y_hi
    out = x1_ref[...] + gt2 * jnp.concatenate([acc_lo, acc_hi], axis=1)

    @pl.when(t < n_ctx_tiles)
    def _():
        op_ref[...] = out

    @pl.when(t >= n_ctx_tiles)
    def _():
        os_ref[...] = out


def _combine(x1, yg, wts, mod, n_ctx, lat_seq):
    n, d = x1.shape
    tm = COMBINE_TILE
    assert n_ctx % tm == 0 and lat_seq % tm == 0
    tiles_per_lat_batch = lat_seq // tm
    n_ctx_tiles = n_ctx // tm
    last = n_ctx_tiles - 1

    def mod_row(t):
        return jnp.where(t < n_ctx_tiles, 0, 1 + (t - n_ctx_tiles) // tiles_per_lat_batch)

    return pl.pallas_call(
        functools.partial(_combine_kernel, n_ctx_tiles),
        out_shape=(jax.ShapeDtypeStruct((n_ctx, d), F32), jax.ShapeDtypeStruct((n - n_ctx, d), F32)),
        grid=(n // tm,),
        in_specs=[pl.BlockSpec((tm, d), lambda t: (t, 0)),
                  pl.BlockSpec((TOP_K, tm, d // 2), lambda t: (0, t, 0)),
                  pl.BlockSpec((tm, TOP_K), lambda t: (t, 0)),
                  pl.BlockSpec((1,) + mod.shape[1:], lambda t: (mod_row(t), 0, 0))],
        out_specs=(pl.BlockSpec((tm, d), lambda t: (jnp.minimum(t, last), 0)),
                   pl.BlockSpec((tm, d), lambda t: (jnp.maximum(t - n_ctx_tiles, 0), 0))),
        compiler_params=_cparams(1),
        name="combine",
    )(x1, yg, wts, mod)


def kernel(x_prompt, x_sample, cache_k, cache_v, c, c_ctx, g_norm1, w_ada, b_ada, w_in, g_q, g_k, rpb,
           w_fmap, w_amap, w_gate, b_gate, w_out, g_norm2, w_router, b_router, w1, b1, w2, b2):
    batch, seq, d = x_prompt.shape
    dec_batch, dec_seq, _ = x_sample.shape
    assert w_ada.shape[0] == 1, "single-layer trunk"
    rows = dec_seq // GRID_W
    da = N_HEADS * HEAD_DIM
    n_ctx = batch * seq
    n_lat = dec_batch * dec_seq
    n = n_ctx + n_lat
    ne = w_router.shape[2]
    assert TOK_TILE % seq == 0 and n_ctx % TOK_TILE == 0 and dec_seq % TOK_TILE == 0
    assert rows % (ATT_TILE // GRID_W) == 0 and rows >= 3 * (ATT_TILE // GRID_W) and rows >= WIN_H_MAX
    assert seq == ATT_TILE and cache_k.shape[3] == ATT_TILE
    tiles_per_lat_batch = dec_seq // TOK_TILE

    n_mod_rows = -(-(1 + dec_batch) // SUBLANES) * SUBLANES
    cvecs = jnp.zeros((n_mod_rows, d), F32).at[0].set(c_ctx).at[1:1 + dec_batch].set(c)
    mod = _ada(cvecs, w_ada[0], b_ada[0]).reshape(n_mod_rows, 6, d)

    xp = x_prompt.reshape(n_ctx, d)
    xs = x_sample.reshape(n_lat, d)
    g1 = g_norm1[0].reshape(1, d)
    g2 = g_norm2[0].reshape(1, d)
    hsum = jnp.asarray(np.kron(np.eye(N_HEADS), np.full((HEAD_DIM, HEAD_DIM), 1.0 / HEAD_DIM)), BF16)
    gq_t = jnp.tile(g_q[0], N_HEADS).reshape(1, da)
    gk_t = jnp.tile(g_k[0], N_HEADS).reshape(1, da)

    u, q, k, v, new_k, new_v = _proj(xp, xs, mod, g1, w_in[0].astype(BF16), hsum, gq_t, gk_t,
                                      batch, seq, tiles_per_lat_batch)

    aa_c = _attn_ctx(q, k, v, batch, seq)
    bias = _window_bias(rpb[0], rows)
    past = cache_k.shape[3]
    kc = cache_k[:, 0].transpose(0, 2, 1, 3).reshape(dec_batch, past, da).astype(BF16)
    vc = cache_v[:, 0].transpose(0, 2, 1, 3).reshape(dec_batch, past, da).astype(BF16)
    aa_l = _attn_lat(q, k, v, kc, vc, bias, n_ctx, dec_batch, rows)

    w1c, ct_c, st_c = _dft_tables(seq)
    _, ct_l, st_l = _dft_tables(dec_seq)
    w1c = jnp.asarray(w1c).astype(BF16)
    af_c = _fourier(u, w1c, jnp.asarray(ct_c).astype(BF16), jnp.asarray(st_c).astype(BF16),
                    batch, seq, 0, seq)
    af_l = _fourier(u, w1c, jnp.asarray(ct_l).astype(BF16), jnp.asarray(st_l).astype(BF16),
                    dec_batch, dec_seq, n_ctx // dec_seq, TOK_TILE)

    x1, h2, idx, wts, rank, counts = _merge(
        xp, xs, mod, g1, g2, af_c, af_l, aa_c, aa_l,
        w_fmap[0].astype(BF16), w_amap[0].astype(BF16), w_gate[0].astype(BF16), b_gate[0].reshape(1, -1),
        w_out[0].astype(BF16), w_router[0], b_router[0].reshape(ne, 1), tiles_per_lat_batch)

    tm = MOE_TILE
    max_tiles = (n * TOP_K) // tm + ne
    cnt = counts[:, 0].astype(jnp.int32)
    tiles_e = (cnt + tm - 1) // tm
    tile_end = jnp.cumsum(tiles_e)
    pad_off = (tile_end - tiles_e) * tm
    n_tiles = tile_end[-1:]
    tile_ids = jnp.arange(max_tiles, dtype=jnp.int32)
    tile_expert = jnp.sum((tile_ids[:, None] >= tile_end[None, :]).astype(jnp.int32), axis=1)
    last_e = jnp.sum((n_tiles - 1 >= tile_end).astype(jnp.int32))
    tile_expert = jnp.minimum(tile_expert, last_e).astype(jnp.int32)
    assert n % POS_COLS == 0
    pos = _positions(pad_off.astype(jnp.int32), idx, rank)[:TOP_K]

    p_rows = max_tiles * tm
    n_workers = SC_CORES * SC_SUBCORES
    assert n % (n_workers * SC_CHUNK * 2) == 0
    chunks = n // (n_workers * SC_CHUNK)
    pos_lists = pos.astype(jnp.int32).reshape(TOP_K, n_workers, chunks, SC_CHUNK).transpose(1, 2, 0, 3)
    pos_lists = pos_lists.reshape(n_workers, chunks * TOP_K, SC_CHUNK)
    xs_sorted = _sc_dispatch(h2, pos_lists, p_rows)
    e_ids = jnp.arange(ne, dtype=jnp.int32)
    later_used = (e_ids[None, :] > e_ids[:, None]) & (tiles_e[None, :] > 0)
    next_of_e = jnp.min(jnp.where(later_used, e_ids[None, :], ne), axis=1)
    next_of_e = jnp.where(next_of_e == ne, -1, next_of_e)
    is_e = tile_expert[:, None] == e_ids[None, :]
    next_expert = jnp.sum(jnp.where(is_e, next_of_e[None, :], 0), axis=1)
    rows_left = jnp.sum(jnp.where(is_e, (cnt + pad_off)[None, :], 0), axis=1) - tile_ids * tm
    rows_valid = jnp.where(tile_ids < n_tiles[0], jnp.clip(rows_left, 0, tm), 0)
    run_of_e = jnp.cumsum((tiles_e > 0).astype(jnp.int32)) - 1
    slots = jnp.sum(jnp.where(is_e, run_of_e[None, :], 0), axis=1) % 2
    y_sorted = _moe(tile_expert, n_tiles.astype(jnp.int32), next_expert.astype(jnp.int32),
                    rows_valid.astype(jnp.int32), slots.astype(jnp.int32), xs_sorted,
                    w1[0], b1[0], w2[0], b2[0])
    yg = _sc_gather(y_sorted, pos_lists, n)
    y_p, y_s = _combine(x1, yg, wts[:TOP_K].T, mod, n_ctx, dec_seq)
    return (y_p.reshape(batch, seq, d), y_s.reshape(dec_batch, dec_seq, d), new_k, new_v)
```

```python
import functools
import math

import numpy as np
import jax
import jax.numpy as jnp
from jax import lax
from jax.experimental import pallas as pl
from jax.experimental.pallas import tpu as pltpu
from jax.experimental.pallas import tpu_sc as plsc

F32 = jnp.float32
BF16 = jnp.bfloat16

GRID_W = 64
N_HEADS = 8
HEAD_DIM = 64
N_FGROUPS = 4
FGROUP_DIM = 128
WIN_H_MAX = 8
WIN_W = 16
TOP_K = 4
SWIGLU_LIMIT = 7.0
SWIGLU_ALPHA = 1.702
EPS = 1e-6
LOG2E = math.log2(math.e)
QK_PRESCALE = HEAD_DIM ** -0.5 * LOG2E

LANES = 128
SUBLANES = 8
ADA_COLS = 2048
POS_COLS = 2048
TOK_TILE = 512
COMBINE_TILE = 1024
ATT_TILE = 256
CTX_SEQS_PER_STEP = 4
MOE_TILE = 1024
MOE_SUB = 256
MERGE_PARTS = 2
VMEM_LIMIT = 56 * 1024 * 1024
NEG_BIG = -1e30

SC_CORES = 2
SC_SUBCORES = 16
SC_CHUNK = 64


def _cparams(n_axes, vmem=VMEM_LIMIT):
    return pltpu.CompilerParams(dimension_semantics=("arbitrary",) * n_axes, vmem_limit_bytes=vmem)


def _rms(x):
    return x * lax.rsqrt(jnp.mean(x * x, axis=-1, keepdims=True) + EPS)


def _pack_halves(x):
    c = x.shape[1] // 2
    lo = lax.bitcast_convert_type(x[:, :c].astype(BF16).astype(F32), jnp.uint32)
    hi = lax.bitcast_convert_type(x[:, c:].astype(BF16).astype(F32), jnp.uint32)
    return lax.bitcast_convert_type(hi | (lo >> 16), jnp.int32)


def _unpack_halves(w):
    u = lax.bitcast_convert_type(w, jnp.uint32)
    lo = lax.bitcast_convert_type(u << 16, F32)
    hi = lax.bitcast_convert_type(u & jnp.uint32(0xFFFF0000), F32)
    return lo, hi


def _ada_kernel(c_ref, w_ref, b_ref, o_ref):
    cv = c_ref[...]
    s = cv * jax.nn.sigmoid(cv)
    w = w_ref[...]
    s_hi = s.astype(BF16)
    s_lo = (s - s_hi.astype(F32)).astype(BF16)
    w_hi = w.astype(BF16)
    w_lo = (w - w_hi.astype(F32)).astype(BF16)
    o_ref[...] = (jnp.dot(s_hi, w_hi, preferred_element_type=F32)
                  + jnp.dot(s_hi, w_lo, preferred_element_type=F32)
                  + jnp.dot(s_lo, w_hi, preferred_element_type=F32) + b_ref[...])


def _ada(cvecs, w_ada, b_ada):
    rows, d = cvecs.shape
    n = w_ada.shape[1]
    blk = ADA_COLS
    return pl.pallas_call(
        _ada_kernel,
        out_shape=jax.ShapeDtypeStruct((rows, n), F32),
        grid=(n // blk,),
        in_specs=[pl.BlockSpec((rows, d), lambda j: (0, 0)),
                  pl.BlockSpec((d, blk), lambda j: (0, j)),
                  pl.BlockSpec((1, blk), lambda j: (0, j))],
        out_specs=pl.BlockSpec((rows, blk), lambda j: (0, j)),
        compiler_params=_cparams(1),
        name="ada",
    )(cvecs, w_ada, b_ada.reshape(1, n))


def _proj_kernel(n_ctx_tiles, xp_ref, xs_ref, mod_ref, g1_ref, win_ref, hsum_ref, gq_ref, gk_ref,
                 u_ref, q_ref, k_ref, v_ref, nk_ref, nv_ref):
    t = pl.program_id(0)
    is_ctx = t < n_ctx_tiles
    sh1 = mod_ref[0, 0:1, :]
    sc1 = mod_ref[0, 1:2, :]
    da = q_ref.shape[1]
    df = u_ref.shape[1]
    nb, _, nh, s, dh = nk_ref.shape
    parts = [slice(b * s, (b + 1) * s) for b in range(nb)]
    hs = [(_rms(jnp.where(is_ctx, xp_ref[rs, :], xs_ref[rs, :])) * g1_ref[...] * (1.0 + sc1) + sh1).astype(BF16)
          for rs in parts]
    projs = [jnp.dot(h, win_ref[...], preferred_element_type=F32) for h in hs]
    kns, vs = [], []
    for rs, proj in zip(parts, projs):
        u_ref[rs, :] = proj[:, :df].astype(BF16)
        q = proj[:, df:df + da]
        k = proj[:, df + da:df + 2 * da]
        v = proj[:, df + 2 * da:]
        msq = jnp.dot((q * q).astype(BF16), hsum_ref[...], preferred_element_type=F32)
        msk = jnp.dot((k * k).astype(BF16), hsum_ref[...], preferred_element_type=F32)
        qn = q * lax.rsqrt(msq + EPS) * gq_ref[...]
        kn = k * lax.rsqrt(msk + EPS) * gk_ref[...]
        q_ref[rs, :] = (qn * QK_PRESCALE).astype(BF16)
        k_ref[rs, :] = kn.astype(BF16)
        v_ref[rs, :] = v.astype(BF16)
        kns.append(kn)
        vs.append(v)

    @pl.when(is_ctx)
    def _():
        for b in range(nb):
            for hd in range(nh):
                nk_ref[b, 0, hd] = kns[b][:, hd * dh:(hd + 1) * dh]
                nv_ref[b, 0, hd] = vs[b][:, hd * dh:(hd + 1) * dh]


def _proj(xp, xs, mod, g1, w_in_b, hsum, gq_t, gk_t, batch, seq, tiles_per_lat_batch):
    n_ctx, d = xp.shape
    n_lat = xs.shape[0]
    tm = TOK_TILE
    n_ctx_tiles = n_ctx // tm
    n_tiles = (n_ctx + n_lat) // tm
    n = n_ctx + n_lat
    d_in = w_in_b.shape[1]
    da = N_HEADS * HEAD_DIM
    df = d_in - 3 * da
    bpt = tm // seq
    last = n_ctx_tiles - 1

    def mod_row(t):
        return jnp.where(t < n_ctx_tiles, 0, 1 + (t - n_ctx_tiles) // tiles_per_lat_batch)

    tok = lambda w: pl.BlockSpec((tm, w), lambda t: (t, 0))
    full = lambda a: pl.BlockSpec(a.shape, lambda t: (0,) * a.ndim)
    kv_spec = pl.BlockSpec((bpt, 1, N_HEADS, seq, HEAD_DIM), lambda t: (jnp.minimum(t, last), 0, 0, 0, 0))
    kv_shape = jax.ShapeDtypeStruct((batch, 1, N_HEADS, seq, HEAD_DIM), F32)
    return pl.pallas_call(
        functools.partial(_proj_kernel, n_ctx_tiles),
        out_shape=(jax.ShapeDtypeStruct((n, df), BF16),) + (jax.ShapeDtypeStruct((n, da), BF16),) * 3
        + (kv_shape, kv_shape),
        grid=(n_tiles,),
        in_specs=[pl.BlockSpec((tm, d), lambda t: (jnp.minimum(t, last), 0)),
                  pl.BlockSpec((tm, d), lambda t: (jnp.maximum(t - n_ctx_tiles, 0), 0)),
                  pl.BlockSpec((1,) + mod.shape[1:], lambda t: (mod_row(t), 0, 0)),
                  full(g1), full(w_in_b), full(hsum), full(gq_t), full(gk_t)],
        out_specs=(tok(df), tok(da), tok(da), tok(da), kv_spec, kv_spec),
        compiler_params=_cparams(1),
        name="proj",
    )(xp, xs, mod, g1, w_in_b, hsum, gq_t, gk_t)


def _softmax_pv(q2, key_blocks, val_blocks, bias_blocks):
    lane = lax.broadcasted_iota(jnp.int32, (1, LANES), 1)
    masks = [lane < HEAD_DIM, lane >= HEAD_DIM]

    def head_scores(half):
        qh = jnp.where(masks[half], q2, jnp.zeros_like(q2))
        scores = []
        for kb, bb in zip(key_blocks, bias_blocks):
            s = lax.dot_general(qh, kb, (((1,), (1,)), ((), ())), preferred_element_type=F32)
            if bb is not None:
                s = s + bb[half]
            scores.append(s)
        return scores

    def head_out(scores):
        m = jnp.max(functools.reduce(jnp.maximum, scores), axis=-1, keepdims=True)
        ps = [jnp.exp2(s - m) for s in scores]
        denom = jnp.sum(functools.reduce(lambda a, b: a + b, ps), axis=-1, keepdims=True)
        o = functools.reduce(lambda a, b: a + b,
                             [jnp.dot(p.astype(BF16), vb, preferred_element_type=F32)
                              for p, vb in zip(ps, val_blocks)])
        return o / denom

    if len(key_blocks) > 1:
        outs = [head_out(s) for s in [head_scores(0), head_scores(1)]]
    else:
        outs = [head_out(head_scores(half)) for half in range(2)]
    return jnp.where(masks[0], outs[0], outs[1])


def _attn_ctx_kernel(seq, q_ref, k_ref, v_ref, o_ref):
    for b in range(q_ref.shape[0] // seq):
        rs = slice(b * seq, (b + 1) * seq)
        for p in range(q_ref.shape[1] // LANES):
            sl = slice(p * LANES, (p + 1) * LANES)
            o = _softmax_pv(q_ref[rs, sl], [k_ref[rs, sl]], [v_ref[rs, sl]], [None])
            o_ref[rs, sl] = o.astype(BF16)


def _attn_ctx(q, k, v, batch, seq):
    da = q.shape[1]
    assert batch % CTX_SEQS_PER_STEP == 0
    spec = pl.BlockSpec((CTX_SEQS_PER_STEP * seq, da), lambda b: (b, 0))
    return pl.pallas_call(
        functools.partial(_attn_ctx_kernel, seq),
        out_shape=jax.ShapeDtypeStruct((batch * seq, da), BF16),
        grid=(batch // CTX_SEQS_PER_STEP,),
        in_specs=[spec, spec, spec],
        out_specs=spec,
        compiler_params=_cparams(1),
        name="attn_ctx",
    )(q, k, v)


def _attn_lat_kernel(q_ref, k0_ref, k1_ref, k2_ref, v0_ref, v1_ref, v2_ref, kc_ref, vc_ref, bias_ref, o_ref):
    tk = k0_ref.shape[0]
    for p in range(q_ref.shape[1] // LANES):
        sl = slice(p * LANES, (p + 1) * LANES)
        keys = [k0_ref[:, sl], k1_ref[:, sl], k2_ref[:, sl], kc_ref[0, :, sl]]
        vals = [v0_ref[:, sl], v1_ref[:, sl], v2_ref[:, sl], vc_ref[0, :, sl]]
        biases = [[bias_ref[0, 2 * p + half, :, d * tk:(d + 1) * tk] for half in range(2)] for d in range(3)]
        o = _softmax_pv(q_ref[:, sl], keys, vals, biases + [None])
        o_ref[:, sl] = o.astype(BF16)


def _attn_lat(q, k, v, kc, vc, bias, n_ctx, dec_batch, rows):
    da = q.shape[1]
    tq = ATT_TILE
    rows_per_tile = tq // GRID_W
    tiles = rows // rows_per_tile
    base = n_ctx // tq
    n_lat = dec_batch * rows * GRID_W
    max_start = tiles - 3

    def qmap(j, b):
        return (base + b * tiles + j, 0)

    def kmap(d):
        return lambda j, b: (base + b * tiles + jnp.clip(j - 1, 0, max_start) + d, 0)

    def bmap(j, b):
        return (jnp.where(j == 0, 0, jnp.where(j == tiles - 1, 2, 1)), 0, 0, 0)

    blk = lambda m: pl.BlockSpec((tq, da), m)
    cspec = pl.BlockSpec((1,) + kc.shape[1:], lambda j, b: (b, 0, 0))
    return pl.pallas_call(
        _attn_lat_kernel,
        out_shape=jax.ShapeDtypeStruct((n_lat, da), BF16),
        grid=(tiles, dec_batch),
        in_specs=[blk(qmap)] + [blk(kmap(d)) for d in range(3)] + [blk(kmap(d)) for d in range(3)]
        + [cspec, cspec, pl.BlockSpec((1,) + bias.shape[1:], bmap)],
        out_specs=pl.BlockSpec((tq, da), lambda j, b: (b * tiles + j, 0)),
        compiler_params=_cparams(2),
        name="attn_lat",
    )(q, k, k, k, v, v, v, kc, vc, bias)


def _window_tables(rows):
    rpt = ATT_TILE // GRID_W
    krows = 3 * rpt
    tiles = rows // rpt
    kh = min(WIN_H_MAX, rows)
    rpb_w = 2 * WIN_W - 1
    arow = -np.ones((3, rpt, krows), np.int64)
    for var, j in enumerate((0, 1, tiles - 1)):
        r0 = j * rpt
        k0 = int(np.clip(j - 1, 0, tiles - 3)) * rpt
        for qi in range(rpt):
            qrow = r0 + qi
            rs = int(np.clip(qrow - kh // 2, 0, rows - kh))
            for ki in range(krows):
                krow = k0 + ki
                if rs <= krow < rs + kh:
                    arow[var, qi, ki] = krow - qrow + WIN_H_MAX - 1
    ec = np.zeros((GRID_W, GRID_W, rpb_w), np.float32)
    for qc in range(GRID_W):
        cs = int(np.clip(qc - WIN_W // 2, 0, GRID_W - WIN_W))
        for kc in range(cs, cs + WIN_W):
            ec[qc, kc, int(np.clip(kc - qc, -(WIN_W - 1), WIN_W - 1)) + WIN_W - 1] = 1.0
    return arow.tolist(), ec


def _bias_kernel(arow, u_ref, o_ref):
    var_id = pl.program_id(0)
    for var, table in enumerate(arow):
        @pl.when(var_id == var)
        def _(table=table):
            for hd in range(u_ref.shape[0]):
                for qi, row in enumerate(table):
                    for ki, a in enumerate(row):
                        blk = u_ref[hd, a] if a >= 0 else jnp.full((GRID_W, GRID_W), NEG_BIG, F32)
                        o_ref[0, hd, qi * GRID_W:(qi + 1) * GRID_W, ki * GRID_W:(ki + 1) * GRID_W] = blk


def _window_bias(rpb, rows):
    arow, ec = _window_tables(rows)
    nh = rpb.shape[0]
    col_blocks = jnp.einsum("hab,xyb->haxy", rpb, ec, precision=lax.Precision.HIGHEST)
    col_blocks = jnp.where(ec.sum(-1) > 0.5, col_blocks * LOG2E, NEG_BIG)
    return pl.pallas_call(
        functools.partial(_bias_kernel, arow),
        out_shape=jax.ShapeDtypeStruct((len(arow), nh, ATT_TILE, 3 * ATT_TILE), F32),
        grid=(len(arow),),
        in_specs=[pl.BlockSpec(col_blocks.shape, lambda v: (0, 0, 0, 0))],
        out_specs=pl.BlockSpec((1, nh, ATT_TILE, 3 * ATT_TILE), lambda v: (v, 0, 0, 0)),
        compiler_params=_cparams(1),
        name="window_bias",
    )(col_blocks)


def _fourier_kernel(t_len, u_ref, w1_ref, ct_ref, st_ref, o_ref, p_scr, q_scr):
    @pl.when(pl.program_id(1) == 0)
    def _():
        pq = jnp.dot(u_ref[...], w1_ref[...], preferred_element_type=F32)
        df = p_scr.shape[1]
        p_scr[...] = pq[:, :df].astype(BF16)
        q_scr[...] = pq[:, df:].astype(BF16)

    rt = ct_ref.shape[0]
    for b in range(u_ref.shape[0] // t_len):
        rs = slice(b * t_len, (b + 1) * t_len)
        o = (jnp.dot(ct_ref[...], p_scr[rs, :], preferred_element_type=F32)
             - jnp.dot(st_ref[...], q_scr[rs, :], preferred_element_type=F32))
        o_ref[b * rt:(b + 1) * rt, :] = o.astype(BF16)


def _fourier(u, w1, ct, st, batch, t_len, first_block, row_tile, seqs=1):
    df = u.shape[1]
    steps = t_len // row_tile
    assert seqs == 1 or (steps == 1 and batch % seqs == 0 and first_block % seqs == 0)
    return pl.pallas_call(
        functools.partial(_fourier_kernel, t_len),
        out_shape=jax.ShapeDtypeStruct((batch * t_len, df), BF16),
        grid=(batch // seqs, steps),
        in_specs=[pl.BlockSpec((seqs * t_len, df), lambda b, i: (first_block // seqs + b, 0)),
                  pl.BlockSpec(w1.shape, lambda b, i: (0, 0)),
                  pl.BlockSpec((row_tile, t_len), lambda b, i: (i, 0)),
                  pl.BlockSpec((row_tile, t_len), lambda b, i: (i, 0))],
        out_specs=pl.BlockSpec((seqs * row_tile, df), lambda b, i: (b * steps + i, 0)),
        scratch_shapes=[pltpu.VMEM((seqs * t_len, df), BF16), pltpu.VMEM((seqs * t_len, df), BF16)],
        compiler_params=_cparams(2),
        name=f"fourier_{t_len}",
    )(u, w1, ct, st)


def _dft_tables(t_len):
    c = FGROUP_DIM
    jk = np.outer(np.arange(c), np.arange(c)) % c
    ang = 2.0 * np.pi * jk / c
    eye = np.eye(N_FGROUPS)
    w1 = np.concatenate([np.kron(eye, np.cos(ang)), np.kron(eye, np.sin(ang))], axis=1) / np.sqrt(c)
    tt = np.outer(np.arange(t_len), np.arange(t_len)) % t_len
    angt = 2.0 * np.pi * tt / t_len
    return (w1.astype(np.float32), (np.cos(angt) / np.sqrt(t_len)).astype(np.float32),
            (np.sin(angt) / np.sqrt(t_len)).astype(np.float32))


def _merge_kernel(n_ctx_tiles, xp_ref, xs_ref, mod_ref, g1_ref, g2_ref, afc_ref, afl_ref, aac_ref, aal_ref,
                  wf_ref, wa_ref, wg_ref, bg_ref, wo_ref, wrt_ref, br_ref,
                  x1_ref, h2_ref, idx_ref, wts_ref, rank_ref, cnt_ref, carry):
    t = pl.program_id(0)
    is_ctx = t < n_ctx_tiles

    @pl.when(t == 0)
    def _():
        carry[...] = jnp.zeros_like(carry)

    sh1 = mod_ref[0, 0:1, :]
    sc1 = mod_ref[0, 1:2, :]
    gt1 = mod_ref[0, 2:3, :]
    sh2 = mod_ref[0, 3:4, :]
    sc2 = mod_ref[0, 4:5, :]
    d = xp_ref.shape[1]
    part_rows = xp_ref.shape[0] // MERGE_PARTS
    parts = [slice(p * part_rows, (p + 1) * part_rows) for p in range(MERGE_PARTS)]
    xs_ = [jnp.where(is_ctx, xp_ref[rs, :], xs_ref[rs, :]) for rs in parts]
    hbs = [(_rms(x) * g1_ref[...] * (1.0 + sc1) + sh1).astype(BF16) for x in xs_]
    gates = [jax.nn.sigmoid(jnp.dot(hb, wg_ref[...], preferred_element_type=F32) + bg_ref[...]) for hb in hbs]
    fas = [jnp.dot(jnp.where(is_ctx, afc_ref[rs, :], afl_ref[rs, :]), wf_ref[...], preferred_element_type=F32)
           for rs in parts]
    fbs = [jnp.dot(jnp.where(is_ctx, aac_ref[rs, :], aal_ref[rs, :]), wa_ref[...], preferred_element_type=F32)
           for rs in parts]
    mixes = [(g[:, :d] * fa + g[:, d:] * fb).astype(BF16) for g, fa, fb in zip(gates, fas, fbs)]
    x1s = [x + gt1 * jnp.dot(mix, wo_ref[...], preferred_element_type=F32) for x, mix in zip(xs_, mixes)]
    h2s = []
    for rs, x1 in zip(parts, x1s):
        x1_ref[rs, :] = x1
        h2 = _rms(x1) * g2_ref[...] * (1.0 + sc2) + sh2
        h2_ref[rs, :] = _pack_halves(h2)
        h2s.append(h2)
    h2 = jnp.concatenate(h2s, axis=0)
    h2_hi = h2.astype(BF16)
    h2_lo = (h2 - h2_hi.astype(F32)).astype(BF16)
    ne = br_ref.shape[0]
    nt_dims = (((1,), (1,)), ((), ()))
    both = lax.dot_general(wrt_ref[...], h2_hi, nt_dims, preferred_element_type=F32)
    cross = lax.dot_general(wrt_ref[:ne, :], h2_lo, nt_dims, preferred_element_type=F32)
    logits_t = both[:ne, :] + both[ne:, :] + cross + br_ref[...]
    idx_o, wts_o, rank_o, counts = _route(logits_t, carry[...])
    idx_ref[...] = idx_o
    wts_ref[...] = wts_o
    rank_ref[...] = rank_o
    carry[...] = counts
    cnt_ref[...] = counts


def _route(logits_t, counts):
    ne, tm = logits_t.shape
    row = lax.broadcasted_iota(jnp.int32, (ne, tm), 0)
    lg = logits_t
    vals, idxs = [], []
    for _ in range(TOP_K):
        m = jnp.max(lg, axis=0, keepdims=True)
        am = jnp.min(jnp.where(lg == m, row, ne), axis=0, keepdims=True)
        vals.append(m)
        idxs.append(am)
        lg = jnp.where(row == am, -jnp.inf, lg)
    es = [jnp.exp(v - vals[0]) for v in vals]
    den = functools.reduce(lambda a, b: a + b, es)
    onehot = functools.reduce(lambda a, b: a + b, [(row == am).astype(F32) for am in idxs])
    s_i = lax.broadcasted_iota(jnp.int32, (tm, tm), 0)
    t_i = lax.broadcasted_iota(jnp.int32, (tm, tm), 1)
    earlier = (s_i < t_i).astype(BF16)
    before = jnp.dot(onehot.astype(BF16), earlier, preferred_element_type=F32) + counts
    out_row = lax.broadcasted_iota(jnp.int32, (SUBLANES, tm), 0)
    idx_o = jnp.zeros((SUBLANES, tm), jnp.int32)
    wts_o = jnp.zeros((SUBLANES, tm), F32)
    rank_o = jnp.zeros((SUBLANES, tm), jnp.int32)
    for kk in range(TOP_K):
        rk = jnp.sum(jnp.where(row == idxs[kk], before, 0.0), axis=0, keepdims=True).astype(jnp.int32)
        idx_o = jnp.where(out_row == kk, idxs[kk], idx_o)
        wts_o = jnp.where(out_row == kk, es[kk] / den, wts_o)
        rank_o = jnp.where(out_row == kk, rk, rank_o)
    return idx_o, wts_o, rank_o, counts + jnp.sum(onehot, axis=1, keepdims=True)


def _merge(xp, xs, mod, g1, g2, af_c, af_l, aa_c, aa_l, wf, wa, wg, bg, wo, wr, br, tiles_per_lat_batch):
    n_ctx, d = xp.shape
    n = n_ctx + xs.shape[0]
    tm = TOK_TILE
    n_ctx_tiles = n_ctx // tm
    last = n_ctx_tiles - 1
    ne = wr.shape[1]
    da = aa_c.shape[1]
    df = af_c.shape[1]

    def mod_row(t):
        return jnp.where(t < n_ctx_tiles, 0, 1 + (t - n_ctx_tiles) // tiles_per_lat_batch)

    cmap = lambda t: (jnp.minimum(t, last), 0)
    lmap = lambda t: (jnp.maximum(t - n_ctx_tiles, 0), 0)
    full = lambda a: pl.BlockSpec(a.shape, lambda t: (0,) * a.ndim)
    tok = lambda w: pl.BlockSpec((tm, w), lambda t: (t, 0))
    per_choice = lambda dt: jax.ShapeDtypeStruct((SUBLANES, n), dt)
    choice_spec = pl.BlockSpec((SUBLANES, tm), lambda t: (0, t))
    wr_hi = wr.astype(BF16)
    wr_lo = (wr - wr_hi.astype(F32)).astype(BF16)
    wr_t = jnp.concatenate([wr_hi.T, wr_lo.T], axis=0)
    return pl.pallas_call(
        functools.partial(_merge_kernel, n_ctx_tiles),
        out_shape=(jax.ShapeDtypeStruct((n, d), F32), jax.ShapeDtypeStruct((n, d // 2), jnp.int32),
                   per_choice(jnp.int32), per_choice(F32), per_choice(jnp.int32),
                   jax.ShapeDtypeStruct((ne, 1), F32)),
        grid=(n // tm,),
        in_specs=[pl.BlockSpec((tm, d), cmap), pl.BlockSpec((tm, d), lmap),
                  pl.BlockSpec((1,) + mod.shape[1:], lambda t: (mod_row(t), 0, 0)),
                  full(g1), full(g2),
                  pl.BlockSpec((tm, df), cmap), pl.BlockSpec((tm, df), lmap),
                  pl.BlockSpec((tm, da), cmap), pl.BlockSpec((tm, da), lmap),
                  full(wf), full(wa), full(wg), full(bg), full(wo), full(wr_t), full(br)],
        out_specs=(tok(d), tok(d // 2), choice_spec, choice_spec, choice_spec,
                   pl.BlockSpec((ne, 1), lambda t: (0, 0))),
        scratch_shapes=[pltpu.VMEM((ne, 1), F32)],
        compiler_params=_cparams(1),
        name="merge",
    )(xp, xs, mod, g1, g2, af_c, af_l, aa_c, aa_l, wf, wa, wg, bg, wo, wr_t, br)


def _pos_kernel(off_ref, idx_ref, rank_ref, o_ref):
    idx = idx_ref[...]
    pos = rank_ref[...]
    for e in range(off_ref.shape[0]):
        pos = pos + jnp.where(idx == e, off_ref[e], 0)
    o_ref[...] = pos


def _positions(first_row, idx, rank):
    rows, n = idx.shape
    blk = POS_COLS
    spec = pl.BlockSpec((rows, blk), lambda j, off: (0, j))
    return pl.pallas_call(
        _pos_kernel,
        out_shape=jax.ShapeDtypeStruct((rows, n), jnp.int32),
        grid_spec=pltpu.PrefetchScalarGridSpec(num_scalar_prefetch=1, grid=(n // blk,),
                                               in_specs=[spec, spec], out_specs=spec),
        compiler_params=_cparams(1),
        name="positions",
    )(first_row, idx, rank)


def _moe_kernel(te_ref, nt_ref, nx_ref, rv_ref, sl_ref, x_ref, w1_hbm, b1_ref, w2_hbm, b2_ref, y_ref,
                w1s, w2s, sem):
    i = pl.program_id(0)
    e = te_ref[i]
    prev = te_ref[jnp.maximum(i - 1, 0)]
    first_of_run = (i == 0) | (e != prev)
    slot = sl_ref[i]

    def stage(expert, s):
        return (pltpu.make_async_copy(w1_hbm.at[expert], w1s.at[s], sem.at[0, s]),
                pltpu.make_async_copy(w2_hbm.at[expert], w2s.at[s], sem.at[1, s]))

    @pl.when(i == 0)
    def _():
        for cp in stage(e, slot):
            cp.start()

    @pl.when(first_of_run)
    def _():
        for cp in stage(e, slot):
            cp.wait()

        @pl.when(nx_ref[i] >= 0)
        def _():
            for cp in stage(nx_ref[i], 1 - slot):
                cp.start()

    def expert_rows(rs):
        dff = w2s.shape[1]
        x_lo, x_hi = _unpack_halves(x_ref[rs, :])
        x = jnp.concatenate([x_lo.astype(BF16), x_hi.astype(BF16)], axis=1)
        gu = jnp.dot(x, w1s[slot].astype(BF16), preferred_element_type=F32) + b1_ref[0]
        gate = jnp.minimum(gu[:, :dff], SWIGLU_LIMIT)
        up = jnp.clip(gu[:, dff:], -SWIGLU_LIMIT, SWIGLU_LIMIT)
        glu = gate * jax.nn.sigmoid(SWIGLU_ALPHA * gate)
        act = ((up + 1.0) * glu).astype(BF16)
        y = jnp.dot(act, w2s[slot].astype(BF16), preferred_element_type=F32) + b2_ref[0]
        y_ref[rs, :] = _pack_halves(y)

    tm = x_ref.shape[0]
    rv = rv_ref[i]

    @pl.when(rv == tm)
    def _():
        expert_rows(slice(0, tm))

    for sub in range(tm // MOE_SUB):
        @pl.when((rv < tm) & (rv > sub * MOE_SUB))
        def _(sub=sub):
            expert_rows(slice(sub * MOE_SUB, (sub + 1) * MOE_SUB))


def _moe(tile_expert, n_tiles, next_expert, rows_valid, slots, xs_sorted, w1, b1, w2, b2):
    p, dh = xs_sorted.shape
    ne, d, dff2 = w1.shape
    dff = w2.shape[1]
    tm = MOE_TILE
    max_tiles = p // tm

    def row_map(i, te, nt, nx, rv, sl):
        return (jnp.minimum(i, nt[0] - 1), 0)

    def b_map(i, te, nt, nx, rv, sl):
        return (te[i], 0, 0)

    grid_spec = pltpu.PrefetchScalarGridSpec(
        num_scalar_prefetch=5,
        grid=(max_tiles,),
        in_specs=[pl.BlockSpec((tm, dh), row_map),
                  pl.BlockSpec(memory_space=pl.ANY),
                  pl.BlockSpec((1, 1, dff2), b_map),
                  pl.BlockSpec(memory_space=pl.ANY),
                  pl.BlockSpec((1, 1, d), b_map)],
        out_specs=pl.BlockSpec((tm, dh), row_map),
        scratch_shapes=[pltpu.VMEM((2, d, dff2), F32), pltpu.VMEM((2, dff, d), F32),
                        pltpu.SemaphoreType.DMA((2, 2))],
    )
    return pl.pallas_call(
        _moe_kernel,
        out_shape=jax.ShapeDtypeStruct((p, dh), jnp.int32),
        grid_spec=grid_spec,
        compiler_params=_cparams(1),
        name="moe",
    )(tile_expert, n_tiles, next_expert, rows_valid, slots, xs_sorted,
      w1, b1.reshape(ne, 1, dff2), w2, b2.reshape(ne, 1, d))


def _sc_mesh():
    return plsc.VectorSubcoreMesh(core_axis_name="c", subcore_axis_name="s",
                                  num_cores=SC_CORES, num_subcores=SC_SUBCORES)


def _sc_dispatch(h, idx, p_rows):
    n, d = h.shape
    nw, items, chunk = idx.shape
    tok_w = n // nw
    n_chunks = items // TOP_K

    @functools.partial(
        pl.kernel, mesh=_sc_mesh(),
        out_type=jax.ShapeDtypeStruct((p_rows, d), h.dtype),
        scratch_types=[pltpu.VMEM((items, chunk), jnp.int32),
                       pltpu.VMEM((2, chunk, d), h.dtype),
                       pltpu.SemaphoreType.DMA((2,)), pltpu.SemaphoreType.DMA((2,))],
        name="sc_dispatch",
    )
    def run(h_hbm, idx_hbm, out_hbm, idx_v, rows_v, lsem, ssem):
        wid = lax.axis_index("s") * SC_CORES + lax.axis_index("c")
        base = wid * tok_w
        pltpu.sync_copy(idx_hbm.at[wid], idx_v)

        def load(j, slot):
            return pltpu.make_async_copy(h_hbm.at[pl.ds(base + j * chunk, chunk)], rows_v.at[slot], lsem.at[slot])

        def scat(j, kk, slot):
            return pltpu.make_async_copy(rows_v.at[slot], out_hbm.at[idx_v.at[j * TOP_K + kk]], ssem.at[slot])

        load(0, 0).start()

        @pl.loop(0, n_chunks, step=2)
        def _(j0):
            for b in range(2):
                j = j0 + b

                @pl.when(j >= 1)
                def _():
                    for kk in range(TOP_K):
                        scat(j - 1, kk, 1 - b).wait()

                @pl.when(j + 1 < n_chunks)
                def _():
                    load(j + 1, 1 - b).start()

                load(j, b).wait()
                for kk in range(TOP_K):
                    scat(j, kk, b).start()

        for kk in range(TOP_K):
            scat(n_chunks - 1, kk, (n_chunks - 1) % 2).wait()

    return run(h, idx)


def _sc_gather(y, idx, n):
    d = y.shape[1]
    nw, items, chunk = idx.shape
    tok_w = n // nw

    @functools.partial(
        pl.kernel, mesh=_sc_mesh(),
        out_type=jax.ShapeDtypeStruct((TOP_K, n, d), y.dtype),
        scratch_types=[pltpu.VMEM((items, chunk), jnp.int32),
                       pltpu.VMEM((2, chunk, d), y.dtype),
                       pltpu.SemaphoreType.DMA((2,)), pltpu.SemaphoreType.DMA((2,))],
        name="sc_gather",
    )
    def run(y_hbm, idx_hbm, out_hbm, idx_v, rows_v, gsem, wsem):
        wid = lax.axis_index("s") * SC_CORES + lax.axis_index("c")
        base = wid * tok_w
        pltpu.sync_copy(idx_hbm.at[wid], idx_v)

        def gather(i, slot):
            return pltpu.make_async_copy(y_hbm.at[idx_v.at[i]], rows_v.at[slot], gsem.at[slot])

        def write(i, slot):
            dst = out_hbm.at[i % TOP_K, pl.ds(base + (i // TOP_K) * chunk, chunk)]
            return pltpu.make_async_copy(rows_v.at[slot], dst, wsem.at[slot])

        gather(0, 0).start()

        @pl.loop(0, items, step=2)
        def _(i0):
            for b in range(2):
                i = i0 + b

                @pl.when(i >= 1)
                def _():
                    write(i - 1, 1 - b).wait()

                @pl.when(i + 1 < items)
                def _():
                    gather(i + 1, 1 - b).start()

                gather(i, b).wait()
                write(i, b).start()

        write(items - 1, (items - 1) % 2).wait()

    return run(y, idx)


def _combine_kernel(n_ctx_tiles, x1_ref, yg_ref, wts_ref, mod_ref, op_ref, os_ref):
    t = pl.program_id(0)
    gt2 = mod_ref[0, 5:6, :]
    w = wts_ref[...]
    acc_lo, acc_hi = None, None
    for kk in range(TOP_K):
        y_lo, y_hi = _unpack_halves(yg_ref[kk])
        wk = w[:, kk:kk + 1]
        acc_lo = wk * y_lo if acc_lo is None else acc_lo + wk * y_lo
        acc_hi = wk * y_hi if acc_hi is None else acc_hi + wk * y_hi
    out = x1_ref[...] + gt2 * jnp.concatenate([acc_lo, acc_hi], axis=1)

    @pl.when(t < n_ctx_tiles)
    def _():
        op_ref[...] = out

    @pl.when(t >= n_ctx_tiles)
    def _():
        os_ref[...] = out


def _combine(x1, yg, wts, mod, n_ctx, lat_seq):
    n, d = x1.shape
    tm = COMBINE_TILE
    assert n_ctx % tm == 0 and lat_seq % tm == 0
    tiles_per_lat_batch = lat_seq // tm
    n_ctx_tiles = n_ctx // tm
    last = n_ctx_tiles - 1

    def mod_row(t):
        return jnp.where(t < n_ctx_tiles, 0, 1 + (t - n_ctx_tiles) // tiles_per_lat_batch)

    return pl.pallas_call(
        functools.partial(_combine_kernel, n_ctx_tiles),
        out_shape=(jax.ShapeDtypeStruct((n_ctx, d), F32), jax.ShapeDtypeStruct((n - n_ctx, d), F32)),
        grid=(n // tm,),
        in_specs=[pl.BlockSpec((tm, d), lambda t: (t, 0)),
                  pl.BlockSpec((TOP_K, tm, d // 2), lambda t: (0, t, 0)),
                  pl.BlockSpec((tm, TOP_K), lambda t: (t, 0)),
                  pl.BlockSpec((1,) + mod.shape[1:], lambda t: (mod_row(t), 0, 0))],
        out_specs=(pl.BlockSpec((tm, d), lambda t: (jnp.minimum(t, last), 0)),
                   pl.BlockSpec((tm, d), lambda t: (jnp.maximum(t - n_ctx_tiles, 0), 0))),
        compiler_params=_cparams(1),
        name="combine",
    )(x1, yg, wts, mod)


def kernel(x_prompt, x_sample, cache_k, cache_v, c, c_ctx, g_norm1, w_ada, b_ada, w_in, g_q, g_k, rpb,
           w_fmap, w_amap, w_gate, b_gate, w_out, g_norm2, w_router, b_router, w1, b1, w2, b2):
    batch, seq, d = x_prompt.shape
    dec_batch, dec_seq, _ = x_sample.shape
    assert w_ada.shape[0] == 1, "single-layer trunk"
    rows = dec_seq // GRID_W
    da = N_HEADS * HEAD_DIM
    n_ctx = batch * seq
    n_lat = dec_batch * dec_seq
    n = n_ctx + n_lat
    ne = w_router.shape[2]
    assert TOK_TILE % seq == 0 and n_ctx % TOK_TILE == 0 and dec_seq % TOK_TILE == 0
    assert rows % (ATT_TILE // GRID_W) == 0 and rows >= 3 * (ATT_TILE // GRID_W) and rows >= WIN_H_MAX
    assert seq == ATT_TILE and cache_k.shape[3] == ATT_TILE
    tiles_per_lat_batch = dec_seq // TOK_TILE

    n_mod_rows = -(-(1 + dec_batch) // SUBLANES) * SUBLANES
    cvecs = jnp.zeros((n_mod_rows, d), F32).at[0].set(c_ctx).at[1:1 + dec_batch].set(c)
    mod = _ada(cvecs, w_ada[0], b_ada[0]).reshape(n_mod_rows, 6, d)

    xp = x_prompt.reshape(n_ctx, d)
    xs = x_sample.reshape(n_lat, d)
    g1 = g_norm1[0].reshape(1, d)
    g2 = g_norm2[0].reshape(1, d)
    hsum = jnp.asarray(np.kron(np.eye(N_HEADS), np.full((HEAD_DIM, HEAD_DIM), 1.0 / HEAD_DIM)), BF16)
    gq_t = jnp.tile(g_q[0], N_HEADS).reshape(1, da)
    gk_t = jnp.tile(g_k[0], N_HEADS).reshape(1, da)

    u, q, k, v, new_k, new_v = _proj(xp, xs, mod, g1, w_in[0].astype(BF16), hsum, gq_t, gk_t,
                                      batch, seq, tiles_per_lat_batch)

    aa_c = _attn_ctx(q, k, v, batch, seq)
    bias = _window_bias(rpb[0], rows)
    past = cache_k.shape[3]
    kc = cache_k[:, 0].transpose(0, 2, 1, 3).reshape(dec_batch, past, da).astype(BF16)
    vc = cache_v[:, 0].transpose(0, 2, 1, 3).reshape(dec_batch, past, da).astype(BF16)
    aa_l = _attn_lat(q, k, v, kc, vc, bias, n_ctx, dec_batch, rows)

    w1c, ct_c, st_c = _dft_tables(seq)
    _, ct_l, st_l = _dft_tables(dec_seq)
    w1c = jnp.asarray(w1c).astype(BF16)
    af_c = _fourier(u, w1c, jnp.asarray(ct_c).astype(BF16), jnp.asarray(st_c).astype(BF16),
                    batch, seq, 0, seq, CTX_SEQS_PER_STEP)
    af_l = _fourier(u, w1c, jnp.asarray(ct_l).astype(BF16), jnp.asarray(st_l).astype(BF16),
                    dec_batch, dec_seq, n_ctx // dec_seq, TOK_TILE)

    x1, h2, idx, wts, rank, counts = _merge(
        xp, xs, mod, g1, g2, af_c, af_l, aa_c, aa_l,
        w_fmap[0].astype(BF16), w_amap[0].astype(BF16), w_gate[0].astype(BF16), b_gate[0].reshape(1, -1),
        w_out[0].astype(BF16), w_router[0], b_router[0].reshape(ne, 1), tiles_per_lat_batch)

    tm = MOE_TILE
    max_tiles = (n * TOP_K) // tm + ne
    cnt = counts[:, 0].astype(jnp.int32)
    tiles_e = (cnt + tm - 1) // tm
    tile_end = jnp.cumsum(tiles_e)
    pad_off = (tile_end - tiles_e) * tm
    n_tiles = tile_end[-1:]
    tile_ids = jnp.arange(max_tiles, dtype=jnp.int32)
    tile_expert = jnp.sum((tile_ids[:, None] >= tile_end[None, :]).astype(jnp.int32), axis=1)
    last_e = jnp.sum((n_tiles - 1 >= tile_end).astype(jnp.int32))
    tile_expert = jnp.minimum(tile_expert, last_e).astype(jnp.int32)
    assert n % POS_COLS == 0
    pos = _positions(pad_off.astype(jnp.int32), idx, rank)[:TOP_K]

    p_rows = max_tiles * tm
    n_workers = SC_CORES * SC_SUBCORES
    assert n % (n_workers * SC_CHUNK * 2) == 0
    chunks = n // (n_workers * SC_CHUNK)
    pos_lists = pos.astype(jnp.int32).reshape(TOP_K, n_workers, chunks, SC_CHUNK).transpose(1, 2, 0, 3)
    pos_lists = pos_lists.reshape(n_workers, chunks * TOP_K, SC_CHUNK)
    xs_sorted = _sc_dispatch(h2, pos_lists, p_rows)
    e_ids = jnp.arange(ne, dtype=jnp.int32)
    later_used = (e_ids[None, :] > e_ids[:, None]) & (tiles_e[None, :] > 0)
    next_of_e = jnp.min(jnp.where(later_used, e_ids[None, :], ne), axis=1)
    next_of_e = jnp.where(next_of_e == ne, -1, next_of_e)
    is_e = tile_expert[:, None] == e_ids[None, :]
    next_expert = jnp.sum(jnp.where(is_e, next_of_e[None, :], 0), axis=1)
    rows_left = jnp.sum(jnp.where(is_e, (cnt + pad_off)[None, :], 0), axis=1) - tile_ids * tm
    rows_valid = jnp.where(tile_ids < n_tiles[0], jnp.clip(rows_left, 0, tm), 0)
    run_of_e = jnp.cumsum((tiles_e > 0).astype(jnp.int32)) - 1
    slots = jnp.sum(jnp.where(is_e, run_of_e[None, :], 0), axis=1) % 2
    y_sorted = _moe(tile_expert, n_tiles.astype(jnp.int32), next_expert.astype(jnp.int32),
                    rows_valid.astype(jnp.int32), slots.astype(jnp.int32), xs_sorted,
                    w1[0], b1[0], w2[0], b2[0])
    yg = _sc_gather(y_sorted, pos_lists, n)
    y_p, y_s = _combine(x1, yg, wts[:TOP_K].T, mod, n_ctx, dec_seq)
    return (y_p.reshape(batch, seq, d), y_s.reshape(dec_batch, dec_seq, d), new_k, new_v)
```

```python
import functools
import math

import numpy as np
import jax
import jax.numpy as jnp
from jax import lax
from jax.experimental import pallas as pl
from jax.experimental.pallas import tpu as pltpu
from jax.experimental.pallas import tpu_sc as plsc

F32 = jnp.float32
BF16 = jnp.bfloat16

GRID_W = 64
N_HEADS = 8
HEAD_DIM = 64
N_FGROUPS = 4
FGROUP_DIM = 128
WIN_H_MAX = 8
WIN_W = 16
TOP_K = 4
SWIGLU_LIMIT = 7.0
SWIGLU_ALPHA = 1.702
EPS = 1e-6
LOG2E = math.log2(math.e)
QK_PRESCALE = HEAD_DIM ** -0.5 * LOG2E

LANES = 128
SUBLANES = 8
ADA_COLS = 2048
POS_COLS = 2048
TOK_TILE = 512
COMBINE_TILE = 1024
ATT_TILE = 256
CTX_SEQS_PER_STEP = 4
MOE_TILE = 1024
MOE_SUB = 256
MERGE_PARTS = 2
VMEM_LIMIT = 56 * 1024 * 1024
NEG_BIG = -1e30

SC_CORES = 2
SC_SUBCORES = 16
SC_CHUNK = 64


def _cparams(n_axes, vmem=VMEM_LIMIT):
    return pltpu.CompilerParams(dimension_semantics=("arbitrary",) * n_axes, vmem_limit_bytes=vmem)


def _rms(x):
    return x * lax.rsqrt(jnp.mean(x * x, axis=-1, keepdims=True) + EPS)


def _pack_halves(x):
    c = x.shape[1] // 2
    lo = lax.bitcast_convert_type(x[:, :c].astype(BF16).astype(F32), jnp.uint32)
    hi = lax.bitcast_convert_type(x[:, c:].astype(BF16).astype(F32), jnp.uint32)
    return lax.bitcast_convert_type(hi | (lo >> 16), jnp.int32)


def _unpack_halves(w):
    u = lax.bitcast_convert_type(w, jnp.uint32)
    lo = lax.bitcast_convert_type(u << 16, F32)
    hi = lax.bitcast_convert_type(u & jnp.uint32(0xFFFF0000), F32)
    return lo, hi


def _ada_kernel(c_ref, w_ref, b_ref, o_ref):
    cv = c_ref[...]
    s = cv * jax.nn.sigmoid(cv)
    w = w_ref[...]
    s_hi = s.astype(BF16)
    s_lo = (s - s_hi.astype(F32)).astype(BF16)
    w_hi = w.astype(BF16)
    w_lo = (w - w_hi.astype(F32)).astype(BF16)
    o_ref[...] = (jnp.dot(s_hi, w_hi, preferred_element_type=F32)
                  + jnp.dot(s_hi, w_lo, preferred_element_type=F32)
                  + jnp.dot(s_lo, w_hi, preferred_element_type=F32) + b_ref[...])


def _ada(cvecs, w_ada, b_ada):
    rows, d = cvecs.shape
    n = w_ada.shape[1]
    blk = ADA_COLS
    return pl.pallas_call(
        _ada_kernel,
        out_shape=jax.ShapeDtypeStruct((rows, n), F32),
        grid=(n // blk,),
        in_specs=[pl.BlockSpec((rows, d), lambda j: (0, 0)),
                  pl.BlockSpec((d, blk), lambda j: (0, j)),
                  pl.BlockSpec((1, blk), lambda j: (0, j))],
        out_specs=pl.BlockSpec((rows, blk), lambda j: (0, j)),
        compiler_params=_cparams(1),
        name="ada",
    )(cvecs, w_ada, b_ada.reshape(1, n))


def _proj_kernel(n_ctx_tiles, xp_ref, xs_ref, mod_ref, g1_ref, win_ref, hsum_ref, gq_ref, gk_ref,
                 u_ref, q_ref, k_ref, v_ref, nk_ref, nv_ref):
    t = pl.program_id(0)
    is_ctx = t < n_ctx_tiles
    sh1 = mod_ref[0, 0:1, :]
    sc1 = mod_ref[0, 1:2, :]
    da = q_ref.shape[1]
    df = u_ref.shape[1]
    nb, _, nh, s, dh = nk_ref.shape
    parts = [slice(b * s, (b + 1) * s) for b in range(nb)]
    hs = [(_rms(jnp.where(is_ctx, xp_ref[rs, :], xs_ref[rs, :])) * g1_ref[...] * (1.0 + sc1) + sh1).astype(BF16)
          for rs in parts]
    projs = [jnp.dot(h, win_ref[...], preferred_element_type=F32) for h in hs]
    kns, vs = [], []
    for rs, proj in zip(parts, projs):
        u_ref[rs, :] = proj[:, :df].astype(BF16)
        q = proj[:, df:df + da]
        k = proj[:, df + da:df + 2 * da]
        v = proj[:, df + 2 * da:]
        msq = jnp.dot((q * q).astype(BF16), hsum_ref[...], preferred_element_type=F32)
        msk = jnp.dot((k * k).astype(BF16), hsum_ref[...], preferred_element_type=F32)
        qn = q * lax.rsqrt(msq + EPS) * gq_ref[...]
        kn = k * lax.rsqrt(msk + EPS) * gk_ref[...]
        q_ref[rs, :] = (qn * QK_PRESCALE).astype(BF16)
        k_ref[rs, :] = kn.astype(BF16)
        v_ref[rs, :] = v.astype(BF16)
        kns.append(kn)
        vs.append(v)

    @pl.when(is_ctx)
    def _():
        for b in range(nb):
            for hd in range(nh):
                nk_ref[b, 0, hd] = kns[b][:, hd * dh:(hd + 1) * dh]
                nv_ref[b, 0, hd] = vs[b][:, hd * dh:(hd + 1) * dh]


def _proj(xp, xs, mod, g1, w_in_b, hsum, gq_t, gk_t, batch, seq, tiles_per_lat_batch):
    n_ctx, d = xp.shape
    n_lat = xs.shape[0]
    tm = TOK_TILE
    n_ctx_tiles = n_ctx // tm
    n_tiles = (n_ctx + n_lat) // tm
    n = n_ctx + n_lat
    d_in = w_in_b.shape[1]
    da = N_HEADS * HEAD_DIM
    df = d_in - 3 * da
    bpt = tm // seq
    last = n_ctx_tiles - 1

    def mod_row(t):
        return jnp.where(t < n_ctx_tiles, 0, 1 + (t - n_ctx_tiles) // tiles_per_lat_batch)

    tok = lambda w: pl.BlockSpec((tm, w), lambda t: (t, 0))
    full = lambda a: pl.BlockSpec(a.shape, lambda t: (0,) * a.ndim)
    kv_spec = pl.BlockSpec((bpt, 1, N_HEADS, seq, HEAD_DIM), lambda t: (jnp.minimum(t, last), 0, 0, 0, 0))
    kv_shape = jax.ShapeDtypeStruct((batch, 1, N_HEADS, seq, HEAD_DIM), F32)
    return pl.pallas_call(
        functools.partial(_proj_kernel, n_ctx_tiles),
        out_shape=(jax.ShapeDtypeStruct((n, df), BF16),) + (jax.ShapeDtypeStruct((n, da), BF16),) * 3
        + (kv_shape, kv_shape),
        grid=(n_tiles,),
        in_specs=[pl.BlockSpec((tm, d), lambda t: (jnp.minimum(t, last), 0)),
                  pl.BlockSpec((tm, d), lambda t: (jnp.maximum(t - n_ctx_tiles, 0), 0)),
                  pl.BlockSpec((1,) + mod.shape[1:], lambda t: (mod_row(t), 0, 0)),
                  full(g1), full(w_in_b), full(hsum), full(gq_t), full(gk_t)],
        out_specs=(tok(df), tok(da), tok(da), tok(da), kv_spec, kv_spec),
        compiler_params=_cparams(1),
        name="proj",
    )(xp, xs, mod, g1, w_in_b, hsum, gq_t, gk_t)


def _softmax_pv(q2, key_blocks, val_blocks, bias_blocks):
    lane = lax.broadcasted_iota(jnp.int32, (1, LANES), 1)
    masks = [lane < HEAD_DIM, lane >= HEAD_DIM]

    def head_scores(half):
        qh = jnp.where(masks[half], q2, jnp.zeros_like(q2))
        scores = []
        for kb, bb in zip(key_blocks, bias_blocks):
            s = lax.dot_general(qh, kb, (((1,), (1,)), ((), ())), preferred_element_type=F32)
            if bb is not None:
                s = s + bb[half]
            scores.append(s)
        return scores

    def head_out(scores):
        m = jnp.max(functools.reduce(jnp.maximum, scores), axis=-1, keepdims=True)
        ps = [jnp.exp2(s - m) for s in scores]
        denom = jnp.sum(functools.reduce(lambda a, b: a + b, ps), axis=-1, keepdims=True)
        o = functools.reduce(lambda a, b: a + b,
                             [jnp.dot(p.astype(BF16), vb, preferred_element_type=F32)
                              for p, vb in zip(ps, val_blocks)])
        return o / denom

    if len(key_blocks) > 1:
        outs = [head_out(s) for s in [head_scores(0), head_scores(1)]]
    else:
        outs = [head_out(head_scores(half)) for half in range(2)]
    return jnp.where(masks[0], outs[0], outs[1])


def _attn_ctx_kernel(seq, q_ref, k_ref, v_ref, o_ref):
    for b in range(q_ref.shape[0] // seq):
        rs = slice(b * seq, (b + 1) * seq)
        for p in range(q_ref.shape[1] // LANES):
            sl = slice(p * LANES, (p + 1) * LANES)
            o = _softmax_pv(q_ref[rs, sl], [k_ref[rs, sl]], [v_ref[rs, sl]], [None])
            o_ref[rs, sl] = o.astype(BF16)


def _attn_ctx(q, k, v, batch, seq):
    da = q.shape[1]
    assert batch % CTX_SEQS_PER_STEP == 0
    spec = pl.BlockSpec((CTX_SEQS_PER_STEP * seq, da), lambda b: (b, 0))
    return pl.pallas_call(
        functools.partial(_attn_ctx_kernel, seq),
        out_shape=jax.ShapeDtypeStruct((batch * seq, da), BF16),
        grid=(batch // CTX_SEQS_PER_STEP,),
        in_specs=[spec, spec, spec],
        out_specs=spec,
        compiler_params=_cparams(1),
        name="attn_ctx",
    )(q, k, v)


def _attn_lat_kernel(live_blocks, q_ref, k0_ref, k1_ref, k2_ref, v0_ref, v1_ref, v2_ref, kc_ref, vc_ref,
                     bias_ref, o_ref):
    tk = k0_ref.shape[0]
    k_refs, v_refs = (k0_ref, k1_ref, k2_ref), (v0_ref, v1_ref, v2_ref)
    n_tiles = pl.num_programs(0)
    j = pl.program_id(0)
    variant = jnp.where(j == 0, 0, jnp.where(j == n_tiles - 1, 2, 1))

    def tile(blocks):
        for p in range(q_ref.shape[1] // LANES):
            sl = slice(p * LANES, (p + 1) * LANES)
            keys = [k_refs[d][:, sl] for d in blocks] + [kc_ref[0, :, sl]]
            vals = [v_refs[d][:, sl] for d in blocks] + [vc_ref[0, :, sl]]
            biases = [[bias_ref[0, 2 * p + half, :, d * tk:(d + 1) * tk] for half in range(2)] for d in blocks]
            o = _softmax_pv(q_ref[:, sl], keys, vals, biases + [None])
            o_ref[:, sl] = o.astype(BF16)

    for var, blocks in enumerate(live_blocks):
        @pl.when(variant == var)
        def _(blocks=blocks):
            tile(blocks)


def _attn_lat(q, k, v, kc, vc, bias, n_ctx, dec_batch, rows):
    da = q.shape[1]
    tq = ATT_TILE
    rows_per_tile = tq // GRID_W
    tiles = rows // rows_per_tile
    arow, _ = _window_tables(rows)
    live_blocks = tuple(
        tuple(d for d in range(3)
              if any(a >= 0 for row in table for a in row[d * rows_per_tile:(d + 1) * rows_per_tile]))
        for table in arow)
    base = n_ctx // tq
    n_lat = dec_batch * rows * GRID_W
    max_start = tiles - 3

    def qmap(j, b):
        return (base + b * tiles + j, 0)

    def kmap(d):
        return lambda j, b: (base + b * tiles + jnp.clip(j - 1, 0, max_start) + d, 0)

    def bmap(j, b):
        return (jnp.where(j == 0, 0, jnp.where(j == tiles - 1, 2, 1)), 0, 0, 0)

    blk = lambda m: pl.BlockSpec((tq, da), m)
    cspec = pl.BlockSpec((1,) + kc.shape[1:], lambda j, b: (b, 0, 0))
    return pl.pallas_call(
        functools.partial(_attn_lat_kernel, live_blocks),
        out_shape=jax.ShapeDtypeStruct((n_lat, da), BF16),
        grid=(tiles, dec_batch),
        in_specs=[blk(qmap)] + [blk(kmap(d)) for d in range(3)] + [blk(kmap(d)) for d in range(3)]
        + [cspec, cspec, pl.BlockSpec((1,) + bias.shape[1:], bmap)],
        out_specs=pl.BlockSpec((tq, da), lambda j, b: (b * tiles + j, 0)),
        compiler_params=_cparams(2),
        name="attn_lat",
    )(q, k, k, k, v, v, v, kc, vc, bias)


def _window_tables(rows):
    rpt = ATT_TILE // GRID_W
    krows = 3 * rpt
    tiles = rows // rpt
    kh = min(WIN_H_MAX, rows)
    rpb_w = 2 * WIN_W - 1
    arow = -np.ones((3, rpt, krows), np.int64)
    for var, j in enumerate((0, 1, tiles - 1)):
        r0 = j * rpt
        k0 = int(np.clip(j - 1, 0, tiles - 3)) * rpt
        for qi in range(rpt):
            qrow = r0 + qi
            rs = int(np.clip(qrow - kh // 2, 0, rows - kh))
            for ki in range(krows):
                krow = k0 + ki
                if rs <= krow < rs + kh:
                    arow[var, qi, ki] = krow - qrow + WIN_H_MAX - 1
    ec = np.zeros((GRID_W, GRID_W, rpb_w), np.float32)
    for qc in range(GRID_W):
        cs = int(np.clip(qc - WIN_W // 2, 0, GRID_W - WIN_W))
        for kc in range(cs, cs + WIN_W):
            ec[qc, kc, int(np.clip(kc - qc, -(WIN_W - 1), WIN_W - 1)) + WIN_W - 1] = 1.0
    return arow.tolist(), ec


def _bias_kernel(arow, u_ref, o_ref):
    var_id = pl.program_id(0)
    for var, table in enumerate(arow):
        @pl.when(var_id == var)
        def _(table=table):
            for hd in range(u_ref.shape[0]):
                for qi, row in enumerate(table):
                    for ki, a in enumerate(row):
                        blk = u_ref[hd, a] if a >= 0 else jnp.full((GRID_W, GRID_W), NEG_BIG, F32)
                        o_ref[0, hd, qi * GRID_W:(qi + 1) * GRID_W, ki * GRID_W:(ki + 1) * GRID_W] = blk


def _window_bias(rpb, rows):
    arow, ec = _window_tables(rows)
    nh = rpb.shape[0]
    col_blocks = jnp.einsum("hab,xyb->haxy", rpb, ec, precision=lax.Precision.HIGHEST)
    col_blocks = jnp.where(ec.sum(-1) > 0.5, col_blocks * LOG2E, NEG_BIG)
    return pl.pallas_call(
        functools.partial(_bias_kernel, arow),
        out_shape=jax.ShapeDtypeStruct((len(arow), nh, ATT_TILE, 3 * ATT_TILE), F32),
        grid=(len(arow),),
        in_specs=[pl.BlockSpec(col_blocks.shape, lambda v: (0, 0, 0, 0))],
        out_specs=pl.BlockSpec((1, nh, ATT_TILE, 3 * ATT_TILE), lambda v: (v, 0, 0, 0)),
        compiler_params=_cparams(1),
        name="window_bias",
    )(col_blocks)


def _fourier_kernel(t_len, u_ref, w1_ref, ct_ref, st_ref, o_ref, p_scr, q_scr):
    @pl.when(pl.program_id(1) == 0)
    def _():
        pq = jnp.dot(u_ref[...], w1_ref[...], preferred_element_type=F32)
        df = p_scr.shape[1]
        p_scr[...] = pq[:, :df].astype(BF16)
        q_scr[...] = pq[:, df:].astype(BF16)

    rt = ct_ref.shape[0]
    for b in range(u_ref.shape[0] // t_len):
        rs = slice(b * t_len, (b + 1) * t_len)
        o = (jnp.dot(ct_ref[...], p_scr[rs, :], preferred_element_type=F32)
             - jnp.dot(st_ref[...], q_scr[rs, :], preferred_element_type=F32))
        o_ref[b * rt:(b + 1) * rt, :] = o.astype(BF16)


def _fourier(u, w1, ct, st, batch, t_len, first_block, row_tile, seqs=1):
    df = u.shape[1]
    steps = t_len // row_tile
    assert seqs == 1 or (steps == 1 and batch % seqs == 0 and first_block % seqs == 0)
    return pl.pallas_call(
        functools.partial(_fourier_kernel, t_len),
        out_shape=jax.ShapeDtypeStruct((batch * t_len, df), BF16),
        grid=(batch // seqs, steps),
        in_specs=[pl.BlockSpec((seqs * t_len, df), lambda b, i: (first_block // seqs + b, 0)),
                  pl.BlockSpec(w1.shape, lambda b, i: (0, 0)),
                  pl.BlockSpec((row_tile, t_len), lambda b, i: (i, 0)),
                  pl.BlockSpec((row_tile, t_len), lambda b, i: (i, 0))],
        out_specs=pl.BlockSpec((seqs * row_tile, df), lambda b, i: (b * steps + i, 0)),
        scratch_shapes=[pltpu.VMEM((seqs * t_len, df), BF16), pltpu.VMEM((seqs * t_len, df), BF16)],
        compiler_params=_cparams(2),
        name=f"fourier_{t_len}",
    )(u, w1, ct, st)


def _dft_tables(t_len):
    c = FGROUP_DIM
    jk = np.outer(np.arange(c), np.arange(c)) % c
    ang = 2.0 * np.pi * jk / c
    eye = np.eye(N_FGROUPS)
    w1 = np.concatenate([np.kron(eye, np.cos(ang)), np.kron(eye, np.sin(ang))], axis=1) / np.sqrt(c)
    tt = np.outer(np.arange(t_len), np.arange(t_len)) % t_len
    angt = 2.0 * np.pi * tt / t_len
    return (w1.astype(np.float32), (np.cos(angt) / np.sqrt(t_len)).astype(np.float32),
            (np.sin(angt) / np.sqrt(t_len)).astype(np.float32))


def _merge_kernel(n_ctx_tiles, xp_ref, xs_ref, mod_ref, g1_ref, g2_ref, afc_ref, afl_ref, aac_ref, aal_ref,
                  wf_ref, wa_ref, wg_ref, bg_ref, wo_ref, wrt_ref, br_ref,
                  x1_ref, h2_ref, idx_ref, wts_ref, rank_ref, cnt_ref, carry):
    t = pl.program_id(0)
    is_ctx = t < n_ctx_tiles

    @pl.when(t == 0)
    def _():
        carry[...] = jnp.zeros_like(carry)

    sh1 = mod_ref[0, 0:1, :]
    sc1 = mod_ref[0, 1:2, :]
    gt1 = mod_ref[0, 2:3, :]
    sh2 = mod_ref[0, 3:4, :]
    sc2 = mod_ref[0, 4:5, :]
    d = xp_ref.shape[1]
    part_rows = xp_ref.shape[0] // MERGE_PARTS
    parts = [slice(p * part_rows, (p + 1) * part_rows) for p in range(MERGE_PARTS)]
    xs_ = [jnp.where(is_ctx, xp_ref[rs, :], xs_ref[rs, :]) for rs in parts]
    hbs = [(_rms(x) * g1_ref[...] * (1.0 + sc1) + sh1).astype(BF16) for x in xs_]
    gates = [jax.nn.sigmoid(jnp.dot(hb, wg_ref[...], preferred_element_type=F32) + bg_ref[...]) for hb in hbs]
    fas = [jnp.dot(jnp.where(is_ctx, afc_ref[rs, :], afl_ref[rs, :]), wf_ref[...], preferred_element_type=F32)
           for rs in parts]
    fbs = [jnp.dot(jnp.where(is_ctx, aac_ref[rs, :], aal_ref[rs, :]), wa_ref[...], preferred_element_type=F32)
           for rs in parts]
    mixes = [(g[:, :d] * fa + g[:, d:] * fb).astype(BF16) for g, fa, fb in zip(gates, fas, fbs)]
    x1s = [x + gt1 * jnp.dot(mix, wo_ref[...], preferred_element_type=F32) for x, mix in zip(xs_, mixes)]
    h2s = []
    for rs, x1 in zip(parts, x1s):
        x1_ref[rs, :] = x1
        h2 = _rms(x1) * g2_ref[...] * (1.0 + sc2) + sh2
        h2_ref[rs, :] = _pack_halves(h2)
        h2s.append(h2)
    h2 = jnp.concatenate(h2s, axis=0)
    h2_hi = h2.astype(BF16)
    h2_lo = (h2 - h2_hi.astype(F32)).astype(BF16)
    ne = br_ref.shape[0]
    nt_dims = (((1,), (1,)), ((), ()))
    both = lax.dot_general(wrt_ref[...], h2_hi, nt_dims, preferred_element_type=F32)
    cross = lax.dot_general(wrt_ref[:ne, :], h2_lo, nt_dims, preferred_element_type=F32)
    logits_t = both[:ne, :] + both[ne:, :] + cross + br_ref[...]
    idx_o, wts_o, rank_o, counts = _route(logits_t, carry[...])
    idx_ref[...] = idx_o
    wts_ref[...] = wts_o
    rank_ref[...] = rank_o
    carry[...] = counts
    cnt_ref[...] = counts


def _route(logits_t, counts):
    ne, tm = logits_t.shape
    row = lax.broadcasted_iota(jnp.int32, (ne, tm), 0)
    lg = logits_t
    vals, idxs = [], []
    for _ in range(TOP_K):
        m = jnp.max(lg, axis=0, keepdims=True)
        am = jnp.min(jnp.where(lg == m, row, ne), axis=0, keepdims=True)
        vals.append(m)
        idxs.append(am)
        lg = jnp.where(row == am, -jnp.inf, lg)
    es = [jnp.exp(v - vals[0]) for v in vals]
    den = functools.reduce(lambda a, b: a + b, es)
    onehot = functools.reduce(lambda a, b: a + b, [(row == am).astype(F32) for am in idxs])
    s_i = lax.broadcasted_iota(jnp.int32, (tm, tm), 0)
    t_i = lax.broadcasted_iota(jnp.int32, (tm, tm), 1)
    earlier = (s_i < t_i).astype(BF16)
    before = jnp.dot(onehot.astype(BF16), earlier, preferred_element_type=F32) + counts
    out_row = lax.broadcasted_iota(jnp.int32, (SUBLANES, tm), 0)
    idx_o = jnp.zeros((SUBLANES, tm), jnp.int32)
    wts_o = jnp.zeros((SUBLANES, tm), F32)
    rank_o = jnp.zeros((SUBLANES, tm), jnp.int32)
    for kk in range(TOP_K):
        rk = jnp.sum(jnp.where(row == idxs[kk], before, 0.0), axis=0, keepdims=True).astype(jnp.int32)
        idx_o = jnp.where(out_row == kk, idxs[kk], idx_o)
        wts_o = jnp.where(out_row == kk, es[kk] / den, wts_o)
        rank_o = jnp.where(out_row == kk, rk, rank_o)
    return idx_o, wts_o, rank_o, counts + jnp.sum(onehot, axis=1, keepdims=True)


def _merge(xp, xs, mod, g1, g2, af_c, af_l, aa_c, aa_l, wf, wa, wg, bg, wo, wr, br, tiles_per_lat_batch):
    n_ctx, d = xp.shape
    n = n_ctx + xs.shape[0]
    tm = TOK_TILE
    n_ctx_tiles = n_ctx // tm
    last = n_ctx_tiles - 1
    ne = wr.shape[1]
    da = aa_c.shape[1]
    df = af_c.shape[1]

    def mod_row(t):
        return jnp.where(t < n_ctx_tiles, 0, 1 + (t - n_ctx_tiles) // tiles_per_lat_batch)

    cmap = lambda t: (jnp.minimum(t, last), 0)
    lmap = lambda t: (jnp.maximum(t - n_ctx_tiles, 0), 0)
    full = lambda a: pl.BlockSpec(a.shape, lambda t: (0,) * a.ndim)
    tok = lambda w: pl.BlockSpec((tm, w), lambda t: (t, 0))
    per_choice = lambda dt: jax.ShapeDtypeStruct((SUBLANES, n), dt)
    choice_spec = pl.BlockSpec((SUBLANES, tm), lambda t: (0, t))
    wr_hi = wr.astype(BF16)
    wr_lo = (wr - wr_hi.astype(F32)).astype(BF16)
    wr_t = jnp.concatenate([wr_hi.T, wr_lo.T], axis=0)
    return pl.pallas_call(
        functools.partial(_merge_kernel, n_ctx_tiles),
        out_shape=(jax.ShapeDtypeStruct((n, d), F32), jax.ShapeDtypeStruct((n, d // 2), jnp.int32),
                   per_choice(jnp.int32), per_choice(F32), per_choice(jnp.int32),
                   jax.ShapeDtypeStruct((ne, 1), F32)),
        grid=(n // tm,),
        in_specs=[pl.BlockSpec((tm, d), cmap), pl.BlockSpec((tm, d), lmap),
                  pl.BlockSpec((1,) + mod.shape[1:], lambda t: (mod_row(t), 0, 0)),
                  full(g1), full(g2),
                  pl.BlockSpec((tm, df), cmap), pl.BlockSpec((tm, df), lmap),
                  pl.BlockSpec((tm, da), cmap), pl.BlockSpec((tm, da), lmap),
                  full(wf), full(wa), full(wg), full(bg), full(wo), full(wr_t), full(br)],
        out_specs=(tok(d), tok(d // 2), choice_spec, choice_spec, choice_spec,
                   pl.BlockSpec((ne, 1), lambda t: (0, 0))),
        scratch_shapes=[pltpu.VMEM((ne, 1), F32)],
        compiler_params=_cparams(1),
        name="merge",
    )(xp, xs, mod, g1, g2, af_c, af_l, aa_c, aa_l, wf, wa, wg, bg, wo, wr_t, br)


def _pos_kernel(off_ref, idx_ref, rank_ref, o_ref):
    idx = idx_ref[...]
    pos = rank_ref[...]
    for e in range(off_ref.shape[0]):
        pos = pos + jnp.where(idx == e, off_ref[e], 0)
    o_ref[...] = pos


def _positions(first_row, idx, rank):
    rows, n = idx.shape
    blk = POS_COLS
    spec = pl.BlockSpec((rows, blk), lambda j, off: (0, j))
    return pl.pallas_call(
        _pos_kernel,
        out_shape=jax.ShapeDtypeStruct((rows, n), jnp.int32),
        grid_spec=pltpu.PrefetchScalarGridSpec(num_scalar_prefetch=1, grid=(n // blk,),
                                               in_specs=[spec, spec], out_specs=spec),
        compiler_params=_cparams(1),
        name="positions",
    )(first_row, idx, rank)


def _moe_kernel(te_ref, nt_ref, nx_ref, rv_ref, sl_ref, x_ref, w1_hbm, b1_ref, w2_hbm, b2_ref, y_ref,
                w1s, w2s, sem):
    i = pl.program_id(0)
    e = te_ref[i]
    prev = te_ref[jnp.maximum(i - 1, 0)]
    first_of_run = (i == 0) | (e != prev)
    slot = sl_ref[i]

    def stage(expert, s):
        return (pltpu.make_async_copy(w1_hbm.at[expert], w1s.at[s], sem.at[0, s]),
                pltpu.make_async_copy(w2_hbm.at[expert], w2s.at[s], sem.at[1, s]))

    @pl.when(i == 0)
    def _():
        for cp in stage(e, slot):
            cp.start()

    @pl.when(first_of_run)
    def _():
        for cp in stage(e, slot):
            cp.wait()

        @pl.when(nx_ref[i] >= 0)
        def _():
            for cp in stage(nx_ref[i], 1 - slot):
                cp.start()

    def expert_rows(rs):
        dff = w2s.shape[1]
        x_lo, x_hi = _unpack_halves(x_ref[rs, :])
        x = jnp.concatenate([x_lo.astype(BF16), x_hi.astype(BF16)], axis=1)
        gu = jnp.dot(x, w1s[slot].astype(BF16), preferred_element_type=F32) + b1_ref[0]
        gate = jnp.minimum(gu[:, :dff], SWIGLU_LIMIT)
        up = jnp.clip(gu[:, dff:], -SWIGLU_LIMIT, SWIGLU_LIMIT)
        glu = gate * jax.nn.sigmoid(SWIGLU_ALPHA * gate)
        act = ((up + 1.0) * glu).astype(BF16)
        y = jnp.dot(act, w2s[slot].astype(BF16), preferred_element_type=F32) + b2_ref[0]
        y_ref[rs, :] = _pack_halves(y)

    tm = x_ref.shape[0]
    rv = rv_ref[i]

    @pl.when(rv == tm)
    def _():
        expert_rows(slice(0, tm))

    for sub in range(tm // MOE_SUB):
        @pl.when((rv < tm) & (rv > sub * MOE_SUB))
        def _(sub=sub):
            expert_rows(slice(sub * MOE_SUB, (sub + 1) * MOE_SUB))


def _moe(tile_expert, n_tiles, next_expert, rows_valid, slots, xs_sorted, w1, b1, w2, b2):
    p, dh = xs_sorted.shape
    ne, d, dff2 = w1.shape
    dff = w2.shape[1]
    tm = MOE_TILE
    max_tiles = p // tm

    def row_map(i, te, nt, nx, rv, sl):
        return (jnp.minimum(i, nt[0] - 1), 0)

    def b_map(i, te, nt, nx, rv, sl):
        return (te[i], 0, 0)

    grid_spec = pltpu.PrefetchScalarGridSpec(
        num_scalar_prefetch=5,
        grid=(max_tiles,),
        in_specs=[pl.BlockSpec((tm, dh), row_map),
                  pl.BlockSpec(memory_space=pl.ANY),
                  pl.BlockSpec((1, 1, dff2), b_map),
                  pl.BlockSpec(memory_space=pl.ANY),
                  pl.BlockSpec((1, 1, d), b_map)],
        out_specs=pl.BlockSpec((tm, dh), row_map),
        scratch_shapes=[pltpu.VMEM((2, d, dff2), F32), pltpu.VMEM((2, dff, d), F32),
                        pltpu.SemaphoreType.DMA((2, 2))],
    )
    return pl.pallas_call(
        _moe_kernel,
        out_shape=jax.ShapeDtypeStruct((p, dh), jnp.int32),
        grid_spec=grid_spec,
        compiler_params=_cparams(1),
        name="moe",
    )(tile_expert, n_tiles, next_expert, rows_valid, slots, xs_sorted,
      w1, b1.reshape(ne, 1, dff2), w2, b2.reshape(ne, 1, d))


def _sc_mesh():
    return plsc.VectorSubcoreMesh(core_axis_name="c", subcore_axis_name="s",
                                  num_cores=SC_CORES, num_subcores=SC_SUBCORES)


def _sc_dispatch(h, idx, p_rows):
    n, d = h.shape
    nw, items, chunk = idx.shape
    tok_w = n // nw
    n_chunks = items // TOP_K

    @functools.partial(
        pl.kernel, mesh=_sc_mesh(),
        out_type=jax.ShapeDtypeStruct((p_rows, d), h.dtype),
        scratch_types=[pltpu.VMEM((items, chunk), jnp.int32),
                       pltpu.VMEM((2, chunk, d), h.dtype),
                       pltpu.SemaphoreType.DMA((2,)), pltpu.SemaphoreType.DMA((2,))],
        name="sc_dispatch",
    )
    def run(h_hbm, idx_hbm, out_hbm, idx_v, rows_v, lsem, ssem):
        wid = lax.axis_index("s") * SC_CORES + lax.axis_index("c")
        base = wid * tok_w
        pltpu.sync_copy(idx_hbm.at[wid], idx_v)

        def load(j, slot):
            return pltpu.make_async_copy(h_hbm.at[pl.ds(base + j * chunk, chunk)], rows_v.at[slot], lsem.at[slot])

        def scat(j, kk, slot):
            return pltpu.make_async_copy(rows_v.at[slot], out_hbm.at[idx_v.at[j * TOP_K + kk]], ssem.at[slot])

        load(0, 0).start()

        @pl.loop(0, n_chunks, step=2)
        def _(j0):
            for b in range(2):
                j = j0 + b

                @pl.when(j >= 1)
                def _():
                    for kk in range(TOP_K):
                        scat(j - 1, kk, 1 - b).wait()

                @pl.when(j + 1 < n_chunks)
                def _():
                    load(j + 1, 1 - b).start()

                load(j, b).wait()
                for kk in range(TOP_K):
                    scat(j, kk, b).start()

        for kk in range(TOP_K):
            scat(n_chunks - 1, kk, (n_chunks - 1) % 2).wait()

    return run(h, idx)


def _sc_gather(y, idx, n):
    d = y.shape[1]
    nw, items, chunk = idx.shape
    tok_w = n // nw

    @functools.partial(
        pl.kernel, mesh=_sc_mesh(),
        out_type=jax.ShapeDtypeStruct((TOP_K, n, d), y.dtype),
        scratch_types=[pltpu.VMEM((items, chunk), jnp.int32),
                       pltpu.VMEM((2, chunk, d), y.dtype),
                       pltpu.SemaphoreType.DMA((2,)), pltpu.SemaphoreType.DMA((2,))],
        name="sc_gather",
    )
    def run(y_hbm, idx_hbm, out_hbm, idx_v, rows_v, gsem, wsem):
        wid = lax.axis_index("s") * SC_CORES + lax.axis_index("c")
        base = wid * tok_w
        pltpu.sync_copy(idx_hbm.at[wid], idx_v)

        def gather(i, slot):
            return pltpu.make_async_copy(y_hbm.at[idx_v.at[i]], rows_v.at[slot], gsem.at[slot])

        def write(i, slot):
            dst = out_hbm.at[i % TOP_K, pl.ds(base + (i // TOP_K) * chunk, chunk)]
            return pltpu.make_async_copy(rows_v.at[slot], dst, wsem.at[slot])

        gather(0, 0).start()

        @pl.loop(0, items, step=2)
        def _(i0):
            for b in range(2):
                i = i0 + b

                @pl.when(i >= 1)
                def _():
                    write(i - 1, 1 - b).wait()

                @pl.when(i + 1 < items)
                def _():
                    gather(i + 1, 1 - b).start()

                gather(i, b).wait()
                write(i, b).start()

        write(items - 1, (items - 1) % 2).wait()

    return run(y, idx)


def _combine_kernel(n_ctx_tiles, x1_ref, yg_ref, wts_ref, mod_ref, op_ref, os_ref):
    t = pl.program_id(0)
    gt2 = mod_ref[0, 5:6, :]
    w = wts_ref[...]
    acc_lo, acc_hi = None, None
    for kk in range(TOP_K):
        y_lo, y_hi = _unpack_halves(yg_ref[kk])
        wk = w[:, kk:kk + 1]
        acc_lo = wk * y_lo if acc_lo is None else acc_lo + wk * y_lo
        acc_hi = wk * y_hi if acc_hi is None else acc_hi + wk * y_hi
    out = x1_ref[...] + gt2 * jnp.concatenate([acc_lo, acc_hi], axis=1)

    @pl.when(t < n_ctx_tiles)
    def _():
        op_ref[...] = out

    @pl.when(t >= n_ctx_tiles)
    def _():
        os_ref[...] = out


def _combine(x1, yg, wts, mod, n_ctx, lat_seq):
    n, d = x1.shape
    tm = COMBINE_TILE
    assert n_ctx % tm == 0 and lat_seq % tm == 0
    tiles_per_lat_batch = lat_seq // tm
    n_ctx_tiles = n_ctx // tm
    last = n_ctx_tiles - 1

    def mod_row(t):
        return jnp.where(t < n_ctx_tiles, 0, 1 + (t - n_ctx_tiles) // tiles_per_lat_batch)

    return pl.pallas_call(
        functools.partial(_combine_kernel, n_ctx_tiles),
        out_shape=(jax.ShapeDtypeStruct((n_ctx, d), F32), jax.ShapeDtypeStruct((n - n_ctx, d), F32)),
        grid=(n // tm,),
        in_specs=[pl.BlockSpec((tm, d), lambda t: (t, 0)),
                  pl.BlockSpec((TOP_K, tm, d // 2), lambda t: (0, t, 0)),
                  pl.BlockSpec((tm, TOP_K), lambda t: (t, 0)),
                  pl.BlockSpec((1,) + mod.shape[1:], lambda t: (mod_row(t), 0, 0))],
        out_specs=(pl.BlockSpec((tm, d), lambda t: (jnp.minimum(t, last), 0)),
                   pl.BlockSpec((tm, d), lambda t: (jnp.maximum(t - n_ctx_tiles, 0), 0))),
        compiler_params=_cparams(1),
        name="combine",
    )(x1, yg, wts, mod)


def kernel(x_prompt, x_sample, cache_k, cache_v, c, c_ctx, g_norm1, w_ada, b_ada, w_in, g_q, g_k, rpb,
           w_fmap, w_amap, w_gate, b_gate, w_out, g_norm2, w_router, b_router, w1, b1, w2, b2):
    batch, seq, d = x_prompt.shape
    dec_batch, dec_seq, _ = x_sample.shape
    assert w_ada.shape[0] == 1, "single-layer trunk"
    rows = dec_seq // GRID_W
    da = N_HEADS * HEAD_DIM
    n_ctx = batch * seq
    n_lat = dec_batch * dec_seq
    n = n_ctx + n_lat
    ne = w_router.shape[2]
    assert TOK_TILE % seq == 0 and n_ctx % TOK_TILE == 0 and dec_seq % TOK_TILE == 0
    assert rows % (ATT_TILE // GRID_W) == 0 and rows >= 3 * (ATT_TILE // GRID_W) and rows >= WIN_H_MAX
    assert seq == ATT_TILE and cache_k.shape[3] == ATT_TILE
    tiles_per_lat_batch = dec_seq // TOK_TILE

    n_mod_rows = -(-(1 + dec_batch) // SUBLANES) * SUBLANES
    cvecs = jnp.zeros((n_mod_rows, d), F32).at[0].set(c_ctx).at[1:1 + dec_batch].set(c)
    mod = _ada(cvecs, w_ada[0], b_ada[0]).reshape(n_mod_rows, 6, d)

    xp = x_prompt.reshape(n_ctx, d)
    xs = x_sample.reshape(n_lat, d)
    g1 = g_norm1[0].reshape(1, d)
    g2 = g_norm2[0].reshape(1, d)
    hsum = jnp.asarray(np.kron(np.eye(N_HEADS), np.full((HEAD_DIM, HEAD_DIM), 1.0 / HEAD_DIM)), BF16)
    gq_t = jnp.tile(g_q[0], N_HEADS).reshape(1, da)
    gk_t = jnp.tile(g_k[0], N_HEADS).reshape(1, da)

    u, q, k, v, new_k, new_v = _proj(xp, xs, mod, g1, w_in[0].astype(BF16), hsum, gq_t, gk_t,
                                      batch, seq, tiles_per_lat_batch)

    aa_c = _attn_ctx(q, k, v, batch, seq)
    bias = _window_bias(rpb[0], rows)
    past = cache_k.shape[3]
    kc = cache_k[:, 0].transpose(0, 2, 1, 3).reshape(dec_batch, past, da).astype(BF16)
    vc = cache_v[:, 0].transpose(0, 2, 1, 3).reshape(dec_batch, past, da).astype(BF16)
    aa_l = _attn_lat(q, k, v, kc, vc, bias, n_ctx, dec_batch, rows)

    w1c, ct_c, st_c = _dft_tables(seq)
    _, ct_l, st_l = _dft_tables(dec_seq)
    w1c = jnp.asarray(w1c).astype(BF16)
    af_c = _fourier(u, w1c, jnp.asarray(ct_c).astype(BF16), jnp.asarray(st_c).astype(BF16),
                    batch, seq, 0, seq, CTX_SEQS_PER_STEP)
    af_l = _fourier(u, w1c, jnp.asarray(ct_l).astype(BF16), jnp.asarray(st_l).astype(BF16),
                    dec_batch, dec_seq, n_ctx // dec_seq, TOK_TILE)

    x1, h2, idx, wts, rank, counts = _merge(
        xp, xs, mod, g1, g2, af_c, af_l, aa_c, aa_l,
        w_fmap[0].astype(BF16), w_amap[0].astype(BF16), w_gate[0].astype(BF16), b_gate[0].reshape(1, -1),
        w_out[0].astype(BF16), w_router[0], b_router[0].reshape(ne, 1), tiles_per_lat_batch)

    tm = MOE_TILE
    max_tiles = (n * TOP_K) // tm + ne
    cnt = counts[:, 0].astype(jnp.int32)
    tiles_e = (cnt + tm - 1) // tm
    tile_end = jnp.cumsum(tiles_e)
    pad_off = (tile_end - tiles_e) * tm
    n_tiles = tile_end[-1:]
    tile_ids = jnp.arange(max_tiles, dtype=jnp.int32)
    tile_expert = jnp.sum((tile_ids[:, None] >= tile_end[None, :]).astype(jnp.int32), axis=1)
    last_e = jnp.sum((n_tiles - 1 >= tile_end).astype(jnp.int32))
    tile_expert = jnp.minimum(tile_expert, last_e).astype(jnp.int32)
    assert n % POS_COLS == 0
    pos = _positions(pad_off.astype(jnp.int32), idx, rank)[:TOP_K]

    p_rows = max_tiles * tm
    n_workers = SC_CORES * SC_SUBCORES
    assert n % (n_workers * SC_CHUNK * 2) == 0
    chunks = n // (n_workers * SC_CHUNK)
    pos_lists = pos.astype(jnp.int32).reshape(TOP_K, n_workers, chunks, SC_CHUNK).transpose(1, 2, 0, 3)
    pos_lists = pos_lists.reshape(n_workers, chunks * TOP_K, SC_CHUNK)
    xs_sorted = _sc_dispatch(h2, pos_lists, p_rows)
    e_ids = jnp.arange(ne, dtype=jnp.int32)
    later_used = (e_ids[None, :] > e_ids[:, None]) & (tiles_e[None, :] > 0)
    next_of_e = jnp.min(jnp.where(later_used, e_ids[None, :], ne), axis=1)
    next_of_e = jnp.where(next_of_e == ne, -1, next_of_e)
    is_e = tile_expert[:, None] == e_ids[None, :]
    next_expert = jnp.sum(jnp.where(is_e, next_of_e[None, :], 0), axis=1)
    rows_left = jnp.sum(jnp.where(is_e, (cnt + pad_off)[None, :], 0), axis=1) - tile_ids * tm
    rows_valid = jnp.where(tile_ids < n_tiles[0], jnp.clip(rows_left, 0, tm), 0)
    run_of_e = jnp.cumsum((tiles_e > 0).astype(jnp.int32)) - 1
    slots = jnp.sum(jnp.where(is_e, run_of_e[None, :], 0), axis=1) % 2
    y_sorted = _moe(tile_expert, n_tiles.astype(jnp.int32), next_expert.astype(jnp.int32),
                    rows_valid.astype(jnp.int32), slots.astype(jnp.int32), xs_sorted,
                    w1[0], b1[0], w2[0], b2[0])
    yg = _sc_gather(y_sorted, pos_lists, n)
    y_p, y_s = _combine(x1, yg, wts[:TOP_K].T, mod, n_ctx, dec_seq)
    return (y_p.reshape(batch, seq, d), y_s.reshape(dec_batch, dec_seq, d), new_k, new_v)
```

```python
import functools
import math

import numpy as np
import jax
import jax.numpy as jnp
from jax import lax
from jax.experimental import pallas as pl
from jax.experimental.pallas import tpu as pltpu
from jax.experimental.pallas import tpu_sc as plsc

F32 = jnp.float32
BF16 = jnp.bfloat16

GRID_W = 64
N_HEADS = 8
HEAD_DIM = 64
N_FGROUPS = 4
FGROUP_DIM = 128
WIN_H_MAX = 8
WIN_W = 16
TOP_K = 4
SWIGLU_LIMIT = 7.0
SWIGLU_ALPHA = 1.702
EPS = 1e-6
LOG2E = math.log2(math.e)
QK_PRESCALE = HEAD_DIM ** -0.5 * LOG2E

LANES = 128
SUBLANES = 8
ADA_COLS = 2048
POS_COLS = 2048
TOK_TILE = 512
COMBINE_TILE = 1024
ATT_TILE = 256
CTX_SEQS_PER_STEP = 4
MOE_TILE = 1024
MOE_SUB = 256
MERGE_PARTS = 2
VMEM_LIMIT = 56 * 1024 * 1024
NEG_BIG = -1e30

SC_CORES = 2
SC_SUBCORES = 16
SC_CHUNK = 64


def _cparams(n_axes, vmem=VMEM_LIMIT):
    return pltpu.CompilerParams(dimension_semantics=("arbitrary",) * n_axes, vmem_limit_bytes=vmem)


def _rms(x):
    return x * lax.rsqrt(jnp.mean(x * x, axis=-1, keepdims=True) + EPS)


def _pack_halves(x):
    c = x.shape[1] // 2
    lo = lax.bitcast_convert_type(x[:, :c].astype(BF16).astype(F32), jnp.uint32)
    hi = lax.bitcast_convert_type(x[:, c:].astype(BF16).astype(F32), jnp.uint32)
    return lax.bitcast_convert_type(hi | (lo >> 16), jnp.int32)


def _unpack_halves(w):
    u = lax.bitcast_convert_type(w, jnp.uint32)
    lo = lax.bitcast_convert_type(u << 16, F32)
    hi = lax.bitcast_convert_type(u & jnp.uint32(0xFFFF0000), F32)
    return lo, hi


def _ada_kernel(c_ref, w_ref, b_ref, o_ref):
    cv = c_ref[...]
    s = cv * jax.nn.sigmoid(cv)
    w = w_ref[...]
    s_hi = s.astype(BF16)
    s_lo = (s - s_hi.astype(F32)).astype(BF16)
    w_hi = w.astype(BF16)
    w_lo = (w - w_hi.astype(F32)).astype(BF16)
    o_ref[...] = (jnp.dot(s_hi, w_hi, preferred_element_type=F32)
                  + jnp.dot(s_hi, w_lo, preferred_element_type=F32)
                  + jnp.dot(s_lo, w_hi, preferred_element_type=F32) + b_ref[...])


def _ada(cvecs, w_ada, b_ada):
    rows, d = cvecs.shape
    n = w_ada.shape[1]
    blk = ADA_COLS
    return pl.pallas_call(
        _ada_kernel,
        out_shape=jax.ShapeDtypeStruct((rows, n), F32),
        grid=(n // blk,),
        in_specs=[pl.BlockSpec((rows, d), lambda j: (0, 0)),
                  pl.BlockSpec((d, blk), lambda j: (0, j)),
                  pl.BlockSpec((1, blk), lambda j: (0, j))],
        out_specs=pl.BlockSpec((rows, blk), lambda j: (0, j)),
        compiler_params=_cparams(1),
        name="ada",
    )(cvecs, w_ada, b_ada.reshape(1, n))


def _proj_kernel(n_ctx_tiles, xp_ref, xs_ref, mod_ref, g1_ref, win_ref, hsum_ref, gq_ref, gk_ref,
                 u_ref, q_ref, k_ref, v_ref, nk_ref, nv_ref):
    t = pl.program_id(0)
    is_ctx = t < n_ctx_tiles
    sh1 = mod_ref[0, 0:1, :]
    sc1 = mod_ref[0, 1:2, :]
    da = q_ref.shape[1]
    df = u_ref.shape[1]
    nb, _, nh, s, dh = nk_ref.shape
    parts = [slice(b * s, (b + 1) * s) for b in range(nb)]
    hs = [(_rms(jnp.where(is_ctx, xp_ref[rs, :], xs_ref[rs, :])) * g1_ref[...] * (1.0 + sc1) + sh1).astype(BF16)
          for rs in parts]
    projs = [jnp.dot(h, win_ref[...], preferred_element_type=F32) for h in hs]
    kns, vs = [], []
    for rs, proj in zip(parts, projs):
        u_ref[rs, :] = proj[:, :df].astype(BF16)
        q = proj[:, df:df + da]
        k = proj[:, df + da:df + 2 * da]
        v = proj[:, df + 2 * da:]
        msq = jnp.dot((q * q).astype(BF16), hsum_ref[...], preferred_element_type=F32)
        msk = jnp.dot((k * k).astype(BF16), hsum_ref[...], preferred_element_type=F32)
        qn = q * lax.rsqrt(msq + EPS) * gq_ref[...]
        kn = k * lax.rsqrt(msk + EPS) * gk_ref[...]
        q_ref[rs, :] = (qn * QK_PRESCALE).astype(BF16)
        k_ref[rs, :] = kn.astype(BF16)
        v_ref[rs, :] = v.astype(BF16)
        kns.append(kn)
        vs.append(v)

    @pl.when(is_ctx)
    def _():
        for b in range(nb):
            for hd in range(nh):
                nk_ref[b, 0, hd] = kns[b][:, hd * dh:(hd + 1) * dh]
                nv_ref[b, 0, hd] = vs[b][:, hd * dh:(hd + 1) * dh]


def _proj(xp, xs, mod, g1, w_in_b, hsum, gq_t, gk_t, batch, seq, tiles_per_lat_batch):
    n_ctx, d = xp.shape
    n_lat = xs.shape[0]
    tm = TOK_TILE
    n_ctx_tiles = n_ctx // tm
    n_tiles = (n_ctx + n_lat) // tm
    n = n_ctx + n_lat
    d_in = w_in_b.shape[1]
    da = N_HEADS * HEAD_DIM
    df = d_in - 3 * da
    bpt = tm // seq
    last = n_ctx_tiles - 1

    def mod_row(t):
        return jnp.where(t < n_ctx_tiles, 0, 1 + (t - n_ctx_tiles) // tiles_per_lat_batch)

    tok = lambda w: pl.BlockSpec((tm, w), lambda t: (t, 0))
    full = lambda a: pl.BlockSpec(a.shape, lambda t: (0,) * a.ndim)
    kv_spec = pl.BlockSpec((bpt, 1, N_HEADS, seq, HEAD_DIM), lambda t: (jnp.minimum(t, last), 0, 0, 0, 0))
    kv_shape = jax.ShapeDtypeStruct((batch, 1, N_HEADS, seq, HEAD_DIM), F32)
    return pl.pallas_call(
        functools.partial(_proj_kernel, n_ctx_tiles),
        out_shape=(jax.ShapeDtypeStruct((n, df), BF16),) + (jax.ShapeDtypeStruct((n, da), BF16),) * 3
        + (kv_shape, kv_shape),
        grid=(n_tiles,),
        in_specs=[pl.BlockSpec((tm, d), lambda t: (jnp.minimum(t, last), 0)),
                  pl.BlockSpec((tm, d), lambda t: (jnp.maximum(t - n_ctx_tiles, 0), 0)),
                  pl.BlockSpec((1,) + mod.shape[1:], lambda t: (mod_row(t), 0, 0)),
                  full(g1), full(w_in_b), full(hsum), full(gq_t), full(gk_t)],
        out_specs=(tok(df), tok(da), tok(da), tok(da), kv_spec, kv_spec),
        compiler_params=_cparams(1),
        name="proj",
    )(xp, xs, mod, g1, w_in_b, hsum, gq_t, gk_t)


def _softmax_pv(q2, key_blocks, val_blocks, bias_blocks):
    lane = lax.broadcasted_iota(jnp.int32, (1, LANES), 1)
    masks = [lane < HEAD_DIM, lane >= HEAD_DIM]

    def head_scores(half):
        qh = jnp.where(masks[half], q2, jnp.zeros_like(q2))
        scores = []
        for kb, bb in zip(key_blocks, bias_blocks):
            s = lax.dot_general(qh, kb, (((1,), (1,)), ((), ())), preferred_element_type=F32)
            if bb is not None:
                s = s + bb[half]
            scores.append(s)
        return scores

    def head_out(scores):
        m = jnp.max(functools.reduce(jnp.maximum, scores), axis=-1, keepdims=True)
        ps = [jnp.exp2(s - m) for s in scores]
        denom = jnp.sum(functools.reduce(lambda a, b: a + b, ps), axis=-1, keepdims=True)
        o = functools.reduce(lambda a, b: a + b,
                             [jnp.dot(p.astype(BF16), vb, preferred_element_type=F32)
                              for p, vb in zip(ps, val_blocks)])
        return o / denom

    if len(key_blocks) > 1:
        outs = [head_out(s) for s in [head_scores(0), head_scores(1)]]
    else:
        outs = [head_out(head_scores(half)) for half in range(2)]
    return jnp.where(masks[0], outs[0], outs[1])


def _attn_ctx_kernel(seq, q_ref, k_ref, v_ref, o_ref):
    for b in range(q_ref.shape[0] // seq):
        rs = slice(b * seq, (b + 1) * seq)
        for p in range(q_ref.shape[1] // LANES):
            sl = slice(p * LANES, (p + 1) * LANES)
            o = _softmax_pv(q_ref[rs, sl], [k_ref[rs, sl]], [v_ref[rs, sl]], [None])
            o_ref[rs, sl] = o.astype(BF16)


def _attn_ctx(q, k, v, batch, seq):
    da = q.shape[1]
    assert batch % CTX_SEQS_PER_STEP == 0
    spec = pl.BlockSpec((CTX_SEQS_PER_STEP * seq, da), lambda b: (b, 0))
    return pl.pallas_call(
        functools.partial(_attn_ctx_kernel, seq),
        out_shape=jax.ShapeDtypeStruct((batch * seq, da), BF16),
        grid=(batch // CTX_SEQS_PER_STEP,),
        in_specs=[spec, spec, spec],
        out_specs=spec,
        compiler_params=_cparams(1),
        name="attn_ctx",
    )(q, k, v)


def _attn_lat_kernel(live_blocks, q_ref, k0_ref, k1_ref, k2_ref, v0_ref, v1_ref, v2_ref, kc_ref, vc_ref,
                     bias_ref, o_ref):
    tk = k0_ref.shape[0]
    k_refs, v_refs = (k0_ref, k1_ref, k2_ref), (v0_ref, v1_ref, v2_ref)
    n_tiles = pl.num_programs(0)
    j = pl.program_id(0)
    variant = jnp.where(j == 0, 0, jnp.where(j == n_tiles - 1, 2, 1))

    def tile(blocks):
        for p in range(q_ref.shape[1] // LANES):
            sl = slice(p * LANES, (p + 1) * LANES)
            keys = [k_refs[d][:, sl] for d in blocks] + [kc_ref[0, :, sl]]
            vals = [v_refs[d][:, sl] for d in blocks] + [vc_ref[0, :, sl]]
            biases = [[bias_ref[0, 2 * p + half, :, d * tk:(d + 1) * tk] for half in range(2)] for d in blocks]
            o = _softmax_pv(q_ref[:, sl], keys, vals, biases + [None])
            o_ref[:, sl] = o.astype(BF16)

    for var, blocks in enumerate(live_blocks):
        @pl.when(variant == var)
        def _(blocks=blocks):
            tile(blocks)


def _attn_lat(q, k, v, kc, vc, bias, n_ctx, dec_batch, rows):
    da = q.shape[1]
    tq = ATT_TILE
    rows_per_tile = tq // GRID_W
    tiles = rows // rows_per_tile
    arow, _ = _window_tables(rows)
    live_blocks = tuple(
        tuple(d for d in range(3)
              if any(a >= 0 for row in table for a in row[d * rows_per_tile:(d + 1) * rows_per_tile]))
        for table in arow)
    base = n_ctx // tq
    n_lat = dec_batch * rows * GRID_W
    max_start = tiles - 3

    def qmap(j, b):
        return (base + b * tiles + j, 0)

    def kmap(d):
        return lambda j, b: (base + b * tiles + jnp.clip(j - 1, 0, max_start) + d, 0)

    def bmap(j, b):
        return (jnp.where(j == 0, 0, jnp.where(j == tiles - 1, 2, 1)), 0, 0, 0)

    blk = lambda m: pl.BlockSpec((tq, da), m)
    cspec = pl.BlockSpec((1,) + kc.shape[1:], lambda j, b: (b, 0, 0))
    return pl.pallas_call(
        functools.partial(_attn_lat_kernel, live_blocks),
        out_shape=jax.ShapeDtypeStruct((n_lat, da), BF16),
        grid=(tiles, dec_batch),
        in_specs=[blk(qmap)] + [blk(kmap(d)) for d in range(3)] + [blk(kmap(d)) for d in range(3)]
        + [cspec, cspec, pl.BlockSpec((1,) + bias.shape[1:], bmap)],
        out_specs=pl.BlockSpec((tq, da), lambda j, b: (b * tiles + j, 0)),
        compiler_params=_cparams(2),
        name="attn_lat",
    )(q, k, k, k, v, v, v, kc, vc, bias)


def _window_tables(rows):
    rpt = ATT_TILE // GRID_W
    krows = 3 * rpt
    tiles = rows // rpt
    kh = min(WIN_H_MAX, rows)
    rpb_w = 2 * WIN_W - 1
    arow = -np.ones((3, rpt, krows), np.int64)
    for var, j in enumerate((0, 1, tiles - 1)):
        r0 = j * rpt
        k0 = int(np.clip(j - 1, 0, tiles - 3)) * rpt
        for qi in range(rpt):
            qrow = r0 + qi
            rs = int(np.clip(qrow - kh // 2, 0, rows - kh))
            for ki in range(krows):
                krow = k0 + ki
                if rs <= krow < rs + kh:
                    arow[var, qi, ki] = krow - qrow + WIN_H_MAX - 1
    ec = np.zeros((GRID_W, GRID_W, rpb_w), np.float32)
    for qc in range(GRID_W):
        cs = int(np.clip(qc - WIN_W // 2, 0, GRID_W - WIN_W))
        for kc in range(cs, cs + WIN_W):
            ec[qc, kc, int(np.clip(kc - qc, -(WIN_W - 1), WIN_W - 1)) + WIN_W - 1] = 1.0
    return arow.tolist(), ec


def _bias_kernel(arow, u_ref, o_ref):
    var_id = pl.program_id(0)
    for var, table in enumerate(arow):
        @pl.when(var_id == var)
        def _(table=table):
            for hd in range(u_ref.shape[0]):
                for qi, row in enumerate(table):
                    for ki, a in enumerate(row):
                        blk = u_ref[hd, a] if a >= 0 else jnp.full((GRID_W, GRID_W), NEG_BIG, F32)
                        o_ref[0, hd, qi * GRID_W:(qi + 1) * GRID_W, ki * GRID_W:(ki + 1) * GRID_W] = blk


def _window_bias(rpb, rows):
    arow, ec = _window_tables(rows)
    nh = rpb.shape[0]
    col_blocks = jnp.einsum("hab,xyb->haxy", rpb, ec, precision=lax.Precision.HIGHEST)
    col_blocks = jnp.where(ec.sum(-1) > 0.5, col_blocks * LOG2E, NEG_BIG)
    return pl.pallas_call(
        functools.partial(_bias_kernel, arow),
        out_shape=jax.ShapeDtypeStruct((len(arow), nh, ATT_TILE, 3 * ATT_TILE), F32),
        grid=(len(arow),),
        in_specs=[pl.BlockSpec(col_blocks.shape, lambda v: (0, 0, 0, 0))],
        out_specs=pl.BlockSpec((1, nh, ATT_TILE, 3 * ATT_TILE), lambda v: (v, 0, 0, 0)),
        compiler_params=_cparams(1),
        name="window_bias",
    )(col_blocks)


def _fourier_kernel(t_len, u_ref, w1_ref, ct_ref, st_ref, o_ref, p_scr, q_scr):
    @pl.when(pl.program_id(1) == 0)
    def _():
        pq = jnp.dot(u_ref[...], w1_ref[...], preferred_element_type=F32)
        df = p_scr.shape[1]
        p_scr[...] = pq[:, :df].astype(BF16)
        q_scr[...] = pq[:, df:].astype(BF16)

    rt = ct_ref.shape[0]
    for b in range(u_ref.shape[0] // t_len):
        rs = slice(b * t_len, (b + 1) * t_len)
        o = (jnp.dot(ct_ref[...], p_scr[rs, :], preferred_element_type=F32)
             - jnp.dot(st_ref[...], q_scr[rs, :], preferred_element_type=F32))
        o_ref[b * rt:(b + 1) * rt, :] = o.astype(BF16)


def _fourier(u, w1, ct, st, batch, t_len, first_block, row_tile, seqs=1):
    df = u.shape[1]
    steps = t_len // row_tile
    assert seqs == 1 or (steps == 1 and batch % seqs == 0 and first_block % seqs == 0)
    return pl.pallas_call(
        functools.partial(_fourier_kernel, t_len),
        out_shape=jax.ShapeDtypeStruct((batch * t_len, df), BF16),
        grid=(batch // seqs, steps),
        in_specs=[pl.BlockSpec((seqs * t_len, df), lambda b, i: (first_block // seqs + b, 0)),
                  pl.BlockSpec(w1.shape, lambda b, i: (0, 0)),
                  pl.BlockSpec((row_tile, t_len), lambda b, i: (i, 0)),
                  pl.BlockSpec((row_tile, t_len), lambda b, i: (i, 0))],
        out_specs=pl.BlockSpec((seqs * row_tile, df), lambda b, i: (b * steps + i, 0)),
        scratch_shapes=[pltpu.VMEM((seqs * t_len, df), BF16), pltpu.VMEM((seqs * t_len, df), BF16)],
        compiler_params=_cparams(2),
        name=f"fourier_{t_len}",
    )(u, w1, ct, st)


def _dft_tables(t_len):
    c = FGROUP_DIM
    jk = np.outer(np.arange(c), np.arange(c)) % c
    ang = 2.0 * np.pi * jk / c
    eye = np.eye(N_FGROUPS)
    w1 = np.concatenate([np.kron(eye, np.cos(ang)), np.kron(eye, np.sin(ang))], axis=1) / np.sqrt(c)
    tt = np.outer(np.arange(t_len), np.arange(t_len)) % t_len
    angt = 2.0 * np.pi * tt / t_len
    as_bf16 = lambda a: jnp.asarray(a.astype(np.float32), dtype=BF16)
    return as_bf16(w1), as_bf16(np.cos(angt) / np.sqrt(t_len)), as_bf16(np.sin(angt) / np.sqrt(t_len))


def _merge_kernel(n_ctx_tiles, xp_ref, xs_ref, mod_ref, g1_ref, g2_ref, afc_ref, afl_ref, aac_ref, aal_ref,
                  wf_ref, wa_ref, wg_ref, bg_ref, wo_ref, wrt_ref, br_ref,
                  x1_ref, h2_ref, idx_ref, wts_ref, rank_ref, cnt_ref, carry):
    t = pl.program_id(0)
    is_ctx = t < n_ctx_tiles

    @pl.when(t == 0)
    def _():
        carry[...] = jnp.zeros_like(carry)

    sh1 = mod_ref[0, 0:1, :]
    sc1 = mod_ref[0, 1:2, :]
    gt1 = mod_ref[0, 2:3, :]
    sh2 = mod_ref[0, 3:4, :]
    sc2 = mod_ref[0, 4:5, :]
    d = xp_ref.shape[1]
    part_rows = xp_ref.shape[0] // MERGE_PARTS
    parts = [slice(p * part_rows, (p + 1) * part_rows) for p in range(MERGE_PARTS)]
    xs_ = [jnp.where(is_ctx, xp_ref[rs, :], xs_ref[rs, :]) for rs in parts]
    hbs = [(_rms(x) * g1_ref[...] * (1.0 + sc1) + sh1).astype(BF16) for x in xs_]
    gates = [jax.nn.sigmoid(jnp.dot(hb, wg_ref[...], preferred_element_type=F32) + bg_ref[...]) for hb in hbs]
    fas = [jnp.dot(jnp.where(is_ctx, afc_ref[rs, :], afl_ref[rs, :]), wf_ref[...], preferred_element_type=F32)
           for rs in parts]
    fbs = [jnp.dot(jnp.where(is_ctx, aac_ref[rs, :], aal_ref[rs, :]), wa_ref[...], preferred_element_type=F32)
           for rs in parts]
    mixes = [(g[:, :d] * fa + g[:, d:] * fb).astype(BF16) for g, fa, fb in zip(gates, fas, fbs)]
    x1s = [x + gt1 * jnp.dot(mix, wo_ref[...], preferred_element_type=F32) for x, mix in zip(xs_, mixes)]
    h2s = []
    for rs, x1 in zip(parts, x1s):
        x1_ref[rs, :] = x1
        h2 = _rms(x1) * g2_ref[...] * (1.0 + sc2) + sh2
        h2_ref[rs, :] = _pack_halves(h2)
        h2s.append(h2)
    h2 = jnp.concatenate(h2s, axis=0)
    h2_hi = h2.astype(BF16)
    h2_lo = (h2 - h2_hi.astype(F32)).astype(BF16)
    ne = br_ref.shape[0]
    nt_dims = (((1,), (1,)), ((), ()))
    both = lax.dot_general(wrt_ref[...], h2_hi, nt_dims, preferred_element_type=F32)
    cross = lax.dot_general(wrt_ref[:ne, :], h2_lo, nt_dims, preferred_element_type=F32)
    logits_t = both[:ne, :] + both[ne:, :] + cross + br_ref[...]
    idx_o, wts_o, rank_o, counts = _route(logits_t, carry[...])
    idx_ref[...] = idx_o
    wts_ref[...] = wts_o
    rank_ref[...] = rank_o
    carry[...] = counts
    cnt_ref[...] = counts


def _route(logits_t, counts):
    ne, tm = logits_t.shape
    row = lax.broadcasted_iota(jnp.int32, (ne, tm), 0)
    lg = logits_t
    vals, idxs = [], []
    for _ in range(TOP_K):
        m = jnp.max(lg, axis=0, keepdims=True)
        am = jnp.min(jnp.where(lg == m, row, ne), axis=0, keepdims=True)
        vals.append(m)
        idxs.append(am)
        lg = jnp.where(row == am, -jnp.inf, lg)
    es = [jnp.exp(v - vals[0]) for v in vals]
    den = functools.reduce(lambda a, b: a + b, es)
    onehot = functools.reduce(lambda a, b: a + b, [(row == am).astype(F32) for am in idxs])
    s_i = lax.broadcasted_iota(jnp.int32, (tm, tm), 0)
    t_i = lax.broadcasted_iota(jnp.int32, (tm, tm), 1)
    earlier = (s_i < t_i).astype(BF16)
    before = jnp.dot(onehot.astype(BF16), earlier, preferred_element_type=F32) + counts
    out_row = lax.broadcasted_iota(jnp.int32, (SUBLANES, tm), 0)
    idx_o = jnp.zeros((SUBLANES, tm), jnp.int32)
    wts_o = jnp.zeros((SUBLANES, tm), F32)
    rank_o = jnp.zeros((SUBLANES, tm), jnp.int32)
    for kk in range(TOP_K):
        rk = jnp.sum(jnp.where(row == idxs[kk], before, 0.0), axis=0, keepdims=True).astype(jnp.int32)
        idx_o = jnp.where(out_row == kk, idxs[kk], idx_o)
        wts_o = jnp.where(out_row == kk, es[kk] / den, wts_o)
        rank_o = jnp.where(out_row == kk, rk, rank_o)
    return idx_o, wts_o, rank_o, counts + jnp.sum(onehot, axis=1, keepdims=True)


def _merge(xp, xs, mod, g1, g2, af_c, af_l, aa_c, aa_l, wf, wa, wg, bg, wo, wr, br, tiles_per_lat_batch):
    n_ctx, d = xp.shape
    n = n_ctx + xs.shape[0]
    tm = TOK_TILE
    n_ctx_tiles = n_ctx // tm
    last = n_ctx_tiles - 1
    ne = wr.shape[1]
    da = aa_c.shape[1]
    df = af_c.shape[1]

    def mod_row(t):
        return jnp.where(t < n_ctx_tiles, 0, 1 + (t - n_ctx_tiles) // tiles_per_lat_batch)

    cmap = lambda t: (jnp.minimum(t, last), 0)
    lmap = lambda t: (jnp.maximum(t - n_ctx_tiles, 0), 0)
    full = lambda a: pl.BlockSpec(a.shape, lambda t: (0,) * a.ndim)
    tok = lambda w: pl.BlockSpec((tm, w), lambda t: (t, 0))
    per_choice = lambda dt: jax.ShapeDtypeStruct((SUBLANES, n), dt)
    choice_spec = pl.BlockSpec((SUBLANES, tm), lambda t: (0, t))
    wr_hi = wr.astype(BF16)
    wr_lo = (wr - wr_hi.astype(F32)).astype(BF16)
    wr_t = jnp.concatenate([wr_hi.T, wr_lo.T], axis=0)
    return pl.pallas_call(
        functools.partial(_merge_kernel, n_ctx_tiles),
        out_shape=(jax.ShapeDtypeStruct((n, d), F32), jax.ShapeDtypeStruct((n, d // 2), jnp.int32),
                   per_choice(jnp.int32), per_choice(F32), per_choice(jnp.int32),
                   jax.ShapeDtypeStruct((ne, 1), F32)),
        grid=(n // tm,),
        in_specs=[pl.BlockSpec((tm, d), cmap), pl.BlockSpec((tm, d), lmap),
                  pl.BlockSpec((1,) + mod.shape[1:], lambda t: (mod_row(t), 0, 0)),
                  full(g1), full(g2),
                  pl.BlockSpec((tm, df), cmap), pl.BlockSpec((tm, df), lmap),
                  pl.BlockSpec((tm, da), cmap), pl.BlockSpec((tm, da), lmap),
                  full(wf), full(wa), full(wg), full(bg), full(wo), full(wr_t), full(br)],
        out_specs=(tok(d), tok(d // 2), choice_spec, choice_spec, choice_spec,
                   pl.BlockSpec((ne, 1), lambda t: (0, 0))),
        scratch_shapes=[pltpu.VMEM((ne, 1), F32)],
        compiler_params=_cparams(1),
        name="merge",
    )(xp, xs, mod, g1, g2, af_c, af_l, aa_c, aa_l, wf, wa, wg, bg, wo, wr_t, br)


def _pos_kernel(off_ref, idx_ref, rank_ref, o_ref):
    idx = idx_ref[...]
    pos = rank_ref[...]
    for e in range(off_ref.shape[0]):
        pos = pos + jnp.where(idx == e, off_ref[e], 0)
    o_ref[...] = pos


def _positions(first_row, idx, rank):
    rows, n = idx.shape
    blk = POS_COLS
    spec = pl.BlockSpec((rows, blk), lambda j, off: (0, j))
    return pl.pallas_call(
        _pos_kernel,
        out_shape=jax.ShapeDtypeStruct((rows, n), jnp.int32),
        grid_spec=pltpu.PrefetchScalarGridSpec(num_scalar_prefetch=1, grid=(n // blk,),
                                               in_specs=[spec, spec], out_specs=spec),
        compiler_params=_cparams(1),
        name="positions",
    )(first_row, idx, rank)


def _moe_kernel(te_ref, nt_ref, nx_ref, rv_ref, sl_ref, x_ref, w1_hbm, b1_ref, w2_hbm, b2_ref, y_ref,
                w1s, w2s, sem):
    i = pl.program_id(0)
    e = te_ref[i]
    prev = te_ref[jnp.maximum(i - 1, 0)]
    first_of_run = (i == 0) | (e != prev)
    slot = sl_ref[i]

    def stage(expert, s):
        return (pltpu.make_async_copy(w1_hbm.at[expert], w1s.at[s], sem.at[0, s]),
                pltpu.make_async_copy(w2_hbm.at[expert], w2s.at[s], sem.at[1, s]))

    @pl.when(i == 0)
    def _():
        for cp in stage(e, slot):
            cp.start()

    @pl.when(first_of_run)
    def _():
        for cp in stage(e, slot):
            cp.wait()

        @pl.when(nx_ref[i] >= 0)
        def _():
            for cp in stage(nx_ref[i], 1 - slot):
                cp.start()

    def expert_rows(rs):
        dff = w2s.shape[1]
        x_lo, x_hi = _unpack_halves(x_ref[rs, :])
        x = jnp.concatenate([x_lo.astype(BF16), x_hi.astype(BF16)], axis=1)
        gu = jnp.dot(x, w1s[slot].astype(BF16), preferred_element_type=F32) + b1_ref[0]
        gate = jnp.minimum(gu[:, :dff], SWIGLU_LIMIT)
        up = jnp.clip(gu[:, dff:], -SWIGLU_LIMIT, SWIGLU_LIMIT)
        glu = gate * jax.nn.sigmoid(SWIGLU_ALPHA * gate)
        act = ((up + 1.0) * glu).astype(BF16)
        y = jnp.dot(act, w2s[slot].astype(BF16), preferred_element_type=F32) + b2_ref[0]
        y_ref[rs, :] = _pack_halves(y)

    tm = x_ref.shape[0]
    rv = rv_ref[i]

    @pl.when(rv == tm)
    def _():
        expert_rows(slice(0, tm))

    for sub in range(tm // MOE_SUB):
        @pl.when((rv < tm) & (rv > sub * MOE_SUB))
        def _(sub=sub):
            expert_rows(slice(sub * MOE_SUB, (sub + 1) * MOE_SUB))


def _moe(tile_expert, n_tiles, next_expert, rows_valid, slots, xs_sorted, w1, b1, w2, b2):
    p, dh = xs_sorted.shape
    ne, d, dff2 = w1.shape
    dff = w2.shape[1]
    tm = MOE_TILE
    max_tiles = p // tm

    def row_map(i, te, nt, nx, rv, sl):
        return (jnp.minimum(i, nt[0] - 1), 0)

    def b_map(i, te, nt, nx, rv, sl):
        return (te[i], 0, 0)

    grid_spec = pltpu.PrefetchScalarGridSpec(
        num_scalar_prefetch=5,
        grid=(max_tiles,),
        in_specs=[pl.BlockSpec((tm, dh), row_map),
                  pl.BlockSpec(memory_space=pl.ANY),
                  pl.BlockSpec((1, 1, dff2), b_map),
                  pl.BlockSpec(memory_space=pl.ANY),
                  pl.BlockSpec((1, 1, d), b_map)],
        out_specs=pl.BlockSpec((tm, dh), row_map),
        scratch_shapes=[pltpu.VMEM((2, d, dff2), F32), pltpu.VMEM((2, dff, d), F32),
                        pltpu.SemaphoreType.DMA((2, 2))],
    )
    return pl.pallas_call(
        _moe_kernel,
        out_shape=jax.ShapeDtypeStruct((p, dh), jnp.int32),
        grid_spec=grid_spec,
        compiler_params=_cparams(1),
        name="moe",
    )(tile_expert, n_tiles, next_expert, rows_valid, slots, xs_sorted,
      w1, b1.reshape(ne, 1, dff2), w2, b2.reshape(ne, 1, d))


def _sc_mesh():
    return plsc.VectorSubcoreMesh(core_axis_name="c", subcore_axis_name="s",
                                  num_cores=SC_CORES, num_subcores=SC_SUBCORES)


def _sc_dispatch(h, idx, p_rows):
    n, d = h.shape
    nw, items, chunk = idx.shape
    tok_w = n // nw
    n_chunks = items // TOP_K

    @functools.partial(
        pl.kernel, mesh=_sc_mesh(),
        out_type=jax.ShapeDtypeStruct((p_rows, d), h.dtype),
        scratch_types=[pltpu.VMEM((items, chunk), jnp.int32),
                       pltpu.VMEM((2, chunk, d), h.dtype),
                       pltpu.SemaphoreType.DMA((2,)), pltpu.SemaphoreType.DMA((2,))],
        name="sc_dispatch",
    )
    def run(h_hbm, idx_hbm, out_hbm, idx_v, rows_v, lsem, ssem):
        wid = lax.axis_index("s") * SC_CORES + lax.axis_index("c")
        base = wid * tok_w
        pltpu.sync_copy(idx_hbm.at[wid], idx_v)

        def load(j, slot):
            return pltpu.make_async_copy(h_hbm.at[pl.ds(base + j * chunk, chunk)], rows_v.at[slot], lsem.at[slot])

        def scat(j, kk, slot):
            return pltpu.make_async_copy(rows_v.at[slot], out_hbm.at[idx_v.at[j * TOP_K + kk]], ssem.at[slot])

        load(0, 0).start()

        @pl.loop(0, n_chunks, step=2)
        def _(j0):
            for b in range(2):
                j = j0 + b

                @pl.when(j >= 1)
                def _():
                    for kk in range(TOP_K):
                        scat(j - 1, kk, 1 - b).wait()

                @pl.when(j + 1 < n_chunks)
                def _():
                    load(j + 1, 1 - b).start()

                load(j, b).wait()
                for kk in range(TOP_K):
                    scat(j, kk, b).start()

        for kk in range(TOP_K):
            scat(n_chunks - 1, kk, (n_chunks - 1) % 2).wait()

    return run(h, idx)


def _sc_gather(y, idx, n):
    d = y.shape[1]
    nw, items, chunk = idx.shape
    tok_w = n // nw

    @functools.partial(
        pl.kernel, mesh=_sc_mesh(),
        out_type=jax.ShapeDtypeStruct((TOP_K, n, d), y.dtype),
        scratch_types=[pltpu.VMEM((items, chunk), jnp.int32),
                       pltpu.VMEM((2, chunk, d), y.dtype),
                       pltpu.SemaphoreType.DMA((2,)), pltpu.SemaphoreType.DMA((2,))],
        name="sc_gather",
    )
    def run(y_hbm, idx_hbm, out_hbm, idx_v, rows_v, gsem, wsem):
        wid = lax.axis_index("s") * SC_CORES + lax.axis_index("c")
        base = wid * tok_w
        pltpu.sync_copy(idx_hbm.at[wid], idx_v)

        def gather(i, slot):
            return pltpu.make_async_copy(y_hbm.at[idx_v.at[i]], rows_v.at[slot], gsem.at[slot])

        def write(i, slot):
            dst = out_hbm.at[i % TOP_K, pl.ds(base + (i // TOP_K) * chunk, chunk)]
            return pltpu.make_async_copy(rows_v.at[slot], dst, wsem.at[slot])

        gather(0, 0).start()

        @pl.loop(0, items, step=2)
        def _(i0):
            for b in range(2):
                i = i0 + b

                @pl.when(i >= 1)
                def _():
                    write(i - 1, 1 - b).wait()

                @pl.when(i + 1 < items)
                def _():
                    gather(i + 1, 1 - b).start()

                gather(i, b).wait()
                write(i, b).start()

        write(items - 1, (items - 1) % 2).wait()

    return run(y, idx)


def _combine_kernel(n_ctx_tiles, x1_ref, yg_ref, wts_ref, mod_ref, op_ref, os_ref):
    t = pl.program_id(0)
    gt2 = mod_ref[0, 5:6, :]
    w = wts_ref[...]
    acc_lo, acc_hi = None, None
    for kk in range(TOP_K):
        y_lo, y_hi = _unpack_halves(yg_ref[kk])
        wk = w[:, kk:kk + 1]
        acc_lo = wk * y_lo if acc_lo is None else acc_lo + wk * y_lo
        acc_hi = wk * y_hi if acc_hi is None else acc_hi + wk * y_hi
    out = x1_ref[...] + gt2 * jnp.concatenate([acc_lo, acc_hi], axis=1)

    @pl.when(t < n_ctx_tiles)
    def _():
        op_ref[...] = out

    @pl.when(t >= n_ctx_tiles)
    def _():
        os_ref[...] = out


def _combine(x1, yg, wts, mod, n_ctx, lat_seq):
    n, d = x1.shape
    tm = COMBINE_TILE
    assert n_ctx % tm == 0 and lat_seq % tm == 0
    tiles_per_lat_batch = lat_seq // tm
    n_ctx_tiles = n_ctx // tm
    last = n_ctx_tiles - 1

    def mod_row(t):
        return jnp.where(t < n_ctx_tiles, 0, 1 + (t - n_ctx_tiles) // tiles_per_lat_batch)

    return pl.pallas_call(
        functools.partial(_combine_kernel, n_ctx_tiles),
        out_shape=(jax.ShapeDtypeStruct((n_ctx, d), F32), jax.ShapeDtypeStruct((n - n_ctx, d), F32)),
        grid=(n // tm,),
        in_specs=[pl.BlockSpec((tm, d), lambda t: (t, 0)),
                  pl.BlockSpec((TOP_K, tm, d // 2), lambda t: (0, t, 0)),
                  pl.BlockSpec((tm, TOP_K), lambda t: (t, 0)),
                  pl.BlockSpec((1,) + mod.shape[1:], lambda t: (mod_row(t), 0, 0))],
        out_specs=(pl.BlockSpec((tm, d), lambda t: (jnp.minimum(t, last), 0)),
                   pl.BlockSpec((tm, d), lambda t: (jnp.maximum(t - n_ctx_tiles, 0), 0))),
        compiler_params=_cparams(1),
        name="combine",
    )(x1, yg, wts, mod)


def kernel(x_prompt, x_sample, cache_k, cache_v, c, c_ctx, g_norm1, w_ada, b_ada, w_in, g_q, g_k, rpb,
           w_fmap, w_amap, w_gate, b_gate, w_out, g_norm2, w_router, b_router, w1, b1, w2, b2):
    batch, seq, d = x_prompt.shape
    dec_batch, dec_seq, _ = x_sample.shape
    assert w_ada.shape[0] == 1, "single-layer trunk"
    rows = dec_seq // GRID_W
    da = N_HEADS * HEAD_DIM
    n_ctx = batch * seq
    n_lat = dec_batch * dec_seq
    n = n_ctx + n_lat
    ne = w_router.shape[2]
    assert TOK_TILE % seq == 0 and n_ctx % TOK_TILE == 0 and dec_seq % TOK_TILE == 0
    assert rows % (ATT_TILE // GRID_W) == 0 and rows >= 3 * (ATT_TILE // GRID_W) and rows >= WIN_H_MAX
    assert seq == ATT_TILE and cache_k.shape[3] == ATT_TILE
    tiles_per_lat_batch = dec_seq // TOK_TILE

    n_mod_rows = -(-(1 + dec_batch) // SUBLANES) * SUBLANES
    cvecs = jnp.zeros((n_mod_rows, d), F32).at[0].set(c_ctx).at[1:1 + dec_batch].set(c)
    mod = _ada(cvecs, w_ada[0], b_ada[0]).reshape(n_mod_rows, 6, d)

    xp = x_prompt.reshape(n_ctx, d)
    xs = x_sample.reshape(n_lat, d)
    g1 = g_norm1[0].reshape(1, d)
    g2 = g_norm2[0].reshape(1, d)
    hsum = jnp.asarray(np.kron(np.eye(N_HEADS), np.full((HEAD_DIM, HEAD_DIM), 1.0 / HEAD_DIM)), BF16)
    gq_t = jnp.tile(g_q[0], N_HEADS).reshape(1, da)
    gk_t = jnp.tile(g_k[0], N_HEADS).reshape(1, da)

    u, q, k, v, new_k, new_v = _proj(xp, xs, mod, g1, w_in[0].astype(BF16), hsum, gq_t, gk_t,
                                      batch, seq, tiles_per_lat_batch)

    aa_c = _attn_ctx(q, k, v, batch, seq)
    bias = _window_bias(rpb[0], rows)
    past = cache_k.shape[3]
    kc = cache_k[:, 0].transpose(0, 2, 1, 3).reshape(dec_batch, past, da).astype(BF16)
    vc = cache_v[:, 0].transpose(0, 2, 1, 3).reshape(dec_batch, past, da).astype(BF16)
    aa_l = _attn_lat(q, k, v, kc, vc, bias, n_ctx, dec_batch, rows)

    w1c, ct_c, st_c = _dft_tables(seq)
    _, ct_l, st_l = _dft_tables(dec_seq)
    af_c = _fourier(u, w1c, ct_c, st_c, batch, seq, 0, seq, CTX_SEQS_PER_STEP)
    af_l = _fourier(u, w1c, ct_l, st_l, dec_batch, dec_seq, n_ctx // dec_seq, TOK_TILE)

    x1, h2, idx, wts, rank, counts = _merge(
        xp, xs, mod, g1, g2, af_c, af_l, aa_c, aa_l,
        w_fmap[0].astype(BF16), w_amap[0].astype(BF16), w_gate[0].astype(BF16), b_gate[0].reshape(1, -1),
        w_out[0].astype(BF16), w_router[0], b_router[0].reshape(ne, 1), tiles_per_lat_batch)

    tm = MOE_TILE
    max_tiles = (n * TOP_K) // tm + ne
    cnt = counts[:, 0].astype(jnp.int32)
    tiles_e = (cnt + tm - 1) // tm
    tile_end = jnp.cumsum(tiles_e)
    pad_off = (tile_end - tiles_e) * tm
    n_tiles = tile_end[-1:]
    tile_ids = jnp.arange(max_tiles, dtype=jnp.int32)
    tile_expert = jnp.sum((tile_ids[:, None] >= tile_end[None, :]).astype(jnp.int32), axis=1)
    last_e = jnp.sum((n_tiles - 1 >= tile_end).astype(jnp.int32))
    tile_expert = jnp.minimum(tile_expert, last_e).astype(jnp.int32)
    assert n % POS_COLS == 0
    pos = _positions(pad_off.astype(jnp.int32), idx, rank)[:TOP_K]

    p_rows = max_tiles * tm
    n_workers = SC_CORES * SC_SUBCORES
    assert n % (n_workers * SC_CHUNK * 2) == 0
    chunks = n // (n_workers * SC_CHUNK)
    pos_lists = pos.astype(jnp.int32).reshape(TOP_K, n_workers, chunks, SC_CHUNK).transpose(1, 2, 0, 3)
    pos_lists = pos_lists.reshape(n_workers, chunks * TOP_K, SC_CHUNK)
    xs_sorted = _sc_dispatch(h2, pos_lists, p_rows)
    e_ids = jnp.arange(ne, dtype=jnp.int32)
    later_used = (e_ids[None, :] > e_ids[:, None]) & (tiles_e[None, :] > 0)
    next_of_e = jnp.min(jnp.where(later_used, e_ids[None, :], ne), axis=1)
    next_of_e = jnp.where(next_of_e == ne, -1, next_of_e)
    is_e = tile_expert[:, None] == e_ids[None, :]
    next_expert = jnp.sum(jnp.where(is_e, next_of_e[None, :], 0), axis=1)
    rows_left = jnp.sum(jnp.where(is_e, (cnt + pad_off)[None, :], 0), axis=1) - tile_ids * tm
    rows_valid = jnp.where(tile_ids < n_tiles[0], jnp.clip(rows_left, 0, tm), 0)
    run_of_e = jnp.cumsum((tiles_e > 0).astype(jnp.int32)) - 1
    slots = jnp.sum(jnp.where(is_e, run_of_e[None, :], 0), axis=1) % 2
    y_sorted = _moe(tile_expert, n_tiles.astype(jnp.int32), next_expert.astype(jnp.int32),
                    rows_valid.astype(jnp.int32), slots.astype(jnp.int32), xs_sorted,
                    w1[0], b1[0], w2[0], b2[0])
    yg = _sc_gather(y_sorted, pos_lists, n)
    y_p, y_s = _combine(x1, yg, wts[:TOP_K].T, mod, n_ctx, dec_seq)
    return (y_p.reshape(batch, seq, d), y_s.reshape(dec_batch, dec_seq, d), new_k, new_v)
```

```python
import functools
import math

import numpy as np
import jax
import jax.numpy as jnp
from jax import lax
from jax.experimental import pallas as pl
from jax.experimental.pallas import tpu as pltpu
from jax.experimental.pallas import tpu_sc as plsc

F32 = jnp.float32
BF16 = jnp.bfloat16

GRID_W = 64
N_HEADS = 8
HEAD_DIM = 64
N_FGROUPS = 4
FGROUP_DIM = 128
WIN_H_MAX = 8
WIN_W = 16
TOP_K = 4
SWIGLU_LIMIT = 7.0
SWIGLU_ALPHA = 1.702
EPS = 1e-6
LOG2E = math.log2(math.e)
QK_PRESCALE = HEAD_DIM ** -0.5 * LOG2E

LANES = 128
SUBLANES = 8
ADA_COLS = 2048
POS_COLS = 2048
TOK_TILE = 512
COMBINE_TILE = 1024
FOURIER_TILE = 1024
ATT_TILE = 256
CTX_SEQS_PER_STEP = 4
MOE_TILE = 1024
MOE_SUB = 256
MERGE_PARTS = 2
VMEM_LIMIT = 56 * 1024 * 1024
NEG_BIG = -1e30

SC_CORES = 2
SC_SUBCORES = 16
SC_CHUNK = 64


def _cparams(n_axes, vmem=VMEM_LIMIT):
    return pltpu.CompilerParams(dimension_semantics=("arbitrary",) * n_axes, vmem_limit_bytes=vmem)


def _rms(x):
    return x * lax.rsqrt(jnp.mean(x * x, axis=-1, keepdims=True) + EPS)


def _pack_halves(x):
    c = x.shape[1] // 2
    lo = lax.bitcast_convert_type(x[:, :c].astype(BF16).astype(F32), jnp.uint32)
    hi = lax.bitcast_convert_type(x[:, c:].astype(BF16).astype(F32), jnp.uint32)
    return lax.bitcast_convert_type(hi | (lo >> 16), jnp.int32)


def _unpack_halves(w):
    u = lax.bitcast_convert_type(w, jnp.uint32)
    lo = lax.bitcast_convert_type(u << 16, F32)
    hi = lax.bitcast_convert_type(u & jnp.uint32(0xFFFF0000), F32)
    return lo, hi


def _ada_kernel(c_ref, w_ref, b_ref, o_ref):
    cv = c_ref[...]
    s = cv * jax.nn.sigmoid(cv)
    w = w_ref[...]
    s_hi = s.astype(BF16)
    s_lo = (s - s_hi.astype(F32)).astype(BF16)
    w_hi = w.astype(BF16)
    w_lo = (w - w_hi.astype(F32)).astype(BF16)
    o_ref[...] = (jnp.dot(s_hi, w_hi, preferred_element_type=F32)
                  + jnp.dot(s_hi, w_lo, preferred_element_type=F32)
                  + jnp.dot(s_lo, w_hi, preferred_element_type=F32) + b_ref[...])


def _ada(cvecs, w_ada, b_ada):
    rows, d = cvecs.shape
    n = w_ada.shape[1]
    blk = ADA_COLS
    return pl.pallas_call(
        _ada_kernel,
        out_shape=jax.ShapeDtypeStruct((rows, n), F32),
        grid=(n // blk,),
        in_specs=[pl.BlockSpec((rows, d), lambda j: (0, 0)),
                  pl.BlockSpec((d, blk), lambda j: (0, j)),
                  pl.BlockSpec((1, blk), lambda j: (0, j))],
        out_specs=pl.BlockSpec((rows, blk), lambda j: (0, j)),
        compiler_params=_cparams(1),
        name="ada",
    )(cvecs, w_ada, b_ada.reshape(1, n))


def _proj_kernel(n_ctx_tiles, xp_ref, xs_ref, mod_ref, g1_ref, win_ref, hsum_ref, gq_ref, gk_ref,
                 u_ref, q_ref, k_ref, v_ref, nk_ref, nv_ref):
    t = pl.program_id(0)
    is_ctx = t < n_ctx_tiles
    sh1 = mod_ref[0, 0:1, :]
    sc1 = mod_ref[0, 1:2, :]
    da = q_ref.shape[1]
    df = u_ref.shape[1]
    nb, _, nh, s, dh = nk_ref.shape
    parts = [slice(b * s, (b + 1) * s) for b in range(nb)]
    hs = [(_rms(jnp.where(is_ctx, xp_ref[rs, :], xs_ref[rs, :])) * g1_ref[...] * (1.0 + sc1) + sh1).astype(BF16)
          for rs in parts]
    projs = [jnp.dot(h, win_ref[...], preferred_element_type=F32) for h in hs]
    kns, vs = [], []
    for rs, proj in zip(parts, projs):
        u_ref[rs, :] = proj[:, :df].astype(BF16)
        q = proj[:, df:df + da]
        k = proj[:, df + da:df + 2 * da]
        v = proj[:, df + 2 * da:]
        msq = jnp.dot((q * q).astype(BF16), hsum_ref[...], preferred_element_type=F32)
        msk = jnp.dot((k * k).astype(BF16), hsum_ref[...], preferred_element_type=F32)
        qn = q * lax.rsqrt(msq + EPS) * gq_ref[...]
        kn = k * lax.rsqrt(msk + EPS) * gk_ref[...]
        q_ref[rs, :] = (qn * QK_PRESCALE).astype(BF16)
        k_ref[rs, :] = kn.astype(BF16)
        v_ref[rs, :] = v.astype(BF16)
        kns.append(kn)
        vs.append(v)

    @pl.when(is_ctx)
    def _():
        for b in range(nb):
            for hd in range(nh):
                nk_ref[b, 0, hd] = kns[b][:, hd * dh:(hd + 1) * dh]
                nv_ref[b, 0, hd] = vs[b][:, hd * dh:(hd + 1) * dh]


def _proj(xp, xs, mod, g1, w_in_b, hsum, gq_t, gk_t, batch, seq, tiles_per_lat_batch):
    n_ctx, d = xp.shape
    n_lat = xs.shape[0]
    tm = TOK_TILE
    n_ctx_tiles = n_ctx // tm
    n_tiles = (n_ctx + n_lat) // tm
    n = n_ctx + n_lat
    d_in = w_in_b.shape[1]
    da = N_HEADS * HEAD_DIM
    df = d_in - 3 * da
    bpt = tm // seq
    last = n_ctx_tiles - 1

    def mod_row(t):
        return jnp.where(t < n_ctx_tiles, 0, 1 + (t - n_ctx_tiles) // tiles_per_lat_batch)

    tok = lambda w: pl.BlockSpec((tm, w), lambda t: (t, 0))
    full = lambda a: pl.BlockSpec(a.shape, lambda t: (0,) * a.ndim)
    kv_spec = pl.BlockSpec((bpt, 1, N_HEADS, seq, HEAD_DIM), lambda t: (jnp.minimum(t, last), 0, 0, 0, 0))
    kv_shape = jax.ShapeDtypeStruct((batch, 1, N_HEADS, seq, HEAD_DIM), F32)
    return pl.pallas_call(
        functools.partial(_proj_kernel, n_ctx_tiles),
        out_shape=(jax.ShapeDtypeStruct((n, df), BF16),) + (jax.ShapeDtypeStruct((n, da), BF16),) * 3
        + (kv_shape, kv_shape),
        grid=(n_tiles,),
        in_specs=[pl.BlockSpec((tm, d), lambda t: (jnp.minimum(t, last), 0)),
                  pl.BlockSpec((tm, d), lambda t: (jnp.maximum(t - n_ctx_tiles, 0), 0)),
                  pl.BlockSpec((1,) + mod.shape[1:], lambda t: (mod_row(t), 0, 0)),
                  full(g1), full(w_in_b), full(hsum), full(gq_t), full(gk_t)],
        out_specs=(tok(df), tok(da), tok(da), tok(da), kv_spec, kv_spec),
        compiler_params=_cparams(1),
        name="proj",
    )(xp, xs, mod, g1, w_in_b, hsum, gq_t, gk_t)


def _softmax_pv(q2, key_blocks, val_blocks, bias_blocks):
    lane = lax.broadcasted_iota(jnp.int32, (1, LANES), 1)
    masks = [lane < HEAD_DIM, lane >= HEAD_DIM]

    def head_scores(half):
        qh = jnp.where(masks[half], q2, jnp.zeros_like(q2))
        scores = []
        for kb, bb in zip(key_blocks, bias_blocks):
            s = lax.dot_general(qh, kb, (((1,), (1,)), ((), ())), preferred_element_type=F32)
            if bb is not None:
                s = s + bb[half]
            scores.append(s)
        return scores

    def head_out(scores):
        m = jnp.max(functools.reduce(jnp.maximum, scores), axis=-1, keepdims=True)
        ps = [jnp.exp2(s - m) for s in scores]
        denom = jnp.sum(functools.reduce(lambda a, b: a + b, ps), axis=-1, keepdims=True)
        o = functools.reduce(lambda a, b: a + b,
                             [jnp.dot(p.astype(BF16), vb, preferred_element_type=F32)
                              for p, vb in zip(ps, val_blocks)])
        return o / denom

    if len(key_blocks) > 1:
        outs = [head_out(s) for s in [head_scores(0), head_scores(1)]]
    else:
        outs = [head_out(head_scores(half)) for half in range(2)]
    return jnp.where(masks[0], outs[0], outs[1])


def _attn_ctx_kernel(seq, q_ref, k_ref, v_ref, o_ref):
    for b in range(q_ref.shape[0] // seq):
        rs = slice(b * seq, (b + 1) * seq)
        for p in range(q_ref.shape[1] // LANES):
            sl = slice(p * LANES, (p + 1) * LANES)
            o = _softmax_pv(q_ref[rs, sl], [k_ref[rs, sl]], [v_ref[rs, sl]], [None])
            o_ref[rs, sl] = o.astype(BF16)


def _attn_ctx(q, k, v, batch, seq):
    da = q.shape[1]
    assert batch % CTX_SEQS_PER_STEP == 0
    spec = pl.BlockSpec((CTX_SEQS_PER_STEP * seq, da), lambda b: (b, 0))
    return pl.pallas_call(
        functools.partial(_attn_ctx_kernel, seq),
        out_shape=jax.ShapeDtypeStruct((batch * seq, da), BF16),
        grid=(batch // CTX_SEQS_PER_STEP,),
        in_specs=[spec, spec, spec],
        out_specs=spec,
        compiler_params=_cparams(1),
        name="attn_ctx",
    )(q, k, v)


def _attn_lat_kernel(live_blocks, q_ref, k0_ref, k1_ref, k2_ref, v0_ref, v1_ref, v2_ref, kc_ref, vc_ref,
                     bias_ref, o_ref):
    tk = k0_ref.shape[0]
    k_refs, v_refs = (k0_ref, k1_ref, k2_ref), (v0_ref, v1_ref, v2_ref)
    n_tiles = pl.num_programs(0)
    j = pl.program_id(0)
    variant = jnp.where(j == 0, 0, jnp.where(j == n_tiles - 1, 2, 1))

    def tile(blocks):
        for p in range(q_ref.shape[1] // LANES):
            sl = slice(p * LANES, (p + 1) * LANES)
            keys = [k_refs[d][:, sl] for d in blocks] + [kc_ref[0, :, sl]]
            vals = [v_refs[d][:, sl] for d in blocks] + [vc_ref[0, :, sl]]
            biases = [[bias_ref[0, 2 * p + half, :, d * tk:(d + 1) * tk] for half in range(2)] for d in blocks]
            o = _softmax_pv(q_ref[:, sl], keys, vals, biases + [None])
            o_ref[:, sl] = o.astype(BF16)

    for var, blocks in enumerate(live_blocks):
        @pl.when(variant == var)
        def _(blocks=blocks):
            tile(blocks)


def _attn_lat(q, k, v, kc, vc, bias, n_ctx, dec_batch, rows):
    da = q.shape[1]
    tq = ATT_TILE
    rows_per_tile = tq // GRID_W
    tiles = rows // rows_per_tile
    arow, _ = _window_tables(rows)
    live_blocks = tuple(
        tuple(d for d in range(3)
              if any(a >= 0 for row in table for a in row[d * rows_per_tile:(d + 1) * rows_per_tile]))
        for table in arow)
    base = n_ctx // tq
    n_lat = dec_batch * rows * GRID_W
    max_start = tiles - 3

    def qmap(j, b):
        return (base + b * tiles + j, 0)

    def kmap(d):
        return lambda j, b: (base + b * tiles + jnp.clip(j - 1, 0, max_start) + d, 0)

    def bmap(j, b):
        return (jnp.where(j == 0, 0, jnp.where(j == tiles - 1, 2, 1)), 0, 0, 0)

    blk = lambda m: pl.BlockSpec((tq, da), m)
    cspec = pl.BlockSpec((1,) + kc.shape[1:], lambda j, b: (b, 0, 0))
    return pl.pallas_call(
        functools.partial(_attn_lat_kernel, live_blocks),
        out_shape=jax.ShapeDtypeStruct((n_lat, da), BF16),
        grid=(tiles, dec_batch),
        in_specs=[blk(qmap)] + [blk(kmap(d)) for d in range(3)] + [blk(kmap(d)) for d in range(3)]
        + [cspec, cspec, pl.BlockSpec((1,) + bias.shape[1:], bmap)],
        out_specs=pl.BlockSpec((tq, da), lambda j, b: (b * tiles + j, 0)),
        compiler_params=_cparams(2),
        name="attn_lat",
    )(q, k, k, k, v, v, v, kc, vc, bias)


def _window_tables(rows):
    rpt = ATT_TILE // GRID_W
    krows = 3 * rpt
    tiles = rows // rpt
    kh = min(WIN_H_MAX, rows)
    rpb_w = 2 * WIN_W - 1
    arow = -np.ones((3, rpt, krows), np.int64)
    for var, j in enumerate((0, 1, tiles - 1)):
        r0 = j * rpt
        k0 = int(np.clip(j - 1, 0, tiles - 3)) * rpt
        for qi in range(rpt):
            qrow = r0 + qi
            rs = int(np.clip(qrow - kh // 2, 0, rows - kh))
            for ki in range(krows):
                krow = k0 + ki
                if rs <= krow < rs + kh:
                    arow[var, qi, ki] = krow - qrow + WIN_H_MAX - 1
    ec = np.zeros((GRID_W, GRID_W, rpb_w), np.float32)
    for qc in range(GRID_W):
        cs = int(np.clip(qc - WIN_W // 2, 0, GRID_W - WIN_W))
        for kc in range(cs, cs + WIN_W):
            ec[qc, kc, int(np.clip(kc - qc, -(WIN_W - 1), WIN_W - 1)) + WIN_W - 1] = 1.0
    return arow.tolist(), ec


def _bias_kernel(arow, u_ref, o_ref):
    var_id = pl.program_id(0)
    for var, table in enumerate(arow):
        @pl.when(var_id == var)
        def _(table=table):
            for hd in range(u_ref.shape[0]):
                for qi, row in enumerate(table):
                    for ki, a in enumerate(row):
                        blk = u_ref[hd, a] if a >= 0 else jnp.full((GRID_W, GRID_W), NEG_BIG, F32)
                        o_ref[0, hd, qi * GRID_W:(qi + 1) * GRID_W, ki * GRID_W:(ki + 1) * GRID_W] = blk


def _window_bias(rpb, rows):
    arow, ec = _window_tables(rows)
    nh = rpb.shape[0]
    col_blocks = jnp.einsum("hab,xyb->haxy", rpb, ec, precision=lax.Precision.HIGHEST)
    col_blocks = jnp.where(ec.sum(-1) > 0.5, col_blocks * LOG2E, NEG_BIG)
    return pl.pallas_call(
        functools.partial(_bias_kernel, arow),
        out_shape=jax.ShapeDtypeStruct((len(arow), nh, ATT_TILE, 3 * ATT_TILE), F32),
        grid=(len(arow),),
        in_specs=[pl.BlockSpec(col_blocks.shape, lambda v: (0, 0, 0, 0))],
        out_specs=pl.BlockSpec((1, nh, ATT_TILE, 3 * ATT_TILE), lambda v: (v, 0, 0, 0)),
        compiler_params=_cparams(1),
        name="window_bias",
    )(col_blocks)


def _fourier_kernel(t_len, u_ref, w1_ref, ct_ref, st_ref, o_ref, p_scr, q_scr):
    @pl.when(pl.program_id(1) == 0)
    def _():
        pq = jnp.dot(u_ref[...], w1_ref[...], preferred_element_type=F32)
        df = p_scr.shape[1]
        p_scr[...] = pq[:, :df].astype(BF16)
        q_scr[...] = pq[:, df:].astype(BF16)

    rt = ct_ref.shape[0]
    for b in range(u_ref.shape[0] // t_len):
        rs = slice(b * t_len, (b + 1) * t_len)
        o = (jnp.dot(ct_ref[...], p_scr[rs, :], preferred_element_type=F32)
             - jnp.dot(st_ref[...], q_scr[rs, :], preferred_element_type=F32))
        o_ref[b * rt:(b + 1) * rt, :] = o.astype(BF16)


def _fourier(u, w1, ct, st, batch, t_len, first_block, row_tile, seqs=1):
    df = u.shape[1]
    steps = t_len // row_tile
    assert seqs == 1 or (steps == 1 and batch % seqs == 0 and first_block % seqs == 0)
    return pl.pallas_call(
        functools.partial(_fourier_kernel, t_len),
        out_shape=jax.ShapeDtypeStruct((batch * t_len, df), BF16),
        grid=(batch // seqs, steps),
        in_specs=[pl.BlockSpec((seqs * t_len, df), lambda b, i: (first_block // seqs + b, 0)),
                  pl.BlockSpec(w1.shape, lambda b, i: (0, 0)),
                  pl.BlockSpec((row_tile, t_len), lambda b, i: (i, 0)),
                  pl.BlockSpec((row_tile, t_len), lambda b, i: (i, 0))],
        out_specs=pl.BlockSpec((seqs * row_tile, df), lambda b, i: (b * steps + i, 0)),
        scratch_shapes=[pltpu.VMEM((seqs * t_len, df), BF16), pltpu.VMEM((seqs * t_len, df), BF16)],
        compiler_params=_cparams(2),
        name=f"fourier_{t_len}",
    )(u, w1, ct, st)


def _dft_tables(t_len):
    c = FGROUP_DIM
    jk = np.outer(np.arange(c), np.arange(c)) % c
    ang = 2.0 * np.pi * jk / c
    eye = np.eye(N_FGROUPS)
    w1 = np.concatenate([np.kron(eye, np.cos(ang)), np.kron(eye, np.sin(ang))], axis=1) / np.sqrt(c)
    tt = np.outer(np.arange(t_len), np.arange(t_len)) % t_len
    angt = 2.0 * np.pi * tt / t_len
    as_bf16 = lambda a: jnp.asarray(a.astype(np.float32), dtype=BF16)
    return as_bf16(w1), as_bf16(np.cos(angt) / np.sqrt(t_len)), as_bf16(np.sin(angt) / np.sqrt(t_len))


def _merge_kernel(n_ctx_tiles, xp_ref, xs_ref, mod_ref, g1_ref, g2_ref, afc_ref, afl_ref, aac_ref, aal_ref,
                  wf_ref, wa_ref, wg_ref, bg_ref, wo_ref, wrt_ref, br_ref,
                  x1_ref, h2_ref, idx_ref, wts_ref, rank_ref, cnt_ref, carry):
    t = pl.program_id(0)
    is_ctx = t < n_ctx_tiles

    @pl.when(t == 0)
    def _():
        carry[...] = jnp.zeros_like(carry)

    sh1 = mod_ref[0, 0:1, :]
    sc1 = mod_ref[0, 1:2, :]
    gt1 = mod_ref[0, 2:3, :]
    sh2 = mod_ref[0, 3:4, :]
    sc2 = mod_ref[0, 4:5, :]
    d = xp_ref.shape[1]
    part_rows = xp_ref.shape[0] // MERGE_PARTS
    parts = [slice(p * part_rows, (p + 1) * part_rows) for p in range(MERGE_PARTS)]
    xs_ = [jnp.where(is_ctx, xp_ref[rs, :], xs_ref[rs, :]) for rs in parts]
    hbs = [(_rms(x) * g1_ref[...] * (1.0 + sc1) + sh1).astype(BF16) for x in xs_]
    gates = [jax.nn.sigmoid(jnp.dot(hb, wg_ref[...], preferred_element_type=F32) + bg_ref[...]) for hb in hbs]
    fas = [jnp.dot(jnp.where(is_ctx, afc_ref[rs, :], afl_ref[rs, :]), wf_ref[...], preferred_element_type=F32)
           for rs in parts]
    fbs = [jnp.dot(jnp.where(is_ctx, aac_ref[rs, :], aal_ref[rs, :]), wa_ref[...], preferred_element_type=F32)
           for rs in parts]
    mixes = [(g[:, :d] * fa + g[:, d:] * fb).astype(BF16) for g, fa, fb in zip(gates, fas, fbs)]
    x1s = [x + gt1 * jnp.dot(mix, wo_ref[...], preferred_element_type=F32) for x, mix in zip(xs_, mixes)]
    h2s = []
    for rs, x1 in zip(parts, x1s):
        x1_ref[rs, :] = x1
        h2 = _rms(x1) * g2_ref[...] * (1.0 + sc2) + sh2
        h2_ref[rs, :] = _pack_halves(h2)
        h2s.append(h2)
    h2 = jnp.concatenate(h2s, axis=0)
    h2_hi = h2.astype(BF16)
    h2_lo = (h2 - h2_hi.astype(F32)).astype(BF16)
    ne = br_ref.shape[0]
    nt_dims = (((1,), (1,)), ((), ()))
    both = lax.dot_general(wrt_ref[...], h2_hi, nt_dims, preferred_element_type=F32)
    cross = lax.dot_general(wrt_ref[:ne, :], h2_lo, nt_dims, preferred_element_type=F32)
    logits_t = both[:ne, :] + both[ne:, :] + cross + br_ref[...]
    idx_o, wts_o, rank_o, counts = _route(logits_t, carry[...])
    idx_ref[...] = idx_o
    wts_ref[...] = wts_o
    rank_ref[...] = rank_o
    carry[...] = counts
    cnt_ref[...] = counts


def _route(logits_t, counts):
    ne, tm = logits_t.shape
    row = lax.broadcasted_iota(jnp.int32, (ne, tm), 0)
    lg = logits_t
    vals, idxs = [], []
    for _ in range(TOP_K):
        m = jnp.max(lg, axis=0, keepdims=True)
        am = jnp.min(jnp.where(lg == m, row, ne), axis=0, keepdims=True)
        vals.append(m)
        idxs.append(am)
        lg = jnp.where(row == am, -jnp.inf, lg)
    es = [jnp.exp(v - vals[0]) for v in vals]
    den = functools.reduce(lambda a, b: a + b, es)
    onehot = functools.reduce(lambda a, b: a + b, [(row == am).astype(F32) for am in idxs])
    s_i = lax.broadcasted_iota(jnp.int32, (tm, tm), 0)
    t_i = lax.broadcasted_iota(jnp.int32, (tm, tm), 1)
    earlier = (s_i < t_i).astype(BF16)
    before = jnp.dot(onehot.astype(BF16), earlier, preferred_element_type=F32) + counts
    out_row = lax.broadcasted_iota(jnp.int32, (SUBLANES, tm), 0)
    idx_o = jnp.zeros((SUBLANES, tm), jnp.int32)
    wts_o = jnp.zeros((SUBLANES, tm), F32)
    rank_o = jnp.zeros((SUBLANES, tm), jnp.int32)
    for kk in range(TOP_K):
        rk = jnp.sum(jnp.where(row == idxs[kk], before, 0.0), axis=0, keepdims=True).astype(jnp.int32)
        idx_o = jnp.where(out_row == kk, idxs[kk], idx_o)
        wts_o = jnp.where(out_row == kk, es[kk] / den, wts_o)
        rank_o = jnp.where(out_row == kk, rk, rank_o)
    return idx_o, wts_o, rank_o, counts + jnp.sum(onehot, axis=1, keepdims=True)


def _merge(xp, xs, mod, g1, g2, af_c, af_l, aa_c, aa_l, wf, wa, wg, bg, wo, wr, br, tiles_per_lat_batch):
    n_ctx, d = xp.shape
    n = n_ctx + xs.shape[0]
    tm = TOK_TILE
    n_ctx_tiles = n_ctx // tm
    last = n_ctx_tiles - 1
    ne = wr.shape[1]
    da = aa_c.shape[1]
    df = af_c.shape[1]

    def mod_row(t):
        return jnp.where(t < n_ctx_tiles, 0, 1 + (t - n_ctx_tiles) // tiles_per_lat_batch)

    cmap = lambda t: (jnp.minimum(t, last), 0)
    lmap = lambda t: (jnp.maximum(t - n_ctx_tiles, 0), 0)
    full = lambda a: pl.BlockSpec(a.shape, lambda t: (0,) * a.ndim)
    tok = lambda w: pl.BlockSpec((tm, w), lambda t: (t, 0))
    per_choice = lambda dt: jax.ShapeDtypeStruct((SUBLANES, n), dt)
    choice_spec = pl.BlockSpec((SUBLANES, tm), lambda t: (0, t))
    wr_hi = wr.astype(BF16)
    wr_lo = (wr - wr_hi.astype(F32)).astype(BF16)
    wr_t = jnp.concatenate([wr_hi.T, wr_lo.T], axis=0)
    return pl.pallas_call(
        functools.partial(_merge_kernel, n_ctx_tiles),
        out_shape=(jax.ShapeDtypeStruct((n, d), F32), jax.ShapeDtypeStruct((n, d // 2), jnp.int32),
                   per_choice(jnp.int32), per_choice(F32), per_choice(jnp.int32),
                   jax.ShapeDtypeStruct((ne, 1), F32)),
        grid=(n // tm,),
        in_specs=[pl.BlockSpec((tm, d), cmap), pl.BlockSpec((tm, d), lmap),
                  pl.BlockSpec((1,) + mod.shape[1:], lambda t: (mod_row(t), 0, 0)),
                  full(g1), full(g2),
                  pl.BlockSpec((tm, df), cmap), pl.BlockSpec((tm, df), lmap),
                  pl.BlockSpec((tm, da), cmap), pl.BlockSpec((tm, da), lmap),
                  full(wf), full(wa), full(wg), full(bg), full(wo), full(wr_t), full(br)],
        out_specs=(tok(d), tok(d // 2), choice_spec, choice_spec, choice_spec,
                   pl.BlockSpec((ne, 1), lambda t: (0, 0))),
        scratch_shapes=[pltpu.VMEM((ne, 1), F32)],
        compiler_params=_cparams(1),
        name="merge",
    )(xp, xs, mod, g1, g2, af_c, af_l, aa_c, aa_l, wf, wa, wg, bg, wo, wr_t, br)


def _pos_kernel(off_ref, idx_ref, rank_ref, o_ref):
    idx = idx_ref[...]
    pos = rank_ref[...]
    for e in range(off_ref.shape[0]):
        pos = pos + jnp.where(idx == e, off_ref[e], 0)
    o_ref[...] = pos


def _positions(first_row, idx, rank):
    rows, n = idx.shape
    blk = POS_COLS
    spec = pl.BlockSpec((rows, blk), lambda j, off: (0, j))
    return pl.pallas_call(
        _pos_kernel,
        out_shape=jax.ShapeDtypeStruct((rows, n), jnp.int32),
        grid_spec=pltpu.PrefetchScalarGridSpec(num_scalar_prefetch=1, grid=(n // blk,),
                                               in_specs=[spec, spec], out_specs=spec),
        compiler_params=_cparams(1),
        name="positions",
    )(first_row, idx, rank)


def _moe_kernel(te_ref, nt_ref, nx_ref, rv_ref, sl_ref, x_ref, w1_hbm, b1_ref, w2_hbm, b2_ref, y_ref,
                w1s, w2s, sem):
    i = pl.program_id(0)
    e = te_ref[i]
    prev = te_ref[jnp.maximum(i - 1, 0)]
    first_of_run = (i == 0) | (e != prev)
    slot = sl_ref[i]

    def stage(expert, s):
        return (pltpu.make_async_copy(w1_hbm.at[expert], w1s.at[s], sem.at[0, s]),
                pltpu.make_async_copy(w2_hbm.at[expert], w2s.at[s], sem.at[1, s]))

    @pl.when(i == 0)
    def _():
        for cp in stage(e, slot):
            cp.start()

    @pl.when(first_of_run)
    def _():
        for cp in stage(e, slot):
            cp.wait()

        @pl.when(nx_ref[i] >= 0)
        def _():
            for cp in stage(nx_ref[i], 1 - slot):
                cp.start()

    def expert_rows(rs):
        dff = w2s.shape[1]
        x_lo, x_hi = _unpack_halves(x_ref[rs, :])
        x = jnp.concatenate([x_lo.astype(BF16), x_hi.astype(BF16)], axis=1)
        gu = jnp.dot(x, w1s[slot].astype(BF16), preferred_element_type=F32) + b1_ref[0]
        gate = jnp.minimum(gu[:, :dff], SWIGLU_LIMIT)
        up = jnp.clip(gu[:, dff:], -SWIGLU_LIMIT, SWIGLU_LIMIT)
        glu = gate * jax.nn.sigmoid(SWIGLU_ALPHA * gate)
        act = ((up + 1.0) * glu).astype(BF16)
        y = jnp.dot(act, w2s[slot].astype(BF16), preferred_element_type=F32) + b2_ref[0]
        y_ref[rs, :] = _pack_halves(y)

    tm = x_ref.shape[0]
    rv = rv_ref[i]

    @pl.when(rv == tm)
    def _():
        expert_rows(slice(0, tm))

    for sub in range(tm // MOE_SUB):
        @pl.when((rv < tm) & (rv > sub * MOE_SUB))
        def _(sub=sub):
            expert_rows(slice(sub * MOE_SUB, (sub + 1) * MOE_SUB))


def _moe(tile_expert, n_tiles, next_expert, rows_valid, slots, xs_sorted, w1, b1, w2, b2):
    p, dh = xs_sorted.shape
    ne, d, dff2 = w1.shape
    dff = w2.shape[1]
    tm = MOE_TILE
    max_tiles = p // tm

    def row_map(i, te, nt, nx, rv, sl):
        return (jnp.minimum(i, nt[0] - 1), 0)

    def b_map(i, te, nt, nx, rv, sl):
        return (te[i], 0, 0)

    grid_spec = pltpu.PrefetchScalarGridSpec(
        num_scalar_prefetch=5,
        grid=(max_tiles,),
        in_specs=[pl.BlockSpec((tm, dh), row_map),
                  pl.BlockSpec(memory_space=pl.ANY),
                  pl.BlockSpec((1, 1, dff2), b_map),
                  pl.BlockSpec(memory_space=pl.ANY),
                  pl.BlockSpec((1, 1, d), b_map)],
        out_specs=pl.BlockSpec((tm, dh), row_map),
        scratch_shapes=[pltpu.VMEM((2, d, dff2), F32), pltpu.VMEM((2, dff, d), F32),
                        pltpu.SemaphoreType.DMA((2, 2))],
    )
    return pl.pallas_call(
        _moe_kernel,
        out_shape=jax.ShapeDtypeStruct((p, dh), jnp.int32),
        grid_spec=grid_spec,
        compiler_params=_cparams(1),
        name="moe",
    )(tile_expert, n_tiles, next_expert, rows_valid, slots, xs_sorted,
      w1, b1.reshape(ne, 1, dff2), w2, b2.reshape(ne, 1, d))


def _sc_mesh():
    return plsc.VectorSubcoreMesh(core_axis_name="c", subcore_axis_name="s",
                                  num_cores=SC_CORES, num_subcores=SC_SUBCORES)


def _sc_dispatch(h, idx, p_rows):
    n, d = h.shape
    nw, items, chunk = idx.shape
    tok_w = n // nw
    n_chunks = items // TOP_K

    @functools.partial(
        pl.kernel, mesh=_sc_mesh(),
        out_type=jax.ShapeDtypeStruct((p_rows, d), h.dtype),
        scratch_types=[pltpu.VMEM((items, chunk), jnp.int32),
                       pltpu.VMEM((2, chunk, d), h.dtype),
                       pltpu.SemaphoreType.DMA((2,)), pltpu.SemaphoreType.DMA((2,))],
        name="sc_dispatch",
    )
    def run(h_hbm, idx_hbm, out_hbm, idx_v, rows_v, lsem, ssem):
        wid = lax.axis_index("s") * SC_CORES + lax.axis_index("c")
        base = wid * tok_w
        pltpu.sync_copy(idx_hbm.at[wid], idx_v)

        def load(j, slot):
            return pltpu.make_async_copy(h_hbm.at[pl.ds(base + j * chunk, chunk)], rows_v.at[slot], lsem.at[slot])

        def scat(j, kk, slot):
            return pltpu.make_async_copy(rows_v.at[slot], out_hbm.at[idx_v.at[j * TOP_K + kk]], ssem.at[slot])

        load(0, 0).start()

        @pl.loop(0, n_chunks, step=2)
        def _(j0):
            for b in range(2):
                j = j0 + b

                @pl.when(j >= 1)
                def _():
                    for kk in range(TOP_K):
                        scat(j - 1, kk, 1 - b).wait()

                @pl.when(j + 1 < n_chunks)
                def _():
                    load(j + 1, 1 - b).start()

                load(j, b).wait()
                for kk in range(TOP_K):
                    scat(j, kk, b).start()

        for kk in range(TOP_K):
            scat(n_chunks - 1, kk, (n_chunks - 1) % 2).wait()

    return run(h, idx)


def _sc_gather(y, idx, n):
    d = y.shape[1]
    nw, items, chunk = idx.shape
    tok_w = n // nw

    @functools.partial(
        pl.kernel, mesh=_sc_mesh(),
        out_type=jax.ShapeDtypeStruct((TOP_K, n, d), y.dtype),
        scratch_types=[pltpu.VMEM((items, chunk), jnp.int32),
                       pltpu.VMEM((2, chunk, d), y.dtype),
                       pltpu.SemaphoreType.DMA((2,)), pltpu.SemaphoreType.DMA((2,))],
        name="sc_gather",
    )
    def run(y_hbm, idx_hbm, out_hbm, idx_v, rows_v, gsem, wsem):
        wid = lax.axis_index("s") * SC_CORES + lax.axis_index("c")
        base = wid * tok_w
        pltpu.sync_copy(idx_hbm.at[wid], idx_v)

        def gather(i, slot):
            return pltpu.make_async_copy(y_hbm.at[idx_v.at[i]], rows_v.at[slot], gsem.at[slot])

        def write(i, slot):
            dst = out_hbm.at[i % TOP_K, pl.ds(base + (i // TOP_K) * chunk, chunk)]
            return pltpu.make_async_copy(rows_v.at[slot], dst, wsem.at[slot])

        gather(0, 0).start()

        @pl.loop(0, items, step=2)
        def _(i0):
            for b in range(2):
                i = i0 + b

                @pl.when(i >= 1)
                def _():
                    write(i - 1, 1 - b).wait()

                @pl.when(i + 1 < items)
                def _():
                    gather(i + 1, 1 - b).start()

                gather(i, b).wait()
                write(i, b).start()

        write(items - 1, (items - 1) % 2).wait()

    return run(y, idx)


def _combine_kernel(n_ctx_tiles, x1_ref, yg_ref, wts_ref, mod_ref, op_ref, os_ref):
    t = pl.program_id(0)
    gt2 = mod_ref[0, 5:6, :]
    w = wts_ref[...]
    acc_lo, acc_hi = None, None
    for kk in range(TOP_K):
        y_lo, y_hi = _unpack_halves(yg_ref[kk])
        wk = w[:, kk:kk + 1]
        acc_lo = wk * y_lo if acc_lo is None else acc_lo + wk * y_lo
        acc_hi = wk * y_hi if acc_hi is None else acc_hi + wk * y_hi
    out = x1_ref[...] + gt2 * jnp.concatenate([acc_lo, acc_hi], axis=1)

    @pl.when(t < n_ctx_tiles)
    def _():
        op_ref[...] = out

    @pl.when(t >= n_ctx_tiles)
    def _():
        os_ref[...] = out


def _combine(x1, yg, wts, mod, n_ctx, lat_seq):
    n, d = x1.shape
    tm = COMBINE_TILE
    assert n_ctx % tm == 0 and lat_seq % tm == 0
    tiles_per_lat_batch = lat_seq // tm
    n_ctx_tiles = n_ctx // tm
    last = n_ctx_tiles - 1

    def mod_row(t):
        return jnp.where(t < n_ctx_tiles, 0, 1 + (t - n_ctx_tiles) // tiles_per_lat_batch)

    return pl.pallas_call(
        functools.partial(_combine_kernel, n_ctx_tiles),
        out_shape=(jax.ShapeDtypeStruct((n_ctx, d), F32), jax.ShapeDtypeStruct((n - n_ctx, d), F32)),
        grid=(n // tm,),
        in_specs=[pl.BlockSpec((tm, d), lambda t: (t, 0)),
                  pl.BlockSpec((TOP_K, tm, d // 2), lambda t: (0, t, 0)),
                  pl.BlockSpec((tm, TOP_K), lambda t: (t, 0)),
                  pl.BlockSpec((1,) + mod.shape[1:], lambda t: (mod_row(t), 0, 0))],
        out_specs=(pl.BlockSpec((tm, d), lambda t: (jnp.minimum(t, last), 0)),
                   pl.BlockSpec((tm, d), lambda t: (jnp.maximum(t - n_ctx_tiles, 0), 0))),
        compiler_params=_cparams(1),
        name="combine",
    )(x1, yg, wts, mod)


def kernel(x_prompt, x_sample, cache_k, cache_v, c, c_ctx, g_norm1, w_ada, b_ada, w_in, g_q, g_k, rpb,
           w_fmap, w_amap, w_gate, b_gate, w_out, g_norm2, w_router, b_router, w1, b1, w2, b2):
    batch, seq, d = x_prompt.shape
    dec_batch, dec_seq, _ = x_sample.shape
    assert w_ada.shape[0] == 1, "single-layer trunk"
    rows = dec_seq // GRID_W
    da = N_HEADS * HEAD_DIM
    n_ctx = batch * seq
    n_lat = dec_batch * dec_seq
    n = n_ctx + n_lat
    ne = w_router.shape[2]
    assert TOK_TILE % seq == 0 and n_ctx % TOK_TILE == 0 and dec_seq % TOK_TILE == 0
    assert dec_seq % FOURIER_TILE == 0 and n_ctx % dec_seq == 0
    assert rows % (ATT_TILE // GRID_W) == 0 and rows >= 3 * (ATT_TILE // GRID_W) and rows >= WIN_H_MAX
    assert seq == ATT_TILE and cache_k.shape[3] == ATT_TILE
    tiles_per_lat_batch = dec_seq // TOK_TILE

    n_mod_rows = -(-(1 + dec_batch) // SUBLANES) * SUBLANES
    cvecs = jnp.zeros((n_mod_rows, d), F32).at[0].set(c_ctx).at[1:1 + dec_batch].set(c)
    mod = _ada(cvecs, w_ada[0], b_ada[0]).reshape(n_mod_rows, 6, d)

    xp = x_prompt.reshape(n_ctx, d)
    xs = x_sample.reshape(n_lat, d)
    g1 = g_norm1[0].reshape(1, d)
    g2 = g_norm2[0].reshape(1, d)
    hsum = jnp.asarray(np.kron(np.eye(N_HEADS), np.full((HEAD_DIM, HEAD_DIM), 1.0 / HEAD_DIM)), BF16)
    gq_t = jnp.tile(g_q[0], N_HEADS).reshape(1, da)
    gk_t = jnp.tile(g_k[0], N_HEADS).reshape(1, da)

    u, q, k, v, new_k, new_v = _proj(xp, xs, mod, g1, w_in[0].astype(BF16), hsum, gq_t, gk_t,
                                      batch, seq, tiles_per_lat_batch)

    aa_c = _attn_ctx(q, k, v, batch, seq)
    bias = _window_bias(rpb[0], rows)
    past = cache_k.shape[3]
    kc = cache_k[:, 0].transpose(0, 2, 1, 3).reshape(dec_batch, past, da).astype(BF16)
    vc = cache_v[:, 0].transpose(0, 2, 1, 3).reshape(dec_batch, past, da).astype(BF16)
    aa_l = _attn_lat(q, k, v, kc, vc, bias, n_ctx, dec_batch, rows)

    w1c, ct_c, st_c = _dft_tables(seq)
    _, ct_l, st_l = _dft_tables(dec_seq)
    af_c = _fourier(u, w1c, ct_c, st_c, batch, seq, 0, seq, CTX_SEQS_PER_STEP)
    af_l = _fourier(u, w1c, ct_l, st_l, dec_batch, dec_seq, n_ctx // dec_seq, FOURIER_TILE)

    x1, h2, idx, wts, rank, counts = _merge(
        xp, xs, mod, g1, g2, af_c, af_l, aa_c, aa_l,
        w_fmap[0].astype(BF16), w_amap[0].astype(BF16), w_gate[0].astype(BF16), b_gate[0].reshape(1, -1),
        w_out[0].astype(BF16), w_router[0], b_router[0].reshape(ne, 1), tiles_per_lat_batch)

    tm = MOE_TILE
    max_tiles = (n * TOP_K) // tm + ne
    cnt = counts[:, 0].astype(jnp.int32)
    tiles_e = (cnt + tm - 1) // tm
    tile_end = jnp.cumsum(tiles_e)
    pad_off = (tile_end - tiles_e) * tm
    n_tiles = tile_end[-1:]
    tile_ids = jnp.arange(max_tiles, dtype=jnp.int32)
    tile_expert = jnp.sum((tile_ids[:, None] >= tile_end[None, :]).astype(jnp.int32), axis=1)
    last_e = jnp.sum((n_tiles - 1 >= tile_end).astype(jnp.int32))
    tile_expert = jnp.minimum(tile_expert, last_e).astype(jnp.int32)
    assert n % POS_COLS == 0
    pos = _positions(pad_off.astype(jnp.int32), idx, rank)[:TOP_K]

    p_rows = max_tiles * tm
    n_workers = SC_CORES * SC_SUBCORES
    assert n % (n_workers * SC_CHUNK * 2) == 0
    chunks = n // (n_workers * SC_CHUNK)
    pos_lists = pos.astype(jnp.int32).reshape(TOP_K, n_workers, chunks, SC_CHUNK).transpose(1, 2, 0, 3)
    pos_lists = pos_lists.reshape(n_workers, chunks * TOP_K, SC_CHUNK)
    xs_sorted = _sc_dispatch(h2, pos_lists, p_rows)
    e_ids = jnp.arange(ne, dtype=jnp.int32)
    later_used = (e_ids[None, :] > e_ids[:, None]) & (tiles_e[None, :] > 0)
    next_of_e = jnp.min(jnp.where(later_used, e_ids[None, :], ne), axis=1)
    next_of_e = jnp.where(next_of_e == ne, -1, next_of_e)
    is_e = tile_expert[:, None] == e_ids[None, :]
    next_expert = jnp.sum(jnp.where(is_e, next_of_e[None, :], 0), axis=1)
    rows_left = jnp.sum(jnp.where(is_e, (cnt + pad_off)[None, :], 0), axis=1) - tile_ids * tm
    rows_valid = jnp.where(tile_ids < n_tiles[0], jnp.clip(rows_left, 0, tm), 0)
    run_of_e = jnp.cumsum((tiles_e > 0).astype(jnp.int32)) - 1
    slots = jnp.sum(jnp.where(is_e, run_of_e[None, :], 0), axis=1) % 2
    y_sorted = _moe(tile_expert, n_tiles.astype(jnp.int32), next_expert.astype(jnp.int32),
                    rows_valid.astype(jnp.int32), slots.astype(jnp.int32), xs_sorted,
                    w1[0], b1[0], w2[0], b2[0])
    yg = _sc_gather(y_sorted, pos_lists, n)
    y_p, y_s = _combine(x1, yg, wts[:TOP_K].T, mod, n_ctx, dec_seq)
    return (y_p.reshape(batch, seq, d), y_s.reshape(dec_batch, dec_seq, d), new_k, new_v)
```

```python
import functools
import math

import numpy as np
import jax
import jax.numpy as jnp
from jax import lax
from jax.experimental import pallas as pl
from jax.experimental.pallas import tpu as pltpu
from jax.experimental.pallas import tpu_sc as plsc

F32 = jnp.float32
BF16 = jnp.bfloat16

GRID_W = 64
N_HEADS = 8
HEAD_DIM = 64
N_FGROUPS = 4
FGROUP_DIM = 128
WIN_H_MAX = 8
WIN_W = 16
TOP_K = 4
SWIGLU_LIMIT = 7.0
SWIGLU_ALPHA = 1.702
EPS = 1e-6
LOG2E = math.log2(math.e)
QK_PRESCALE = HEAD_DIM ** -0.5 * LOG2E

LANES = 128
SUBLANES = 8
ADA_COLS = 2048
POS_COLS = 2048
TOK_TILE = 512
COMBINE_TILE = 1024
FOURIER_TILE = 1024
ATT_TILE = 256
CTX_SEQS_PER_STEP = 4
MOE_TILE = 1024
MOE_SUB = 256
MERGE_PARTS = 2
VMEM_LIMIT = 56 * 1024 * 1024
NEG_BIG = -1e30

SC_CORES = 2
SC_SUBCORES = 16
SC_CHUNK = 64


def _cparams(n_axes, vmem=VMEM_LIMIT):
    return pltpu.CompilerParams(dimension_semantics=("arbitrary",) * n_axes, vmem_limit_bytes=vmem)


def _rms(x):
    return x * lax.rsqrt(jnp.mean(x * x, axis=-1, keepdims=True) + EPS)


def _pack_halves(x):
    c = x.shape[1] // 2
    lo = lax.bitcast_convert_type(x[:, :c].astype(BF16).astype(F32), jnp.uint32)
    hi = lax.bitcast_convert_type(x[:, c:].astype(BF16).astype(F32), jnp.uint32)
    return lax.bitcast_convert_type(hi | (lo >> 16), jnp.int32)


def _unpack_halves(w):
    u = lax.bitcast_convert_type(w, jnp.uint32)
    lo = lax.bitcast_convert_type(u << 16, F32)
    hi = lax.bitcast_convert_type(u & jnp.uint32(0xFFFF0000), F32)
    return lo, hi


def _ada_kernel(c_ref, w_ref, b_ref, o_ref):
    cv = c_ref[...]
    s = cv * jax.nn.sigmoid(cv)
    w = w_ref[...]
    s_hi = s.astype(BF16)
    s_lo = (s - s_hi.astype(F32)).astype(BF16)
    w_hi = w.astype(BF16)
    w_lo = (w - w_hi.astype(F32)).astype(BF16)
    o_ref[...] = (jnp.dot(s_hi, w_hi, preferred_element_type=F32)
                  + jnp.dot(s_hi, w_lo, preferred_element_type=F32)
                  + jnp.dot(s_lo, w_hi, preferred_element_type=F32) + b_ref[...])


def _ada(cvecs, w_ada, b_ada):
    rows, d = cvecs.shape
    n = w_ada.shape[1]
    blk = ADA_COLS
    return pl.pallas_call(
        _ada_kernel,
        out_shape=jax.ShapeDtypeStruct((rows, n), F32),
        grid=(n // blk,),
        in_specs=[pl.BlockSpec((rows, d), lambda j: (0, 0)),
                  pl.BlockSpec((d, blk), lambda j: (0, j)),
                  pl.BlockSpec((1, blk), lambda j: (0, j))],
        out_specs=pl.BlockSpec((rows, blk), lambda j: (0, j)),
        compiler_params=_cparams(1),
        name="ada",
    )(cvecs, w_ada, b_ada.reshape(1, n))


def _proj_kernel(n_ctx_tiles, xp_ref, xs_ref, mod_ref, g1_ref, win_ref, hsum_ref, gq_ref, gk_ref,
                 u_ref, q_ref, k_ref, v_ref, nk_ref, nv_ref):
    t = pl.program_id(0)
    is_ctx = t < n_ctx_tiles
    sh1 = mod_ref[0, 0:1, :]
    sc1 = mod_ref[0, 1:2, :]
    da = q_ref.shape[1]
    df = u_ref.shape[1]
    nb, _, nh, s, dh = nk_ref.shape
    parts = [slice(b * s, (b + 1) * s) for b in range(nb)]
    hs = [(_rms(jnp.where(is_ctx, xp_ref[rs, :], xs_ref[rs, :])) * g1_ref[...] * (1.0 + sc1) + sh1).astype(BF16)
          for rs in parts]
    projs = [jnp.dot(h, win_ref[...], preferred_element_type=F32) for h in hs]
    kns, vs = [], []
    for rs, proj in zip(parts, projs):
        u_ref[rs, :] = proj[:, :df].astype(BF16)
        q = proj[:, df:df + da]
        k = proj[:, df + da:df + 2 * da]
        v = proj[:, df + 2 * da:]
        msq = jnp.dot((q * q).astype(BF16), hsum_ref[...], preferred_element_type=F32)
        msk = jnp.dot((k * k).astype(BF16), hsum_ref[...], preferred_element_type=F32)
        qn = q * lax.rsqrt(msq + EPS) * gq_ref[...]
        kn = k * lax.rsqrt(msk + EPS) * gk_ref[...]
        q_ref[rs, :] = (qn * QK_PRESCALE).astype(BF16)
        k_ref[rs, :] = kn.astype(BF16)
        v_ref[rs, :] = v.astype(BF16)
        kns.append(kn)
        vs.append(v)

    @pl.when(is_ctx)
    def _():
        for b in range(nb):
            for hd in range(nh):
                nk_ref[b, 0, hd] = kns[b][:, hd * dh:(hd + 1) * dh]
                nv_ref[b, 0, hd] = vs[b][:, hd * dh:(hd + 1) * dh]


def _proj(xp, xs, mod, g1, w_in_b, hsum, gq_t, gk_t, batch, seq, tiles_per_lat_batch):
    n_ctx, d = xp.shape
    n_lat = xs.shape[0]
    tm = TOK_TILE
    n_ctx_tiles = n_ctx // tm
    n_tiles = (n_ctx + n_lat) // tm
    n = n_ctx + n_lat
    d_in = w_in_b.shape[1]
    da = N_HEADS * HEAD_DIM
    df = d_in - 3 * da
    bpt = tm // seq
    last = n_ctx_tiles - 1

    def mod_row(t):
        return jnp.where(t < n_ctx_tiles, 0, 1 + (t - n_ctx_tiles) // tiles_per_lat_batch)

    tok = lambda w: pl.BlockSpec((tm, w), lambda t: (t, 0))
    full = lambda a: pl.BlockSpec(a.shape, lambda t: (0,) * a.ndim)
    kv_spec = pl.BlockSpec((bpt, 1, N_HEADS, seq, HEAD_DIM), lambda t: (jnp.minimum(t, last), 0, 0, 0, 0))
    kv_shape = jax.ShapeDtypeStruct((batch, 1, N_HEADS, seq, HEAD_DIM), F32)
    return pl.pallas_call(
        functools.partial(_proj_kernel, n_ctx_tiles),
        out_shape=(jax.ShapeDtypeStruct((n, df), BF16),) + (jax.ShapeDtypeStruct((n, da), BF16),) * 3
        + (kv_shape, kv_shape),
        grid=(n_tiles,),
        in_specs=[pl.BlockSpec((tm, d), lambda t: (jnp.minimum(t, last), 0)),
                  pl.BlockSpec((tm, d), lambda t: (jnp.maximum(t - n_ctx_tiles, 0), 0)),
                  pl.BlockSpec((1,) + mod.shape[1:], lambda t: (mod_row(t), 0, 0)),
                  full(g1), full(w_in_b), full(hsum), full(gq_t), full(gk_t)],
        out_specs=(tok(df), tok(da), tok(da), tok(da), kv_spec, kv_spec),
        compiler_params=_cparams(1),
        name="proj",
    )(xp, xs, mod, g1, w_in_b, hsum, gq_t, gk_t)


def _softmax_pv(q2, key_blocks, val_blocks, bias_blocks):
    lane = lax.broadcasted_iota(jnp.int32, (1, LANES), 1)
    masks = [lane < HEAD_DIM, lane >= HEAD_DIM]

    def head_scores(half):
        qh = jnp.where(masks[half], q2, jnp.zeros_like(q2))
        scores = []
        for kb, bb in zip(key_blocks, bias_blocks):
            s = lax.dot_general(qh, kb, (((1,), (1,)), ((), ())), preferred_element_type=F32)
            if bb is not None:
                s = s + bb[half]
            scores.append(s)
        return scores

    def head_out(scores):
        m = jnp.max(functools.reduce(jnp.maximum, scores), axis=-1, keepdims=True)
        ps = [jnp.exp2(s - m) for s in scores]
        denom = jnp.sum(functools.reduce(lambda a, b: a + b, ps), axis=-1, keepdims=True)
        o = functools.reduce(lambda a, b: a + b,
                             [jnp.dot(p.astype(BF16), vb, preferred_element_type=F32)
                              for p, vb in zip(ps, val_blocks)])
        return o / denom

    if len(key_blocks) > 1:
        outs = [head_out(s) for s in [head_scores(0), head_scores(1)]]
    else:
        outs = [head_out(head_scores(half)) for half in range(2)]
    return jnp.where(masks[0], outs[0], outs[1])


def _attn_ctx_kernel(seq, q_ref, k_ref, v_ref, o_ref):
    for b in range(q_ref.shape[0] // seq):
        rs = slice(b * seq, (b + 1) * seq)
        for p in range(q_ref.shape[1] // LANES):
            sl = slice(p * LANES, (p + 1) * LANES)
            o = _softmax_pv(q_ref[rs, sl], [k_ref[rs, sl]], [v_ref[rs, sl]], [None])
            o_ref[rs, sl] = o.astype(BF16)


def _attn_ctx(q, k, v, batch, seq):
    da = q.shape[1]
    assert batch % CTX_SEQS_PER_STEP == 0
    spec = pl.BlockSpec((CTX_SEQS_PER_STEP * seq, da), lambda b: (b, 0))
    return pl.pallas_call(
        functools.partial(_attn_ctx_kernel, seq),
        out_shape=jax.ShapeDtypeStruct((batch * seq, da), BF16),
        grid=(batch // CTX_SEQS_PER_STEP,),
        in_specs=[spec, spec, spec],
        out_specs=spec,
        compiler_params=_cparams(1),
        name="attn_ctx",
    )(q, k, v)


def _attn_lat_kernel(live_blocks, q_ref, k0_ref, k1_ref, k2_ref, v0_ref, v1_ref, v2_ref, kc_ref, vc_ref,
                     bias_ref, o_ref):
    tk = k0_ref.shape[0]
    k_refs, v_refs = (k0_ref, k1_ref, k2_ref), (v0_ref, v1_ref, v2_ref)
    n_tiles = pl.num_programs(0)
    j = pl.program_id(0)
    variant = jnp.where(j == 0, 0, jnp.where(j == n_tiles - 1, 2, 1))

    def tile(blocks):
        for p in range(q_ref.shape[1] // LANES):
            sl = slice(p * LANES, (p + 1) * LANES)
            keys = [k_refs[d][:, sl] for d in blocks] + [kc_ref[0, :, sl]]
            vals = [v_refs[d][:, sl] for d in blocks] + [vc_ref[0, :, sl]]
            biases = [[bias_ref[0, 2 * p + half, :, d * tk:(d + 1) * tk] for half in range(2)] for d in blocks]
            o = _softmax_pv(q_ref[:, sl], keys, vals, biases + [None])
            o_ref[:, sl] = o.astype(BF16)

    for var, blocks in enumerate(live_blocks):
        @pl.when(variant == var)
        def _(blocks=blocks):
            tile(blocks)


def _attn_lat(q, k, v, kc, vc, bias, n_ctx, dec_batch, rows):
    da = q.shape[1]
    tq = ATT_TILE
    rows_per_tile = tq // GRID_W
    tiles = rows // rows_per_tile
    arow, _ = _window_tables(rows)
    live_blocks = tuple(
        tuple(d for d in range(3)
              if any(a >= 0 for row in table for a in row[d * rows_per_tile:(d + 1) * rows_per_tile]))
        for table in arow)
    base = n_ctx // tq
    n_lat = dec_batch * rows * GRID_W
    max_start = tiles - 3

    def qmap(j, b):
        return (base + b * tiles + j, 0)

    def kmap(d):
        return lambda j, b: (base + b * tiles + jnp.clip(j - 1, 0, max_start) + d, 0)

    def bmap(j, b):
        return (jnp.where(j == 0, 0, jnp.where(j == tiles - 1, 2, 1)), 0, 0, 0)

    blk = lambda m: pl.BlockSpec((tq, da), m)
    cspec = pl.BlockSpec((1,) + kc.shape[1:], lambda j, b: (b, 0, 0))
    return pl.pallas_call(
        functools.partial(_attn_lat_kernel, live_blocks),
        out_shape=jax.ShapeDtypeStruct((n_lat, da), BF16),
        grid=(tiles, dec_batch),
        in_specs=[blk(qmap)] + [blk(kmap(d)) for d in range(3)] + [blk(kmap(d)) for d in range(3)]
        + [cspec, cspec, pl.BlockSpec((1,) + bias.shape[1:], bmap)],
        out_specs=pl.BlockSpec((tq, da), lambda j, b: (b * tiles + j, 0)),
        compiler_params=_cparams(2),
        name="attn_lat",
    )(q, k, k, k, v, v, v, kc, vc, bias)


def _window_tables(rows):
    rpt = ATT_TILE // GRID_W
    krows = 3 * rpt
    tiles = rows // rpt
    kh = min(WIN_H_MAX, rows)
    rpb_w = 2 * WIN_W - 1
    arow = -np.ones((3, rpt, krows), np.int64)
    for var, j in enumerate((0, 1, tiles - 1)):
        r0 = j * rpt
        k0 = int(np.clip(j - 1, 0, tiles - 3)) * rpt
        for qi in range(rpt):
            qrow = r0 + qi
            rs = int(np.clip(qrow - kh // 2, 0, rows - kh))
            for ki in range(krows):
                krow = k0 + ki
                if rs <= krow < rs + kh:
                    arow[var, qi, ki] = krow - qrow + WIN_H_MAX - 1
    ec = np.zeros((GRID_W, GRID_W, rpb_w), np.float32)
    for qc in range(GRID_W):
        cs = int(np.clip(qc - WIN_W // 2, 0, GRID_W - WIN_W))
        for kc in range(cs, cs + WIN_W):
            ec[qc, kc, int(np.clip(kc - qc, -(WIN_W - 1), WIN_W - 1)) + WIN_W - 1] = 1.0
    return arow.tolist(), ec


def _bias_kernel(arow, u_ref, o_ref):
    var_id = pl.program_id(0)
    for var, table in enumerate(arow):
        @pl.when(var_id == var)
        def _(table=table):
            for hd in range(u_ref.shape[0]):
                for qi, row in enumerate(table):
                    for ki, a in enumerate(row):
                        blk = u_ref[hd, a] if a >= 0 else jnp.full((GRID_W, GRID_W), NEG_BIG, F32)
                        o_ref[0, hd, qi * GRID_W:(qi + 1) * GRID_W, ki * GRID_W:(ki + 1) * GRID_W] = blk


def _window_bias(rpb, rows):
    arow, ec = _window_tables(rows)
    nh = rpb.shape[0]
    col_blocks = jnp.einsum("hab,xyb->haxy", rpb, ec, precision=lax.Precision.HIGHEST)
    col_blocks = jnp.where(ec.sum(-1) > 0.5, col_blocks * LOG2E, NEG_BIG)
    return pl.pallas_call(
        functools.partial(_bias_kernel, arow),
        out_shape=jax.ShapeDtypeStruct((len(arow), nh, ATT_TILE, 3 * ATT_TILE), F32),
        grid=(len(arow),),
        in_specs=[pl.BlockSpec(col_blocks.shape, lambda v: (0, 0, 0, 0))],
        out_specs=pl.BlockSpec((1, nh, ATT_TILE, 3 * ATT_TILE), lambda v: (v, 0, 0, 0)),
        compiler_params=_cparams(1),
        name="window_bias",
    )(col_blocks)


def _fourier_kernel(t_len, u_ref, w1_ref, ct_ref, st_ref, o_ref, p_scr, q_scr):
    @pl.when(pl.program_id(1) == 0)
    def _():
        pq = jnp.dot(u_ref[...], w1_ref[...], preferred_element_type=F32)
        df = p_scr.shape[1]
        p_scr[...] = pq[:, :df].astype(BF16)
        q_scr[...] = pq[:, df:].astype(BF16)

    rt = ct_ref.shape[0]
    for b in range(u_ref.shape[0] // t_len):
        rs = slice(b * t_len, (b + 1) * t_len)
        o = (jnp.dot(ct_ref[...], p_scr[rs, :], preferred_element_type=F32)
             - jnp.dot(st_ref[...], q_scr[rs, :], preferred_element_type=F32))
        o_ref[b * rt:(b + 1) * rt, :] = o.astype(BF16)


def _fourier(u, w1, ct, st, batch, t_len, first_block, row_tile, seqs=1):
    df = u.shape[1]
    steps = t_len // row_tile
    assert seqs == 1 or (steps == 1 and batch % seqs == 0 and first_block % seqs == 0)
    return pl.pallas_call(
        functools.partial(_fourier_kernel, t_len),
        out_shape=jax.ShapeDtypeStruct((batch * t_len, df), BF16),
        grid=(batch // seqs, steps),
        in_specs=[pl.BlockSpec((seqs * t_len, df), lambda b, i: (first_block // seqs + b, 0)),
                  pl.BlockSpec(w1.shape, lambda b, i: (0, 0)),
                  pl.BlockSpec((row_tile, t_len), lambda b, i: (i, 0)),
                  pl.BlockSpec((row_tile, t_len), lambda b, i: (i, 0))],
        out_specs=pl.BlockSpec((seqs * row_tile, df), lambda b, i: (b * steps + i, 0)),
        scratch_shapes=[pltpu.VMEM((seqs * t_len, df), BF16), pltpu.VMEM((seqs * t_len, df), BF16)],
        compiler_params=_cparams(2),
        name=f"fourier_{t_len}",
    )(u, w1, ct, st)


def _dft_tables(t_len):
    c = FGROUP_DIM
    jk = np.outer(np.arange(c), np.arange(c)) % c
    ang = 2.0 * np.pi * jk / c
    eye = np.eye(N_FGROUPS)
    w1 = np.concatenate([np.kron(eye, np.cos(ang)), np.kron(eye, np.sin(ang))], axis=1) / np.sqrt(c)
    tt = np.outer(np.arange(t_len), np.arange(t_len)) % t_len
    angt = 2.0 * np.pi * tt / t_len
    as_bf16 = lambda a: jnp.asarray(a.astype(np.float32), dtype=BF16)
    return as_bf16(w1), as_bf16(np.cos(angt) / np.sqrt(t_len)), as_bf16(np.sin(angt) / np.sqrt(t_len))


def _merge_kernel(n_ctx_tiles, xp_ref, xs_ref, mod_ref, g1_ref, g2_ref, afc_ref, afl_ref, aac_ref, aal_ref,
                  wf_ref, wa_ref, wg_ref, bg_ref, wo_ref, wrt_ref, br_ref,
                  x1_ref, h2_ref, idx_ref, wts_ref, rank_ref, cnt_ref, carry):
    t = pl.program_id(0)
    is_ctx = t < n_ctx_tiles

    @pl.when(t == 0)
    def _():
        carry[...] = jnp.zeros_like(carry)

    sh1 = mod_ref[0, 0:1, :]
    sc1 = mod_ref[0, 1:2, :]
    gt1 = mod_ref[0, 2:3, :]
    sh2 = mod_ref[0, 3:4, :]
    sc2 = mod_ref[0, 4:5, :]
    d = xp_ref.shape[1]
    part_rows = xp_ref.shape[0] // MERGE_PARTS
    parts = [slice(p * part_rows, (p + 1) * part_rows) for p in range(MERGE_PARTS)]
    xs_ = [jnp.where(is_ctx, xp_ref[rs, :], xs_ref[rs, :]) for rs in parts]
    hbs = [(_rms(x) * g1_ref[...] * (1.0 + sc1) + sh1).astype(BF16) for x in xs_]
    gates = [jax.nn.sigmoid(jnp.dot(hb, wg_ref[...], preferred_element_type=F32) + bg_ref[...]) for hb in hbs]
    fas = [jnp.dot(jnp.where(is_ctx, afc_ref[rs, :], afl_ref[rs, :]), wf_ref[...], preferred_element_type=F32)
           for rs in parts]
    fbs = [jnp.dot(jnp.where(is_ctx, aac_ref[rs, :], aal_ref[rs, :]), wa_ref[...], preferred_element_type=F32)
           for rs in parts]
    mixes = [(g[:, :d] * fa + g[:, d:] * fb).astype(BF16) for g, fa, fb in zip(gates, fas, fbs)]
    x1s = [x + gt1 * jnp.dot(mix, wo_ref[...], preferred_element_type=F32) for x, mix in zip(xs_, mixes)]
    h2s = []
    for rs, x1 in zip(parts, x1s):
        x1_ref[rs, :] = x1
        h2 = _rms(x1) * g2_ref[...] * (1.0 + sc2) + sh2
        h2_ref[rs, :] = _pack_halves(h2)
        h2s.append(h2)
    h2 = jnp.concatenate(h2s, axis=0)
    h2_hi = h2.astype(BF16)
    h2_lo = (h2 - h2_hi.astype(F32)).astype(BF16)
    ne = br_ref.shape[0]
    nt_dims = (((1,), (1,)), ((), ()))
    both = lax.dot_general(wrt_ref[...], h2_hi, nt_dims, preferred_element_type=F32)
    cross = lax.dot_general(wrt_ref[:ne, :], h2_lo, nt_dims, preferred_element_type=F32)
    logits_t = both[:ne, :] + both[ne:, :] + cross + br_ref[...]
    idx_o, wts_o, rank_o, counts = _route(logits_t, carry[...])
    idx_ref[...] = idx_o
    wts_ref[...] = wts_o
    rank_ref[...] = rank_o
    carry[...] = counts
    cnt_ref[...] = counts


def _route(logits_t, counts):
    ne, tm = logits_t.shape
    row = lax.broadcasted_iota(jnp.int32, (ne, tm), 0)
    lg = logits_t
    vals, idxs = [], []
    for _ in range(TOP_K):
        m = jnp.max(lg, axis=0, keepdims=True)
        am = jnp.min(jnp.where(lg == m, row, ne), axis=0, keepdims=True)
        vals.append(m)
        idxs.append(am)
        lg = jnp.where(row == am, -jnp.inf, lg)
    es = [jnp.exp(v - vals[0]) for v in vals]
    den = functools.reduce(lambda a, b: a + b, es)
    onehot = functools.reduce(lambda a, b: a + b, [(row == am).astype(F32) for am in idxs])
    s_i = lax.broadcasted_iota(jnp.int32, (tm, tm), 0)
    t_i = lax.broadcasted_iota(jnp.int32, (tm, tm), 1)
    earlier = (s_i < t_i).astype(BF16)
    before = jnp.dot(onehot.astype(BF16), earlier, preferred_element_type=F32) + counts
    out_row = lax.broadcasted_iota(jnp.int32, (SUBLANES, tm), 0)
    idx_o = jnp.zeros((SUBLANES, tm), jnp.int32)
    wts_o = jnp.zeros((SUBLANES, tm), F32)
    rank_o = jnp.zeros((SUBLANES, tm), jnp.int32)
    for kk in range(TOP_K):
        rk = jnp.sum(jnp.where(row == idxs[kk], before, 0.0), axis=0, keepdims=True).astype(jnp.int32)
        idx_o = jnp.where(out_row == kk, idxs[kk], idx_o)
        wts_o = jnp.where(out_row == kk, es[kk] / den, wts_o)
        rank_o = jnp.where(out_row == kk, rk, rank_o)
    return idx_o, wts_o, rank_o, counts + jnp.sum(onehot, axis=1, keepdims=True)


def _merge(xp, xs, mod, g1, g2, af_c, af_l, aa_c, aa_l, wf, wa, wg, bg, wo, wr, br, tiles_per_lat_batch):
    n_ctx, d = xp.shape
    n = n_ctx + xs.shape[0]
    tm = TOK_TILE
    n_ctx_tiles = n_ctx // tm
    last = n_ctx_tiles - 1
    ne = wr.shape[1]
    da = aa_c.shape[1]
    df = af_c.shape[1]

    def mod_row(t):
        return jnp.where(t < n_ctx_tiles, 0, 1 + (t - n_ctx_tiles) // tiles_per_lat_batch)

    cmap = lambda t: (jnp.minimum(t, last), 0)
    lmap = lambda t: (jnp.maximum(t - n_ctx_tiles, 0), 0)
    full = lambda a: pl.BlockSpec(a.shape, lambda t: (0,) * a.ndim)
    tok = lambda w: pl.BlockSpec((tm, w), lambda t: (t, 0))
    per_choice = lambda dt: jax.ShapeDtypeStruct((SUBLANES, n), dt)
    choice_spec = pl.BlockSpec((SUBLANES, tm), lambda t: (0, t))
    wr_hi = wr.astype(BF16)
    wr_lo = (wr - wr_hi.astype(F32)).astype(BF16)
    wr_t = jnp.concatenate([wr_hi.T, wr_lo.T], axis=0)
    return pl.pallas_call(
        functools.partial(_merge_kernel, n_ctx_tiles),
        out_shape=(jax.ShapeDtypeStruct((n, d), F32), jax.ShapeDtypeStruct((n, d // 2), jnp.int32),
                   per_choice(jnp.int32), per_choice(F32), per_choice(jnp.int32),
                   jax.ShapeDtypeStruct((ne, 1), F32)),
        grid=(n // tm,),
        in_specs=[pl.BlockSpec((tm, d), cmap), pl.BlockSpec((tm, d), lmap),
                  pl.BlockSpec((1,) + mod.shape[1:], lambda t: (mod_row(t), 0, 0)),
                  full(g1), full(g2),
                  pl.BlockSpec((tm, df), cmap), pl.BlockSpec((tm, df), lmap),
                  pl.BlockSpec((tm, da), cmap), pl.BlockSpec((tm, da), lmap),
                  full(wf), full(wa), full(wg), full(bg), full(wo), full(wr_t), full(br)],
        out_specs=(tok(d), tok(d // 2), choice_spec, choice_spec, choice_spec,
                   pl.BlockSpec((ne, 1), lambda t: (0, 0))),
        scratch_shapes=[pltpu.VMEM((ne, 1), F32)],
        compiler_params=_cparams(1),
        name="merge",
    )(xp, xs, mod, g1, g2, af_c, af_l, aa_c, aa_l, wf, wa, wg, bg, wo, wr_t, br)


def _pos_kernel(off_ref, idx_ref, rank_ref, o_ref):
    idx = idx_ref[...]
    pos = rank_ref[...]
    for e in range(off_ref.shape[0]):
        pos = pos + jnp.where(idx == e, off_ref[e], 0)
    o_ref[...] = pos


def _positions(first_row, idx, rank):
    rows, n = idx.shape
    blk = POS_COLS
    spec = pl.BlockSpec((rows, blk), lambda j, off: (0, j))
    return pl.pallas_call(
        _pos_kernel,
        out_shape=jax.ShapeDtypeStruct((rows, n), jnp.int32),
        grid_spec=pltpu.PrefetchScalarGridSpec(num_scalar_prefetch=1, grid=(n // blk,),
                                               in_specs=[spec, spec], out_specs=spec),
        compiler_params=_cparams(1),
        name="positions",
    )(first_row, idx, rank)


def _moe_kernel(te_ref, nt_ref, nx_ref, rv_ref, sl_ref, x_ref, w1_hbm, b1_ref, w2_hbm, b2_ref, y_ref,
                w1s, w2s, sem):
    i = pl.program_id(0)
    e = te_ref[i]
    prev = te_ref[jnp.maximum(i - 1, 0)]
    first_of_run = (i == 0) | (e != prev)
    slot = sl_ref[i]

    def stage(expert, s):
        return (pltpu.make_async_copy(w1_hbm.at[expert], w1s.at[s], sem.at[0, s]),
                pltpu.make_async_copy(w2_hbm.at[expert], w2s.at[s], sem.at[1, s]))

    @pl.when(i == 0)
    def _():
        for cp in stage(e, slot):
            cp.start()

    @pl.when(first_of_run)
    def _():
        for cp in stage(e, slot):
            cp.wait()

        @pl.when(nx_ref[i] >= 0)
        def _():
            for cp in stage(nx_ref[i], 1 - slot):
                cp.start()

    def expert_rows(rs):
        dff = w2s.shape[1]
        x_lo, x_hi = _unpack_halves(x_ref[rs, :])
        x = jnp.concatenate([x_lo.astype(BF16), x_hi.astype(BF16)], axis=1)
        gu = jnp.dot(x, w1s[slot].astype(BF16), preferred_element_type=F32) + b1_ref[0]
        gate = jnp.minimum(gu[:, :dff], SWIGLU_LIMIT)
        up = jnp.clip(gu[:, dff:], -SWIGLU_LIMIT, SWIGLU_LIMIT)
        glu = gate * jax.nn.sigmoid(SWIGLU_ALPHA * gate)
        act = ((up + 1.0) * glu).astype(BF16)
        y = jnp.dot(act, w2s[slot].astype(BF16), preferred_element_type=F32) + b2_ref[0]
        y_ref[rs, :] = _pack_halves(y)

    tm = x_ref.shape[0]
    rv = rv_ref[i]

    @pl.when(rv == tm)
    def _():
        expert_rows(slice(0, tm))

    for sub in range(tm // MOE_SUB):
        @pl.when((rv < tm) & (rv > sub * MOE_SUB))
        def _(sub=sub):
            expert_rows(slice(sub * MOE_SUB, (sub + 1) * MOE_SUB))


def _moe(tile_expert, n_tiles, next_expert, rows_valid, slots, xs_sorted, w1, b1, w2, b2):
    p, dh = xs_sorted.shape
    ne, d, dff2 = w1.shape
    dff = w2.shape[1]
    tm = MOE_TILE
    max_tiles = p // tm

    def row_map(i, te, nt, nx, rv, sl):
        return (jnp.minimum(i, nt[0] - 1), 0)

    def b_map(i, te, nt, nx, rv, sl):
        return (te[i], 0, 0)

    grid_spec = pltpu.PrefetchScalarGridSpec(
        num_scalar_prefetch=5,
        grid=(max_tiles,),
        in_specs=[pl.BlockSpec((tm, dh), row_map),
                  pl.BlockSpec(memory_space=pl.ANY),
                  pl.BlockSpec((1, 1, dff2), b_map),
                  pl.BlockSpec(memory_space=pl.ANY),
                  pl.BlockSpec((1, 1, d), b_map)],
        out_specs=pl.BlockSpec((tm, dh), row_map),
        scratch_shapes=[pltpu.VMEM((2, d, dff2), F32), pltpu.VMEM((2, dff, d), F32),
                        pltpu.SemaphoreType.DMA((2, 2))],
    )
    return pl.pallas_call(
        _moe_kernel,
        out_shape=jax.ShapeDtypeStruct((p, dh), jnp.int32),
        grid_spec=grid_spec,
        compiler_params=_cparams(1),
        name="moe",
    )(tile_expert, n_tiles, next_expert, rows_valid, slots, xs_sorted,
      w1, b1.reshape(ne, 1, dff2), w2, b2.reshape(ne, 1, d))


def _sc_mesh():
    return plsc.VectorSubcoreMesh(core_axis_name="c", subcore_axis_name="s",
                                  num_cores=SC_CORES, num_subcores=SC_SUBCORES)


def _load_index_lists(pos_hbm, idx_v, isem, base, n_chunks, chunk):
    copies = [pltpu.make_async_copy(pos_hbm.at[kk, pl.ds(base + j * chunk, chunk)], idx_v.at[j * TOP_K + kk], isem)
              for j in range(n_chunks) for kk in range(TOP_K)]
    for cp in copies:
        cp.start()
    for cp in copies:
        cp.wait()


def _sc_dispatch(h, pos, p_rows):
    n, d = h.shape
    nw = SC_CORES * SC_SUBCORES
    chunk = SC_CHUNK
    assert n % (nw * chunk * 2) == 0
    tok_w = n // nw
    n_chunks = tok_w // chunk
    items = n_chunks * TOP_K

    @functools.partial(
        pl.kernel, mesh=_sc_mesh(),
        out_type=jax.ShapeDtypeStruct((p_rows, d), h.dtype),
        scratch_types=[pltpu.VMEM((items, chunk), jnp.int32),
                       pltpu.VMEM((2, chunk, d), h.dtype),
                       pltpu.SemaphoreType.DMA((2,)), pltpu.SemaphoreType.DMA((2,)), pltpu.SemaphoreType.DMA],
        name="sc_dispatch",
    )
    def run(h_hbm, pos_hbm, out_hbm, idx_v, rows_v, lsem, ssem, isem):
        wid = lax.axis_index("s") * SC_CORES + lax.axis_index("c")
        base = wid * tok_w
        _load_index_lists(pos_hbm, idx_v, isem, base, n_chunks, chunk)

        def load(j, slot):
            return pltpu.make_async_copy(h_hbm.at[pl.ds(base + j * chunk, chunk)], rows_v.at[slot], lsem.at[slot])

        def scat(j, kk, slot):
            return pltpu.make_async_copy(rows_v.at[slot], out_hbm.at[idx_v.at[j * TOP_K + kk]], ssem.at[slot])

        load(0, 0).start()

        @pl.loop(0, n_chunks, step=2)
        def _(j0):
            for b in range(2):
                j = j0 + b

                @pl.when(j >= 1)
                def _():
                    for kk in range(TOP_K):
                        scat(j - 1, kk, 1 - b).wait()

                @pl.when(j + 1 < n_chunks)
                def _():
                    load(j + 1, 1 - b).start()

                load(j, b).wait()
                for kk in range(TOP_K):
                    scat(j, kk, b).start()

        for kk in range(TOP_K):
            scat(n_chunks - 1, kk, (n_chunks - 1) % 2).wait()

    return run(h, pos)


def _sc_gather(y, pos):
    d = y.shape[1]
    n = pos.shape[1]
    nw = SC_CORES * SC_SUBCORES
    chunk = SC_CHUNK
    assert n % (nw * chunk * 2) == 0
    tok_w = n // nw
    n_chunks = tok_w // chunk
    items = n_chunks * TOP_K

    @functools.partial(
        pl.kernel, mesh=_sc_mesh(),
        out_type=jax.ShapeDtypeStruct((TOP_K, n, d), y.dtype),
        scratch_types=[pltpu.VMEM((items, chunk), jnp.int32),
                       pltpu.VMEM((2, chunk, d), y.dtype),
                       pltpu.SemaphoreType.DMA((2,)), pltpu.SemaphoreType.DMA((2,)), pltpu.SemaphoreType.DMA],
        name="sc_gather",
    )
    def run(y_hbm, pos_hbm, out_hbm, idx_v, rows_v, gsem, wsem, isem):
        wid = lax.axis_index("s") * SC_CORES + lax.axis_index("c")
        base = wid * tok_w
        _load_index_lists(pos_hbm, idx_v, isem, base, n_chunks, chunk)

        def gather(i, slot):
            return pltpu.make_async_copy(y_hbm.at[idx_v.at[i]], rows_v.at[slot], gsem.at[slot])

        def write(i, slot):
            dst = out_hbm.at[i % TOP_K, pl.ds(base + (i // TOP_K) * chunk, chunk)]
            return pltpu.make_async_copy(rows_v.at[slot], dst, wsem.at[slot])

        gather(0, 0).start()

        @pl.loop(0, items, step=2)
        def _(i0):
            for b in range(2):
                i = i0 + b

                @pl.when(i >= 1)
                def _():
                    write(i - 1, 1 - b).wait()

                @pl.when(i + 1 < items)
                def _():
                    gather(i + 1, 1 - b).start()

                gather(i, b).wait()
                write(i, b).start()

        write(items - 1, (items - 1) % 2).wait()

    return run(y, pos)


def _combine_kernel(n_ctx_tiles, x1_ref, yg_ref, wts_ref, mod_ref, op_ref, os_ref):
    t = pl.program_id(0)
    gt2 = mod_ref[0, 5:6, :]
    w = wts_ref[...]
    acc_lo, acc_hi = None, None
    for kk in range(TOP_K):
        y_lo, y_hi = _unpack_halves(yg_ref[kk])
        wk = w[:, kk:kk + 1]
        acc_lo = wk * y_lo if acc_lo is None else acc_lo + wk * y_lo
        acc_hi = wk * y_hi if acc_hi is None else acc_hi + wk * y_hi
    out = x1_ref[...] + gt2 * jnp.concatenate([acc_lo, acc_hi], axis=1)

    @pl.when(t < n_ctx_tiles)
    def _():
        op_ref[...] = out

    @pl.when(t >= n_ctx_tiles)
    def _():
        os_ref[...] = out


def _combine(x1, yg, wts, mod, n_ctx, lat_seq):
    n, d = x1.shape
    tm = COMBINE_TILE
    assert n_ctx % tm == 0 and lat_seq % tm == 0
    tiles_per_lat_batch = lat_seq // tm
    n_ctx_tiles = n_ctx // tm
    last = n_ctx_tiles - 1

    def mod_row(t):
        return jnp.where(t < n_ctx_tiles, 0, 1 + (t - n_ctx_tiles) // tiles_per_lat_batch)

    return pl.pallas_call(
        functools.partial(_combine_kernel, n_ctx_tiles),
        out_shape=(jax.ShapeDtypeStruct((n_ctx, d), F32), jax.ShapeDtypeStruct((n - n_ctx, d), F32)),
        grid=(n // tm,),
        in_specs=[pl.BlockSpec((tm, d), lambda t: (t, 0)),
                  pl.BlockSpec((TOP_K, tm, d // 2), lambda t: (0, t, 0)),
                  pl.BlockSpec((tm, TOP_K), lambda t: (t, 0)),
                  pl.BlockSpec((1,) + mod.shape[1:], lambda t: (mod_row(t), 0, 0))],
        out_specs=(pl.BlockSpec((tm, d), lambda t: (jnp.minimum(t, last), 0)),
                   pl.BlockSpec((tm, d), lambda t: (jnp.maximum(t - n_ctx_tiles, 0), 0))),
        compiler_params=_cparams(1),
        name="combine",
    )(x1, yg, wts, mod)


def kernel(x_prompt, x_sample, cache_k, cache_v, c, c_ctx, g_norm1, w_ada, b_ada, w_in, g_q, g_k, rpb,
           w_fmap, w_amap, w_gate, b_gate, w_out, g_norm2, w_router, b_router, w1, b1, w2, b2):
    batch, seq, d = x_prompt.shape
    dec_batch, dec_seq, _ = x_sample.shape
    assert w_ada.shape[0] == 1, "single-layer trunk"
    rows = dec_seq // GRID_W
    da = N_HEADS * HEAD_DIM
    n_ctx = batch * seq
    n_lat = dec_batch * dec_seq
    n = n_ctx + n_lat
    ne = w_router.shape[2]
    assert TOK_TILE % seq == 0 and n_ctx % TOK_TILE == 0 and dec_seq % TOK_TILE == 0
    assert dec_seq % FOURIER_TILE == 0 and n_ctx % dec_seq == 0
    assert rows % (ATT_TILE // GRID_W) == 0 and rows >= 3 * (ATT_TILE // GRID_W) and rows >= WIN_H_MAX
    assert seq == ATT_TILE and cache_k.shape[3] == ATT_TILE
    tiles_per_lat_batch = dec_seq // TOK_TILE

    n_mod_rows = -(-(1 + dec_batch) // SUBLANES) * SUBLANES
    cvecs = jnp.zeros((n_mod_rows, d), F32).at[0].set(c_ctx).at[1:1 + dec_batch].set(c)
    mod = _ada(cvecs, w_ada[0], b_ada[0]).reshape(n_mod_rows, 6, d)

    xp = x_prompt.reshape(n_ctx, d)
    xs = x_sample.reshape(n_lat, d)
    g1 = g_norm1[0].reshape(1, d)
    g2 = g_norm2[0].reshape(1, d)
    hsum = jnp.asarray(np.kron(np.eye(N_HEADS), np.full((HEAD_DIM, HEAD_DIM), 1.0 / HEAD_DIM)), BF16)
    gq_t = jnp.tile(g_q[0], N_HEADS).reshape(1, da)
    gk_t = jnp.tile(g_k[0], N_HEADS).reshape(1, da)

    u, q, k, v, new_k, new_v = _proj(xp, xs, mod, g1, w_in[0].astype(BF16), hsum, gq_t, gk_t,
                                      batch, seq, tiles_per_lat_batch)

    aa_c = _attn_ctx(q, k, v, batch, seq)
    bias = _window_bias(rpb[0], rows)
    past = cache_k.shape[3]
    kc = cache_k[:, 0].transpose(0, 2, 1, 3).reshape(dec_batch, past, da).astype(BF16)
    vc = cache_v[:, 0].transpose(0, 2, 1, 3).reshape(dec_batch, past, da).astype(BF16)
    aa_l = _attn_lat(q, k, v, kc, vc, bias, n_ctx, dec_batch, rows)

    w1c, ct_c, st_c = _dft_tables(seq)
    _, ct_l, st_l = _dft_tables(dec_seq)
    af_c = _fourier(u, w1c, ct_c, st_c, batch, seq, 0, seq, CTX_SEQS_PER_STEP)
    af_l = _fourier(u, w1c, ct_l, st_l, dec_batch, dec_seq, n_ctx // dec_seq, FOURIER_TILE)

    x1, h2, idx, wts, rank, counts = _merge(
        xp, xs, mod, g1, g2, af_c, af_l, aa_c, aa_l,
        w_fmap[0].astype(BF16), w_amap[0].astype(BF16), w_gate[0].astype(BF16), b_gate[0].reshape(1, -1),
        w_out[0].astype(BF16), w_router[0], b_router[0].reshape(ne, 1), tiles_per_lat_batch)

    tm = MOE_TILE
    max_tiles = (n * TOP_K) // tm + ne
    cnt = counts[:, 0].astype(jnp.int32)
    tiles_e = (cnt + tm - 1) // tm
    tile_end = jnp.cumsum(tiles_e)
    pad_off = (tile_end - tiles_e) * tm
    n_tiles = tile_end[-1:]
    tile_ids = jnp.arange(max_tiles, dtype=jnp.int32)
    tile_expert = jnp.sum((tile_ids[:, None] >= tile_end[None, :]).astype(jnp.int32), axis=1)
    last_e = jnp.sum((n_tiles - 1 >= tile_end).astype(jnp.int32))
    tile_expert = jnp.minimum(tile_expert, last_e).astype(jnp.int32)
    assert n % POS_COLS == 0
    pos = _positions(pad_off.astype(jnp.int32), idx, rank)

    p_rows = max_tiles * tm
    xs_sorted = _sc_dispatch(h2, pos, p_rows)
    e_ids = jnp.arange(ne, dtype=jnp.int32)
    later_used = (e_ids[None, :] > e_ids[:, None]) & (tiles_e[None, :] > 0)
    next_of_e = jnp.min(jnp.where(later_used, e_ids[None, :], ne), axis=1)
    next_of_e = jnp.where(next_of_e == ne, -1, next_of_e)
    is_e = tile_expert[:, None] == e_ids[None, :]
    next_expert = jnp.sum(jnp.where(is_e, next_of_e[None, :], 0), axis=1)
    rows_left = jnp.sum(jnp.where(is_e, (cnt + pad_off)[None, :], 0), axis=1) - tile_ids * tm
    rows_valid = jnp.where(tile_ids < n_tiles[0], jnp.clip(rows_left, 0, tm), 0)
    run_of_e = jnp.cumsum((tiles_e > 0).astype(jnp.int32)) - 1
    slots = jnp.sum(jnp.where(is_e, run_of_e[None, :], 0), axis=1) % 2
    y_sorted = _moe(tile_expert, n_tiles.astype(jnp.int32), next_expert.astype(jnp.int32),
                    rows_valid.astype(jnp.int32), slots.astype(jnp.int32), xs_sorted,
                    w1[0], b1[0], w2[0], b2[0])
    yg = _sc_gather(y_sorted, pos)
    y_p, y_s = _combine(x1, yg, wts[:TOP_K].T, mod, n_ctx, dec_seq)
    return (y_p.reshape(batch, seq, d), y_s.reshape(dec_batch, dec_seq, d), new_k, new_v)
```

```python
import functools
import math

import numpy as np
import jax
import jax.numpy as jnp
from jax import lax
from jax.experimental import pallas as pl
from jax.experimental.pallas import tpu as pltpu
from jax.experimental.pallas import tpu_sc as plsc

F32 = jnp.float32
BF16 = jnp.bfloat16

GRID_W = 64
N_HEADS = 8
HEAD_DIM = 64
N_FGROUPS = 4
FGROUP_DIM = 128
WIN_H_MAX = 8
WIN_W = 16
TOP_K = 4
SWIGLU_LIMIT = 7.0
SWIGLU_ALPHA = 1.702
EPS = 1e-6
LOG2E = math.log2(math.e)
QK_PRESCALE = HEAD_DIM ** -0.5 * LOG2E

LANES = 128
SUBLANES = 8
ADA_COLS = 1024
POS_COLS = 2048
TOK_TILE = 512
COMBINE_TILE = 1024
FOURIER_TILE = 1024
ATT_TILE = 256
CTX_SEQS_PER_STEP = 4
MOE_TILE = 1024
MOE_SUB = 256
MERGE_PARTS = 2
VMEM_LIMIT = 56 * 1024 * 1024
NEG_BIG = -1e30

SC_CORES = 2
SC_SUBCORES = 16
SC_CHUNK = 64


def _cparams(n_axes, vmem=VMEM_LIMIT):
    return pltpu.CompilerParams(dimension_semantics=("arbitrary",) * n_axes, vmem_limit_bytes=vmem)


def _rms(x):
    return x * lax.rsqrt(jnp.mean(x * x, axis=-1, keepdims=True) + EPS)


def _pack_halves(x):
    c = x.shape[1] // 2
    lo = lax.bitcast_convert_type(x[:, :c].astype(BF16).astype(F32), jnp.uint32)
    hi = lax.bitcast_convert_type(x[:, c:].astype(BF16).astype(F32), jnp.uint32)
    return lax.bitcast_convert_type(hi | (lo >> 16), jnp.int32)


def _unpack_halves(w):
    u = lax.bitcast_convert_type(w, jnp.uint32)
    lo = lax.bitcast_convert_type(u << 16, F32)
    hi = lax.bitcast_convert_type(u & jnp.uint32(0xFFFF0000), F32)
    return lo, hi


def _ada_kernel(c_ref, w_ref, b_ref, o_ref):
    cv = c_ref[...]
    s = cv * jax.nn.sigmoid(cv)
    w = w_ref[...]
    s_hi = s.astype(BF16)
    s_lo = (s - s_hi.astype(F32)).astype(BF16)
    w_hi = w.astype(BF16)
    w_lo = (w - w_hi.astype(F32)).astype(BF16)
    o_ref[...] = (jnp.dot(s_hi, w_hi, preferred_element_type=F32)
                  + jnp.dot(s_hi, w_lo, preferred_element_type=F32)
                  + jnp.dot(s_lo, w_hi, preferred_element_type=F32) + b_ref[...])


def _ada(cvecs, w_ada, b_ada):
    rows, d = cvecs.shape
    n = w_ada.shape[1]
    blk = ADA_COLS
    return pl.pallas_call(
        _ada_kernel,
        out_shape=jax.ShapeDtypeStruct((rows, n), F32),
        grid=(n // blk,),
        in_specs=[pl.BlockSpec((rows, d), lambda j: (0, 0)),
                  pl.BlockSpec((d, blk), lambda j: (0, j)),
                  pl.BlockSpec((1, blk), lambda j: (0, j))],
        out_specs=pl.BlockSpec((rows, blk), lambda j: (0, j)),
        compiler_params=_cparams(1),
        name="ada",
    )(cvecs, w_ada, b_ada.reshape(1, n))


def _proj_kernel(n_ctx_tiles, xp_ref, xs_ref, mod_ref, g1_ref, win_ref, hsum_ref, gq_ref, gk_ref,
                 u_ref, q_ref, k_ref, v_ref, nk_ref, nv_ref):
    t = pl.program_id(0)
    is_ctx = t < n_ctx_tiles
    sh1 = mod_ref[0, 0:1, :]
    sc1 = mod_ref[0, 1:2, :]
    da = q_ref.shape[1]
    df = u_ref.shape[1]
    nb, _, nh, s, dh = nk_ref.shape
    parts = [slice(b * s, (b + 1) * s) for b in range(nb)]
    hs = [(_rms(jnp.where(is_ctx, xp_ref[rs, :], xs_ref[rs, :])) * g1_ref[...] * (1.0 + sc1) + sh1).astype(BF16)
          for rs in parts]
    projs = [jnp.dot(h, win_ref[...], preferred_element_type=F32) for h in hs]
    kns, vs = [], []
    for rs, proj in zip(parts, projs):
        u_ref[rs, :] = proj[:, :df].astype(BF16)
        q = proj[:, df:df + da]
        k = proj[:, df + da:df + 2 * da]
        v = proj[:, df + 2 * da:]
        msq = jnp.dot((q * q).astype(BF16), hsum_ref[...], preferred_element_type=F32)
        msk = jnp.dot((k * k).astype(BF16), hsum_ref[...], preferred_element_type=F32)
        qn = q * lax.rsqrt(msq + EPS) * gq_ref[...]
        kn = k * lax.rsqrt(msk + EPS) * gk_ref[...]
        q_ref[rs, :] = (qn * QK_PRESCALE).astype(BF16)
        k_ref[rs, :] = kn.astype(BF16)
        v_ref[rs, :] = v.astype(BF16)
        kns.append(kn)
        vs.append(v)

    @pl.when(is_ctx)
    def _():
        for b in range(nb):
            for hd in range(nh):
                nk_ref[b, 0, hd] = kns[b][:, hd * dh:(hd + 1) * dh]
                nv_ref[b, 0, hd] = vs[b][:, hd * dh:(hd + 1) * dh]


def _proj(xp, xs, mod, g1, w_in_b, hsum, gq_t, gk_t, batch, seq, tiles_per_lat_batch):
    n_ctx, d = xp.shape
    n_lat = xs.shape[0]
    tm = TOK_TILE
    n_ctx_tiles = n_ctx // tm
    n_tiles = (n_ctx + n_lat) // tm
    n = n_ctx + n_lat
    d_in = w_in_b.shape[1]
    da = N_HEADS * HEAD_DIM
    df = d_in - 3 * da
    bpt = tm // seq
    last = n_ctx_tiles - 1

    def mod_row(t):
        return jnp.where(t < n_ctx_tiles, 0, 1 + (t - n_ctx_tiles) // tiles_per_lat_batch)

    tok = lambda w: pl.BlockSpec((tm, w), lambda t: (t, 0))
    full = lambda a: pl.BlockSpec(a.shape, lambda t: (0,) * a.ndim)
    kv_spec = pl.BlockSpec((bpt, 1, N_HEADS, seq, HEAD_DIM), lambda t: (jnp.minimum(t, last), 0, 0, 0, 0))
    kv_shape = jax.ShapeDtypeStruct((batch, 1, N_HEADS, seq, HEAD_DIM), F32)
    return pl.pallas_call(
        functools.partial(_proj_kernel, n_ctx_tiles),
        out_shape=(jax.ShapeDtypeStruct((n, df), BF16),) + (jax.ShapeDtypeStruct((n, da), BF16),) * 3
        + (kv_shape, kv_shape),
        grid=(n_tiles,),
        in_specs=[pl.BlockSpec((tm, d), lambda t: (jnp.minimum(t, last), 0)),
                  pl.BlockSpec((tm, d), lambda t: (jnp.maximum(t - n_ctx_tiles, 0), 0)),
                  pl.BlockSpec((1,) + mod.shape[1:], lambda t: (mod_row(t), 0, 0)),
                  full(g1), full(w_in_b), full(hsum), full(gq_t), full(gk_t)],
        out_specs=(tok(df), tok(da), tok(da), tok(da), kv_spec, kv_spec),
        compiler_params=_cparams(1),
        name="proj",
    )(xp, xs, mod, g1, w_in_b, hsum, gq_t, gk_t)


def _softmax_pv(q2, key_blocks, val_blocks, bias_blocks):
    lane = lax.broadcasted_iota(jnp.int32, (1, LANES), 1)
    masks = [lane < HEAD_DIM, lane >= HEAD_DIM]

    def head_scores(half):
        qh = jnp.where(masks[half], q2, jnp.zeros_like(q2))
        scores = []
        for kb, bb in zip(key_blocks, bias_blocks):
            s = lax.dot_general(qh, kb, (((1,), (1,)), ((), ())), preferred_element_type=F32)
            if bb is not None:
                s = s + bb[half]
            scores.append(s)
        return scores

    def head_out(scores):
        m = jnp.max(functools.reduce(jnp.maximum, scores), axis=-1, keepdims=True)
        ps = [jnp.exp2(s - m) for s in scores]
        denom = jnp.sum(functools.reduce(lambda a, b: a + b, ps), axis=-1, keepdims=True)
        o = functools.reduce(lambda a, b: a + b,
                             [jnp.dot(p.astype(BF16), vb, preferred_element_type=F32)
                              for p, vb in zip(ps, val_blocks)])
        return o / denom

    if len(key_blocks) > 1:
        outs = [head_out(s) for s in [head_scores(0), head_scores(1)]]
    else:
        outs = [head_out(head_scores(half)) for half in range(2)]
    return jnp.where(masks[0], outs[0], outs[1])


def _attn_ctx_kernel(seq, q_ref, k_ref, v_ref, o_ref):
    for b in range(q_ref.shape[0] // seq):
        rs = slice(b * seq, (b + 1) * seq)
        for p in range(q_ref.shape[1] // LANES):
            sl = slice(p * LANES, (p + 1) * LANES)
            o = _softmax_pv(q_ref[rs, sl], [k_ref[rs, sl]], [v_ref[rs, sl]], [None])
            o_ref[rs, sl] = o.astype(BF16)


def _attn_ctx(q, k, v, batch, seq):
    da = q.shape[1]
    assert batch % CTX_SEQS_PER_STEP == 0
    spec = pl.BlockSpec((CTX_SEQS_PER_STEP * seq, da), lambda b: (b, 0))
    return pl.pallas_call(
        functools.partial(_attn_ctx_kernel, seq),
        out_shape=jax.ShapeDtypeStruct((batch * seq, da), BF16),
        grid=(batch // CTX_SEQS_PER_STEP,),
        in_specs=[spec, spec, spec],
        out_specs=spec,
        compiler_params=_cparams(1),
        name="attn_ctx",
    )(q, k, v)


def _attn_lat_kernel(live_blocks, q_ref, k0_ref, k1_ref, k2_ref, v0_ref, v1_ref, v2_ref, kc_ref, vc_ref,
                     bias_ref, o_ref):
    tk = k0_ref.shape[0]
    k_refs, v_refs = (k0_ref, k1_ref, k2_ref), (v0_ref, v1_ref, v2_ref)
    n_tiles = pl.num_programs(0)
    j = pl.program_id(0)
    variant = jnp.where(j == 0, 0, jnp.where(j == n_tiles - 1, 2, 1))

    def tile(blocks):
        for p in range(q_ref.shape[1] // LANES):
            sl = slice(p * LANES, (p + 1) * LANES)
            keys = [k_refs[d][:, sl] for d in blocks] + [kc_ref[0, :, sl]]
            vals = [v_refs[d][:, sl] for d in blocks] + [vc_ref[0, :, sl]]
            biases = [[bias_ref[0, 2 * p + half, :, d * tk:(d + 1) * tk] for half in range(2)] for d in blocks]
            o = _softmax_pv(q_ref[:, sl], keys, vals, biases + [None])
            o_ref[:, sl] = o.astype(BF16)

    for var, blocks in enumerate(live_blocks):
        @pl.when(variant == var)
        def _(blocks=blocks):
            tile(blocks)


def _attn_lat(q, k, v, kc, vc, bias, n_ctx, dec_batch, rows):
    da = q.shape[1]
    tq = ATT_TILE
    rows_per_tile = tq // GRID_W
    tiles = rows // rows_per_tile
    arow, _ = _window_tables(rows)
    live_blocks = tuple(
        tuple(d for d in range(3)
              if any(a >= 0 for row in table for a in row[d * rows_per_tile:(d + 1) * rows_per_tile]))
        for table in arow)
    base = n_ctx // tq
    n_lat = dec_batch * rows * GRID_W
    max_start = tiles - 3

    def qmap(j, b):
        return (base + b * tiles + j, 0)

    def kmap(d):
        return lambda j, b: (base + b * tiles + jnp.clip(j - 1, 0, max_start) + d, 0)

    def bmap(j, b):
        return (jnp.where(j == 0, 0, jnp.where(j == tiles - 1, 2, 1)), 0, 0, 0)

    blk = lambda m: pl.BlockSpec((tq, da), m)
    cspec = pl.BlockSpec((1,) + kc.shape[1:], lambda j, b: (b, 0, 0))
    return pl.pallas_call(
        functools.partial(_attn_lat_kernel, live_blocks),
        out_shape=jax.ShapeDtypeStruct((n_lat, da), BF16),
        grid=(tiles, dec_batch),
        in_specs=[blk(qmap)] + [blk(kmap(d)) for d in range(3)] + [blk(kmap(d)) for d in range(3)]
        + [cspec, cspec, pl.BlockSpec((1,) + bias.shape[1:], bmap)],
        out_specs=pl.BlockSpec((tq, da), lambda j, b: (b * tiles + j, 0)),
        compiler_params=_cparams(2),
        name="attn_lat",
    )(q, k, k, k, v, v, v, kc, vc, bias)


def _window_tables(rows):
    rpt = ATT_TILE // GRID_W
    krows = 3 * rpt
    tiles = rows // rpt
    kh = min(WIN_H_MAX, rows)
    rpb_w = 2 * WIN_W - 1
    arow = -np.ones((3, rpt, krows), np.int64)
    for var, j in enumerate((0, 1, tiles - 1)):
        r0 = j * rpt
        k0 = int(np.clip(j - 1, 0, tiles - 3)) * rpt
        for qi in range(rpt):
            qrow = r0 + qi
            rs = int(np.clip(qrow - kh // 2, 0, rows - kh))
            for ki in range(krows):
                krow = k0 + ki
                if rs <= krow < rs + kh:
                    arow[var, qi, ki] = krow - qrow + WIN_H_MAX - 1
    ec = np.zeros((GRID_W, GRID_W, rpb_w), np.float32)
    for qc in range(GRID_W):
        cs = int(np.clip(qc - WIN_W // 2, 0, GRID_W - WIN_W))
        for kc in range(cs, cs + WIN_W):
            ec[qc, kc, int(np.clip(kc - qc, -(WIN_W - 1), WIN_W - 1)) + WIN_W - 1] = 1.0
    return arow.tolist(), ec


def _bias_kernel(arow, u_ref, o_ref):
    var_id = pl.program_id(0)
    for var, table in enumerate(arow):
        @pl.when(var_id == var)
        def _(table=table):
            for hd in range(u_ref.shape[0]):
                for qi, row in enumerate(table):
                    for ki, a in enumerate(row):
                        blk = u_ref[hd, a] if a >= 0 else jnp.full((GRID_W, GRID_W), NEG_BIG, F32)
                        o_ref[0, hd, qi * GRID_W:(qi + 1) * GRID_W, ki * GRID_W:(ki + 1) * GRID_W] = blk


def _window_bias(rpb, rows):
    arow, ec = _window_tables(rows)
    nh = rpb.shape[0]
    col_blocks = jnp.einsum("hab,xyb->haxy", rpb, ec, precision=lax.Precision.HIGHEST)
    col_blocks = jnp.where(ec.sum(-1) > 0.5, col_blocks * LOG2E, NEG_BIG)
    return pl.pallas_call(
        functools.partial(_bias_kernel, arow),
        out_shape=jax.ShapeDtypeStruct((len(arow), nh, ATT_TILE, 3 * ATT_TILE), F32),
        grid=(len(arow),),
        in_specs=[pl.BlockSpec(col_blocks.shape, lambda v: (0, 0, 0, 0))],
        out_specs=pl.BlockSpec((1, nh, ATT_TILE, 3 * ATT_TILE), lambda v: (v, 0, 0, 0)),
        compiler_params=_cparams(1),
        name="window_bias",
    )(col_blocks)


def _fourier_kernel(t_len, u_ref, w1_ref, ct_ref, st_ref, o_ref, p_scr, q_scr):
    @pl.when(pl.program_id(1) == 0)
    def _():
        pq = jnp.dot(u_ref[...], w1_ref[...], preferred_element_type=F32)
        df = p_scr.shape[1]
        p_scr[...] = pq[:, :df].astype(BF16)
        q_scr[...] = pq[:, df:].astype(BF16)

    rt = ct_ref.shape[0]
    for b in range(u_ref.shape[0] // t_len):
        rs = slice(b * t_len, (b + 1) * t_len)
        o = (jnp.dot(ct_ref[...], p_scr[rs, :], preferred_element_type=F32)
             - jnp.dot(st_ref[...], q_scr[rs, :], preferred_element_type=F32))
        o_ref[b * rt:(b + 1) * rt, :] = o.astype(BF16)


def _fourier(u, w1, ct, st, batch, t_len, first_block, row_tile, seqs=1):
    df = u.shape[1]
    steps = t_len // row_tile
    assert seqs == 1 or (steps == 1 and batch % seqs == 0 and first_block % seqs == 0)
    return pl.pallas_call(
        functools.partial(_fourier_kernel, t_len),
        out_shape=jax.ShapeDtypeStruct((batch * t_len, df), BF16),
        grid=(batch // seqs, steps),
        in_specs=[pl.BlockSpec((seqs * t_len, df), lambda b, i: (first_block // seqs + b, 0)),
                  pl.BlockSpec(w1.shape, lambda b, i: (0, 0)),
                  pl.BlockSpec((row_tile, t_len), lambda b, i: (i, 0)),
                  pl.BlockSpec((row_tile, t_len), lambda b, i: (i, 0))],
        out_specs=pl.BlockSpec((seqs * row_tile, df), lambda b, i: (b * steps + i, 0)),
        scratch_shapes=[pltpu.VMEM((seqs * t_len, df), BF16), pltpu.VMEM((seqs * t_len, df), BF16)],
        compiler_params=_cparams(2),
        name=f"fourier_{t_len}",
    )(u, w1, ct, st)


def _dft_tables(t_len):
    c = FGROUP_DIM
    jk = np.outer(np.arange(c), np.arange(c)) % c
    ang = 2.0 * np.pi * jk / c
    eye = np.eye(N_FGROUPS)
    w1 = np.concatenate([np.kron(eye, np.cos(ang)), np.kron(eye, np.sin(ang))], axis=1) / np.sqrt(c)
    tt = np.outer(np.arange(t_len), np.arange(t_len)) % t_len
    angt = 2.0 * np.pi * tt / t_len
    as_bf16 = lambda a: jnp.asarray(a.astype(np.float32), dtype=BF16)
    return as_bf16(w1), as_bf16(np.cos(angt) / np.sqrt(t_len)), as_bf16(np.sin(angt) / np.sqrt(t_len))


def _merge_kernel(n_ctx_tiles, xp_ref, xs_ref, mod_ref, g1_ref, g2_ref, afc_ref, afl_ref, aac_ref, aal_ref,
                  wf_ref, wa_ref, wg_ref, bg_ref, wo_ref, wrt_ref, br_ref,
                  x1_ref, h2_ref, idx_ref, wts_ref, rank_ref, cnt_ref, carry):
    t = pl.program_id(0)
    is_ctx = t < n_ctx_tiles

    @pl.when(t == 0)
    def _():
        carry[...] = jnp.zeros_like(carry)

    sh1 = mod_ref[0, 0:1, :]
    sc1 = mod_ref[0, 1:2, :]
    gt1 = mod_ref[0, 2:3, :]
    sh2 = mod_ref[0, 3:4, :]
    sc2 = mod_ref[0, 4:5, :]
    d = xp_ref.shape[1]
    part_rows = xp_ref.shape[0] // MERGE_PARTS
    parts = [slice(p * part_rows, (p + 1) * part_rows) for p in range(MERGE_PARTS)]
    xs_ = [jnp.where(is_ctx, xp_ref[rs, :], xs_ref[rs, :]) for rs in parts]
    hbs = [(_rms(x) * g1_ref[...] * (1.0 + sc1) + sh1).astype(BF16) for x in xs_]
    gates = [jax.nn.sigmoid(jnp.dot(hb, wg_ref[...], preferred_element_type=F32) + bg_ref[...]) for hb in hbs]
    fas = [jnp.dot(jnp.where(is_ctx, afc_ref[rs, :], afl_ref[rs, :]), wf_ref[...], preferred_element_type=F32)
           for rs in parts]
    fbs = [jnp.dot(jnp.where(is_ctx, aac_ref[rs, :], aal_ref[rs, :]), wa_ref[...], preferred_element_type=F32)
           for rs in parts]
    mixes = [(g[:, :d] * fa + g[:, d:] * fb).astype(BF16) for g, fa, fb in zip(gates, fas, fbs)]
    x1s = [x + gt1 * jnp.dot(mix, wo_ref[...], preferred_element_type=F32) for x, mix in zip(xs_, mixes)]
    h2s = []
    for rs, x1 in zip(parts, x1s):
        x1_ref[rs, :] = x1
        h2 = _rms(x1) * g2_ref[...] * (1.0 + sc2) + sh2
        h2_ref[rs, :] = _pack_halves(h2)
        h2s.append(h2)
    h2 = jnp.concatenate(h2s, axis=0)
    h2_hi = h2.astype(BF16)
    h2_lo = (h2 - h2_hi.astype(F32)).astype(BF16)
    ne = br_ref.shape[0]
    nt_dims = (((1,), (1,)), ((), ()))
    both = lax.dot_general(wrt_ref[...], h2_hi, nt_dims, preferred_element_type=F32)
    cross = lax.dot_general(wrt_ref[:ne, :], h2_lo, nt_dims, preferred_element_type=F32)
    logits_t = both[:ne, :] + both[ne:, :] + cross + br_ref[...]
    idx_o, wts_o, rank_o, counts = _route(logits_t, carry[...])
    idx_ref[...] = idx_o
    wts_ref[...] = wts_o
    rank_ref[...] = rank_o
    carry[...] = counts
    cnt_ref[...] = counts


def _route(logits_t, counts):
    ne, tm = logits_t.shape
    row = lax.broadcasted_iota(jnp.int32, (ne, tm), 0)
    lg = logits_t
    vals, idxs = [], []
    for _ in range(TOP_K):
        m = jnp.max(lg, axis=0, keepdims=True)
        am = jnp.min(jnp.where(lg == m, row, ne), axis=0, keepdims=True)
        vals.append(m)
        idxs.append(am)
        lg = jnp.where(row == am, -jnp.inf, lg)
    es = [jnp.exp(v - vals[0]) for v in vals]
    den = functools.reduce(lambda a, b: a + b, es)
    onehot = functools.reduce(lambda a, b: a + b, [(row == am).astype(F32) for am in idxs])
    s_i = lax.broadcasted_iota(jnp.int32, (tm, tm), 0)
    t_i = lax.broadcasted_iota(jnp.int32, (tm, tm), 1)
    earlier = (s_i < t_i).astype(BF16)
    before = jnp.dot(onehot.astype(BF16), earlier, preferred_element_type=F32) + counts
    out_row = lax.broadcasted_iota(jnp.int32, (SUBLANES, tm), 0)
    idx_o = jnp.zeros((SUBLANES, tm), jnp.int32)
    wts_o = jnp.zeros((SUBLANES, tm), F32)
    rank_o = jnp.zeros((SUBLANES, tm), jnp.int32)
    for kk in range(TOP_K):
        rk = jnp.sum(jnp.where(row == idxs[kk], before, 0.0), axis=0, keepdims=True).astype(jnp.int32)
        idx_o = jnp.where(out_row == kk, idxs[kk], idx_o)
        wts_o = jnp.where(out_row == kk, es[kk] / den, wts_o)
        rank_o = jnp.where(out_row == kk, rk, rank_o)
    return idx_o, wts_o, rank_o, counts + jnp.sum(onehot, axis=1, keepdims=True)


def _merge(xp, xs, mod, g1, g2, af_c, af_l, aa_c, aa_l, wf, wa, wg, bg, wo, wr, br, tiles_per_lat_batch):
    n_ctx, d = xp.shape
    n = n_ctx + xs.shape[0]
    tm = TOK_TILE
    n_ctx_tiles = n_ctx // tm
    last = n_ctx_tiles - 1
    ne = wr.shape[1]
    da = aa_c.shape[1]
    df = af_c.shape[1]

    def mod_row(t):
        return jnp.where(t < n_ctx_tiles, 0, 1 + (t - n_ctx_tiles) // tiles_per_lat_batch)

    cmap = lambda t: (jnp.minimum(t, last), 0)
    lmap = lambda t: (jnp.maximum(t - n_ctx_tiles, 0), 0)
    full = lambda a: pl.BlockSpec(a.shape, lambda t: (0,) * a.ndim)
    tok = lambda w: pl.BlockSpec((tm, w), lambda t: (t, 0))
    per_choice = lambda dt: jax.ShapeDtypeStruct((SUBLANES, n), dt)
    choice_spec = pl.BlockSpec((SUBLANES, tm), lambda t: (0, t))
    wr_hi = wr.astype(BF16)
    wr_lo = (wr - wr_hi.astype(F32)).astype(BF16)
    wr_t = jnp.concatenate([wr_hi.T, wr_lo.T], axis=0)
    return pl.pallas_call(
        functools.partial(_merge_kernel, n_ctx_tiles),
        out_shape=(jax.ShapeDtypeStruct((n, d), F32), jax.ShapeDtypeStruct((n, d // 2), jnp.int32),
                   per_choice(jnp.int32), per_choice(F32), per_choice(jnp.int32),
                   jax.ShapeDtypeStruct((ne, 1), F32)),
        grid=(n // tm,),
        in_specs=[pl.BlockSpec((tm, d), cmap), pl.BlockSpec((tm, d), lmap),
                  pl.BlockSpec((1,) + mod.shape[1:], lambda t: (mod_row(t), 0, 0)),
                  full(g1), full(g2),
                  pl.BlockSpec((tm, df), cmap), pl.BlockSpec((tm, df), lmap),
                  pl.BlockSpec((tm, da), cmap), pl.BlockSpec((tm, da), lmap),
                  full(wf), full(wa), full(wg), full(bg), full(wo), full(wr_t), full(br)],
        out_specs=(tok(d), tok(d // 2), choice_spec, choice_spec, choice_spec,
                   pl.BlockSpec((ne, 1), lambda t: (0, 0))),
        scratch_shapes=[pltpu.VMEM((ne, 1), F32)],
        compiler_params=_cparams(1),
        name="merge",
    )(xp, xs, mod, g1, g2, af_c, af_l, aa_c, aa_l, wf, wa, wg, bg, wo, wr_t, br)


def _pos_kernel(off_ref, idx_ref, rank_ref, o_ref):
    idx = idx_ref[...]
    pos = rank_ref[...]
    for e in range(off_ref.shape[0]):
        pos = pos + jnp.where(idx == e, off_ref[e], 0)
    o_ref[...] = pos


def _positions(first_row, idx, rank):
    rows, n = idx.shape
    blk = POS_COLS
    spec = pl.BlockSpec((rows, blk), lambda j, off: (0, j))
    return pl.pallas_call(
        _pos_kernel,
        out_shape=jax.ShapeDtypeStruct((rows, n), jnp.int32),
        grid_spec=pltpu.PrefetchScalarGridSpec(num_scalar_prefetch=1, grid=(n // blk,),
                                               in_specs=[spec, spec], out_specs=spec),
        compiler_params=_cparams(1),
        name="positions",
    )(first_row, idx, rank)


def _moe_kernel(te_ref, nt_ref, nx_ref, rv_ref, sl_ref, x_ref, w1_hbm, b1_ref, w2_hbm, b2_ref, y_ref,
                w1s, w2s, sem):
    i = pl.program_id(0)
    e = te_ref[i]
    prev = te_ref[jnp.maximum(i - 1, 0)]
    first_of_run = (i == 0) | (e != prev)
    slot = sl_ref[i]

    def stage(expert, s):
        return (pltpu.make_async_copy(w1_hbm.at[expert], w1s.at[s], sem.at[0, s]),
                pltpu.make_async_copy(w2_hbm.at[expert], w2s.at[s], sem.at[1, s]))

    @pl.when(i == 0)
    def _():
        for cp in stage(e, slot):
            cp.start()

    @pl.when(first_of_run)
    def _():
        for cp in stage(e, slot):
            cp.wait()

        @pl.when(nx_ref[i] >= 0)
        def _():
            for cp in stage(nx_ref[i], 1 - slot):
                cp.start()

    def expert_rows(rs):
        dff = w2s.shape[1]
        x_lo, x_hi = _unpack_halves(x_ref[rs, :])
        x = jnp.concatenate([x_lo.astype(BF16), x_hi.astype(BF16)], axis=1)
        gu = jnp.dot(x, w1s[slot].astype(BF16), preferred_element_type=F32) + b1_ref[0]
        gate = jnp.minimum(gu[:, :dff], SWIGLU_LIMIT)
        up = jnp.clip(gu[:, dff:], -SWIGLU_LIMIT, SWIGLU_LIMIT)
        glu = gate * jax.nn.sigmoid(SWIGLU_ALPHA * gate)
        act = ((up + 1.0) * glu).astype(BF16)
        y = jnp.dot(act, w2s[slot].astype(BF16), preferred_element_type=F32) + b2_ref[0]
        y_ref[rs, :] = _pack_halves(y)

    tm = x_ref.shape[0]
    rv = rv_ref[i]

    @pl.when(rv == tm)
    def _():
        expert_rows(slice(0, tm))

    for sub in range(tm // MOE_SUB):
        @pl.when((rv < tm) & (rv > sub * MOE_SUB))
        def _(sub=sub):
            expert_rows(slice(sub * MOE_SUB, (sub + 1) * MOE_SUB))


def _moe(tile_expert, n_tiles, next_expert, rows_valid, slots, xs_sorted, w1, b1, w2, b2):
    p, dh = xs_sorted.shape
    ne, d, dff2 = w1.shape
    dff = w2.shape[1]
    tm = MOE_TILE
    max_tiles = p // tm

    def row_map(i, te, nt, nx, rv, sl):
        return (jnp.minimum(i, nt[0] - 1), 0)

    def b_map(i, te, nt, nx, rv, sl):
        return (te[i], 0, 0)

    grid_spec = pltpu.PrefetchScalarGridSpec(
        num_scalar_prefetch=5,
        grid=(max_tiles,),
        in_specs=[pl.BlockSpec((tm, dh), row_map),
                  pl.BlockSpec(memory_space=pl.ANY),
                  pl.BlockSpec((1, 1, dff2), b_map),
                  pl.BlockSpec(memory_space=pl.ANY),
                  pl.BlockSpec((1, 1, d), b_map)],
        out_specs=pl.BlockSpec((tm, dh), row_map),
        scratch_shapes=[pltpu.VMEM((2, d, dff2), F32), pltpu.VMEM((2, dff, d), F32),
                        pltpu.SemaphoreType.DMA((2, 2))],
    )
    return pl.pallas_call(
        _moe_kernel,
        out_shape=jax.ShapeDtypeStruct((p, dh), jnp.int32),
        grid_spec=grid_spec,
        compiler_params=_cparams(1),
        name="moe",
    )(tile_expert, n_tiles, next_expert, rows_valid, slots, xs_sorted,
      w1, b1.reshape(ne, 1, dff2), w2, b2.reshape(ne, 1, d))


def _sc_mesh():
    return plsc.VectorSubcoreMesh(core_axis_name="c", subcore_axis_name="s",
                                  num_cores=SC_CORES, num_subcores=SC_SUBCORES)


def _load_index_lists(pos_hbm, idx_v, isem, base, n_chunks, chunk):
    copies = [pltpu.make_async_copy(pos_hbm.at[kk, pl.ds(base + j * chunk, chunk)], idx_v.at[j * TOP_K + kk], isem)
              for j in range(n_chunks) for kk in range(TOP_K)]
    for cp in copies:
        cp.start()
    for cp in copies:
        cp.wait()


def _sc_dispatch(h, pos, p_rows):
    n, d = h.shape
    nw = SC_CORES * SC_SUBCORES
    chunk = SC_CHUNK
    assert n % (nw * chunk * 2) == 0
    tok_w = n // nw
    n_chunks = tok_w // chunk
    items = n_chunks * TOP_K

    @functools.partial(
        pl.kernel, mesh=_sc_mesh(),
        out_type=jax.ShapeDtypeStruct((p_rows, d), h.dtype),
        scratch_types=[pltpu.VMEM((items, chunk), jnp.int32),
                       pltpu.VMEM((2, chunk, d), h.dtype),
                       pltpu.SemaphoreType.DMA((2,)), pltpu.SemaphoreType.DMA((2,)), pltpu.SemaphoreType.DMA],
        name="sc_dispatch",
    )
    def run(h_hbm, pos_hbm, out_hbm, idx_v, rows_v, lsem, ssem, isem):
        wid = lax.axis_index("s") * SC_CORES + lax.axis_index("c")
        base = wid * tok_w
        _load_index_lists(pos_hbm, idx_v, isem, base, n_chunks, chunk)

        def load(j, slot):
            return pltpu.make_async_copy(h_hbm.at[pl.ds(base + j * chunk, chunk)], rows_v.at[slot], lsem.at[slot])

        def scat(j, kk, slot):
            return pltpu.make_async_copy(rows_v.at[slot], out_hbm.at[idx_v.at[j * TOP_K + kk]], ssem.at[slot])

        load(0, 0).start()

        @pl.loop(0, n_chunks, step=2)
        def _(j0):
            for b in range(2):
                j = j0 + b

                @pl.when(j >= 1)
                def _():
                    for kk in range(TOP_K):
                        scat(j - 1, kk, 1 - b).wait()

                @pl.when(j + 1 < n_chunks)
                def _():
                    load(j + 1, 1 - b).start()

                load(j, b).wait()
                for kk in range(TOP_K):
                    scat(j, kk, b).start()

        for kk in range(TOP_K):
            scat(n_chunks - 1, kk, (n_chunks - 1) % 2).wait()

    return run(h, pos)


def _sc_gather(y, pos):
    d = y.shape[1]
    n = pos.shape[1]
    nw = SC_CORES * SC_SUBCORES
    chunk = SC_CHUNK
    assert n % (nw * chunk * 2) == 0
    tok_w = n // nw
    n_chunks = tok_w // chunk
    items = n_chunks * TOP_K

    @functools.partial(
        pl.kernel, mesh=_sc_mesh(),
        out_type=jax.ShapeDtypeStruct((TOP_K, n, d), y.dtype),
        scratch_types=[pltpu.VMEM((items, chunk), jnp.int32),
                       pltpu.VMEM((2, chunk, d), y.dtype),
                       pltpu.SemaphoreType.DMA((2,)), pltpu.SemaphoreType.DMA((2,)), pltpu.SemaphoreType.DMA],
        name="sc_gather",
    )
    def run(y_hbm, pos_hbm, out_hbm, idx_v, rows_v, gsem, wsem, isem):
        wid = lax.axis_index("s") * SC_CORES + lax.axis_index("c")
        base = wid * tok_w
        _load_index_lists(pos_hbm, idx_v, isem, base, n_chunks, chunk)

        def gather(i, slot):
            return pltpu.make_async_copy(y_hbm.at[idx_v.at[i]], rows_v.at[slot], gsem.at[slot])

        def write(i, slot):
            dst = out_hbm.at[i % TOP_K, pl.ds(base + (i // TOP_K) * chunk, chunk)]
            return pltpu.make_async_copy(rows_v.at[slot], dst, wsem.at[slot])

        gather(0, 0).start()

        @pl.loop(0, items, step=2)
        def _(i0):
            for b in range(2):
                i = i0 + b

                @pl.when(i >= 1)
                def _():
                    write(i - 1, 1 - b).wait()

                @pl.when(i + 1 < items)
                def _():
                    gather(i + 1, 1 - b).start()

                gather(i, b).wait()
                write(i, b).start()

        write(items - 1, (items - 1) % 2).wait()

    return run(y, pos)


def _combine_kernel(n_ctx_tiles, x1_ref, yg_ref, wts_ref, mod_ref, op_ref, os_ref):
    t = pl.program_id(0)
    gt2 = mod_ref[0, 5:6, :]
    w = wts_ref[...]
    acc_lo, acc_hi = None, None
    for kk in range(TOP_K):
        y_lo, y_hi = _unpack_halves(yg_ref[kk])
        wk = w[:, kk:kk + 1]
        acc_lo = wk * y_lo if acc_lo is None else acc_lo + wk * y_lo
        acc_hi = wk * y_hi if acc_hi is None else acc_hi + wk * y_hi
    out = x1_ref[...] + gt2 * jnp.concatenate([acc_lo, acc_hi], axis=1)

    @pl.when(t < n_ctx_tiles)
    def _():
        op_ref[...] = out

    @pl.when(t >= n_ctx_tiles)
    def _():
        os_ref[...] = out


def _combine(x1, yg, wts, mod, n_ctx, lat_seq):
    n, d = x1.shape
    tm = COMBINE_TILE
    assert n_ctx % tm == 0 and lat_seq % tm == 0
    tiles_per_lat_batch = lat_seq // tm
    n_ctx_tiles = n_ctx // tm
    last = n_ctx_tiles - 1

    def mod_row(t):
        return jnp.where(t < n_ctx_tiles, 0, 1 + (t - n_ctx_tiles) // tiles_per_lat_batch)

    return pl.pallas_call(
        functools.partial(_combine_kernel, n_ctx_tiles),
        out_shape=(jax.ShapeDtypeStruct((n_ctx, d), F32), jax.ShapeDtypeStruct((n - n_ctx, d), F32)),
        grid=(n // tm,),
        in_specs=[pl.BlockSpec((tm, d), lambda t: (t, 0)),
                  pl.BlockSpec((TOP_K, tm, d // 2), lambda t: (0, t, 0)),
                  pl.BlockSpec((tm, TOP_K), lambda t: (t, 0)),
                  pl.BlockSpec((1,) + mod.shape[1:], lambda t: (mod_row(t), 0, 0))],
        out_specs=(pl.BlockSpec((tm, d), lambda t: (jnp.minimum(t, last), 0)),
                   pl.BlockSpec((tm, d), lambda t: (jnp.maximum(t - n_ctx_tiles, 0), 0))),
        compiler_params=_cparams(1),
        name="combine",
    )(x1, yg, wts, mod)


def kernel(x_prompt, x_sample, cache_k, cache_v, c, c_ctx, g_norm1, w_ada, b_ada, w_in, g_q, g_k, rpb,
           w_fmap, w_amap, w_gate, b_gate, w_out, g_norm2, w_router, b_router, w1, b1, w2, b2):
    batch, seq, d = x_prompt.shape
    dec_batch, dec_seq, _ = x_sample.shape
    assert w_ada.shape[0] == 1, "single-layer trunk"
    rows = dec_seq // GRID_W
    da = N_HEADS * HEAD_DIM
    n_ctx = batch * seq
    n_lat = dec_batch * dec_seq
    n = n_ctx + n_lat
    ne = w_router.shape[2]
    assert TOK_TILE % seq == 0 and n_ctx % TOK_TILE == 0 and dec_seq % TOK_TILE == 0
    assert dec_seq % FOURIER_TILE == 0 and n_ctx % dec_seq == 0
    assert rows % (ATT_TILE // GRID_W) == 0 and rows >= 3 * (ATT_TILE // GRID_W) and rows >= WIN_H_MAX
    assert seq == ATT_TILE and cache_k.shape[3] == ATT_TILE
    tiles_per_lat_batch = dec_seq // TOK_TILE

    n_mod_rows = -(-(1 + dec_batch) // SUBLANES) * SUBLANES
    cvecs = jnp.zeros((n_mod_rows, d), F32).at[0].set(c_ctx).at[1:1 + dec_batch].set(c)
    mod = _ada(cvecs, w_ada[0], b_ada[0]).reshape(n_mod_rows, 6, d)

    xp = x_prompt.reshape(n_ctx, d)
    xs = x_sample.reshape(n_lat, d)
    g1 = g_norm1[0].reshape(1, d)
    g2 = g_norm2[0].reshape(1, d)
    hsum = jnp.asarray(np.kron(np.eye(N_HEADS), np.full((HEAD_DIM, HEAD_DIM), 1.0 / HEAD_DIM)), BF16)
    gq_t = jnp.tile(g_q[0], N_HEADS).reshape(1, da)
    gk_t = jnp.tile(g_k[0], N_HEADS).reshape(1, da)

    u, q, k, v, new_k, new_v = _proj(xp, xs, mod, g1, w_in[0].astype(BF16), hsum, gq_t, gk_t,
                                      batch, seq, tiles_per_lat_batch)

    aa_c = _attn_ctx(q, k, v, batch, seq)
    bias = _window_bias(rpb[0], rows)
    past = cache_k.shape[3]
    kc = cache_k[:, 0].transpose(0, 2, 1, 3).reshape(dec_batch, past, da).astype(BF16)
    vc = cache_v[:, 0].transpose(0, 2, 1, 3).reshape(dec_batch, past, da).astype(BF16)
    aa_l = _attn_lat(q, k, v, kc, vc, bias, n_ctx, dec_batch, rows)

    w1c, ct_c, st_c = _dft_tables(seq)
    _, ct_l, st_l = _dft_tables(dec_seq)
    af_c = _fourier(u, w1c, ct_c, st_c, batch, seq, 0, seq, CTX_SEQS_PER_STEP)
    af_l = _fourier(u, w1c, ct_l, st_l, dec_batch, dec_seq, n_ctx // dec_seq, FOURIER_TILE)

    x1, h2, idx, wts, rank, counts = _merge(
        xp, xs, mod, g1, g2, af_c, af_l, aa_c, aa_l,
        w_fmap[0].astype(BF16), w_amap[0].astype(BF16), w_gate[0].astype(BF16), b_gate[0].reshape(1, -1),
        w_out[0].astype(BF16), w_router[0], b_router[0].reshape(ne, 1), tiles_per_lat_batch)

    tm = MOE_TILE
    max_tiles = (n * TOP_K) // tm + ne
    cnt = counts[:, 0].astype(jnp.int32)
    tiles_e = (cnt + tm - 1) // tm
    tile_end = jnp.cumsum(tiles_e)
    pad_off = (tile_end - tiles_e) * tm
    n_tiles = tile_end[-1:]
    tile_ids = jnp.arange(max_tiles, dtype=jnp.int32)
    tile_expert = jnp.sum((tile_ids[:, None] >= tile_end[None, :]).astype(jnp.int32), axis=1)
    last_e = jnp.sum((n_tiles - 1 >= tile_end).astype(jnp.int32))
    tile_expert = jnp.minimum(tile_expert, last_e).astype(jnp.int32)
    assert n % POS_COLS == 0
    pos = _positions(pad_off.astype(jnp.int32), idx, rank)

    p_rows = max_tiles * tm
    xs_sorted = _sc_dispatch(h2, pos, p_rows)
    e_ids = jnp.arange(ne, dtype=jnp.int32)
    later_used = (e_ids[None, :] > e_ids[:, None]) & (tiles_e[None, :] > 0)
    next_of_e = jnp.min(jnp.where(later_used, e_ids[None, :], ne), axis=1)
    next_of_e = jnp.where(next_of_e == ne, -1, next_of_e)
    is_e = tile_expert[:, None] == e_ids[None, :]
    next_expert = jnp.sum(jnp.where(is_e, next_of_e[None, :], 0), axis=1)
    rows_left = jnp.sum(jnp.where(is_e, (cnt + pad_off)[None, :], 0), axis=1) - tile_ids * tm
    rows_valid = jnp.where(tile_ids < n_tiles[0], jnp.clip(rows_left, 0, tm), 0)
    run_of_e = jnp.cumsum((tiles_e > 0).astype(jnp.int32)) - 1
    slots = jnp.sum(jnp.where(is_e, run_of_e[None, :], 0), axis=1) % 2
    y_sorted = _moe(tile_expert, n_tiles.astype(jnp.int32), next_expert.astype(jnp.int32),
                    rows_valid.astype(jnp.int32), slots.astype(jnp.int32), xs_sorted,
                    w1[0], b1[0], w2[0], b2[0])
    yg = _sc_gather(y_sorted, pos)
    y_p, y_s = _combine(x1, yg, wts[:TOP_K].T, mod, n_ctx, dec_seq)
    return (y_p.reshape(batch, seq, d), y_s.reshape(dec_batch, dec_seq, d), new_k, new_v)
```
